```python
import jax, jax.numpy as jnp
from jax import lax
import numpy as np

D_MODEL = 1024
BATCH = 8
SEQ = 2048
DEPTH = 4
DEC_BATCH = 128
DEC_SEQ = 8
PAST_LEN = 16384
PAGE_SIZE = 128

N_MIXERS = 4
N_MLSTM = (DEPTH + 3) // 4
N_HGRN = (DEPTH + 2) // 4
N_GLA = (DEPTH + 1) // 4
N_RWKV = DEPTH // 4

MLSTM_HEADS = 4
MLSTM_DK = D_MODEL // (2 * MLSTM_HEADS)
MLSTM_DV = D_MODEL // MLSTM_HEADS
MLSTM_CHUNK = 64

HGRN_DK = 128
HGRN_HEADS = D_MODEL // HGRN_DK
HGRN_DV = D_MODEL // HGRN_HEADS

GLA_HEADS = 4
GLA_DK = D_MODEL // (2 * GLA_HEADS)
GLA_DV = D_MODEL // GLA_HEADS
GLA_GATE_RANK = 16
GLA_GATE_NORMALIZER = 16.0
LIN_CHUNK = 32

RWKV_N = 64
RWKV_HEADS = D_MODEL // RWKV_N
RWKV_DECAY_LORA = 64
RWKV_AAA_LORA = 64
RWKV_GATE_LORA = 128
RWKV_GN_EPS = 64e-5

FFN_HIDDEN = ((8 * D_MODEL + 3 * 256 - 1) // (3 * 256)) * 256
NORM_EPS = 1e-6

kernel_name = 'mlstm_hgrn2_gla_rwkv7_hybrid_step'


def _rmsnorm(x, g):
    xf = x.astype(jnp.float32)
    y = xf * lax.rsqrt(jnp.mean(xf * xf, axis=-1, keepdims=True) + NORM_EPS)
    return (y * g.astype(jnp.float32)).astype(x.dtype)


def _head_rmsnorm(h, g):
    H, Dh = h.shape[-2:]
    y = h * lax.rsqrt(jnp.mean(h * h, axis=-1, keepdims=True) + NORM_EPS)
    return y * g.astype(jnp.float32).reshape(H, Dh)


def _chunk_len(T, cap):
    return max(d for d in range(1, min(cap, T) + 1) if T % d == 0)


def _to_chunks(a, L):
    B, T = a.shape[:2]
    return jnp.moveaxis(a.reshape((B, T // L, L) + a.shape[2:]), 1, 0)


def _from_chunks(a):
    n, B, L = a.shape[:3]
    return jnp.moveaxis(a, 0, 1).reshape((B, n * L) + a.shape[3:])


def _mlstm_chunked(q, k, v, i_pre, logf, C0, n0, m0):
    L = _chunk_len(q.shape[1], MLSTM_CHUNK)
    causal = jnp.tril(jnp.ones((L, L), bool))

    def step(carry, xs):
        C, n, m = carry
        qc, kc, vc, ic, fc = xs
        F = jnp.cumsum(fc, axis=1)
        Dm = F[:, :, None, :] - F[:, None, :, :] + ic[:, None, :, :]
        Dm = jnp.where(causal[None, :, :, None], Dm, -jnp.inf)
        g = F + m[:, None, :]
        mt = jnp.maximum(g, jnp.max(Dm, axis=2))
        S = jnp.einsum('bthd,bjhd->btjh', qc, kc) * jnp.exp(Dm - mt[:, :, None, :])
        eg = jnp.exp(g - mt)
        num = eg[..., None] * jnp.einsum('bhvd,bthd->bthv', C, qc) + jnp.einsum('btjh,bjhv->bthv', S, vc)
        den = eg * jnp.einsum('bhd,bthd->bth', n, qc) + jnp.sum(S, axis=2)
        h = num / jnp.maximum(jnp.abs(den), jnp.exp(-mt))[..., None]
        m_new = mt[:, -1]
        wC = jnp.exp(F[:, -1] + m - m_new)
        wj = jnp.exp(F[:, -1:, :] - F + ic - m_new[:, None, :])
        C_new = wC[..., None, None] * C + jnp.einsum('bjhv,bjhd->bhvd', vc * wj[..., None], kc)
        n_new = wC[..., None] * n + jnp.einsum('bjh,bjhd->bhd', wj, kc)
        return (C_new, n_new, m_new), h

    xs = tuple(_to_chunks(a, L) for a in (q, k, v, i_pre, logf))
    (C, n, m), h = lax.scan(step, (C0, n0, m0), xs)
    return _from_chunks(h), C, n, m


def _gla_chunked(q, k, v, logdecay, S0):
    L = _chunk_len(q.shape[1], LIN_CHUNK)
    causal = jnp.tril(jnp.ones((L, L), bool))

    def step(S, xs):
        qc, kc, vc, gc = xs
        b = jnp.cumsum(gc, axis=1)
        qe = qc * jnp.exp(b)
        ke = kc * jnp.exp(-b)
        A = jnp.where(causal, jnp.einsum('blhd,bmhd->bhlm', qe, ke), 0.0)
        o = jnp.einsum('blhd,bhdv->blhv', qe, S) + jnp.einsum('bhlm,bmhv->blhv', A, vc)
        bL = b[:, -1]
        S_new = jnp.exp(bL)[..., None] * S + jnp.einsum('bmhd,bmhv->bhdv', kc * jnp.exp(bL[:, None] - b), vc)
        return S_new, o

    xs = tuple(_to_chunks(a, L) for a in (q, k, v, logdecay))
    S, o = lax.scan(step, S0, xs)
    return _from_chunks(o), S


def _rwkv7_scan(r, w, k, v, a, b, S0):
    def step(S, xs):
        rt, wt, kt, vt, at, bt = xs
        sa = jnp.einsum('bhvk,bhk->bhv', S, at)
        S = S * wt[:, :, None, :] + sa[..., None] * bt[:, :, None, :] + vt[..., None] * kt[:, :, None, :]
        return S, jnp.einsum('bhvk,bhk->bhv', S, rt)

    xs = tuple(jnp.moveaxis(t, 1, 0) for t in (r, w, k, v, a, b))
    S, y = lax.scan(step, S0, xs)
    return jnp.moveaxis(y, 0, 1), S


def _mlstm_mixer(h, p, j, C0, n0, m0):
    B, T, _ = h.shape
    H, DK, DV = MLSTM_HEADS, MLSTM_DK, MLSTM_DV
    HK, HV = H * DK, H * DV
    pr = jnp.einsum('btd,de->bte', h, p['mlstm_w_in'][j]).astype(jnp.float32)
    q, k, v, o, ig, fg = jnp.split(pr, [HK, 2 * HK, 2 * HK + HV, 2 * HK + 2 * HV, 2 * HK + 2 * HV + H], axis=-1)
    q = q.reshape(B, T, H, DK) * (DK ** -0.5)
    k = k.reshape(B, T, H, DK)
    v = v.reshape(B, T, H, DV)
    bg = p['mlstm_b_gates'][j].astype(jnp.float32)
    i_pre = ig + bg[:H]
    logf = jax.nn.log_sigmoid(fg + bg[H:])
    hh, C, n, m = _mlstm_chunked(q, k, v, i_pre, logf, C0.astype(jnp.float32),
                                 n0.astype(jnp.float32), m0.astype(jnp.float32))
    y = jax.nn.sigmoid(o) * _head_rmsnorm(hh, p['mlstm_norm'][j]).reshape(B, T, HV)
    return jnp.einsum('bte,ed->btd', y.astype(h.dtype), p['mlstm_w_out'][j]), C, n, m


def _hgrn2_mixer(h, p, j, lb, S0):
    B, T, _ = h.shape
    H, DK, DV = HGRN_HEADS, HGRN_DK, HGRN_DV
    HK, HV = H * DK, H * DV
    pr = jnp.einsum('btd,de->bte', h, p['hgrn_w_in'][j]).astype(jnp.float32)
    q, f, i, g = jnp.split(pr, [HK, 2 * HK, 2 * HK + HV], axis=-1)
    lb = lb.astype(jnp.float32)
    logf = jnp.log(lb + (1.0 - lb) * jax.nn.sigmoid(f))
    key = (1.0 - lb) * jax.nn.sigmoid(-f)
    q = jax.nn.silu(q)
    o, S = _gla_chunked(q.reshape(B, T, H, DK), key.reshape(B, T, H, DK), i.reshape(B, T, H, DV),
                        logf.reshape(B, T, H, DK), S0.astype(jnp.float32))
    y = _head_rmsnorm(o, p['hgrn_norm'][j]).reshape(B, T, HV) * jax.nn.silu(g)
    return jnp.einsum('bte,ed->btd', y.astype(h.dtype), p['hgrn_w_out'][j]), S


def _gla_mixer(h, p, j, S0):
    B, T, _ = h.shape
    H, DK, DV = GLA_HEADS, GLA_DK, GLA_DV
    HK, HV = H * DK, H * DV
    pr = jnp.einsum('btd,de->bte', h, p['gla_w_in'][j]).astype(jnp.float32)
    q, k, v, g, gr = jnp.split(pr, [HK, 2 * HK, 2 * HK + HV, 2 * HK + 2 * HV], axis=-1)
    gk = jnp.einsum('btr,re->bte', gr, p['gla_w_gate_up'][j].astype(jnp.float32)) + p['gla_b_gate'][j].astype(jnp.float32)
    logdecay = jax.nn.log_sigmoid(gk) / GLA_GATE_NORMALIZER
    o, S = _gla_chunked(q.reshape(B, T, H, DK) * (DK ** -0.5), k.reshape(B, T, H, DK), v.reshape(B, T, H, DV),
                        logdecay.reshape(B, T, H, DK), S0.astype(jnp.float32))
    y = _head_rmsnorm(o, p['gla_norm'][j]).reshape(B, T, HV) * jax.nn.silu(g)
    return jnp.einsum('bte,ed->btd', y.astype(h.dtype), p['gla_w_out'][j]), S


def _rwkv7_mixer(h, p, j, shift0, S0):
    B, T, D = h.shape
    H, N = RWKV_HEADS, RWKV_N
    f32 = jnp.float32
    hf = h.astype(f32)
    prev = jnp.concatenate([shift0[:, None, :].astype(f32), hf[:, :-1]], axis=1)
    xx = prev - hf
    mu = p['rwkv_mu'][j].astype(f32)
    x_rkv = hf[None] + xx[None] * mu[:3, None, None, :]
    xw = hf + xx * mu[3]
    xa = hf + xx * mu[4]
    xg = hf + xx * mu[5]
    rkv = jnp.einsum('cbtd,cde->cbte', x_rkv, p['rwkv_w_rkv'][j].astype(f32))
    r, k, v = rkv[0], rkv[1], rkv[2]
    wl = -jax.nn.softplus(-(p['rwkv_w0'][j].astype(f32) + jnp.tanh(xw @ p['rwkv_w_lora_a'][j].astype(f32)) @ p['rwkv_w_lora_b'][j].astype(f32))) - 0.5
    decay = jnp.exp(-jnp.exp(wl))
    alr = jax.nn.sigmoid(p['rwkv_a0'][j].astype(f32) + (xa @ p['rwkv_a_lora_a'][j].astype(f32)) @ p['rwkv_a_lora_b'][j].astype(f32))
    gate = jax.nn.sigmoid(xg @ p['rwkv_g_lora_a'][j].astype(f32)) @ p['rwkv_g_lora_b'][j].astype(f32)
    kk = (k * p['rwkv_k_k'][j].astype(f32)).reshape(B, T, H, N)
    kk = kk / jnp.maximum(jnp.sqrt(jnp.sum(kk * kk, axis=-1, keepdims=True)), 1e-12)
    k = k * (1.0 + (alr - 1.0) * p['rwkv_k_a'][j].astype(f32))
    hd = lambda t: t.reshape(B, T, H, N)
    r4, k4, v4, a4 = hd(r), hd(k), hd(v), hd(alr)
    y, S = _rwkv7_scan(r4, hd(decay), k4, v4, -kk, kk * a4, S0.astype(f32))
    mean = jnp.mean(y, axis=-1, keepdims=True)
    var = jnp.mean(jnp.square(y - mean), axis=-1, keepdims=True)
    yn = ((y - mean) * lax.rsqrt(var + RWKV_GN_EPS)).reshape(B, T, D)
    yn = yn * p['rwkv_ln_w'][j].astype(f32) + p['rwkv_ln_b'][j].astype(f32)
    bonus = jnp.sum(r4 * k4 * p['rwkv_r_k'][j].astype(f32), axis=-1, keepdims=True) * v4
    out = (yn + bonus.reshape(B, T, D)) * gate
    return jnp.einsum('bte,ed->btd', out.astype(h.dtype), p['rwkv_w_out'][j]), S, hf[:, -1]


def _swiglu(h, w_gu, w_down):
    gu = jnp.einsum('btd,df->btf', h, w_gu)
    g, u = jnp.split(gu, 2, axis=-1)
    return jnp.einsum('btf,fd->btd', jax.nn.silu(g) * u, w_down)


def _trunk(x, mC, mn, mm, hS, gS, rS, rsh, p):
    sm = jax.nn.softmax(p['hgrn_lb_logits'].astype(jnp.float32), axis=0)
    lb_all = jnp.cumsum(sm, axis=0) - sm
    nC, nn_, nm, nh, ng, nr, nsh = [], [], [], [], [], [], []
    for li in range(DEPTH):
        j = li // N_MIXERS
        kind = li % N_MIXERS
        hn = _rmsnorm(x, p['norm_mix'][li])
        if kind == 0:
            out, C, n, m = _mlstm_mixer(hn, p, j, mC[j], mn[j], mm[j])
            nC.append(C); nn_.append(n); nm.append(m)
        elif kind == 1:
            out, S = _hgrn2_mixer(hn, p, j, lb_all[li], hS[j])
            nh.append(S)
        elif kind == 2:
            out, S = _gla_mixer(hn, p, j, gS[j])
            ng.append(S)
        else:
            out, S, sh = _rwkv7_mixer(hn, p, j, rsh[j], rS[j])
            nr.append(S); nsh.append(sh)
        x = x + out.astype(x.dtype)
        x = x + _swiglu(_rmsnorm(x, p['norm_ffn'][li]), p['ffn_w_gate_up'][li], p['ffn_w_down'][li]).astype(x.dtype)
    y = _rmsnorm(x, p['norm_final'])
    return y, (jnp.stack(nC), jnp.stack(nn_), jnp.stack(nm), jnp.stack(nh), jnp.stack(ng), jnp.stack(nr), jnp.stack(nsh))


def setup_inputs(seed: int = 0) -> dict:
    key = jax.random.key(seed)
    ks = list(jax.random.split(key, 64))

    def nk():
        return ks.pop()

    def nrm(shape, scale):
        return jax.random.normal(nk(), shape, jnp.float32) * scale

    def gain(shape):
        return 1.0 + nrm(shape, 0.02)

    D = D_MODEL
    mHK, mHV = MLSTM_HEADS * MLSTM_DK, MLSTM_HEADS * MLSTM_DV
    hHK, hHV = HGRN_HEADS * HGRN_DK, HGRN_HEADS * HGRN_DV
    gHK, gHV = GLA_HEADS * GLA_DK, GLA_HEADS * GLA_DV
    m_in = 2 * mHK + 2 * mHV + 2 * MLSTM_HEADS
    h_in = 2 * hHK + 2 * hHV
    g_in = 2 * gHK + 2 * gHV + GLA_GATE_RANK
    return {
        'x_prompt': nrm((BATCH, SEQ, D), 1.0),
        'x_sample': nrm((DEC_BATCH, DEC_SEQ, D), 1.0),
        'state_mlstm_C': nrm((N_MLSTM, DEC_BATCH, MLSTM_HEADS, MLSTM_DV, MLSTM_DK), 0.1),
        'state_mlstm_n': nrm((N_MLSTM, DEC_BATCH, MLSTM_HEADS, MLSTM_DK), 0.1),
        'state_mlstm_m': nrm((N_MLSTM, DEC_BATCH, MLSTM_HEADS), 1.0),
        'state_hgrn_S': nrm((N_HGRN, DEC_BATCH, HGRN_HEADS, HGRN_DK, HGRN_DV), 0.3),
        'state_gla_S': nrm((N_GLA, DEC_BATCH, GLA_HEADS, GLA_DK, GLA_DV), 0.3),
        'state_rwkv_S': nrm((N_RWKV, DEC_BATCH, RWKV_HEADS, RWKV_N, RWKV_N), 0.3),
        'state_rwkv_shift': nrm((N_RWKV, DEC_BATCH, D), 1.0),
        'norm_mix': gain((DEPTH, D)),
        'norm_ffn': gain((DEPTH, D)),
        'norm_final': gain((D,)),
        'mlstm_w_in': nrm((N_MLSTM, D, m_in), D ** -0.5),
        'mlstm_b_gates': nrm((N_MLSTM, 2 * MLSTM_HEADS), 0.1).at[:, MLSTM_HEADS:].add(3.0),
        'mlstm_norm': gain((N_MLSTM, mHV)),
        'mlstm_w_out': nrm((N_MLSTM, mHV, D), mHV ** -0.5),
        'hgrn_w_in': nrm((N_HGRN, D, h_in), D ** -0.5),
        'hgrn_lb_logits': nrm((DEPTH, hHK), 0.5),
        'hgrn_norm': gain((N_HGRN, hHV)),
        'hgrn_w_out': nrm((N_HGRN, hHV, D), hHV ** -0.5),
        'gla_w_in': nrm((N_GLA, D, g_in), D ** -0.5),
        'gla_w_gate_up': nrm((N_GLA, GLA_GATE_RANK, gHK), GLA_GATE_RANK ** -0.5),
        'gla_b_gate': nrm((N_GLA, gHK), 0.1),
        'gla_norm': gain((N_GLA, gHV)),
        'gla_w_out': nrm((N_GLA, gHV, D), gHV ** -0.5),
        'rwkv_mu': jax.random.uniform(nk(), (N_RWKV, 6, D), jnp.float32),
        'rwkv_w_rkv': nrm((N_RWKV, 3, D, D), D ** -0.5),
        'rwkv_w_lora_a': nrm((N_RWKV, D, RWKV_DECAY_LORA), D ** -0.5),
        'rwkv_w_lora_b': nrm((N_RWKV, RWKV_DECAY_LORA, D), 0.5 * RWKV_DECAY_LORA ** -0.5),
        'rwkv_w0': nrm((N_RWKV, D), 0.5),
        'rwkv_a_lora_a': nrm((N_RWKV, D, RWKV_AAA_LORA), D ** -0.5),
        'rwkv_a_lora_b': nrm((N_RWKV, RWKV_AAA_LORA, D), 0.5 * RWKV_AAA_LORA ** -0.5),
        'rwkv_a0': nrm((N_RWKV, D), 0.1),
        'rwkv_g_lora_a': nrm((N_RWKV, D, RWKV_GATE_LORA), D ** -0.5),
        'rwkv_g_lora_b': nrm((N_RWKV, RWKV_GATE_LORA, D), RWKV_GATE_LORA ** -0.5),
        'rwkv_k_k': 1.0 + nrm((N_RWKV, D), 0.1),
        'rwkv_k_a': 1.0 + nrm((N_RWKV, D), 0.1),
        'rwkv_r_k': nrm((N_RWKV, RWKV_HEADS, RWKV_N), 0.1),
        'rwkv_ln_w': gain((N_RWKV, D)),
        'rwkv_ln_b': nrm((N_RWKV, D), 0.01),
        'rwkv_w_out': nrm((N_RWKV, D, D), D ** -0.5),
        'ffn_w_gate_up': nrm((DEPTH, D, 2 * FFN_HIDDEN), D ** -0.5),
        'ffn_w_down': nrm((DEPTH, FFN_HIDDEN, D), FFN_HIDDEN ** -0.5),
    }


def reference(x_prompt, x_sample, state_mlstm_C, state_mlstm_n, state_mlstm_m, state_hgrn_S, state_gla_S,
              state_rwkv_S, state_rwkv_shift, norm_mix, norm_ffn, norm_final,
              mlstm_w_in, mlstm_b_gates, mlstm_norm, mlstm_w_out,
              hgrn_w_in, hgrn_lb_logits, hgrn_norm, hgrn_w_out,
              gla_w_in, gla_w_gate_up, gla_b_gate, gla_norm, gla_w_out,
              rwkv_mu, rwkv_w_rkv, rwkv_w_lora_a, rwkv_w_lora_b, rwkv_w0, rwkv_a_lora_a, rwkv_a_lora_b, rwkv_a0,
              rwkv_g_lora_a, rwkv_g_lora_b, rwkv_k_k, rwkv_k_a, rwkv_r_k, rwkv_ln_w, rwkv_ln_b, rwkv_w_out,
              ffn_w_gate_up, ffn_w_down):
    p = dict(norm_mix=norm_mix, norm_ffn=norm_ffn, norm_final=norm_final,
             mlstm_w_in=mlstm_w_in, mlstm_b_gates=mlstm_b_gates, mlstm_norm=mlstm_norm, mlstm_w_out=mlstm_w_out,
             hgrn_w_in=hgrn_w_in, hgrn_lb_logits=hgrn_lb_logits, hgrn_norm=hgrn_norm, hgrn_w_out=hgrn_w_out,
             gla_w_in=gla_w_in, gla_w_gate_up=gla_w_gate_up, gla_b_gate=gla_b_gate, gla_norm=gla_norm, gla_w_out=gla_w_out,
             rwkv_mu=rwkv_mu, rwkv_w_rkv=rwkv_w_rkv, rwkv_w_lora_a=rwkv_w_lora_a, rwkv_w_lora_b=rwkv_w_lora_b,
             rwkv_w0=rwkv_w0, rwkv_a_lora_a=rwkv_a_lora_a, rwkv_a_lora_b=rwkv_a_lora_b, rwkv_a0=rwkv_a0,
             rwkv_g_lora_a=rwkv_g_lora_a, rwkv_g_lora_b=rwkv_g_lora_b, rwkv_k_k=rwkv_k_k, rwkv_k_a=rwkv_k_a,
             rwkv_r_k=rwkv_r_k, rwkv_ln_w=rwkv_ln_w, rwkv_ln_b=rwkv_ln_b, rwkv_w_out=rwkv_w_out,
             ffn_w_gate_up=ffn_w_gate_up, ffn_w_down=ffn_w_down)
    Bp = x_prompt.shape[0]
    z = lambda shape: jnp.zeros(shape, jnp.float32)
    y_prompt, (p_C, p_n, p_m, p_hS, p_gS, p_rS, p_sh) = _trunk(
        x_prompt,
        z((N_MLSTM, Bp, MLSTM_HEADS, MLSTM_DV, MLSTM_DK)), z((N_MLSTM, Bp, MLSTM_HEADS, MLSTM_DK)),
        z((N_MLSTM, Bp, MLSTM_HEADS)), z((N_HGRN, Bp, HGRN_HEADS, HGRN_DK, HGRN_DV)),
        z((N_GLA, Bp, GLA_HEADS, GLA_DK, GLA_DV)), z((N_RWKV, Bp, RWKV_HEADS, RWKV_N, RWKV_N)),
        z((N_RWKV, Bp, D_MODEL)), p)
    y_sample, (s_C, s_n, s_m, s_hS, s_gS, s_rS, s_sh) = _trunk(
        x_sample, state_mlstm_C, state_mlstm_n, state_mlstm_m, state_hgrn_S, state_gla_S,
        state_rwkv_S, state_rwkv_shift, p)
    return (y_prompt, y_sample, p_C, p_n, p_m, p_hS, p_gS, p_rS, p_sh, s_C, s_n, s_m, s_hS, s_gS, s_rS, s_sh)
```

```python
import functools

import jax
import jax.numpy as jnp
from jax import lax
from jax.experimental import pallas as pl
from jax.experimental.pallas import tpu as pltpu

f32 = jnp.float32
bf16 = jnp.bfloat16

D_MODEL = 1024
DEPTH = 4
NORM_EPS = 1e-6

MLSTM_HEADS, MLSTM_DK, MLSTM_DV, MLSTM_CHUNK = 4, 128, 256, 64
HGRN_HEADS, HGRN_DK, HGRN_DV = 8, 128, 128
GLA_HEADS, GLA_DK, GLA_DV = 4, 128, 256
GLA_GATE_RANK = 16
GLA_GATE_NORMALIZER = 16.0
LIN_CHUNK = 32
RWKV_HEADS, RWKV_N = 16, 64
RWKV_GN_EPS = 64e-5
FFN_HIDDEN = 2816

LANES = 128
SUBLANES = 8
VMEM_LIMIT_BYTES = 52 * 1024 * 1024


def _cparams(*sem):
    return pltpu.CompilerParams(dimension_semantics=sem, vmem_limit_bytes=VMEM_LIMIT_BYTES)


def _chunk_len(t, cap):
    return max(d for d in range(1, min(cap, t) + 1) if t % d == 0)


def _rms(x, g):
    ms = jnp.mean(x * x, axis=-1, keepdims=True)
    return x * lax.rsqrt(ms + NORM_EPS) * g


def _sigmoid(x):
    return jax.nn.sigmoid(x)


def _softplus(x):
    return jnp.maximum(x, 0.0) + jnp.log1p(jnp.exp(-jnp.abs(x)))


def _log_sigmoid(x):
    return -_softplus(-x)


def _mm(a, b, mx):
    if mx == bf16:
        return jnp.dot(a.astype(bf16), b.astype(bf16), preferred_element_type=f32)
    return jnp.dot(a, b, preferred_element_type=f32, precision=lax.Precision.HIGHEST)


def _mm_nt(a, b, mx):
    dn = (((1,), (1,)), ((), ()))
    if mx == bf16:
        return lax.dot_general(a.astype(bf16), b.astype(bf16), dn, preferred_element_type=f32)
    return lax.dot_general(a, b, dn, preferred_element_type=f32, precision=lax.Precision.HIGHEST)


def _mm_tn(a, b, mx):
    dn = (((0,), (0,)), ((), ()))
    if mx == bf16:
        return lax.dot_general(a.astype(bf16), b.astype(bf16), dn, preferred_element_type=f32)
    rows = a.shape[0]
    if rows < LANES:
        a = jnp.concatenate([a, jnp.zeros((LANES - rows, a.shape[1]), f32)], axis=0)
        b = jnp.concatenate([b, jnp.zeros((LANES - rows, b.shape[1]), f32)], axis=0)
    return lax.dot_general(a, b, dn, preferred_element_type=f32, precision=lax.Precision.HIGHEST)


def _cumsum_rows(x, tril_b, causal_rows):
    L = x.shape[0]
    if L <= SUBLANES:
        acc = jnp.zeros_like(x)
        rowi = lax.broadcasted_iota(jnp.int32, x.shape, 0)
        for j in range(L):
            acc = acc + jnp.where(rowi >= j, x[j:j + 1, :], 0.0)
        return acc
    hi = x.astype(bf16)
    r1 = x - hi.astype(f32)
    mid = r1.astype(bf16)
    lo = (r1 - mid.astype(f32)).astype(bf16)
    return (jnp.dot(tril_b, hi, preferred_element_type=f32)
            + jnp.dot(tril_b, mid, preferred_element_type=f32)
            + jnp.dot(tril_b, lo, preferred_element_type=f32))


def _norm_proj_kernel(x_ref, g_ref, w_ref, o_ref, hn_ref):
    @pl.when(pl.program_id(1) == 0)
    def _():
        hn_ref[...] = _rms(x_ref[...], g_ref[...]).astype(bf16)

    o_ref[...] = jnp.dot(hn_ref[...], w_ref[...], preferred_element_type=f32)


def _norm_proj(x, g, w, tm, tn):
    n, d = x.shape
    e = w.shape[1]
    return pl.pallas_call(
        _norm_proj_kernel,
        grid=(n // tm, e // tn),
        in_specs=[pl.BlockSpec((tm, d), lambda i, j: (i, 0)),
                  pl.BlockSpec((1, d), lambda i, j: (0, 0)),
                  pl.BlockSpec((d, tn), lambda i, j: (0, j))],
        out_specs=pl.BlockSpec((tm, tn), lambda i, j: (i, j)),
        out_shape=jax.ShapeDtypeStruct((n, e), f32),
        scratch_shapes=[pltpu.VMEM((tm, d), bf16)],
        compiler_params=_cparams("parallel", "arbitrary"),
        name="norm_proj",
    )(x, g.reshape(1, d), w)


def _out_proj_kernel(*refs, gated):
    if gated:
        y_ref, gate_ref, w_ref, res_ref, o_ref = refs
        y = y_ref[...] * gate_ref[...]
    else:
        y_ref, w_ref, res_ref, o_ref = refs
        y = y_ref[...]
    o_ref[...] = res_ref[...] + jnp.dot(y.astype(bf16), w_ref[...], preferred_element_type=f32)


def _out_proj(y, w, res, tm, gate=None):
    n, e = y.shape
    d = w.shape[1]
    row = lambda i: (i, 0)
    args = [y] + ([gate] if gate is not None else []) + [w, res]
    in_specs = ([pl.BlockSpec((tm, e), row)] + ([pl.BlockSpec((tm, e), row)] if gate is not None else [])
                + [pl.BlockSpec((e, d), lambda i: (0, 0)), pl.BlockSpec((tm, d), row)])
    return pl.pallas_call(
        functools.partial(_out_proj_kernel, gated=gate is not None),
        grid=(n // tm,),
        in_specs=in_specs,
        out_specs=pl.BlockSpec((tm, d), row),
        out_shape=jax.ShapeDtypeStruct((n, d), f32),
        compiler_params=_cparams("parallel"),
        name="out_proj",
    )(*args)


def _ffn_kernel(x_ref, g_ref, wg_ref, wu_ref, wd_ref, o_ref, hn_ref, acc_ref):
    j = pl.program_id(1)

    @pl.when(j == 0)
    def _():
        hn_ref[...] = _rms(x_ref[...], g_ref[...]).astype(bf16)
        acc_ref[...] = jnp.zeros_like(acc_ref)

    h = hn_ref[...]
    gt = jnp.dot(h, wg_ref[...], preferred_element_type=f32)
    up = jnp.dot(h, wu_ref[...], preferred_element_type=f32)
    act = (gt * _sigmoid(gt) * up).astype(bf16)
    acc_ref[...] += jnp.dot(act, wd_ref[...], preferred_element_type=f32)

    @pl.when(j == pl.num_programs(1) - 1)
    def _():
        o_ref[...] = x_ref[...] + acc_ref[...]


def _ffn(x, g, w_gu, w_down, tm, tf):
    n, d = x.shape
    nf = FFN_HIDDEN // tf
    return pl.pallas_call(
        _ffn_kernel,
        grid=(n // tm, nf),
        in_specs=[pl.BlockSpec((tm, d), lambda i, j: (i, 0)),
                  pl.BlockSpec((1, d), lambda i, j: (0, 0)),
                  pl.BlockSpec((d, tf), lambda i, j: (0, j)),
                  pl.BlockSpec((d, tf), lambda i, j: (0, j + nf)),
                  pl.BlockSpec((tf, d), lambda i, j: (j, 0))],
        out_specs=pl.BlockSpec((tm, d), lambda i, j: (i, 0)),
        out_shape=jax.ShapeDtypeStruct((n, d), f32),
        scratch_shapes=[pltpu.VMEM((tm, d), bf16), pltpu.VMEM((tm, d), f32)],
        compiler_params=_cparams("parallel", "arbitrary"),
        name="ffn",
    )(x, g.reshape(1, d), w_gu, w_gu, w_down)


def _rmsnorm_kernel(x_ref, g_ref, o_ref):
    o_ref[...] = _rms(x_ref[...], g_ref[...])


def _rmsnorm(x, g, tm):
    n, d = x.shape
    return pl.pallas_call(
        _rmsnorm_kernel,
        grid=(n // tm,),
        in_specs=[pl.BlockSpec((tm, d), lambda i: (i, 0)), pl.BlockSpec((1, d), lambda i: (0, 0))],
        out_specs=pl.BlockSpec((tm, d), lambda i: (i, 0)),
        out_shape=jax.ShapeDtypeStruct((n, d), f32),
        compiler_params=_cparams("parallel"),
        name="rmsnorm",
    )(x, g.reshape(1, d))


def _mlstm_kernel(*refs, L, nchunks, has_init, mx):
    H, DK, DV = MLSTM_HEADS, MLSTM_DK, MLSTM_DV
    q_ref, k_ref, v_ref, o_ref, gt_ref, bias_ref, gain_ref = refs[:7]
    if has_init:
        c0_ref, n0_ref, m0_ref = refs[7:10]
        rest = refs[10:]
    else:
        rest = refs[7:]
    y_ref, c_ref, n_ref, m_ref, c_s, n_s, m_s = rest
    j = pl.program_id(1)

    @pl.when(j == 0)
    def _():
        if has_init:
            c_s[...] = c0_ref[0]
            n_s[...] = n0_ref[0]
            m_s[...] = m0_ref[0]
        else:
            c_s[...] = jnp.zeros_like(c_s)
            n_s[...] = jnp.zeros_like(n_s)
            m_s[...] = jnp.zeros_like(m_s)

    rowi = lax.broadcasted_iota(jnp.int32, (L, L), 0)
    coli = lax.broadcasted_iota(jnp.int32, (L, L), 1)
    causal = coli <= rowi
    eye = coli == rowi
    scale = DK ** -0.5

    def to_row(col):
        return jnp.sum(jnp.where(eye, col, 0.0), axis=0, keepdims=True)

    def chunk(c, carry):
        rows = pl.ds(pl.multiple_of(c * L, L), L)
        gts = gt_ref[rows, :] + bias_ref[...]
        lsg = _log_sigmoid(gts)
        for h in range(H):
            i_col = gts[:, h:h + 1]
            f_col = lsg[:, H + h:H + h + 1]
            f_row = to_row(f_col)
            fc_col = jnp.sum(jnp.where(causal, f_row, 0.0), axis=1, keepdims=True)
            fc_row = to_row(fc_col)
            i_row = to_row(i_col)
            m = m_s[h]
            dm = jnp.where(causal, fc_col - fc_row + i_row, -jnp.inf)
            g = fc_col + m
            mt = jnp.maximum(g, jnp.max(dm, axis=1, keepdims=True))
            p = jnp.exp(dm - mt)
            qc = q_ref[rows, h * DK:(h + 1) * DK] * scale
            kc = k_ref[rows, h * DK:(h + 1) * DK]
            vc = v_ref[rows, h * DV:(h + 1) * DV]
            s = _mm_nt(qc, kc, mx) * p
            eg = jnp.exp(g - mt)
            cst = c_s[h]
            nst = n_s[h]
            num = eg * _mm_nt(qc, cst, mx) + _mm(s, vc, mx)
            den = eg * jnp.sum(qc * nst, axis=1, keepdims=True) + jnp.sum(s, axis=1, keepdims=True)
            hh = num / jnp.maximum(jnp.abs(den), jnp.exp(-mt))
            m_new = mt[L - 1:L, :]
            f_last = fc_col[L - 1:L, :]
            w_c = jnp.exp(f_last + m - m_new)
            w_j = jnp.exp(f_last - fc_col + i_col - m_new)
            c_s[h] = w_c * cst + _mm_tn(vc * w_j, kc, mx)
            n_s[h] = w_c * nst + jnp.sum(w_j * kc, axis=0, keepdims=True)
            m_s[h] = m_new
            yn = hh * lax.rsqrt(jnp.mean(hh * hh, axis=1, keepdims=True) + NORM_EPS)
            yn = yn * gain_ref[:, h * DV:(h + 1) * DV]
            y_ref[rows, h * DV:(h + 1) * DV] = _sigmoid(o_ref[rows, h * DV:(h + 1) * DV]) * yn
        return carry

    lax.fori_loop(0, nchunks, chunk, 0)

    @pl.when(j == pl.num_programs(1) - 1)
    def _():
        c_ref[0] = c_s[...]
        n_ref[0] = n_s[...]
        m_ref[0] = m_s[...]


def _mlstm_mix(pr, b_gates, gain, B, T, state, tq):
    H, DK, DV = MLSTM_HEADS, MLSTM_DK, MLSTM_DV
    L = _chunk_len(T, MLSTM_CHUNK)
    nj = T // tq
    has_init = state is not None
    mx = f32 if L < 16 else bf16
    rmap = lambda c: (lambda b, j: (b * nj + j, c))
    bmap4 = lambda b, j: (b, 0, 0, 0)
    bias = jnp.zeros((1, LANES), f32).at[0, :2 * H].set(b_gates)
    in_specs = [pl.BlockSpec((tq, H * DK), rmap(0)), pl.BlockSpec((tq, H * DK), rmap(1)),
                pl.BlockSpec((tq, H * DV), rmap(1)), pl.BlockSpec((tq, H * DV), rmap(2)),
                pl.BlockSpec((tq, LANES), rmap((2 * H * DK + 2 * H * DV) // LANES)),
                pl.BlockSpec((1, LANES), lambda b, j: (0, 0)),
                pl.BlockSpec((1, H * DV), lambda b, j: (0, 0))]
    args = [pr, pr, pr, pr, pr, bias, gain.reshape(1, H * DV)]
    if has_init:
        c0, n0, m0 = state
        in_specs += [pl.BlockSpec((1, H, DV, DK), bmap4), pl.BlockSpec((1, H, 1, DK), bmap4),
                     pl.BlockSpec((1, H, 1, 1), bmap4)]
        args += [c0, n0.reshape(B, H, 1, DK), m0.reshape(B, H, 1, 1)]
    y, c, n, m = pl.pallas_call(
        functools.partial(_mlstm_kernel, L=L, nchunks=tq // L, has_init=has_init, mx=mx),
        grid=(B, nj),
        in_specs=in_specs,
        out_specs=[pl.BlockSpec((tq, H * DV), rmap(0)), pl.BlockSpec((1, H, DV, DK), bmap4),
                   pl.BlockSpec((1, H, 1, DK), bmap4), pl.BlockSpec((1, H, 1, 1), bmap4)],
        out_shape=[jax.ShapeDtypeStruct((B * T, H * DV), f32), jax.ShapeDtypeStruct((B, H, DV, DK), f32),
                   jax.ShapeDtypeStruct((B, H, 1, DK), f32), jax.ShapeDtypeStruct((B, H, 1, 1), f32)],
        scratch_shapes=[pltpu.VMEM((H, DV, DK), f32), pltpu.VMEM((H, 1, DK), f32), pltpu.VMEM((H, 1, 1), f32)],
        compiler_params=_cparams("parallel", "arbitrary"),
        name="mlstm_mix",
    )(*args)
    return y, c, n.reshape(B, H, DK), m.reshape(B, H)


def _gla_kernel(*refs, kind, layer, L, nchunks, H, DK, DV, has_init, mx):
    if kind == "hgrn":
        q_ref, f_ref, v_ref, g_ref, lbl_ref, gain_ref = refs[:6]
        rest = refs[6:]
    else:
        q_ref, k_ref, v_ref, g_ref, gr_ref, wup_ref, bgate_ref, gain_ref = refs[:8]
        rest = refs[8:]
    if has_init:
        s0_ref = rest[0]
        rest = rest[1:]
    y_ref, s_ref, st_s = rest
    j = pl.program_id(1)

    @pl.when(j == 0)
    def _():
        for h in range(H):
            if has_init:
                st_s[h] = s0_ref[0, h].T
            else:
                st_s[h] = jnp.zeros((DV, DK), f32)

    rowi = lax.broadcasted_iota(jnp.int32, (L, L), 0)
    coli = lax.broadcasted_iota(jnp.int32, (L, L), 1)
    causal = coli <= rowi
    tril_b = causal.astype(bf16)

    if kind == "hgrn":
        logits = lbl_ref[...]
        e = jnp.exp(logits - jnp.max(logits, axis=0, keepdims=True))
        sm = e / jnp.sum(e, axis=0, keepdims=True)
        lb_all = jnp.zeros((1, H * DK), f32)
        for li in range(layer):
            lb_all = lb_all + sm[li:li + 1, :]
    else:
        scale = DK ** -0.5

    def chunk(c, carry):
        rows = pl.ds(pl.multiple_of(c * L, L), L)
        if kind == "gla":
            gk = _mm(gr_ref[rows, :], wup_ref[...], mx) + bgate_ref[...]
            ld_all = _log_sigmoid(gk) * (1.0 / GLA_GATE_NORMALIZER)
        for h in range(H):
            ks = slice(h * DK, (h + 1) * DK)
            vs = slice(h * DV, (h + 1) * DV)
            if kind == "hgrn":
                qraw = q_ref[rows, ks]
                fg = f_ref[rows, ks]
                lb = lb_all[:, ks]
                gc = jnp.log(lb + (1.0 - lb) * _sigmoid(fg))
                kc = (1.0 - lb) * _sigmoid(-fg)
                qc = qraw * _sigmoid(qraw)
            else:
                qc = q_ref[rows, ks] * scale
                kc = k_ref[rows, ks]
                gc = ld_all[:, ks]
            vc = v_ref[rows, vs]
            bc = _cumsum_rows(gc, tril_b, causal)
            qe = qc * jnp.exp(bc)
            ke = kc * jnp.exp(-bc)
            a = jnp.where(causal, _mm_nt(qe, ke, mx), 0.0)
            st = st_s[h]
            o = _mm_nt(qe, st, mx) + _mm(a, vc, mx)
            b_last = bc[L - 1:L, :]
            kdec = kc * jnp.exp(b_last - bc)
            st_s[h] = st * jnp.exp(b_last) + _mm_tn(vc, kdec, mx)
            yn = o * lax.rsqrt(jnp.mean(o * o, axis=1, keepdims=True) + NORM_EPS) * gain_ref[:, vs]
            gg = g_ref[rows, vs]
            y_ref[rows, vs] = yn * (gg * _sigmoid(gg))
        return carry

    lax.fori_loop(0, nchunks, chunk, 0)

    @pl.when(j == pl.num_programs(1) - 1)
    def _():
        for h in range(H):
            s_ref[0, h] = st_s[h].T


def _gla_mix(kind, pr, extra, gain, B, T, state, tq, layer=0):
    if kind == "hgrn":
        H, DK, DV = HGRN_HEADS, HGRN_DK, HGRN_DV
    else:
        H, DK, DV = GLA_HEADS, GLA_DK, GLA_DV
    L = _chunk_len(T, LIN_CHUNK)
    nj = T // tq
    has_init = state is not None
    mx = f32 if L < 16 else bf16
    rmap = lambda c: (lambda b, j: (b * nj + j, c))
    bmap4 = lambda b, j: (b, 0, 0, 0)
    const2 = lambda b, j: (0, 0)
    hk, hv = H * DK, H * DV
    if kind == "hgrn":
        (lb_logits,) = extra
        in_specs = [pl.BlockSpec((tq, hk), rmap(0)), pl.BlockSpec((tq, hk), rmap(1)),
                    pl.BlockSpec((tq, hv), rmap(2)), pl.BlockSpec((tq, hv), rmap(3)),
                    pl.BlockSpec((DEPTH, hk), const2), pl.BlockSpec((1, hv), const2)]
        args = [pr, pr, pr, pr, lb_logits, gain.reshape(1, hv)]
    else:
        w_up, b_gate = extra
        w_up_p = jnp.zeros((LANES, hk), f32).at[:GLA_GATE_RANK].set(w_up)
        in_specs = [pl.BlockSpec((tq, hk), rmap(0)), pl.BlockSpec((tq, hk), rmap(1)),
                    pl.BlockSpec((tq, hv), rmap(1)), pl.BlockSpec((tq, hv), rmap(2)),
                    pl.BlockSpec((tq, LANES), rmap((2 * hk + 2 * hv) // LANES)),
                    pl.BlockSpec((LANES, hk), const2), pl.BlockSpec((1, hk), const2),
                    pl.BlockSpec((1, hv), const2)]
        args = [pr, pr, pr, pr, pr, w_up_p, b_gate.reshape(1, hk), gain.reshape(1, hv)]
    if has_init:
        in_specs.append(pl.BlockSpec((1, H, DK, DV), bmap4))
        args.append(state)
    y, s = pl.pallas_call(
        functools.partial(_gla_kernel, kind=kind, layer=layer, L=L, nchunks=tq // L, H=H, DK=DK, DV=DV,
                          has_init=has_init, mx=mx),
        grid=(B, nj),
        in_specs=in_specs,
        out_specs=[pl.BlockSpec((tq, hv), rmap(0)), pl.BlockSpec((1, H, DK, DV), bmap4)],
        out_shape=[jax.ShapeDtypeStruct((B * T, hv), f32), jax.ShapeDtypeStruct((B, H, DK, DV), f32)],
        scratch_shapes=[pltpu.VMEM((H, DV, DK), f32)],
        compiler_params=_cparams("parallel", "arbitrary"),
        name=kind + "_mix",
    )(*args)
    return y, s


def _rwkv_pre_kernel(x_ref, aux_ref, sh_ref, g_ref, mu_ref, wrkv_ref, la_ref, lbw_ref, lba_ref, lbg_ref,
                     w0_ref, a0_ref, r_ref, w_ref, k_ref, v_ref, alr_ref, gate_ref, *, long_seq, blocks_per_seq,
                     seq_len):
    g = g_ref[...]
    hn = _rms(x_ref[...], g)
    tm = hn.shape[0]
    rowi = lax.broadcasted_iota(jnp.int32, (tm, 1), 0)
    rolled = pltpu.roll(hn, 1, axis=0)
    if long_seq:
        p_last = _rms(aux_ref[SUBLANES - 1:SUBLANES, :], g)
        at_start = (pl.program_id(0) % blocks_per_seq) == 0
        first = jnp.where(at_start, sh_ref[0], p_last)
        prev = jnp.where(rowi == 0, first, rolled)
    else:
        prev = jnp.where(rowi % seq_len == 0, aux_ref[...], rolled)
    xx = prev - hn

    def lerp(c):
        return (hn + xx * mu_ref[c:c + 1, :]).astype(bf16)

    r_ref[...] = jnp.dot(lerp(0), wrkv_ref[0], preferred_element_type=f32)
    k_ref[...] = jnp.dot(lerp(1), wrkv_ref[1], preferred_element_type=f32)
    v_ref[...] = jnp.dot(lerp(2), wrkv_ref[2], preferred_element_type=f32)
    lw = jnp.tanh(jnp.dot(lerp(3), la_ref[:, 0:64], preferred_element_type=f32))
    wl = -_softplus(-(w0_ref[...] + jnp.dot(lw.astype(bf16), lbw_ref[...], preferred_element_type=f32))) - 0.5
    w_ref[...] = jnp.exp(-jnp.exp(wl))
    la = jnp.dot(lerp(4), la_ref[:, 64:128], preferred_element_type=f32)
    alr_ref[...] = _sigmoid(a0_ref[...] + jnp.dot(la.astype(bf16), lba_ref[...], preferred_element_type=f32))
    lg = _sigmoid(jnp.dot(lerp(5), la_ref[:, 128:256], preferred_element_type=f32))
    gate_ref[...] = jnp.dot(lg.astype(bf16), lbg_ref[...], preferred_element_type=f32)


def _rwkv_pre(x, shift0, g, mu, wrkv, la, lbw, lba, lbg, w0, a0, B, T, tm):
    n, d = x.shape
    long_seq = T % tm == 0
    row = lambda i: (i, 0)
    const2 = lambda i: (0, 0)
    if long_seq:
        bps = T // tm
        sub = tm // SUBLANES
        aux = x
        aux_spec = pl.BlockSpec((SUBLANES, d), lambda i: (jnp.maximum(i * sub - 1, 0), 0))
        sh = shift0.reshape(B, 1, d)
        sh_spec = pl.BlockSpec((1, 1, d), lambda i: (i // bps, 0, 0))
    else:
        assert tm % T == 0
        bps = 1
        aux = jnp.repeat(shift0, T, axis=0)
        aux_spec = pl.BlockSpec((tm, d), row)
        sh = shift0.reshape(B, 1, d)
        sh_spec = pl.BlockSpec((1, 1, d), lambda i: (0, 0, 0))
    out = jax.ShapeDtypeStruct((n, d), f32)
    return pl.pallas_call(
        functools.partial(_rwkv_pre_kernel, long_seq=long_seq, blocks_per_seq=bps, seq_len=T),
        grid=(n // tm,),
        in_specs=[pl.BlockSpec((tm, d), row), aux_spec, sh_spec, pl.BlockSpec((1, d), const2),
                  pl.BlockSpec((6, d), const2), pl.BlockSpec((3, d, d), lambda i: (0, 0, 0)),
                  pl.BlockSpec((d, 256), const2), pl.BlockSpec((64, d), const2), pl.BlockSpec((64, d), const2),
                  pl.BlockSpec((128, d), const2), pl.BlockSpec((1, d), const2), pl.BlockSpec((1, d), const2)],
        out_specs=[pl.BlockSpec((tm, d), row)] * 6,
        out_shape=[out] * 6,
        compiler_params=_cparams("parallel"),
        name="rwkv_pre",
    )(x, aux, sh, g.reshape(1, d), mu, wrkv, la, lbw, lba, lbg, w0.reshape(1, d), a0.reshape(1, d))


def _rwkv_scan_kernel(*refs, tt, has_init):
    N = RWKV_N
    r_ref, w_ref, k_ref, v_ref, alr_ref, kk_ref, ka_ref, rk_ref, lnw_ref, lnb_ref = refs[:10]
    if has_init:
        s0_ref = refs[10]
        rest = refs[11:]
    else:
        rest = refs[10:]
    z_ref, sout_ref, s_s, y_s = rest
    j = pl.program_id(1)

    @pl.when(j == 0)
    def _():
        if has_init:
            s_s[...] = s0_ref[...]
        else:
            s_s[...] = jnp.zeros_like(s_s)

    def step(t, carry):
        r = r_ref[t]
        w = w_ref[t]
        k = k_ref[t]
        v = v_ref[t]
        alr = alr_ref[t]
        kkraw = k * kk_ref[...]
        nrm = jnp.sqrt(jnp.sum(kkraw * kkraw, axis=0, keepdims=True))
        kk = kkraw / jnp.maximum(nrm, 1e-12)
        k2 = k * (1.0 + (alr - 1.0) * ka_ref[...])
        a = -kk
        b = kk * alr
        for vi in range(N):
            sv = s_s[vi]
            sa = jnp.sum(sv * a, axis=0, keepdims=True)
            sv = sv * w + sa * b + v[vi:vi + 1, :] * k2
            s_s[vi] = sv
            y_s[vi:vi + 1, :] = jnp.sum(sv * r, axis=0, keepdims=True)
        y = y_s[...]
        mean = jnp.mean(y, axis=0, keepdims=True)
        yc = y - mean
        var = jnp.mean(yc * yc, axis=0, keepdims=True)
        yn = yc * lax.rsqrt(var + RWKV_GN_EPS)
        bonus = jnp.sum(r * k2 * rk_ref[...], axis=0, keepdims=True) * v
        z_ref[t] = yn * lnw_ref[...] + lnb_ref[...] + bonus
        return carry

    lax.fori_loop(0, tt, step, 0)

    @pl.when(j == pl.num_programs(1) - 1)
    def _():
        sout_ref[...] = s_s[...]


def _rwkv_scan(r, w, k, v, alr, slabs, s0, tt):
    T, N, nl = r.shape
    has_init = s0 is not None
    tmap = lambda g, j: (j, 0, g)
    smap = lambda g, j: (0, 0, g)
    const2 = lambda g, j: (0, 0)
    stream = pl.BlockSpec((tt, N, LANES), tmap)
    in_specs = [stream] * 5 + [pl.BlockSpec((N, LANES), const2)] * 5
    args = [r, w, k, v, alr] + list(slabs)
    if has_init:
        in_specs.append(pl.BlockSpec((N, N, LANES), smap))
        args.append(s0)
    return pl.pallas_call(
        functools.partial(_rwkv_scan_kernel, tt=tt, has_init=has_init),
        grid=(nl // LANES, T // tt),
        in_specs=in_specs,
        out_specs=[stream, pl.BlockSpec((N, N, LANES), smap)],
        out_shape=[jax.ShapeDtypeStruct((T, N, nl), f32), jax.ShapeDtypeStruct((N, N, nl), f32)],
        scratch_shapes=[pltpu.VMEM((N, N, LANES), f32), pltpu.VMEM((N, LANES), f32)],
        compiler_params=_cparams("parallel", "arbitrary"),
        name="rwkv_scan",
    )(*args)


def _lane_slab(p):
    return jnp.tile(p.reshape(RWKV_HEADS, RWKV_N).T, (1, LANES // RWKV_HEADS))


def _rwkv_mix(x, p, B, T, state, tm):
    H, N = RWKV_HEADS, RWKV_N
    if state is None:
        shift0 = jnp.zeros((B, D_MODEL), f32)
        s0 = None
    else:
        s_in, shift0 = state
        s0 = s_in.transpose(2, 3, 0, 1).reshape(N, N, B * H)
    r, w, k, v, alr, gate = _rwkv_pre(x, shift0, p["g"], p["mu"], p["wrkv"], p["la"], p["lbw"], p["lba"], p["lbg"],
                                      p["w0"], p["a0"], B, T, tm)
    to_scan = lambda a: a.reshape(B, T, H, N).transpose(1, 3, 0, 2).reshape(T, N, B * H)
    slabs = [_lane_slab(p[n]) for n in ("k_k", "k_a", "r_k", "ln_w", "ln_b")]
    z, s = _rwkv_scan(to_scan(r), to_scan(w), to_scan(k), to_scan(v), to_scan(alr), slabs, s0, min(T, 32))
    z = z.reshape(T, N, B, H).transpose(2, 0, 3, 1).reshape(B * T, D_MODEL)
    s = s.reshape(N, N, B, H).transpose(2, 3, 0, 1)
    return z, gate, s


def _pad_cols(w, to):
    return jnp.pad(w, ((0, 0), (0, to - w.shape[1])))


def _trunk(x, B, T, states, p):
    n = x.shape[0]
    tm = min(512, n)
    tq = min(256, T)
    fresh = states is None
    if not fresh:
        m_c, m_n, m_m, h_s, g_s, r_s, r_sh = states
    new = {}
    for li in range(DEPTH):
        g_mix = p["norm_mix"][li]
        if li == 0:
            pr = _norm_proj(x, g_mix, p["mlstm_w_in"], tm, 640)
            y, c, nn, m = _mlstm_mix(pr, p["mlstm_b_gates"], p["mlstm_norm"], B, T,
                                     None if fresh else (m_c[0], m_n[0], m_m[0]), tq)
            new["C"], new["n"], new["m"] = c[None], nn[None], m[None]
            x = _out_proj(y, p["mlstm_w_out"], x, tm)
        elif li == 1:
            pr = _norm_proj(x, g_mix, p["hgrn_w_in"], tm, 1024)
            y, s = _gla_mix("hgrn", pr, (p["hgrn_lb_logits"],), p["hgrn_norm"], B, T,
                            None if fresh else h_s[0], tq, layer=li)
            new["hS"] = s[None]
            x = _out_proj(y, p["hgrn_w_out"], x, tm)
        elif li == 2:
            pr = _norm_proj(x, g_mix, p["gla_w_in"], tm, 640)
            y, s = _gla_mix("gla", pr, (p["gla_w_gate_up"], p["gla_b_gate"]), p["gla_norm"], B, T,
                            None if fresh else g_s[0], tq)
            new["gS"] = s[None]
            x = _out_proj(y, p["gla_w_out"], x, tm)
        else:
            rp = dict(p["rwkv"], g=g_mix)
            z, gate, s = _rwkv_mix(x, rp, B, T, None if fresh else (r_s[0], r_sh[0]), min(256, n))
            x_last = x.reshape(B, T, D_MODEL)[:, T - 1, :]
            new["rS"] = s[None]
            new["sh"] = _rmsnorm(x_last, g_mix, B)[None]
            x = _out_proj(z, p["rwkv_w_out"], x, tm, gate=gate)
        x = _ffn(x, p["norm_ffn"][li], p["ffn_w_gate_up"][li], p["ffn_w_down"][li], tm, 256)
    y = _rmsnorm(x, p["norm_final"], tm)
    return y.reshape(B, T, D_MODEL), (new["C"], new["n"], new["m"], new["hS"], new["gS"], new["rS"], new["sh"])


def kernel(x_prompt, x_sample, state_mlstm_C, state_mlstm_n, state_mlstm_m, state_hgrn_S, state_gla_S, state_rwkv_S, state_rwkv_shift, norm_mix, norm_ffn, norm_final, mlstm_w_in, mlstm_b_gates, mlstm_norm, mlstm_w_out, hgrn_w_in, hgrn_lb_logits, hgrn_norm, hgrn_w_out, gla_w_in, gla_w_gate_up, gla_b_gate, gla_norm, gla_w_out, rwkv_mu, rwkv_w_rkv, rwkv_w_lora_a, rwkv_w_lora_b, rwkv_w0, rwkv_a_lora_a, rwkv_a_lora_b, rwkv_a0, rwkv_g_lora_a, rwkv_g_lora_b, rwkv_k_k, rwkv_k_a, rwkv_r_k, rwkv_ln_w, rwkv_ln_b, rwkv_w_out, ffn_w_gate_up, ffn_w_down):
    cast = lambda w: w.astype(bf16)
    p = dict(
        norm_mix=norm_mix, norm_ffn=norm_ffn, norm_final=norm_final,
        mlstm_w_in=cast(_pad_cols(mlstm_w_in[0], 3200)), mlstm_b_gates=mlstm_b_gates[0], mlstm_norm=mlstm_norm[0],
        mlstm_w_out=cast(mlstm_w_out[0]),
        hgrn_w_in=cast(hgrn_w_in[0]), hgrn_lb_logits=hgrn_lb_logits, hgrn_norm=hgrn_norm[0],
        hgrn_w_out=cast(hgrn_w_out[0]),
        gla_w_in=cast(_pad_cols(gla_w_in[0], 3200)), gla_w_gate_up=gla_w_gate_up[0], gla_b_gate=gla_b_gate[0],
        gla_norm=gla_norm[0], gla_w_out=cast(gla_w_out[0]),
        rwkv=dict(mu=rwkv_mu[0], wrkv=cast(rwkv_w_rkv[0]),
                  la=cast(jnp.concatenate([rwkv_w_lora_a[0], rwkv_a_lora_a[0], rwkv_g_lora_a[0]], axis=1)),
                  lbw=cast(rwkv_w_lora_b[0]), lba=cast(rwkv_a_lora_b[0]), lbg=cast(rwkv_g_lora_b[0]),
                  w0=rwkv_w0[0], a0=rwkv_a0[0], k_k=rwkv_k_k[0], k_a=rwkv_k_a[0], r_k=rwkv_r_k[0].reshape(-1),
                  ln_w=rwkv_ln_w[0], ln_b=rwkv_ln_b[0]),
        rwkv_w_out=cast(rwkv_w_out[0]),
        ffn_w_gate_up=cast(ffn_w_gate_up), ffn_w_down=cast(ffn_w_down),
    )
    bp, tp, _ = x_prompt.shape
    bs, ts, _ = x_sample.shape
    y_p, st_p = _trunk(x_prompt.reshape(bp * tp, D_MODEL), bp, tp, None, p)
    y_s, st_s = _trunk(x_sample.reshape(bs * ts, D_MODEL), bs, ts,
                       (state_mlstm_C, state_mlstm_n, state_mlstm_m, state_hgrn_S, state_gla_S, state_rwkv_S,
                        state_rwkv_shift), p)
    return (y_p, y_s) + st_p + st_s
```

```python
import functools

import jax
import jax.numpy as jnp
from jax import lax
from jax.experimental import pallas as pl
from jax.experimental.pallas import tpu as pltpu

f32 = jnp.float32
bf16 = jnp.bfloat16

D_MODEL = 1024
DEPTH = 4
NORM_EPS = 1e-6

MLSTM_HEADS, MLSTM_DK, MLSTM_DV, MLSTM_CHUNK = 4, 128, 256, 64
HGRN_HEADS, HGRN_DK, HGRN_DV = 8, 128, 128
GLA_HEADS, GLA_DK, GLA_DV = 4, 128, 256
GLA_GATE_RANK = 16
GLA_GATE_NORMALIZER = 16.0
LIN_CHUNK = 32
RWKV_HEADS, RWKV_N = 16, 64
RWKV_GN_EPS = 64e-5
FFN_HIDDEN = 2816

LANES = 128
SUBLANES = 8
VMEM_LIMIT_BYTES = 52 * 1024 * 1024


def _cparams(*sem):
    return pltpu.CompilerParams(dimension_semantics=sem, vmem_limit_bytes=VMEM_LIMIT_BYTES)


def _chunk_len(t, cap):
    return max(d for d in range(1, min(cap, t) + 1) if t % d == 0)


def _rms(x, g):
    ms = jnp.mean(x * x, axis=-1, keepdims=True)
    return x * lax.rsqrt(ms + NORM_EPS) * g


def _sigmoid(x):
    return jax.nn.sigmoid(x)


def _softplus(x):
    return jnp.maximum(x, 0.0) + jnp.log1p(jnp.exp(-jnp.abs(x)))


def _log_sigmoid(x):
    return -_softplus(-x)


def _mm(a, b, mx):
    if mx == bf16:
        return jnp.dot(a.astype(bf16), b.astype(bf16), preferred_element_type=f32)
    return jnp.dot(a, b, preferred_element_type=f32, precision=lax.Precision.HIGHEST)


def _mm_nt(a, b, mx):
    dn = (((1,), (1,)), ((), ()))
    if mx == bf16:
        return lax.dot_general(a.astype(bf16), b.astype(bf16), dn, preferred_element_type=f32)
    return lax.dot_general(a, b, dn, preferred_element_type=f32, precision=lax.Precision.HIGHEST)


def _mm_tn(a, b, mx):
    dn = (((0,), (0,)), ((), ()))
    if mx == bf16:
        return lax.dot_general(a.astype(bf16), b.astype(bf16), dn, preferred_element_type=f32)
    rows = a.shape[0]
    if rows < LANES:
        a = jnp.concatenate([a, jnp.zeros((LANES - rows, a.shape[1]), f32)], axis=0)
        b = jnp.concatenate([b, jnp.zeros((LANES - rows, b.shape[1]), f32)], axis=0)
    return lax.dot_general(a, b, dn, preferred_element_type=f32, precision=lax.Precision.HIGHEST)


def _cumsum_rows(x, tril_b, causal_rows):
    L = x.shape[0]
    if L <= SUBLANES:
        acc = jnp.zeros_like(x)
        rowi = lax.broadcasted_iota(jnp.int32, x.shape, 0)
        for j in range(L):
            acc = acc + jnp.where(rowi >= j, x[j:j + 1, :], 0.0)
        return acc
    hi = x.astype(bf16)
    r1 = x - hi.astype(f32)
    mid = r1.astype(bf16)
    lo = (r1 - mid.astype(f32)).astype(bf16)
    return (jnp.dot(tril_b, hi, preferred_element_type=f32)
            + jnp.dot(tril_b, mid, preferred_element_type=f32)
            + jnp.dot(tril_b, lo, preferred_element_type=f32))


def _norm_proj_kernel(x_ref, g_ref, w_ref, o_ref, hn_ref):
    @pl.when(pl.program_id(1) == 0)
    def _():
        hn_ref[...] = _rms(x_ref[...], g_ref[...]).astype(bf16)

    o_ref[...] = jnp.dot(hn_ref[...], w_ref[...], preferred_element_type=f32)


def _norm_proj(x, g, w, tm, tn):
    n, d = x.shape
    e = w.shape[1]
    return pl.pallas_call(
        _norm_proj_kernel,
        grid=(n // tm, e // tn),
        in_specs=[pl.BlockSpec((tm, d), lambda i, j: (i, 0)),
                  pl.BlockSpec((1, d), lambda i, j: (0, 0)),
                  pl.BlockSpec((d, tn), lambda i, j: (0, j))],
        out_specs=pl.BlockSpec((tm, tn), lambda i, j: (i, j)),
        out_shape=jax.ShapeDtypeStruct((n, e), f32),
        scratch_shapes=[pltpu.VMEM((tm, d), bf16)],
        compiler_params=_cparams("parallel", "arbitrary"),
        name="norm_proj",
    )(x, g.reshape(1, d), w)


def _out_proj_kernel(*refs, gated):
    if gated:
        y_ref, gate_ref, w_ref, res_ref, o_ref = refs
        y = y_ref[...] * gate_ref[...]
    else:
        y_ref, w_ref, res_ref, o_ref = refs
        y = y_ref[...]
    o_ref[...] = res_ref[...] + jnp.dot(y.astype(bf16), w_ref[...], preferred_element_type=f32)


def _out_proj(y, w, res, tm, gate=None):
    n, e = y.shape
    d = w.shape[1]
    row = lambda i: (i, 0)
    args = [y] + ([gate] if gate is not None else []) + [w, res]
    in_specs = ([pl.BlockSpec((tm, e), row)] + ([pl.BlockSpec((tm, e), row)] if gate is not None else [])
                + [pl.BlockSpec((e, d), lambda i: (0, 0)), pl.BlockSpec((tm, d), row)])
    return pl.pallas_call(
        functools.partial(_out_proj_kernel, gated=gate is not None),
        grid=(n // tm,),
        in_specs=in_specs,
        out_specs=pl.BlockSpec((tm, d), row),
        out_shape=jax.ShapeDtypeStruct((n, d), f32),
        compiler_params=_cparams("parallel"),
        name="out_proj",
    )(*args)


def _ffn_kernel(x_ref, g_ref, wg_ref, wu_ref, wd_ref, o_ref, hn_ref, acc_ref):
    j = pl.program_id(1)

    @pl.when(j == 0)
    def _():
        hn_ref[...] = _rms(x_ref[...], g_ref[...]).astype(bf16)
        acc_ref[...] = jnp.zeros_like(acc_ref)

    h = hn_ref[...]
    gt = jnp.dot(h, wg_ref[...], preferred_element_type=f32)
    up = jnp.dot(h, wu_ref[...], preferred_element_type=f32)
    act = (gt * _sigmoid(gt) * up).astype(bf16)
    acc_ref[...] += jnp.dot(act, wd_ref[...], preferred_element_type=f32)

    @pl.when(j == pl.num_programs(1) - 1)
    def _():
        o_ref[...] = x_ref[...] + acc_ref[...]


def _ffn(x, g, w_gu, w_down, tm, tf):
    n, d = x.shape
    nf = FFN_HIDDEN // tf
    return pl.pallas_call(
        _ffn_kernel,
        grid=(n // tm, nf),
        in_specs=[pl.BlockSpec((tm, d), lambda i, j: (i, 0)),
                  pl.BlockSpec((1, d), lambda i, j: (0, 0)),
                  pl.BlockSpec((d, tf), lambda i, j: (0, j)),
                  pl.BlockSpec((d, tf), lambda i, j: (0, j + nf)),
                  pl.BlockSpec((tf, d), lambda i, j: (j, 0))],
        out_specs=pl.BlockSpec((tm, d), lambda i, j: (i, 0)),
        out_shape=jax.ShapeDtypeStruct((n, d), f32),
        scratch_shapes=[pltpu.VMEM((tm, d), bf16), pltpu.VMEM((tm, d), f32)],
        compiler_params=_cparams("parallel", "arbitrary"),
        name="ffn",
    )(x, g.reshape(1, d), w_gu, w_gu, w_down)


def _rmsnorm_kernel(x_ref, g_ref, o_ref):
    o_ref[...] = _rms(x_ref[...], g_ref[...])


def _rmsnorm(x, g, tm):
    n, d = x.shape
    return pl.pallas_call(
        _rmsnorm_kernel,
        grid=(n // tm,),
        in_specs=[pl.BlockSpec((tm, d), lambda i: (i, 0)), pl.BlockSpec((1, d), lambda i: (0, 0))],
        out_specs=pl.BlockSpec((tm, d), lambda i: (i, 0)),
        out_shape=jax.ShapeDtypeStruct((n, d), f32),
        compiler_params=_cparams("parallel"),
        name="rmsnorm",
    )(x, g.reshape(1, d))


def _mlstm_kernel(*refs, L, nchunks, has_init, mx):
    H, DK, DV = MLSTM_HEADS, MLSTM_DK, MLSTM_DV
    q_ref, k_ref, v_ref, o_ref, gt_ref, bias_ref, gain_ref = refs[:7]
    if has_init:
        c0_ref, n0_ref, m0_ref = refs[7:10]
        rest = refs[10:]
    else:
        rest = refs[7:]
    y_ref, c_ref, n_ref, m_ref, c_s, n_s, m_s = rest
    j = pl.program_id(1)

    @pl.when(j == 0)
    def _():
        if has_init:
            c_s[...] = c0_ref[0]
            n_s[...] = n0_ref[0]
            m_s[...] = m0_ref[0]
        else:
            c_s[...] = jnp.zeros_like(c_s)
            n_s[...] = jnp.zeros_like(n_s)
            m_s[...] = jnp.zeros_like(m_s)

    rowi = lax.broadcasted_iota(jnp.int32, (L, L), 0)
    coli = lax.broadcasted_iota(jnp.int32, (L, L), 1)
    causal = coli <= rowi
    eye = coli == rowi
    scale = DK ** -0.5

    def to_row(col):
        return jnp.sum(jnp.where(eye, col, 0.0), axis=0, keepdims=True)

    def chunk(c, carry):
        rows = pl.ds(pl.multiple_of(c * L, L), L)
        gts = gt_ref[rows, :] + bias_ref[...]
        lsg = _log_sigmoid(gts)
        for h in range(H):
            i_col = gts[:, h:h + 1]
            f_col = lsg[:, H + h:H + h + 1]
            f_row = to_row(f_col)
            fc_col = jnp.sum(jnp.where(causal, f_row, 0.0), axis=1, keepdims=True)
            fc_row = to_row(fc_col)
            i_row = to_row(i_col)
            m = m_s[h]
            dm = jnp.where(causal, fc_col - fc_row + i_row, -jnp.inf)
            g = fc_col + m
            mt = jnp.maximum(g, jnp.max(dm, axis=1, keepdims=True))
            p = jnp.exp(dm - mt)
            qc = q_ref[rows, h * DK:(h + 1) * DK] * scale
            kc = k_ref[rows, h * DK:(h + 1) * DK]
            vc = v_ref[rows, h * DV:(h + 1) * DV]
            s = _mm_nt(qc, kc, mx) * p
            eg = jnp.exp(g - mt)
            cst = c_s[h]
            nst = n_s[h]
            num = eg * _mm_nt(qc, cst, mx) + _mm(s, vc, mx)
            den = eg * jnp.sum(qc * nst, axis=1, keepdims=True) + jnp.sum(s, axis=1, keepdims=True)
            hh = num / jnp.maximum(jnp.abs(den), jnp.exp(-mt))
            m_new = mt[L - 1:L, :]
            f_last = fc_col[L - 1:L, :]
            w_c = jnp.exp(f_last + m - m_new)
            w_j = jnp.exp(f_last - fc_col + i_col - m_new)
            c_s[h] = w_c * cst + _mm_tn(vc * w_j, kc, mx)
            n_s[h] = w_c * nst + jnp.sum(w_j * kc, axis=0, keepdims=True)
            m_s[h] = m_new
            yn = hh * lax.rsqrt(jnp.mean(hh * hh, axis=1, keepdims=True) + NORM_EPS)
            yn = yn * gain_ref[:, h * DV:(h + 1) * DV]
            y_ref[rows, h * DV:(h + 1) * DV] = _sigmoid(o_ref[rows, h * DV:(h + 1) * DV]) * yn
        return carry

    lax.fori_loop(0, nchunks, chunk, 0)

    @pl.when(j == pl.num_programs(1) - 1)
    def _():
        c_ref[0] = c_s[...]
        n_ref[0] = n_s[...]
        m_ref[0] = m_s[...]


def _mlstm_mix(pr, b_gates, gain, B, T, state, tq):
    H, DK, DV = MLSTM_HEADS, MLSTM_DK, MLSTM_DV
    L = _chunk_len(T, MLSTM_CHUNK)
    nj = T // tq
    has_init = state is not None
    mx = f32 if L < 16 else bf16
    rmap = lambda c: (lambda b, j: (b * nj + j, c))
    bmap4 = lambda b, j: (b, 0, 0, 0)
    bias = jnp.zeros((1, LANES), f32).at[0, :2 * H].set(b_gates)
    in_specs = [pl.BlockSpec((tq, H * DK), rmap(0)), pl.BlockSpec((tq, H * DK), rmap(1)),
                pl.BlockSpec((tq, H * DV), rmap(1)), pl.BlockSpec((tq, H * DV), rmap(2)),
                pl.BlockSpec((tq, LANES), rmap((2 * H * DK + 2 * H * DV) // LANES)),
                pl.BlockSpec((1, LANES), lambda b, j: (0, 0)),
                pl.BlockSpec((1, H * DV), lambda b, j: (0, 0))]
    args = [pr, pr, pr, pr, pr, bias, gain.reshape(1, H * DV)]
    if has_init:
        c0, n0, m0 = state
        in_specs += [pl.BlockSpec((1, H, DV, DK), bmap4), pl.BlockSpec((1, H, 1, DK), bmap4),
                     pl.BlockSpec((1, H, 1, 1), bmap4)]
        args += [c0, n0.reshape(B, H, 1, DK), m0.reshape(B, H, 1, 1)]
    y, c, n, m = pl.pallas_call(
        functools.partial(_mlstm_kernel, L=L, nchunks=tq // L, has_init=has_init, mx=mx),
        grid=(B, nj),
        in_specs=in_specs,
        out_specs=[pl.BlockSpec((tq, H * DV), rmap(0)), pl.BlockSpec((1, H, DV, DK), bmap4),
                   pl.BlockSpec((1, H, 1, DK), bmap4), pl.BlockSpec((1, H, 1, 1), bmap4)],
        out_shape=[jax.ShapeDtypeStruct((B * T, H * DV), f32), jax.ShapeDtypeStruct((B, H, DV, DK), f32),
                   jax.ShapeDtypeStruct((B, H, 1, DK), f32), jax.ShapeDtypeStruct((B, H, 1, 1), f32)],
        scratch_shapes=[pltpu.VMEM((H, DV, DK), f32), pltpu.VMEM((H, 1, DK), f32), pltpu.VMEM((H, 1, 1), f32)],
        compiler_params=_cparams("parallel", "arbitrary"),
        name="mlstm_mix",
    )(*args)
    return y, c, n.reshape(B, H, DK), m.reshape(B, H)


def _mlstm_block_kernel(q_ref, k_ref, v_ref, o_ref, gt_ref, bias_ref, gain_ref, y_ref, c_ref, n_ref, m_ref,
                        c_s, n_s, m_s, *, L, nc):
    H, DK, DV = MLSTM_HEADS, MLSTM_DK, MLSTM_DV
    P = 2 * L
    j = pl.program_id(1)

    @pl.when(j == 0)
    def _():
        c_s[...] = jnp.zeros_like(c_s)
        n_s[...] = jnp.zeros_like(n_s)
        m_s[...] = jnp.zeros_like(m_s)

    rowi = lax.broadcasted_iota(jnp.int32, (L, L), 0)
    coli = lax.broadcasted_iota(jnp.int32, (L, L), 1)
    causal = coli <= rowi
    r2 = lax.broadcasted_iota(jnp.int32, (P, P), 0)
    c2 = lax.broadcasted_iota(jnp.int32, (P, P), 1)
    pair_tril = ((c2 <= r2) & ((c2 >= L) == (r2 >= L))).astype(bf16)
    scale = DK ** -0.5

    gates = []
    for pp in range(nc // 2):
        g = gt_ref[pp * P:(pp + 1) * P, :] + bias_ref[...]
        hi, mid, lo = _split3(_log_sigmoid(g))
        fcum = (jnp.dot(pair_tril, hi, preferred_element_type=f32)
                + jnp.dot(pair_tril, mid, preferred_element_type=f32)
                + jnp.dot(pair_tril, lo, preferred_element_type=f32))
        g_t = g.T
        f_t = fcum.T
        for half in range(2):
            rs = slice(half * L, (half + 1) * L)
            gates.append((g[rs, :], fcum[rs, :], g_t[:, rs], f_t[:, rs]))

    pairs = [(c, h) for c in range(nc) for h in range(H)]
    rows_of = lambda c: slice(c * L, (c + 1) * L)
    ks_of = lambda h: slice(h * DK, (h + 1) * DK)
    vs_of = lambda h: slice(h * DV, (h + 1) * DV)
    dms, rmax, fcols, icols, qks = {}, {}, {}, {}, {}
    for c, h in pairs:
        g_c, f_c, g_r, f_r = gates[c]
        icols[c, h] = g_c[:, h:h + 1]
        fcols[c, h] = f_c[:, H + h:H + h + 1]
        dm = jnp.where(causal, fcols[c, h] - f_r[H + h:H + h + 1, :] + g_r[h:h + 1, :], -jnp.inf)
        dms[c, h] = dm
        rmax[c, h] = jnp.max(dm, axis=1, keepdims=True)
        qb = (q_ref[rows_of(c), ks_of(h)] * scale).astype(bf16)
        qks[c, h] = _mm_nt(qb, k_ref[rows_of(c), ks_of(h)], bf16)
    m_in, m_out = {}, {}
    for h in range(H):
        m = m_s[h]
        for c in range(nc):
            m_in[c, h] = m
            m = jnp.maximum(fcols[c, h][L - 1:L, :] + m, rmax[c, h][L - 1:L, :])
            m_out[c, h] = m
        m_s[h] = m
    mts, egs, ss, ssums, wcs, dcs, dns = {}, {}, {}, {}, {}, {}, {}
    for c, h in pairs:
        fc_col = fcols[c, h]
        gg = fc_col + m_in[c, h]
        mt = jnp.maximum(gg, rmax[c, h])
        mts[c, h] = mt
        egs[c, h] = jnp.exp(gg - mt)
        s = qks[c, h] * jnp.exp(dms[c, h] - mt)
        ssums[c, h] = jnp.sum(s, axis=1, keepdims=True)
        ss[c, h] = s.astype(bf16)
        f_last = fc_col[L - 1:L, :]
        wcs[c, h] = jnp.exp(f_last + m_in[c, h] - m_out[c, h])
        w_j = jnp.exp(f_last - fc_col + icols[c, h] - m_out[c, h])
        kc = k_ref[rows_of(c), ks_of(h)]
        dcs[c, h] = _mm_tn(v_ref[rows_of(c), vs_of(h)] * w_j, kc, bf16)
        dns[c, h] = jnp.sum(w_j * kc, axis=0, keepdims=True)
    c_in, n_in = {}, {}
    for h in range(H):
        cst = c_s[h]
        nst = n_s[h]
        for c in range(nc):
            c_in[c, h] = cst.astype(bf16)
            n_in[c, h] = nst
            cst = wcs[c, h] * cst + dcs[c, h]
            nst = wcs[c, h] * nst + dns[c, h]
        c_s[h] = cst
        n_s[h] = nst
    hhs = {}
    for c, h in pairs:
        rows, ks, vs = rows_of(c), ks_of(h), vs_of(h)
        qc = q_ref[rows, ks] * scale
        num = egs[c, h] * _mm_nt(qc, c_in[c, h], bf16) + _mm(ss[c, h], v_ref[rows, vs], bf16)
        den = egs[c, h] * jnp.sum(qc * n_in[c, h], axis=1, keepdims=True) + ssums[c, h]
        hhs[c, h] = num / jnp.maximum(jnp.abs(den), jnp.exp(-mts[c, h]))
    for c, h in pairs:
        rows, vs = rows_of(c), vs_of(h)
        hh = hhs[c, h]
        yn = hh * lax.rsqrt(jnp.mean(hh * hh, axis=1, keepdims=True) + NORM_EPS) * gain_ref[:, vs]
        y_ref[rows, vs] = _sigmoid(o_ref[rows, vs]) * yn

    @pl.when(j == pl.num_programs(1) - 1)
    def _():
        c_ref[0] = c_s[...]
        n_ref[0] = n_s[...]
        m_ref[0] = m_s[...]


def _mlstm_mix_fresh(pr, b_gates, gain, B, T, tq):
    H, DK, DV = MLSTM_HEADS, MLSTM_DK, MLSTM_DV
    L = MLSTM_CHUNK
    nj = T // tq
    rmap = lambda c: (lambda b, j: (b * nj + j, c))
    bmap4 = lambda b, j: (b, 0, 0, 0)
    bias = jnp.zeros((1, LANES), f32).at[0, :2 * H].set(b_gates)
    y, c, n, m = pl.pallas_call(
        functools.partial(_mlstm_block_kernel, L=L, nc=tq // L),
        grid=(B, nj),
        in_specs=[pl.BlockSpec((tq, H * DK), rmap(0)), pl.BlockSpec((tq, H * DK), rmap(1)),
                  pl.BlockSpec((tq, H * DV), rmap(1)), pl.BlockSpec((tq, H * DV), rmap(2)),
                  pl.BlockSpec((tq, LANES), rmap((2 * H * DK + 2 * H * DV) // LANES)),
                  pl.BlockSpec((1, LANES), lambda b, j: (0, 0)),
                  pl.BlockSpec((1, H * DV), lambda b, j: (0, 0))],
        out_specs=[pl.BlockSpec((tq, H * DV), rmap(0)), pl.BlockSpec((1, H, DV, DK), bmap4),
                   pl.BlockSpec((1, H, 1, DK), bmap4), pl.BlockSpec((1, H, 1, 1), bmap4)],
        out_shape=[jax.ShapeDtypeStruct((B * T, H * DV), f32), jax.ShapeDtypeStruct((B, H, DV, DK), f32),
                   jax.ShapeDtypeStruct((B, H, 1, DK), f32), jax.ShapeDtypeStruct((B, H, 1, 1), f32)],
        scratch_shapes=[pltpu.VMEM((H, DV, DK), f32), pltpu.VMEM((H, 1, DK), f32), pltpu.VMEM((H, 1, 1), f32)],
        compiler_params=_cparams("parallel", "arbitrary"),
        name="mlstm_mix_fresh",
    )(pr, pr, pr, pr, pr, bias, gain.reshape(1, H * DV))
    return y, c, n.reshape(B, H, DK), m.reshape(B, H)


def _gla_kernel(*refs, kind, layer, L, nchunks, H, DK, DV, has_init, mx):
    if kind == "hgrn":
        q_ref, f_ref, v_ref, g_ref, lbl_ref, gain_ref = refs[:6]
        rest = refs[6:]
    else:
        q_ref, k_ref, v_ref, g_ref, gr_ref, wup_ref, bgate_ref, gain_ref = refs[:8]
        rest = refs[8:]
    if has_init:
        s0_ref = rest[0]
        rest = rest[1:]
    y_ref, s_ref, st_s = rest
    j = pl.program_id(1)

    @pl.when(j == 0)
    def _():
        for h in range(H):
            if has_init:
                st_s[h] = s0_ref[0, h].T
            else:
                st_s[h] = jnp.zeros((DV, DK), f32)

    rowi = lax.broadcasted_iota(jnp.int32, (L, L), 0)
    coli = lax.broadcasted_iota(jnp.int32, (L, L), 1)
    causal = coli <= rowi
    tril_b = causal.astype(bf16)

    if kind == "hgrn":
        logits = lbl_ref[...]
        e = jnp.exp(logits - jnp.max(logits, axis=0, keepdims=True))
        sm = e / jnp.sum(e, axis=0, keepdims=True)
        lb_all = jnp.zeros((1, H * DK), f32)
        for li in range(layer):
            lb_all = lb_all + sm[li:li + 1, :]
    else:
        scale = DK ** -0.5

    def chunk(c, carry):
        rows = pl.ds(pl.multiple_of(c * L, L), L)
        if kind == "gla":
            gk = _mm(gr_ref[rows, :], wup_ref[...], mx) + bgate_ref[...]
            ld_all = _log_sigmoid(gk) * (1.0 / GLA_GATE_NORMALIZER)
        for h in range(H):
            ks = slice(h * DK, (h + 1) * DK)
            vs = slice(h * DV, (h + 1) * DV)
            if kind == "hgrn":
                qraw = q_ref[rows, ks]
                fg = f_ref[rows, ks]
                lb = lb_all[:, ks]
                gc = jnp.log(lb + (1.0 - lb) * _sigmoid(fg))
                kc = (1.0 - lb) * _sigmoid(-fg)
                qc = qraw * _sigmoid(qraw)
            else:
                qc = q_ref[rows, ks] * scale
                kc = k_ref[rows, ks]
                gc = ld_all[:, ks]
            vc = v_ref[rows, vs]
            bc = _cumsum_rows(gc, tril_b, causal)
            qe = qc * jnp.exp(bc)
            ke = kc * jnp.exp(-bc)
            a = jnp.where(causal, _mm_nt(qe, ke, mx), 0.0)
            st = st_s[h]
            o = _mm_nt(qe, st, mx) + _mm(a, vc, mx)
            b_last = bc[L - 1:L, :]
            kdec = kc * jnp.exp(b_last - bc)
            st_s[h] = st * jnp.exp(b_last) + _mm_tn(vc, kdec, mx)
            yn = o * lax.rsqrt(jnp.mean(o * o, axis=1, keepdims=True) + NORM_EPS) * gain_ref[:, vs]
            gg = g_ref[rows, vs]
            y_ref[rows, vs] = yn * (gg * _sigmoid(gg))
        return carry

    lax.fori_loop(0, nchunks, chunk, 0)

    @pl.when(j == pl.num_programs(1) - 1)
    def _():
        for h in range(H):
            s_ref[0, h] = st_s[h].T


def _gla_mix(kind, pr, extra, gain, B, T, state, tq, layer=0):
    if kind == "hgrn":
        H, DK, DV = HGRN_HEADS, HGRN_DK, HGRN_DV
    else:
        H, DK, DV = GLA_HEADS, GLA_DK, GLA_DV
    L = _chunk_len(T, LIN_CHUNK)
    nj = T // tq
    has_init = state is not None
    mx = f32 if L < 16 else bf16
    rmap = lambda c: (lambda b, j: (b * nj + j, c))
    bmap4 = lambda b, j: (b, 0, 0, 0)
    const2 = lambda b, j: (0, 0)
    hk, hv = H * DK, H * DV
    if kind == "hgrn":
        (lb_logits,) = extra
        in_specs = [pl.BlockSpec((tq, hk), rmap(0)), pl.BlockSpec((tq, hk), rmap(1)),
                    pl.BlockSpec((tq, hv), rmap(2)), pl.BlockSpec((tq, hv), rmap(3)),
                    pl.BlockSpec((DEPTH, hk), const2), pl.BlockSpec((1, hv), const2)]
        args = [pr, pr, pr, pr, lb_logits, gain.reshape(1, hv)]
    else:
        w_up, b_gate = extra
        w_up_p = jnp.zeros((LANES, hk), f32).at[:GLA_GATE_RANK].set(w_up)
        in_specs = [pl.BlockSpec((tq, hk), rmap(0)), pl.BlockSpec((tq, hk), rmap(1)),
                    pl.BlockSpec((tq, hv), rmap(1)), pl.BlockSpec((tq, hv), rmap(2)),
                    pl.BlockSpec((tq, LANES), rmap((2 * hk + 2 * hv) // LANES)),
                    pl.BlockSpec((LANES, hk), const2), pl.BlockSpec((1, hk), const2),
                    pl.BlockSpec((1, hv), const2)]
        args = [pr, pr, pr, pr, pr, w_up_p, b_gate.reshape(1, hk), gain.reshape(1, hv)]
    if has_init:
        in_specs.append(pl.BlockSpec((1, H, DK, DV), bmap4))
        args.append(state)
    y, s = pl.pallas_call(
        functools.partial(_gla_kernel, kind=kind, layer=layer, L=L, nchunks=tq // L, H=H, DK=DK, DV=DV,
                          has_init=has_init, mx=mx),
        grid=(B, nj),
        in_specs=in_specs,
        out_specs=[pl.BlockSpec((tq, hv), rmap(0)), pl.BlockSpec((1, H, DK, DV), bmap4)],
        out_shape=[jax.ShapeDtypeStruct((B * T, hv), f32), jax.ShapeDtypeStruct((B, H, DK, DV), f32)],
        scratch_shapes=[pltpu.VMEM((H, DV, DK), f32)],
        compiler_params=_cparams("parallel", "arbitrary"),
        name=kind + "_mix",
    )(*args)
    return y, s


def _split3(x):
    hi = x.astype(bf16)
    r1 = x - hi.astype(f32)
    mid = r1.astype(bf16)
    lo = (r1 - mid.astype(f32)).astype(bf16)
    return hi, mid, lo


def _gla_block_kernel(*refs, kind, layer, L, nc, H, DK, DV):
    if kind == "hgrn":
        q_ref, f_ref, v_ref, g_ref, lbl_ref, gain_ref = refs[:6]
        rest = refs[6:]
    else:
        q_ref, k_ref, v_ref, g_ref, gr_ref, wup_ref, bgate_ref, gain_ref = refs[:8]
        rest = refs[8:]
    y_ref, s_ref, st_s, qe_s, ke_s, qs_s, kd_s, el_s = rest
    j = pl.program_id(1)

    @pl.when(j == 0)
    def _():
        st_s[...] = jnp.zeros_like(st_s)

    rowi = lax.broadcasted_iota(jnp.int32, (L, L), 0)
    coli = lax.broadcasted_iota(jnp.int32, (L, L), 1)
    causal = coli <= rowi
    tril_b = causal.astype(bf16)

    if kind == "hgrn":
        logits = lbl_ref[...]
        e = jnp.exp(logits - jnp.max(logits, axis=0, keepdims=True))
        sm = e / jnp.sum(e, axis=0, keepdims=True)
        lb = jnp.zeros((1, H * DK), f32)
        for li in range(layer):
            lb = lb + sm[li:li + 1, :]
    else:
        scale = DK ** -0.5

    for c in range(nc):
        rows = slice(c * L, (c + 1) * L)
        if kind == "hgrn":
            qraw = q_ref[rows, :]
            fg = f_ref[rows, :]
            gc = jnp.log(lb + (1.0 - lb) * _sigmoid(fg))
            kc = (1.0 - lb) * _sigmoid(-fg)
            qc = qraw * _sigmoid(qraw)
        else:
            gk = jnp.dot(gr_ref[rows, :].astype(bf16), wup_ref[...].astype(bf16),
                         preferred_element_type=f32) + bgate_ref[...]
            gc = _log_sigmoid(gk) * (1.0 / GLA_GATE_NORMALIZER)
            qc = q_ref[rows, :] * scale
            kc = k_ref[rows, :]
        hi, mid, lo = _split3(gc)
        bc = (jnp.dot(tril_b, hi, preferred_element_type=f32) + jnp.dot(tril_b, mid, preferred_element_type=f32)
              + jnp.dot(tril_b, lo, preferred_element_type=f32))
        b_mid = bc[L // 2 - 1:L // 2, :]
        b_last = bc[L - 1:L, :]
        qe_s[rows, :] = (qc * jnp.exp(bc - b_mid)).astype(bf16)
        ke_s[rows, :] = (kc * jnp.exp(b_mid - bc)).astype(bf16)
        qs_s[rows, :] = (qc * jnp.exp(bc)).astype(bf16)
        kd_s[rows, :] = (kc * jnp.exp(b_last - bc)).astype(bf16)
        el_s[c] = jnp.exp(b_last)

    dn_nt = (((1,), (1,)), ((), ()))
    dn_tn = (((0,), (0,)), ((), ()))
    pairs = [(c, h) for c in range(nc) for h in range(H)]
    rows_of = lambda c: slice(c * L, (c + 1) * L)
    ks_of = lambda h: slice(h * DK, (h + 1) * DK)
    vs_of = lambda h: slice(h * DV, (h + 1) * DV)
    vbs, amats, dsts = {}, {}, {}
    for c, h in pairs:
        rows, ks = rows_of(c), ks_of(h)
        vb = v_ref[rows, vs_of(h)].astype(bf16)
        a = lax.dot_general(qe_s[rows, ks], ke_s[rows, ks], dn_nt, preferred_element_type=f32)
        vbs[c, h] = vb
        amats[c, h] = jnp.where(causal, a, 0.0).astype(bf16)
        dsts[c, h] = lax.dot_general(vb, kd_s[rows, ks], dn_tn, preferred_element_type=f32)
    sts = {}
    for h in range(H):
        st = st_s[h]
        for c in range(nc):
            sts[c, h] = st.astype(bf16)
            st = st * el_s[c][:, ks_of(h)] + dsts[c, h]
        st_s[h] = st
    outs = {}
    for c, h in pairs:
        rows, ks = rows_of(c), ks_of(h)
        outs[c, h] = (lax.dot_general(qs_s[rows, ks], sts[c, h], dn_nt, preferred_element_type=f32)
                      + jnp.dot(amats[c, h], vbs[c, h], preferred_element_type=f32))
    for c, h in pairs:
        rows, vs = rows_of(c), vs_of(h)
        o = outs[c, h]
        yn = o * lax.rsqrt(jnp.mean(o * o, axis=1, keepdims=True) + NORM_EPS) * gain_ref[:, vs]
        gg = g_ref[rows, vs]
        y_ref[rows, vs] = yn * (gg * _sigmoid(gg))

    @pl.when(j == pl.num_programs(1) - 1)
    def _():
        for h in range(H):
            s_ref[0, h] = st_s[h].T


def _gla_mix_fresh(kind, pr, extra, gain, B, T, tq, layer=0):
    if kind == "hgrn":
        H, DK, DV = HGRN_HEADS, HGRN_DK, HGRN_DV
    else:
        H, DK, DV = GLA_HEADS, GLA_DK, GLA_DV
    L = 2 * LIN_CHUNK
    nc = tq // L
    nj = T // tq
    rmap = lambda c: (lambda b, j: (b * nj + j, c))
    bmap4 = lambda b, j: (b, 0, 0, 0)
    const2 = lambda b, j: (0, 0)
    hk, hv = H * DK, H * DV
    if kind == "hgrn":
        (lb_logits,) = extra
        in_specs = [pl.BlockSpec((tq, hk), rmap(0)), pl.BlockSpec((tq, hk), rmap(1)),
                    pl.BlockSpec((tq, hv), rmap(2)), pl.BlockSpec((tq, hv), rmap(3)),
                    pl.BlockSpec((DEPTH, hk), const2), pl.BlockSpec((1, hv), const2)]
        args = [pr, pr, pr, pr, lb_logits, gain.reshape(1, hv)]
    else:
        w_up, b_gate = extra
        w_up_p = jnp.zeros((LANES, hk), f32).at[:GLA_GATE_RANK].set(w_up)
        in_specs = [pl.BlockSpec((tq, hk), rmap(0)), pl.BlockSpec((tq, hk), rmap(1)),
                    pl.BlockSpec((tq, hv), rmap(1)), pl.BlockSpec((tq, hv), rmap(2)),
                    pl.BlockSpec((tq, LANES), rmap((2 * hk + 2 * hv) // LANES)),
                    pl.BlockSpec((LANES, hk), const2), pl.BlockSpec((1, hk), const2),
                    pl.BlockSpec((1, hv), const2)]
        args = [pr, pr, pr, pr, pr, w_up_p, b_gate.reshape(1, hk), gain.reshape(1, hv)]
    return pl.pallas_call(
        functools.partial(_gla_block_kernel, kind=kind, layer=layer, L=L, nc=nc, H=H, DK=DK, DV=DV),
        grid=(B, nj),
        in_specs=in_specs,
        out_specs=[pl.BlockSpec((tq, hv), rmap(0)), pl.BlockSpec((1, H, DK, DV), bmap4)],
        out_shape=[jax.ShapeDtypeStruct((B * T, hv), f32), jax.ShapeDtypeStruct((B, H, DK, DV), f32)],
        scratch_shapes=[pltpu.VMEM((H, DV, DK), f32)] + [pltpu.VMEM((tq, hk), bf16)] * 4
                       + [pltpu.VMEM((nc, 1, hk), f32)],
        compiler_params=_cparams("parallel", "arbitrary"),
        name=kind + "_mix_fresh",
    )(*args)


def _rwkv_pre_kernel(x_ref, aux_ref, sh_ref, g_ref, mu_ref, wrkv_ref, la_ref, lbw_ref, lba_ref, lbg_ref,
                     w0_ref, a0_ref, r_ref, w_ref, k_ref, v_ref, alr_ref, gate_ref, *, long_seq, blocks_per_seq,
                     seq_len):
    g = g_ref[...]
    hn = _rms(x_ref[...], g)
    tm = hn.shape[0]
    rowi = lax.broadcasted_iota(jnp.int32, (tm, 1), 0)
    rolled = pltpu.roll(hn, 1, axis=0)
    if long_seq:
        p_last = _rms(aux_ref[SUBLANES - 1:SUBLANES, :], g)
        at_start = (pl.program_id(0) % blocks_per_seq) == 0
        first = jnp.where(at_start, sh_ref[0], p_last)
        prev = jnp.where(rowi == 0, first, rolled)
    else:
        prev = jnp.where(rowi % seq_len == 0, aux_ref[...], rolled)
    xx = prev - hn

    def lerp(c):
        return (hn + xx * mu_ref[c:c + 1, :]).astype(bf16)

    r_ref[...] = jnp.dot(lerp(0), wrkv_ref[0], preferred_element_type=f32)
    k_ref[...] = jnp.dot(lerp(1), wrkv_ref[1], preferred_element_type=f32)
    v_ref[...] = jnp.dot(lerp(2), wrkv_ref[2], preferred_element_type=f32)
    lw = jnp.tanh(jnp.dot(lerp(3), la_ref[:, 0:64], preferred_element_type=f32))
    wl = -_softplus(-(w0_ref[...] + jnp.dot(lw.astype(bf16), lbw_ref[...], preferred_element_type=f32))) - 0.5
    w_ref[...] = jnp.exp(-jnp.exp(wl))
    la = jnp.dot(lerp(4), la_ref[:, 64:128], preferred_element_type=f32)
    alr_ref[...] = _sigmoid(a0_ref[...] + jnp.dot(la.astype(bf16), lba_ref[...], preferred_element_type=f32))
    lg = _sigmoid(jnp.dot(lerp(5), la_ref[:, 128:256], preferred_element_type=f32))
    gate_ref[...] = jnp.dot(lg.astype(bf16), lbg_ref[...], preferred_element_type=f32)


def _rwkv_pre(x, shift0, g, mu, wrkv, la, lbw, lba, lbg, w0, a0, B, T, tm):
    n, d = x.shape
    long_seq = T % tm == 0
    row = lambda i: (i, 0)
    const2 = lambda i: (0, 0)
    if long_seq:
        bps = T // tm
        sub = tm // SUBLANES
        aux = x
        aux_spec = pl.BlockSpec((SUBLANES, d), lambda i: (jnp.maximum(i * sub - 1, 0), 0))
        sh = shift0.reshape(B, 1, d)
        sh_spec = pl.BlockSpec((1, 1, d), lambda i: (i // bps, 0, 0))
    else:
        assert tm % T == 0
        bps = 1
        aux = jnp.repeat(shift0, T, axis=0)
        aux_spec = pl.BlockSpec((tm, d), row)
        sh = shift0.reshape(B, 1, d)
        sh_spec = pl.BlockSpec((1, 1, d), lambda i: (0, 0, 0))
    out = jax.ShapeDtypeStruct((n, d), f32)
    return pl.pallas_call(
        functools.partial(_rwkv_pre_kernel, long_seq=long_seq, blocks_per_seq=bps, seq_len=T),
        grid=(n // tm,),
        in_specs=[pl.BlockSpec((tm, d), row), aux_spec, sh_spec, pl.BlockSpec((1, d), const2),
                  pl.BlockSpec((6, d), const2), pl.BlockSpec((3, d, d), lambda i: (0, 0, 0)),
                  pl.BlockSpec((d, 256), const2), pl.BlockSpec((64, d), const2), pl.BlockSpec((64, d), const2),
                  pl.BlockSpec((128, d), const2), pl.BlockSpec((1, d), const2), pl.BlockSpec((1, d), const2)],
        out_specs=[pl.BlockSpec((tm, d), row)] * 6,
        out_shape=[out] * 6,
        compiler_params=_cparams("parallel"),
        name="rwkv_pre",
    )(x, aux, sh, g.reshape(1, d), mu, wrkv, la, lbw, lba, lbg, w0.reshape(1, d), a0.reshape(1, d))


def _rwkv_scan_kernel(*refs, tt, has_init):
    N = RWKV_N
    r_ref, w_ref, k_ref, v_ref, alr_ref, kk_ref, ka_ref, rk_ref, lnw_ref, lnb_ref = refs[:10]
    if has_init:
        s0_ref = refs[10]
        rest = refs[11:]
    else:
        rest = refs[10:]
    z_ref, sout_ref, s_s, y_s = rest
    j = pl.program_id(1)

    @pl.when(j == 0)
    def _():
        if has_init:
            s_s[...] = s0_ref[...]
        else:
            s_s[...] = jnp.zeros_like(s_s)

    def step(t, carry):
        r = r_ref[t]
        w = w_ref[t]
        k = k_ref[t]
        v = v_ref[t]
        alr = alr_ref[t]
        kkraw = k * kk_ref[...]
        nrm = jnp.sqrt(jnp.sum(kkraw * kkraw, axis=0, keepdims=True))
        kk = kkraw / jnp.maximum(nrm, 1e-12)
        k2 = k * (1.0 + (alr - 1.0) * ka_ref[...])
        a = -kk
        b = kk * alr
        for vi in range(N):
            sv = s_s[vi]
            sa = jnp.sum(sv * a, axis=0, keepdims=True)
            sv = sv * w + sa * b + v[vi:vi + 1, :] * k2
            s_s[vi] = sv
            y_s[vi:vi + 1, :] = jnp.sum(sv * r, axis=0, keepdims=True)
        y = y_s[...]
        mean = jnp.mean(y, axis=0, keepdims=True)
        yc = y - mean
        var = jnp.mean(yc * yc, axis=0, keepdims=True)
        yn = yc * lax.rsqrt(var + RWKV_GN_EPS)
        bonus = jnp.sum(r * k2 * rk_ref[...], axis=0, keepdims=True) * v
        z_ref[t] = yn * lnw_ref[...] + lnb_ref[...] + bonus
        return carry

    lax.fori_loop(0, tt, step, 0)

    @pl.when(j == pl.num_programs(1) - 1)
    def _():
        sout_ref[...] = s_s[...]


def _rwkv_scan(r, w, k, v, alr, slabs, s0, tt):
    T, N, nl = r.shape
    has_init = s0 is not None
    tmap = lambda g, j: (j, 0, g)
    smap = lambda g, j: (0, 0, g)
    const2 = lambda g, j: (0, 0)
    stream = pl.BlockSpec((tt, N, LANES), tmap)
    in_specs = [stream] * 5 + [pl.BlockSpec((N, LANES), const2)] * 5
    args = [r, w, k, v, alr] + list(slabs)
    if has_init:
        in_specs.append(pl.BlockSpec((N, N, LANES), smap))
        args.append(s0)
    return pl.pallas_call(
        functools.partial(_rwkv_scan_kernel, tt=tt, has_init=has_init),
        grid=(nl // LANES, T // tt),
        in_specs=in_specs,
        out_specs=[stream, pl.BlockSpec((N, N, LANES), smap)],
        out_shape=[jax.ShapeDtypeStruct((T, N, nl), f32), jax.ShapeDtypeStruct((N, N, nl), f32)],
        scratch_shapes=[pltpu.VMEM((N, N, LANES), f32), pltpu.VMEM((N, LANES), f32)],
        compiler_params=_cparams("parallel", "arbitrary"),
        name="rwkv_scan",
    )(*args)


def _lane_slab(p):
    return jnp.tile(p.reshape(RWKV_HEADS, RWKV_N).T, (1, LANES // RWKV_HEADS))


def _rwkv_mix(x, p, B, T, state, tm):
    H, N = RWKV_HEADS, RWKV_N
    if state is None:
        shift0 = jnp.zeros((B, D_MODEL), f32)
        s0 = None
    else:
        s_in, shift0 = state
        s0 = s_in.transpose(2, 3, 0, 1).reshape(N, N, B * H)
    r, w, k, v, alr, gate = _rwkv_pre(x, shift0, p["g"], p["mu"], p["wrkv"], p["la"], p["lbw"], p["lba"], p["lbg"],
                                      p["w0"], p["a0"], B, T, tm)
    to_scan = lambda a: a.reshape(B, T, H, N).transpose(1, 3, 0, 2).reshape(T, N, B * H)
    slabs = [_lane_slab(p[n]) for n in ("k_k", "k_a", "r_k", "ln_w", "ln_b")]
    z, s = _rwkv_scan(to_scan(r), to_scan(w), to_scan(k), to_scan(v), to_scan(alr), slabs, s0, min(T, 32))
    z = z.reshape(T, N, B, H).transpose(2, 0, 3, 1).reshape(B * T, D_MODEL)
    s = s.reshape(N, N, B, H).transpose(2, 3, 0, 1)
    return z, gate, s


def _pad_cols(w, to):
    return jnp.pad(w, ((0, 0), (0, to - w.shape[1])))


def _trunk(x, B, T, states, p):
    n = x.shape[0]
    tm = min(512, n)
    tmm = min(1024, n)
    tq = min(256, T)
    fresh = states is None
    block_ok = fresh and T % tq == 0 and tq % (2 * LIN_CHUNK) == 0
    if not fresh:
        m_c, m_n, m_m, h_s, g_s, r_s, r_sh = states
    new = {}
    for li in range(DEPTH):
        g_mix = p["norm_mix"][li]
        if li == 0:
            pr = _norm_proj(x, g_mix, p["mlstm_w_in"], tmm, 1664)
            if fresh and T % tq == 0 and tq % (2 * MLSTM_CHUNK) == 0:
                y, c, nn, m = _mlstm_mix_fresh(pr, p["mlstm_b_gates"], p["mlstm_norm"], B, T, tq)
            else:
                y, c, nn, m = _mlstm_mix(pr, p["mlstm_b_gates"], p["mlstm_norm"], B, T,
                                         None if fresh else (m_c[0], m_n[0], m_m[0]), tq)
            new["C"], new["n"], new["m"] = c[None], nn[None], m[None]
            x = _out_proj(y, p["mlstm_w_out"], x, tm)
        elif li == 1:
            pr = _norm_proj(x, g_mix, p["hgrn_w_in"], tmm, 2048)
            if block_ok:
                y, s = _gla_mix_fresh("hgrn", pr, (p["hgrn_lb_logits"],), p["hgrn_norm"], B, T, tq, layer=li)
            else:
                y, s = _gla_mix("hgrn", pr, (p["hgrn_lb_logits"],), p["hgrn_norm"], B, T,
                                None if fresh else h_s[0], tq, layer=li)
            new["hS"] = s[None]
            x = _out_proj(y, p["hgrn_w_out"], x, tm)
        elif li == 2:
            pr = _norm_proj(x, g_mix, p["gla_w_in"], tmm, 1664)
            if block_ok:
                y, s = _gla_mix_fresh("gla", pr, (p["gla_w_gate_up"], p["gla_b_gate"]), p["gla_norm"], B, T, tq)
            else:
                y, s = _gla_mix("gla", pr, (p["gla_w_gate_up"], p["gla_b_gate"]), p["gla_norm"], B, T,
                                None if fresh else g_s[0], tq)
            new["gS"] = s[None]
            x = _out_proj(y, p["gla_w_out"], x, tm)
        else:
            rp = dict(p["rwkv"], g=g_mix)
            z, gate, s = _rwkv_mix(x, rp, B, T, None if fresh else (r_s[0], r_sh[0]), min(256, n))
            x_last = x.reshape(B, T, D_MODEL)[:, T - 1, :]
            new["rS"] = s[None]
            new["sh"] = _rmsnorm(x_last, g_mix, B)[None]
            x = _out_proj(z, p["rwkv_w_out"], x, tm, gate=gate)
        x = _ffn(x, p["norm_ffn"][li], p["ffn_w_gate_up"][li], p["ffn_w_down"][li], tm, 1408)
    y = _rmsnorm(x, p["norm_final"], tm)
    return y.reshape(B, T, D_MODEL), (new["C"], new["n"], new["m"], new["hS"], new["gS"], new["rS"], new["sh"])


def kernel(x_prompt, x_sample, state_mlstm_C, state_mlstm_n, state_mlstm_m, state_hgrn_S, state_gla_S, state_rwkv_S, state_rwkv_shift, norm_mix, norm_ffn, norm_final, mlstm_w_in, mlstm_b_gates, mlstm_norm, mlstm_w_out, hgrn_w_in, hgrn_lb_logits, hgrn_norm, hgrn_w_out, gla_w_in, gla_w_gate_up, gla_b_gate, gla_norm, gla_w_out, rwkv_mu, rwkv_w_rkv, rwkv_w_lora_a, rwkv_w_lora_b, rwkv_w0, rwkv_a_lora_a, rwkv_a_lora_b, rwkv_a0, rwkv_g_lora_a, rwkv_g_lora_b, rwkv_k_k, rwkv_k_a, rwkv_r_k, rwkv_ln_w, rwkv_ln_b, rwkv_w_out, ffn_w_gate_up, ffn_w_down):
    cast = lambda w: w.astype(bf16)
    p = dict(
        norm_mix=norm_mix, norm_ffn=norm_ffn, norm_final=norm_final,
        mlstm_w_in=cast(_pad_cols(mlstm_w_in[0], 3328)), mlstm_b_gates=mlstm_b_gates[0], mlstm_norm=mlstm_norm[0],
        mlstm_w_out=cast(mlstm_w_out[0]),
        hgrn_w_in=cast(hgrn_w_in[0]), hgrn_lb_logits=hgrn_lb_logits, hgrn_norm=hgrn_norm[0],
        hgrn_w_out=cast(hgrn_w_out[0]),
        gla_w_in=cast(_pad_cols(gla_w_in[0], 3328)), gla_w_gate_up=gla_w_gate_up[0], gla_b_gate=gla_b_gate[0],
        gla_norm=gla_norm[0], gla_w_out=cast(gla_w_out[0]),
        rwkv=dict(mu=rwkv_mu[0], wrkv=cast(rwkv_w_rkv[0]),
                  la=cast(jnp.concatenate([rwkv_w_lora_a[0], rwkv_a_lora_a[0], rwkv_g_lora_a[0]], axis=1)),
                  lbw=cast(rwkv_w_lora_b[0]), lba=cast(rwkv_a_lora_b[0]), lbg=cast(rwkv_g_lora_b[0]),
                  w0=rwkv_w0[0], a0=rwkv_a0[0], k_k=rwkv_k_k[0], k_a=rwkv_k_a[0], r_k=rwkv_r_k[0].reshape(-1),
                  ln_w=rwkv_ln_w[0], ln_b=rwkv_ln_b[0]),
        rwkv_w_out=cast(rwkv_w_out[0]),
        ffn_w_gate_up=cast(ffn_w_gate_up), ffn_w_down=cast(ffn_w_down),
    )
    bp, tp, _ = x_prompt.shape
    bs, ts, _ = x_sample.shape
    y_p, st_p = _trunk(x_prompt.reshape(bp * tp, D_MODEL), bp, tp, None, p)
    y_s, st_s = _trunk(x_sample.reshape(bs * ts, D_MODEL), bs, ts,
                       (state_mlstm_C, state_mlstm_n, state_mlstm_m, state_hgrn_S, state_gla_S, state_rwkv_S,
                        state_rwkv_shift), p)
    return (y_p, y_s) + st_p + st_s
```

```python
import functools

import jax
import jax.numpy as jnp
from jax import lax
from jax.experimental import pallas as pl
from jax.experimental.pallas import tpu as pltpu

f32 = jnp.float32
bf16 = jnp.bfloat16

D_MODEL = 1024
DEPTH = 4
NORM_EPS = 1e-6

MLSTM_HEADS, MLSTM_DK, MLSTM_DV, MLSTM_CHUNK = 4, 128, 256, 64
HGRN_HEADS, HGRN_DK, HGRN_DV = 8, 128, 128
GLA_HEADS, GLA_DK, GLA_DV = 4, 128, 256
GLA_GATE_RANK = 16
GLA_GATE_NORMALIZER = 16.0
LIN_CHUNK = 32
RWKV_HEADS, RWKV_N = 16, 64
RWKV_GN_EPS = 64e-5
FFN_HIDDEN = 2816

LANES = 128
SUBLANES = 8
VMEM_LIMIT_BYTES = 52 * 1024 * 1024


def _cparams(*sem):
    return pltpu.CompilerParams(dimension_semantics=sem, vmem_limit_bytes=VMEM_LIMIT_BYTES)


def _chunk_len(t, cap):
    return max(d for d in range(1, min(cap, t) + 1) if t % d == 0)


def _rms(x, g):
    ms = jnp.mean(x * x, axis=-1, keepdims=True)
    return x * lax.rsqrt(ms + NORM_EPS) * g


def _sigmoid(x):
    return jax.nn.sigmoid(x)


def _softplus(x):
    return jnp.maximum(x, 0.0) + jnp.log1p(jnp.exp(-jnp.abs(x)))


def _log_sigmoid(x):
    return -_softplus(-x)


def _mm(a, b, mx):
    if mx == bf16:
        return jnp.dot(a.astype(bf16), b.astype(bf16), preferred_element_type=f32)
    return jnp.dot(a, b, preferred_element_type=f32, precision=lax.Precision.HIGHEST)


def _mm_nt(a, b, mx):
    dn = (((1,), (1,)), ((), ()))
    if mx == bf16:
        return lax.dot_general(a.astype(bf16), b.astype(bf16), dn, preferred_element_type=f32)
    return lax.dot_general(a, b, dn, preferred_element_type=f32, precision=lax.Precision.HIGHEST)


def _mm_tn(a, b, mx):
    dn = (((0,), (0,)), ((), ()))
    if mx == bf16:
        return lax.dot_general(a.astype(bf16), b.astype(bf16), dn, preferred_element_type=f32)
    rows = a.shape[0]
    if rows < LANES:
        a = jnp.concatenate([a, jnp.zeros((LANES - rows, a.shape[1]), f32)], axis=0)
        b = jnp.concatenate([b, jnp.zeros((LANES - rows, b.shape[1]), f32)], axis=0)
    return lax.dot_general(a, b, dn, preferred_element_type=f32, precision=lax.Precision.HIGHEST)


def _cumsum_rows(x, tril_b, causal_rows):
    L = x.shape[0]
    if L <= SUBLANES:
        acc = jnp.zeros_like(x)
        rowi = lax.broadcasted_iota(jnp.int32, x.shape, 0)
        for j in range(L):
            acc = acc + jnp.where(rowi >= j, x[j:j + 1, :], 0.0)
        return acc
    hi = x.astype(bf16)
    r1 = x - hi.astype(f32)
    mid = r1.astype(bf16)
    lo = (r1 - mid.astype(f32)).astype(bf16)
    return (jnp.dot(tril_b, hi, preferred_element_type=f32)
            + jnp.dot(tril_b, mid, preferred_element_type=f32)
            + jnp.dot(tril_b, lo, preferred_element_type=f32))


def _norm_proj_kernel(x_ref, g_ref, w_ref, o_ref, hn_ref):
    @pl.when(pl.program_id(1) == 0)
    def _():
        hn_ref[...] = _rms(x_ref[...], g_ref[...]).astype(bf16)

    o_ref[...] = jnp.dot(hn_ref[...], w_ref[...], preferred_element_type=f32)


def _norm_proj(x, g, w, tm, tn):
    n, d = x.shape
    e = w.shape[1]
    return pl.pallas_call(
        _norm_proj_kernel,
        grid=(n // tm, e // tn),
        in_specs=[pl.BlockSpec((tm, d), lambda i, j: (i, 0)),
                  pl.BlockSpec((1, d), lambda i, j: (0, 0)),
                  pl.BlockSpec((d, tn), lambda i, j: (0, j))],
        out_specs=pl.BlockSpec((tm, tn), lambda i, j: (i, j)),
        out_shape=jax.ShapeDtypeStruct((n, e), f32),
        scratch_shapes=[pltpu.VMEM((tm, d), bf16)],
        compiler_params=_cparams("parallel", "arbitrary"),
        name="norm_proj",
    )(x, g.reshape(1, d), w)


def _out_proj_kernel(*refs, gated):
    if gated:
        y_ref, gate_ref, w_ref, res_ref, o_ref = refs
        y = y_ref[...] * gate_ref[...]
    else:
        y_ref, w_ref, res_ref, o_ref = refs
        y = y_ref[...]
    o_ref[...] = res_ref[...] + jnp.dot(y.astype(bf16), w_ref[...], preferred_element_type=f32)


def _out_proj(y, w, res, tm, gate=None):
    n, e = y.shape
    d = w.shape[1]
    row = lambda i: (i, 0)
    args = [y] + ([gate] if gate is not None else []) + [w, res]
    in_specs = ([pl.BlockSpec((tm, e), row)] + ([pl.BlockSpec((tm, e), row)] if gate is not None else [])
                + [pl.BlockSpec((e, d), lambda i: (0, 0)), pl.BlockSpec((tm, d), row)])
    return pl.pallas_call(
        functools.partial(_out_proj_kernel, gated=gate is not None),
        grid=(n // tm,),
        in_specs=in_specs,
        out_specs=pl.BlockSpec((tm, d), row),
        out_shape=jax.ShapeDtypeStruct((n, d), f32),
        compiler_params=_cparams("parallel"),
        name="out_proj",
    )(*args)


def _ffn_kernel(x_ref, g_ref, wg_ref, wu_ref, wd_ref, o_ref, hn_ref, acc_ref):
    j = pl.program_id(1)

    @pl.when(j == 0)
    def _():
        hn_ref[...] = _rms(x_ref[...], g_ref[...]).astype(bf16)
        acc_ref[...] = jnp.zeros_like(acc_ref)

    h = hn_ref[...]
    gt = jnp.dot(h, wg_ref[...], preferred_element_type=f32)
    up = jnp.dot(h, wu_ref[...], preferred_element_type=f32)
    act = (gt * _sigmoid(gt) * up).astype(bf16)
    acc_ref[...] += jnp.dot(act, wd_ref[...], preferred_element_type=f32)

    @pl.when(j == pl.num_programs(1) - 1)
    def _():
        o_ref[...] = x_ref[...] + acc_ref[...]


def _ffn(x, g, w_gu, w_down, tm, tf):
    n, d = x.shape
    nf = FFN_HIDDEN // tf
    return pl.pallas_call(
        _ffn_kernel,
        grid=(n // tm, nf),
        in_specs=[pl.BlockSpec((tm, d), lambda i, j: (i, 0)),
                  pl.BlockSpec((1, d), lambda i, j: (0, 0)),
                  pl.BlockSpec((d, tf), lambda i, j: (0, j)),
                  pl.BlockSpec((d, tf), lambda i, j: (0, j + nf)),
                  pl.BlockSpec((tf, d), lambda i, j: (j, 0))],
        out_specs=pl.BlockSpec((tm, d), lambda i, j: (i, 0)),
        out_shape=jax.ShapeDtypeStruct((n, d), f32),
        scratch_shapes=[pltpu.VMEM((tm, d), bf16), pltpu.VMEM((tm, d), f32)],
        compiler_params=_cparams("parallel", "arbitrary"),
        name="ffn",
    )(x, g.reshape(1, d), w_gu, w_gu, w_down)


def _rmsnorm_kernel(x_ref, g_ref, o_ref):
    o_ref[...] = _rms(x_ref[...], g_ref[...])


def _rmsnorm(x, g, tm):
    n, d = x.shape
    return pl.pallas_call(
        _rmsnorm_kernel,
        grid=(n // tm,),
        in_specs=[pl.BlockSpec((tm, d), lambda i: (i, 0)), pl.BlockSpec((1, d), lambda i: (0, 0))],
        out_specs=pl.BlockSpec((tm, d), lambda i: (i, 0)),
        out_shape=jax.ShapeDtypeStruct((n, d), f32),
        compiler_params=_cparams("parallel"),
        name="rmsnorm",
    )(x, g.reshape(1, d))


def _mlstm_kernel(*refs, L, nchunks, has_init, mx):
    H, DK, DV = MLSTM_HEADS, MLSTM_DK, MLSTM_DV
    q_ref, k_ref, v_ref, o_ref, gt_ref, bias_ref, gain_ref = refs[:7]
    if has_init:
        c0_ref, n0_ref, m0_ref = refs[7:10]
        rest = refs[10:]
    else:
        rest = refs[7:]
    y_ref, c_ref, n_ref, m_ref, c_s, n_s, m_s = rest
    j = pl.program_id(1)

    @pl.when(j == 0)
    def _():
        if has_init:
            c_s[...] = c0_ref[0]
            n_s[...] = n0_ref[0]
            m_s[...] = m0_ref[0]
        else:
            c_s[...] = jnp.zeros_like(c_s)
            n_s[...] = jnp.zeros_like(n_s)
            m_s[...] = jnp.zeros_like(m_s)

    rowi = lax.broadcasted_iota(jnp.int32, (L, L), 0)
    coli = lax.broadcasted_iota(jnp.int32, (L, L), 1)
    causal = coli <= rowi
    eye = coli == rowi
    scale = DK ** -0.5

    def to_row(col):
        return jnp.sum(jnp.where(eye, col, 0.0), axis=0, keepdims=True)

    def chunk(c, carry):
        rows = pl.ds(pl.multiple_of(c * L, L), L)
        gts = gt_ref[rows, :] + bias_ref[...]
        lsg = _log_sigmoid(gts)
        for h in range(H):
            i_col = gts[:, h:h + 1]
            f_col = lsg[:, H + h:H + h + 1]
            f_row = to_row(f_col)
            fc_col = jnp.sum(jnp.where(causal, f_row, 0.0), axis=1, keepdims=True)
            fc_row = to_row(fc_col)
            i_row = to_row(i_col)
            m = m_s[h]
            dm = jnp.where(causal, fc_col - fc_row + i_row, -jnp.inf)
            g = fc_col + m
            mt = jnp.maximum(g, jnp.max(dm, axis=1, keepdims=True))
            p = jnp.exp(dm - mt)
            qc = q_ref[rows, h * DK:(h + 1) * DK] * scale
            kc = k_ref[rows, h * DK:(h + 1) * DK]
            vc = v_ref[rows, h * DV:(h + 1) * DV]
            s = _mm_nt(qc, kc, mx) * p
            eg = jnp.exp(g - mt)
            cst = c_s[h]
            nst = n_s[h]
            num = eg * _mm_nt(qc, cst, mx) + _mm(s, vc, mx)
            den = eg * jnp.sum(qc * nst, axis=1, keepdims=True) + jnp.sum(s, axis=1, keepdims=True)
            hh = num / jnp.maximum(jnp.abs(den), jnp.exp(-mt))
            m_new = mt[L - 1:L, :]
            f_last = fc_col[L - 1:L, :]
            w_c = jnp.exp(f_last + m - m_new)
            w_j = jnp.exp(f_last - fc_col + i_col - m_new)
            c_s[h] = w_c * cst + _mm_tn(vc * w_j, kc, mx)
            n_s[h] = w_c * nst + jnp.sum(w_j * kc, axis=0, keepdims=True)
            m_s[h] = m_new
            yn = hh * lax.rsqrt(jnp.mean(hh * hh, axis=1, keepdims=True) + NORM_EPS)
            yn = yn * gain_ref[:, h * DV:(h + 1) * DV]
            y_ref[rows, h * DV:(h + 1) * DV] = _sigmoid(o_ref[rows, h * DV:(h + 1) * DV]) * yn
        return carry

    lax.fori_loop(0, nchunks, chunk, 0)

    @pl.when(j == pl.num_programs(1) - 1)
    def _():
        c_ref[0] = c_s[...]
        n_ref[0] = n_s[...]
        m_ref[0] = m_s[...]


def _mlstm_mix(pr, b_gates, gain, B, T, state, tq):
    H, DK, DV = MLSTM_HEADS, MLSTM_DK, MLSTM_DV
    L = _chunk_len(T, MLSTM_CHUNK)
    nj = T // tq
    has_init = state is not None
    mx = f32 if L < 16 else bf16
    rmap = lambda c: (lambda b, j: (b * nj + j, c))
    bmap4 = lambda b, j: (b, 0, 0, 0)
    bias = jnp.zeros((1, LANES), f32).at[0, :2 * H].set(b_gates)
    in_specs = [pl.BlockSpec((tq, H * DK), rmap(0)), pl.BlockSpec((tq, H * DK), rmap(1)),
                pl.BlockSpec((tq, H * DV), rmap(1)), pl.BlockSpec((tq, H * DV), rmap(2)),
                pl.BlockSpec((tq, LANES), rmap((2 * H * DK + 2 * H * DV) // LANES)),
                pl.BlockSpec((1, LANES), lambda b, j: (0, 0)),
                pl.BlockSpec((1, H * DV), lambda b, j: (0, 0))]
    args = [pr, pr, pr, pr, pr, bias, gain.reshape(1, H * DV)]
    if has_init:
        c0, n0, m0 = state
        in_specs += [pl.BlockSpec((1, H, DV, DK), bmap4), pl.BlockSpec((1, H, 1, DK), bmap4),
                     pl.BlockSpec((1, H, 1, 1), bmap4)]
        args += [c0, n0.reshape(B, H, 1, DK), m0.reshape(B, H, 1, 1)]
    y, c, n, m = pl.pallas_call(
        functools.partial(_mlstm_kernel, L=L, nchunks=tq // L, has_init=has_init, mx=mx),
        grid=(B, nj),
        in_specs=in_specs,
        out_specs=[pl.BlockSpec((tq, H * DV), rmap(0)), pl.BlockSpec((1, H, DV, DK), bmap4),
                   pl.BlockSpec((1, H, 1, DK), bmap4), pl.BlockSpec((1, H, 1, 1), bmap4)],
        out_shape=[jax.ShapeDtypeStruct((B * T, H * DV), f32), jax.ShapeDtypeStruct((B, H, DV, DK), f32),
                   jax.ShapeDtypeStruct((B, H, 1, DK), f32), jax.ShapeDtypeStruct((B, H, 1, 1), f32)],
        scratch_shapes=[pltpu.VMEM((H, DV, DK), f32), pltpu.VMEM((H, 1, DK), f32), pltpu.VMEM((H, 1, 1), f32)],
        compiler_params=_cparams("parallel", "arbitrary"),
        name="mlstm_mix",
    )(*args)
    return y, c, n.reshape(B, H, DK), m.reshape(B, H)


def _mlstm_block_kernel(q_ref, k_ref, v_ref, o_ref, gt_ref, bias_ref, gain_ref, y_ref, c_ref, n_ref, m_ref,
                        c_s, n_s, m_s, *, L, nc):
    H, DK, DV = MLSTM_HEADS, MLSTM_DK, MLSTM_DV
    P = 2 * L
    j = pl.program_id(1)

    @pl.when(j == 0)
    def _():
        c_s[...] = jnp.zeros_like(c_s)
        n_s[...] = jnp.zeros_like(n_s)
        m_s[...] = jnp.zeros_like(m_s)

    rowi = lax.broadcasted_iota(jnp.int32, (L, L), 0)
    coli = lax.broadcasted_iota(jnp.int32, (L, L), 1)
    causal = coli <= rowi
    r2 = lax.broadcasted_iota(jnp.int32, (P, P), 0)
    c2 = lax.broadcasted_iota(jnp.int32, (P, P), 1)
    pair_tril = ((c2 <= r2) & ((c2 >= L) == (r2 >= L))).astype(bf16)
    scale = DK ** -0.5

    gates = []
    for pp in range(nc // 2):
        g = gt_ref[pp * P:(pp + 1) * P, :] + bias_ref[...]
        hi, mid, lo = _split3(_log_sigmoid(g))
        fcum = (jnp.dot(pair_tril, hi, preferred_element_type=f32)
                + jnp.dot(pair_tril, mid, preferred_element_type=f32)
                + jnp.dot(pair_tril, lo, preferred_element_type=f32))
        g_t = g.T
        f_t = fcum.T
        for half in range(2):
            rs = slice(half * L, (half + 1) * L)
            gates.append((g[rs, :], fcum[rs, :], g_t[:, rs], f_t[:, rs]))

    pairs = [(c, h) for c in range(nc) for h in range(H)]
    rows_of = lambda c: slice(c * L, (c + 1) * L)
    ks_of = lambda h: slice(h * DK, (h + 1) * DK)
    vs_of = lambda h: slice(h * DV, (h + 1) * DV)
    dms, rmax, fcols, icols, qks = {}, {}, {}, {}, {}
    for c, h in pairs:
        g_c, f_c, g_r, f_r = gates[c]
        icols[c, h] = g_c[:, h:h + 1]
        fcols[c, h] = f_c[:, H + h:H + h + 1]
        dm = jnp.where(causal, fcols[c, h] - f_r[H + h:H + h + 1, :] + g_r[h:h + 1, :], -jnp.inf)
        dms[c, h] = dm
        rmax[c, h] = jnp.max(dm, axis=1, keepdims=True)
        qb = (q_ref[rows_of(c), ks_of(h)] * scale).astype(bf16)
        qks[c, h] = _mm_nt(qb, k_ref[rows_of(c), ks_of(h)], bf16)
    m_in, m_out = {}, {}
    for h in range(H):
        m = m_s[h]
        for c in range(nc):
            m_in[c, h] = m
            m = jnp.maximum(fcols[c, h][L - 1:L, :] + m, rmax[c, h][L - 1:L, :])
            m_out[c, h] = m
        m_s[h] = m
    mts, egs, ss, ssums, wcs, dcs, dns = {}, {}, {}, {}, {}, {}, {}
    for c, h in pairs:
        fc_col = fcols[c, h]
        gg = fc_col + m_in[c, h]
        mt = jnp.maximum(gg, rmax[c, h])
        mts[c, h] = mt
        egs[c, h] = jnp.exp(gg - mt)
        s = qks[c, h] * jnp.exp(dms[c, h] - mt)
        ssums[c, h] = jnp.sum(s, axis=1, keepdims=True)
        ss[c, h] = s.astype(bf16)
        f_last = fc_col[L - 1:L, :]
        wcs[c, h] = jnp.exp(f_last + m_in[c, h] - m_out[c, h])
        w_j = jnp.exp(f_last - fc_col + icols[c, h] - m_out[c, h])
        kc = k_ref[rows_of(c), ks_of(h)]
        dcs[c, h] = _mm_tn(v_ref[rows_of(c), vs_of(h)] * w_j, kc, bf16)
        dns[c, h] = jnp.sum(w_j * kc, axis=0, keepdims=True)
    c_in, n_in = {}, {}
    for h in range(H):
        cst = c_s[h]
        nst = n_s[h]
        for c in range(nc):
            c_in[c, h] = cst.astype(bf16)
            n_in[c, h] = nst
            cst = wcs[c, h] * cst + dcs[c, h]
            nst = wcs[c, h] * nst + dns[c, h]
        c_s[h] = cst
        n_s[h] = nst
    hhs = {}
    for c, h in pairs:
        rows, ks, vs = rows_of(c), ks_of(h), vs_of(h)
        qc = q_ref[rows, ks] * scale
        num = egs[c, h] * _mm_nt(qc, c_in[c, h], bf16) + _mm(ss[c, h], v_ref[rows, vs], bf16)
        den = egs[c, h] * jnp.sum(qc * n_in[c, h], axis=1, keepdims=True) + ssums[c, h]
        hhs[c, h] = num / jnp.maximum(jnp.abs(den), jnp.exp(-mts[c, h]))
    for c, h in pairs:
        rows, vs = rows_of(c), vs_of(h)
        hh = hhs[c, h]
        yn = hh * lax.rsqrt(jnp.mean(hh * hh, axis=1, keepdims=True) + NORM_EPS) * gain_ref[:, vs]
        y_ref[rows, vs] = _sigmoid(o_ref[rows, vs]) * yn

    @pl.when(j == pl.num_programs(1) - 1)
    def _():
        c_ref[0] = c_s[...]
        n_ref[0] = n_s[...]
        m_ref[0] = m_s[...]


def _mlstm_mix_fresh(pr, b_gates, gain, B, T, tq):
    H, DK, DV = MLSTM_HEADS, MLSTM_DK, MLSTM_DV
    L = MLSTM_CHUNK
    nj = T // tq
    rmap = lambda c: (lambda b, j: (b * nj + j, c))
    bmap4 = lambda b, j: (b, 0, 0, 0)
    bias = jnp.zeros((1, LANES), f32).at[0, :2 * H].set(b_gates)
    y, c, n, m = pl.pallas_call(
        functools.partial(_mlstm_block_kernel, L=L, nc=tq // L),
        grid=(B, nj),
        in_specs=[pl.BlockSpec((tq, H * DK), rmap(0)), pl.BlockSpec((tq, H * DK), rmap(1)),
                  pl.BlockSpec((tq, H * DV), rmap(1)), pl.BlockSpec((tq, H * DV), rmap(2)),
                  pl.BlockSpec((tq, LANES), rmap((2 * H * DK + 2 * H * DV) // LANES)),
                  pl.BlockSpec((1, LANES), lambda b, j: (0, 0)),
                  pl.BlockSpec((1, H * DV), lambda b, j: (0, 0))],
        out_specs=[pl.BlockSpec((tq, H * DV), rmap(0)), pl.BlockSpec((1, H, DV, DK), bmap4),
                   pl.BlockSpec((1, H, 1, DK), bmap4), pl.BlockSpec((1, H, 1, 1), bmap4)],
        out_shape=[jax.ShapeDtypeStruct((B * T, H * DV), f32), jax.ShapeDtypeStruct((B, H, DV, DK), f32),
                   jax.ShapeDtypeStruct((B, H, 1, DK), f32), jax.ShapeDtypeStruct((B, H, 1, 1), f32)],
        scratch_shapes=[pltpu.VMEM((H, DV, DK), f32), pltpu.VMEM((H, 1, DK), f32), pltpu.VMEM((H, 1, 1), f32)],
        compiler_params=_cparams("parallel", "arbitrary"),
        name="mlstm_mix_fresh",
    )(pr, pr, pr, pr, pr, bias, gain.reshape(1, H * DV))
    return y, c, n.reshape(B, H, DK), m.reshape(B, H)


def _gla_kernel(*refs, kind, layer, L, nchunks, H, DK, DV, has_init, mx):
    if kind == "hgrn":
        q_ref, f_ref, v_ref, g_ref, lbl_ref, gain_ref = refs[:6]
        rest = refs[6:]
    else:
        q_ref, k_ref, v_ref, g_ref, gr_ref, wup_ref, bgate_ref, gain_ref = refs[:8]
        rest = refs[8:]
    if has_init:
        s0_ref = rest[0]
        rest = rest[1:]
    y_ref, s_ref, st_s = rest
    j = pl.program_id(1)

    @pl.when(j == 0)
    def _():
        for h in range(H):
            if has_init:
                st_s[h] = s0_ref[0, h].T
            else:
                st_s[h] = jnp.zeros((DV, DK), f32)

    rowi = lax.broadcasted_iota(jnp.int32, (L, L), 0)
    coli = lax.broadcasted_iota(jnp.int32, (L, L), 1)
    causal = coli <= rowi
    tril_b = causal.astype(bf16)

    if kind == "hgrn":
        logits = lbl_ref[...]
        e = jnp.exp(logits - jnp.max(logits, axis=0, keepdims=True))
        sm = e / jnp.sum(e, axis=0, keepdims=True)
        lb_all = jnp.zeros((1, H * DK), f32)
        for li in range(layer):
            lb_all = lb_all + sm[li:li + 1, :]
    else:
        scale = DK ** -0.5

    def chunk(c, carry):
        rows = pl.ds(pl.multiple_of(c * L, L), L)
        if kind == "gla":
            gk = _mm(gr_ref[rows, :], wup_ref[...], mx) + bgate_ref[...]
            ld_all = _log_sigmoid(gk) * (1.0 / GLA_GATE_NORMALIZER)
        for h in range(H):
            ks = slice(h * DK, (h + 1) * DK)
            vs = slice(h * DV, (h + 1) * DV)
            if kind == "hgrn":
                qraw = q_ref[rows, ks]
                fg = f_ref[rows, ks]
                lb = lb_all[:, ks]
                gc = jnp.log(lb + (1.0 - lb) * _sigmoid(fg))
                kc = (1.0 - lb) * _sigmoid(-fg)
                qc = qraw * _sigmoid(qraw)
            else:
                qc = q_ref[rows, ks] * scale
                kc = k_ref[rows, ks]
                gc = ld_all[:, ks]
            vc = v_ref[rows, vs]
            bc = _cumsum_rows(gc, tril_b, causal)
            qe = qc * jnp.exp(bc)
            ke = kc * jnp.exp(-bc)
            a = jnp.where(causal, _mm_nt(qe, ke, mx), 0.0)
            st = st_s[h]
            o = _mm_nt(qe, st, mx) + _mm(a, vc, mx)
            b_last = bc[L - 1:L, :]
            kdec = kc * jnp.exp(b_last - bc)
            st_s[h] = st * jnp.exp(b_last) + _mm_tn(vc, kdec, mx)
            yn = o * lax.rsqrt(jnp.mean(o * o, axis=1, keepdims=True) + NORM_EPS) * gain_ref[:, vs]
            gg = g_ref[rows, vs]
            y_ref[rows, vs] = yn * (gg * _sigmoid(gg))
        return carry

    lax.fori_loop(0, nchunks, chunk, 0)

    @pl.when(j == pl.num_programs(1) - 1)
    def _():
        for h in range(H):
            s_ref[0, h] = st_s[h].T


def _gla_mix(kind, pr, extra, gain, B, T, state, tq, layer=0):
    if kind == "hgrn":
        H, DK, DV = HGRN_HEADS, HGRN_DK, HGRN_DV
    else:
        H, DK, DV = GLA_HEADS, GLA_DK, GLA_DV
    L = _chunk_len(T, LIN_CHUNK)
    nj = T // tq
    has_init = state is not None
    mx = f32 if L < 16 else bf16
    rmap = lambda c: (lambda b, j: (b * nj + j, c))
    bmap4 = lambda b, j: (b, 0, 0, 0)
    const2 = lambda b, j: (0, 0)
    hk, hv = H * DK, H * DV
    if kind == "hgrn":
        (lb_logits,) = extra
        in_specs = [pl.BlockSpec((tq, hk), rmap(0)), pl.BlockSpec((tq, hk), rmap(1)),
                    pl.BlockSpec((tq, hv), rmap(2)), pl.BlockSpec((tq, hv), rmap(3)),
                    pl.BlockSpec((DEPTH, hk), const2), pl.BlockSpec((1, hv), const2)]
        args = [pr, pr, pr, pr, lb_logits, gain.reshape(1, hv)]
    else:
        w_up, b_gate = extra
        w_up_p = jnp.zeros((LANES, hk), f32).at[:GLA_GATE_RANK].set(w_up)
        in_specs = [pl.BlockSpec((tq, hk), rmap(0)), pl.BlockSpec((tq, hk), rmap(1)),
                    pl.BlockSpec((tq, hv), rmap(1)), pl.BlockSpec((tq, hv), rmap(2)),
                    pl.BlockSpec((tq, LANES), rmap((2 * hk + 2 * hv) // LANES)),
                    pl.BlockSpec((LANES, hk), const2), pl.BlockSpec((1, hk), const2),
                    pl.BlockSpec((1, hv), const2)]
        args = [pr, pr, pr, pr, pr, w_up_p, b_gate.reshape(1, hk), gain.reshape(1, hv)]
    if has_init:
        in_specs.append(pl.BlockSpec((1, H, DK, DV), bmap4))
        args.append(state)
    y, s = pl.pallas_call(
        functools.partial(_gla_kernel, kind=kind, layer=layer, L=L, nchunks=tq // L, H=H, DK=DK, DV=DV,
                          has_init=has_init, mx=mx),
        grid=(B, nj),
        in_specs=in_specs,
        out_specs=[pl.BlockSpec((tq, hv), rmap(0)), pl.BlockSpec((1, H, DK, DV), bmap4)],
        out_shape=[jax.ShapeDtypeStruct((B * T, hv), f32), jax.ShapeDtypeStruct((B, H, DK, DV), f32)],
        scratch_shapes=[pltpu.VMEM((H, DV, DK), f32)],
        compiler_params=_cparams("parallel", "arbitrary"),
        name=kind + "_mix",
    )(*args)
    return y, s


def _split3(x):
    hi = x.astype(bf16)
    r1 = x - hi.astype(f32)
    mid = r1.astype(bf16)
    lo = (r1 - mid.astype(f32)).astype(bf16)
    return hi, mid, lo


def _gla_block_kernel(*refs, kind, layer, L, nc, H, DK, DV):
    if kind == "hgrn":
        q_ref, f_ref, v_ref, g_ref, lbl_ref, gain_ref = refs[:6]
        rest = refs[6:]
    else:
        q_ref, k_ref, v_ref, g_ref, gr_ref, wup_ref, bgate_ref, gain_ref = refs[:8]
        rest = refs[8:]
    y_ref, s_ref, st_s, qe_s, ke_s, qs_s, kd_s, el_s = rest
    j = pl.program_id(1)

    @pl.when(j == 0)
    def _():
        st_s[...] = jnp.zeros_like(st_s)

    rowi = lax.broadcasted_iota(jnp.int32, (L, L), 0)
    coli = lax.broadcasted_iota(jnp.int32, (L, L), 1)
    causal = coli <= rowi
    tril_b = causal.astype(bf16)

    if kind == "hgrn":
        logits = lbl_ref[...]
        e = jnp.exp(logits - jnp.max(logits, axis=0, keepdims=True))
        sm = e / jnp.sum(e, axis=0, keepdims=True)
        lb = jnp.zeros((1, H * DK), f32)
        for li in range(layer):
            lb = lb + sm[li:li + 1, :]
    else:
        scale = DK ** -0.5

    for c in range(nc):
        rows = slice(c * L, (c + 1) * L)
        if kind == "hgrn":
            qraw = q_ref[rows, :]
            fg = f_ref[rows, :]
            gc = jnp.log(lb + (1.0 - lb) * _sigmoid(fg))
            kc = (1.0 - lb) * _sigmoid(-fg)
            qc = qraw * _sigmoid(qraw)
        else:
            gk = jnp.dot(gr_ref[rows, :].astype(bf16), wup_ref[...].astype(bf16),
                         preferred_element_type=f32) + bgate_ref[...]
            gc = _log_sigmoid(gk) * (1.0 / GLA_GATE_NORMALIZER)
            qc = q_ref[rows, :] * scale
            kc = k_ref[rows, :]
        hi, mid, lo = _split3(gc)
        bc = (jnp.dot(tril_b, hi, preferred_element_type=f32) + jnp.dot(tril_b, mid, preferred_element_type=f32)
              + jnp.dot(tril_b, lo, preferred_element_type=f32))
        b_mid = bc[L // 2 - 1:L // 2, :]
        b_last = bc[L - 1:L, :]
        qe_s[rows, :] = (qc * jnp.exp(bc - b_mid)).astype(bf16)
        ke_s[rows, :] = (kc * jnp.exp(b_mid - bc)).astype(bf16)
        qs_s[rows, :] = (qc * jnp.exp(bc)).astype(bf16)
        kd_s[rows, :] = (kc * jnp.exp(b_last - bc)).astype(bf16)
        el_s[c] = jnp.exp(b_last)

    dn_nt = (((1,), (1,)), ((), ()))
    dn_tn = (((0,), (0,)), ((), ()))
    pairs = [(c, h) for c in range(nc) for h in range(H)]
    rows_of = lambda c: slice(c * L, (c + 1) * L)
    ks_of = lambda h: slice(h * DK, (h + 1) * DK)
    vs_of = lambda h: slice(h * DV, (h + 1) * DV)
    vbs, amats, dsts = {}, {}, {}
    for c, h in pairs:
        rows, ks = rows_of(c), ks_of(h)
        vb = v_ref[rows, vs_of(h)].astype(bf16)
        a = lax.dot_general(qe_s[rows, ks], ke_s[rows, ks], dn_nt, preferred_element_type=f32)
        vbs[c, h] = vb
        amats[c, h] = jnp.where(causal, a, 0.0).astype(bf16)
        dsts[c, h] = lax.dot_general(vb, kd_s[rows, ks], dn_tn, preferred_element_type=f32)
    sts = {}
    for h in range(H):
        st = st_s[h]
        for c in range(nc):
            sts[c, h] = st.astype(bf16)
            st = st * el_s[c][:, ks_of(h)] + dsts[c, h]
        st_s[h] = st
    outs = {}
    for c, h in pairs:
        rows, ks = rows_of(c), ks_of(h)
        outs[c, h] = (lax.dot_general(qs_s[rows, ks], sts[c, h], dn_nt, preferred_element_type=f32)
                      + jnp.dot(amats[c, h], vbs[c, h], preferred_element_type=f32))
    for c, h in pairs:
        rows, vs = rows_of(c), vs_of(h)
        o = outs[c, h]
        yn = o * lax.rsqrt(jnp.mean(o * o, axis=1, keepdims=True) + NORM_EPS) * gain_ref[:, vs]
        gg = g_ref[rows, vs]
        y_ref[rows, vs] = yn * (gg * _sigmoid(gg))

    @pl.when(j == pl.num_programs(1) - 1)
    def _():
        for h in range(H):
            s_ref[0, h] = st_s[h].T


def _gla_mix_fresh(kind, pr, extra, gain, B, T, tq, layer=0):
    if kind == "hgrn":
        H, DK, DV = HGRN_HEADS, HGRN_DK, HGRN_DV
    else:
        H, DK, DV = GLA_HEADS, GLA_DK, GLA_DV
    L = 2 * LIN_CHUNK
    nc = tq // L
    nj = T // tq
    rmap = lambda c: (lambda b, j: (b * nj + j, c))
    bmap4 = lambda b, j: (b, 0, 0, 0)
    const2 = lambda b, j: (0, 0)
    hk, hv = H * DK, H * DV
    if kind == "hgrn":
        (lb_logits,) = extra
        in_specs = [pl.BlockSpec((tq, hk), rmap(0)), pl.BlockSpec((tq, hk), rmap(1)),
                    pl.BlockSpec((tq, hv), rmap(2)), pl.BlockSpec((tq, hv), rmap(3)),
                    pl.BlockSpec((DEPTH, hk), const2), pl.BlockSpec((1, hv), const2)]
        args = [pr, pr, pr, pr, lb_logits, gain.reshape(1, hv)]
    else:
        w_up, b_gate = extra
        w_up_p = jnp.zeros((LANES, hk), f32).at[:GLA_GATE_RANK].set(w_up)
        in_specs = [pl.BlockSpec((tq, hk), rmap(0)), pl.BlockSpec((tq, hk), rmap(1)),
                    pl.BlockSpec((tq, hv), rmap(1)), pl.BlockSpec((tq, hv), rmap(2)),
                    pl.BlockSpec((tq, LANES), rmap((2 * hk + 2 * hv) // LANES)),
                    pl.BlockSpec((LANES, hk), const2), pl.BlockSpec((1, hk), const2),
                    pl.BlockSpec((1, hv), const2)]
        args = [pr, pr, pr, pr, pr, w_up_p, b_gate.reshape(1, hk), gain.reshape(1, hv)]
    return pl.pallas_call(
        functools.partial(_gla_block_kernel, kind=kind, layer=layer, L=L, nc=nc, H=H, DK=DK, DV=DV),
        grid=(B, nj),
        in_specs=in_specs,
        out_specs=[pl.BlockSpec((tq, hv), rmap(0)), pl.BlockSpec((1, H, DK, DV), bmap4)],
        out_shape=[jax.ShapeDtypeStruct((B * T, hv), f32), jax.ShapeDtypeStruct((B, H, DK, DV), f32)],
        scratch_shapes=[pltpu.VMEM((H, DV, DK), f32)] + [pltpu.VMEM((tq, hk), bf16)] * 4
                       + [pltpu.VMEM((nc, 1, hk), f32)],
        compiler_params=_cparams("parallel", "arbitrary"),
        name=kind + "_mix_fresh",
    )(*args)


def _rwkv_pre_kernel(x_ref, aux_ref, sh_ref, g_ref, mu_ref, wrkv_ref, la_ref, lbw_ref, lba_ref, lbg_ref,
                     w0_ref, a0_ref, r_ref, w_ref, k_ref, v_ref, alr_ref, gate_ref, *, long_seq, blocks_per_seq,
                     seq_len):
    g = g_ref[...]
    hn = _rms(x_ref[...], g)
    tm = hn.shape[0]
    rowi = lax.broadcasted_iota(jnp.int32, (tm, 1), 0)
    rolled = pltpu.roll(hn, 1, axis=0)
    if long_seq:
        p_last = _rms(aux_ref[SUBLANES - 1:SUBLANES, :], g)
        at_start = (pl.program_id(0) % blocks_per_seq) == 0
        first = jnp.where(at_start, sh_ref[0], p_last)
        prev = jnp.where(rowi == 0, first, rolled)
    else:
        prev = jnp.where(rowi % seq_len == 0, aux_ref[...], rolled)
    xx = prev - hn

    def lerp(c):
        return (hn + xx * mu_ref[c:c + 1, :]).astype(bf16)

    r_ref[...] = jnp.dot(lerp(0), wrkv_ref[0], preferred_element_type=f32)
    k_ref[...] = jnp.dot(lerp(1), wrkv_ref[1], preferred_element_type=f32)
    v_ref[...] = jnp.dot(lerp(2), wrkv_ref[2], preferred_element_type=f32)
    lw = jnp.tanh(jnp.dot(lerp(3), la_ref[:, 0:64], preferred_element_type=f32))
    wl = -_softplus(-(w0_ref[...] + jnp.dot(lw.astype(bf16), lbw_ref[...], preferred_element_type=f32))) - 0.5
    w_ref[...] = jnp.exp(-jnp.exp(wl))
    la = jnp.dot(lerp(4), la_ref[:, 64:128], preferred_element_type=f32)
    alr_ref[...] = _sigmoid(a0_ref[...] + jnp.dot(la.astype(bf16), lba_ref[...], preferred_element_type=f32))
    lg = _sigmoid(jnp.dot(lerp(5), la_ref[:, 128:256], preferred_element_type=f32))
    gate_ref[...] = jnp.dot(lg.astype(bf16), lbg_ref[...], preferred_element_type=f32)


def _rwkv_pre(x, shift0, g, mu, wrkv, la, lbw, lba, lbg, w0, a0, B, T, tm):
    n, d = x.shape
    long_seq = T % tm == 0
    row = lambda i: (i, 0)
    const2 = lambda i: (0, 0)
    if long_seq:
        bps = T // tm
        sub = tm // SUBLANES
        aux = x
        aux_spec = pl.BlockSpec((SUBLANES, d), lambda i: (jnp.maximum(i * sub - 1, 0), 0))
        sh = shift0.reshape(B, 1, d)
        sh_spec = pl.BlockSpec((1, 1, d), lambda i: (i // bps, 0, 0))
    else:
        assert tm % T == 0
        bps = 1
        aux = jnp.repeat(shift0, T, axis=0)
        aux_spec = pl.BlockSpec((tm, d), row)
        sh = shift0.reshape(B, 1, d)
        sh_spec = pl.BlockSpec((1, 1, d), lambda i: (0, 0, 0))
    out = jax.ShapeDtypeStruct((n, d), f32)
    return pl.pallas_call(
        functools.partial(_rwkv_pre_kernel, long_seq=long_seq, blocks_per_seq=bps, seq_len=T),
        grid=(n // tm,),
        in_specs=[pl.BlockSpec((tm, d), row), aux_spec, sh_spec, pl.BlockSpec((1, d), const2),
                  pl.BlockSpec((6, d), const2), pl.BlockSpec((3, d, d), lambda i: (0, 0, 0)),
                  pl.BlockSpec((d, 256), const2), pl.BlockSpec((64, d), const2), pl.BlockSpec((64, d), const2),
                  pl.BlockSpec((128, d), const2), pl.BlockSpec((1, d), const2), pl.BlockSpec((1, d), const2)],
        out_specs=[pl.BlockSpec((tm, d), row)] * 6,
        out_shape=[out] * 6,
        compiler_params=_cparams("parallel"),
        name="rwkv_pre",
    )(x, aux, sh, g.reshape(1, d), mu, wrkv, la, lbw, lba, lbg, w0.reshape(1, d), a0.reshape(1, d))


def _rwkv_scan_kernel(*refs, tt, has_init, unroll):
    N = RWKV_N
    r_ref, w_ref, k_ref, v_ref, alr_ref, kk_ref, ka_ref, rk_ref, lnw_ref, lnb_ref = refs[:10]
    if has_init:
        s0_ref = refs[10]
        rest = refs[11:]
    else:
        rest = refs[10:]
    z_ref, sout_ref, s_s, y_s, vec_s = rest
    j = pl.program_id(1)

    @pl.when(j == 0)
    def _():
        if has_init:
            s_s[...] = s0_ref[...]
        else:
            s_s[...] = jnp.zeros_like(s_s)

    def load_token(ref, t):
        return ref[:, t].reshape(LANES, N).T

    def store_token(ref, t, z):
        ref[:, t] = z.T.reshape(RWKV_SEQ_PER_STEP, RWKV_HEADS, N)

    VA, VW, VB, VK, VR, VV = range(6)

    def stage(tk, r, w, k, v, alr):
        kkraw = k * kk_ref[...]
        nrm = jnp.sqrt(jnp.sum(kkraw * kkraw, axis=0, keepdims=True))
        kk = kkraw / jnp.maximum(nrm, 1e-12)
        vec_s[tk, VA] = -kk
        vec_s[tk, VW] = w
        vec_s[tk, VB] = kk * alr
        vec_s[tk, VK] = k * (1.0 + (alr - 1.0) * ka_ref[...])
        vec_s[tk, VR] = r
        vec_s[tk, VV] = v

    def token(tk):
        def rows(i, carry):
            for u in range(unroll):
                vi = i * unroll + u
                sv = s_s[vi]
                sa = jnp.sum(sv * vec_s[tk, VA], axis=0, keepdims=True)
                sv = sv * vec_s[tk, VW] + sa * vec_s[tk, VB] + vec_s[tk, VV, pl.ds(vi, 1), :] * vec_s[tk, VK]
                s_s[vi] = sv
                y_s[pl.ds(vi, 1), :] = jnp.sum(sv * vec_s[tk, VR], axis=0, keepdims=True)
            return carry

        lax.fori_loop(0, N // unroll, rows, 0)
        y = y_s[...]
        mean = jnp.mean(y, axis=0, keepdims=True)
        yc = y - mean
        var = jnp.mean(yc * yc, axis=0, keepdims=True)
        yn = yc * lax.rsqrt(var + RWKV_GN_EPS)
        bonus = jnp.sum(vec_s[tk, VR] * vec_s[tk, VK] * rk_ref[...], axis=0, keepdims=True) * vec_s[tk, VV]
        return yn * lnw_ref[...] + lnb_ref[...] + bonus

    def stage_token(slot, t):
        stage(slot, load_token(r_ref, t), load_token(w_ref, t), load_token(k_ref, t), load_token(v_ref, t),
              load_token(alr_ref, t))

    stage_token(0, 0)

    def step(t, carry):
        slot = t % 2
        z = token(slot)
        stage_token(1 - slot, jnp.minimum(t + 1, tt - 1))
        store_token(z_ref, t, z)
        return carry

    lax.fori_loop(0, tt, step, 0)

    @pl.when(j == pl.num_programs(1) - 1)
    def _():
        sout_ref[...] = s_s[...]


RWKV_SEQ_PER_STEP = LANES // RWKV_HEADS


def _rwkv_scan(r, w, k, v, alr, slabs, s0, B, T, tt):
    N = RWKV_N
    G = B // RWKV_SEQ_PER_STEP
    has_init = s0 is not None
    view = lambda a: a.reshape(B, T, RWKV_HEADS, N)
    tmap = lambda g, j: (g, j, 0, 0)
    smap = lambda g, j: (0, 0, g)
    const2 = lambda g, j: (0, 0)
    stream = pl.BlockSpec((RWKV_SEQ_PER_STEP, tt, RWKV_HEADS, N), tmap)
    in_specs = [stream] * 5 + [pl.BlockSpec((N, LANES), const2)] * 5
    args = [view(r), view(w), view(k), view(v), view(alr)] + list(slabs)
    if has_init:
        in_specs.append(pl.BlockSpec((N, N, LANES), smap))
        args.append(s0)
    z, s = pl.pallas_call(
        functools.partial(_rwkv_scan_kernel, tt=tt, has_init=has_init, unroll=16),
        grid=(G, T // tt),
        in_specs=in_specs,
        out_specs=[stream, pl.BlockSpec((N, N, LANES), smap)],
        out_shape=[jax.ShapeDtypeStruct((B, T, RWKV_HEADS, N), f32),
                   jax.ShapeDtypeStruct((N, N, G * LANES), f32)],
        scratch_shapes=[pltpu.VMEM((N, N, LANES), f32), pltpu.VMEM((N, LANES), f32),
                        pltpu.VMEM((2, 6, N, LANES), f32)],
        compiler_params=_cparams("parallel", "arbitrary"),
        name="rwkv_scan",
    )(*args)
    return z.reshape(B * T, D_MODEL), s


def _lane_slab(p):
    return jnp.tile(p.reshape(RWKV_HEADS, RWKV_N).T, (1, RWKV_SEQ_PER_STEP))


def _rwkv_mix(x, p, B, T, state, tm):
    H, N = RWKV_HEADS, RWKV_N
    G = B // RWKV_SEQ_PER_STEP
    if state is None:
        shift0 = jnp.zeros((B, D_MODEL), f32)
        s0 = None
    else:
        s_in, shift0 = state
        s0 = s_in.transpose(2, 3, 0, 1).reshape(N, N, B * H)
    r, w, k, v, alr, gate = _rwkv_pre(x, shift0, p["g"], p["mu"], p["wrkv"], p["la"], p["lbw"], p["lba"], p["lbg"],
                                      p["w0"], p["a0"], B, T, tm)
    slabs = [_lane_slab(p[n]) for n in ("k_k", "k_a", "r_k", "ln_w", "ln_b")]
    z, s = _rwkv_scan(r, w, k, v, alr, slabs, s0, B, T, min(T, 32))
    s = s.reshape(N, N, B, H).transpose(2, 3, 0, 1)
    return z, gate, s


def _pad_cols(w, to):
    return jnp.pad(w, ((0, 0), (0, to - w.shape[1])))


def _trunk(x, B, T, states, p):
    n = x.shape[0]
    tm = min(512, n)
    tmm = min(1024, n)
    tq = min(256, T)
    fresh = states is None
    block_ok = fresh and T % tq == 0 and tq % (2 * LIN_CHUNK) == 0
    if not fresh:
        m_c, m_n, m_m, h_s, g_s, r_s, r_sh = states
    new = {}
    for li in range(DEPTH):
        g_mix = p["norm_mix"][li]
        if li == 0:
            pr = _norm_proj(x, g_mix, p["mlstm_w_in"], tmm, 1664)
            if fresh and T % tq == 0 and tq % (2 * MLSTM_CHUNK) == 0:
                y, c, nn, m = _mlstm_mix_fresh(pr, p["mlstm_b_gates"], p["mlstm_norm"], B, T, tq)
            else:
                y, c, nn, m = _mlstm_mix(pr, p["mlstm_b_gates"], p["mlstm_norm"], B, T,
                                         None if fresh else (m_c[0], m_n[0], m_m[0]), tq)
            new["C"], new["n"], new["m"] = c[None], nn[None], m[None]
            x = _out_proj(y, p["mlstm_w_out"], x, tm)
        elif li == 1:
            pr = _norm_proj(x, g_mix, p["hgrn_w_in"], tmm, 2048)
            if block_ok:
                y, s = _gla_mix_fresh("hgrn", pr, (p["hgrn_lb_logits"],), p["hgrn_norm"], B, T, tq, layer=li)
            else:
                y, s = _gla_mix("hgrn", pr, (p["hgrn_lb_logits"],), p["hgrn_norm"], B, T,
                                None if fresh else h_s[0], tq, layer=li)
            new["hS"] = s[None]
            x = _out_proj(y, p["hgrn_w_out"], x, tm)
        elif li == 2:
            pr = _norm_proj(x, g_mix, p["gla_w_in"], tmm, 1664)
            if block_ok:
                y, s = _gla_mix_fresh("gla", pr, (p["gla_w_gate_up"], p["gla_b_gate"]), p["gla_norm"], B, T, tq)
            else:
                y, s = _gla_mix("gla", pr, (p["gla_w_gate_up"], p["gla_b_gate"]), p["gla_norm"], B, T,
                                None if fresh else g_s[0], tq)
            new["gS"] = s[None]
            x = _out_proj(y, p["gla_w_out"], x, tm)
        else:
            rp = dict(p["rwkv"], g=g_mix)
            z, gate, s = _rwkv_mix(x, rp, B, T, None if fresh else (r_s[0], r_sh[0]), min(256, n))
            x_last = x.reshape(B, T, D_MODEL)[:, T - 1, :]
            new["rS"] = s[None]
            new["sh"] = _rmsnorm(x_last, g_mix, B)[None]
            x = _out_proj(z, p["rwkv_w_out"], x, tm, gate=gate)
        x = _ffn(x, p["norm_ffn"][li], p["ffn_w_gate_up"][li], p["ffn_w_down"][li], tm, 1408)
    y = _rmsnorm(x, p["norm_final"], tm)
    return y.reshape(B, T, D_MODEL), (new["C"], new["n"], new["m"], new["hS"], new["gS"], new["rS"], new["sh"])


def kernel(x_prompt, x_sample, state_mlstm_C, state_mlstm_n, state_mlstm_m, state_hgrn_S, state_gla_S, state_rwkv_S, state_rwkv_shift, norm_mix, norm_ffn, norm_final, mlstm_w_in, mlstm_b_gates, mlstm_norm, mlstm_w_out, hgrn_w_in, hgrn_lb_logits, hgrn_norm, hgrn_w_out, gla_w_in, gla_w_gate_up, gla_b_gate, gla_norm, gla_w_out, rwkv_mu, rwkv_w_rkv, rwkv_w_lora_a, rwkv_w_lora_b, rwkv_w0, rwkv_a_lora_a, rwkv_a_lora_b, rwkv_a0, rwkv_g_lora_a, rwkv_g_lora_b, rwkv_k_k, rwkv_k_a, rwkv_r_k, rwkv_ln_w, rwkv_ln_b, rwkv_w_out, ffn_w_gate_up, ffn_w_down):
    cast = lambda w: w.astype(bf16)
    p = dict(
        norm_mix=norm_mix, norm_ffn=norm_ffn, norm_final=norm_final,
        mlstm_w_in=cast(_pad_cols(mlstm_w_in[0], 3328)), mlstm_b_gates=mlstm_b_gates[0], mlstm_norm=mlstm_norm[0],
        mlstm_w_out=cast(mlstm_w_out[0]),
        hgrn_w_in=cast(hgrn_w_in[0]), hgrn_lb_logits=hgrn_lb_logits, hgrn_norm=hgrn_norm[0],
        hgrn_w_out=cast(hgrn_w_out[0]),
        gla_w_in=cast(_pad_cols(gla_w_in[0], 3328)), gla_w_gate_up=gla_w_gate_up[0], gla_b_gate=gla_b_gate[0],
        gla_norm=gla_norm[0], gla_w_out=cast(gla_w_out[0]),
        rwkv=dict(mu=rwkv_mu[0], wrkv=cast(rwkv_w_rkv[0]),
                  la=cast(jnp.concatenate([rwkv_w_lora_a[0], rwkv_a_lora_a[0], rwkv_g_lora_a[0]], axis=1)),
                  lbw=cast(rwkv_w_lora_b[0]), lba=cast(rwkv_a_lora_b[0]), lbg=cast(rwkv_g_lora_b[0]),
                  w0=rwkv_w0[0], a0=rwkv_a0[0], k_k=rwkv_k_k[0], k_a=rwkv_k_a[0], r_k=rwkv_r_k[0].reshape(-1),
                  ln_w=rwkv_ln_w[0], ln_b=rwkv_ln_b[0]),
        rwkv_w_out=cast(rwkv_w_out[0]),
        ffn_w_gate_up=cast(ffn_w_gate_up), ffn_w_down=cast(ffn_w_down),
    )
    bp, tp, _ = x_prompt.shape
    bs, ts, _ = x_sample.shape
    y_p, st_p = _trunk(x_prompt.reshape(bp * tp, D_MODEL), bp, tp, None, p)
    y_s, st_s = _trunk(x_sample.reshape(bs * ts, D_MODEL), bs, ts,
                       (state_mlstm_C, state_mlstm_n, state_mlstm_m, state_hgrn_S, state_gla_S, state_rwkv_S,
                        state_rwkv_shift), p)
    return (y_p, y_s) + st_p + st_s
```

```python
import functools

import jax
import jax.numpy as jnp
from jax import lax
from jax.experimental import pallas as pl
from jax.experimental.pallas import tpu as pltpu

f32 = jnp.float32
bf16 = jnp.bfloat16

D_MODEL = 1024
DEPTH = 4
NORM_EPS = 1e-6

MLSTM_HEADS, MLSTM_DK, MLSTM_DV, MLSTM_CHUNK = 4, 128, 256, 64
HGRN_HEADS, HGRN_DK, HGRN_DV = 8, 128, 128
GLA_HEADS, GLA_DK, GLA_DV = 4, 128, 256
GLA_GATE_RANK = 16
GLA_GATE_NORMALIZER = 16.0
LIN_CHUNK = 32
RWKV_HEADS, RWKV_N = 16, 64
RWKV_GN_EPS = 64e-5
FFN_HIDDEN = 2816

LANES = 128
SUBLANES = 8
BF16_ROWS = 16
VMEM_LIMIT_BYTES = 52 * 1024 * 1024


def _cparams(*sem):
    return pltpu.CompilerParams(dimension_semantics=sem, vmem_limit_bytes=VMEM_LIMIT_BYTES)


def _chunk_len(t, cap):
    return max(d for d in range(1, min(cap, t) + 1) if t % d == 0)


def _rms(x, g):
    ms = jnp.mean(x * x, axis=-1, keepdims=True)
    return x * lax.rsqrt(ms + NORM_EPS) * g


def _sigmoid(x):
    return jax.nn.sigmoid(x)


def _softplus(x):
    return jnp.maximum(x, 0.0) + jnp.log1p(jnp.exp(-jnp.abs(x)))


def _log_sigmoid(x):
    return -_softplus(-x)


def _mm(a, b, mx=bf16):
    return jnp.dot(a.astype(mx), b.astype(mx), preferred_element_type=f32)


def _mm_nt(a, b, mx=bf16):
    return lax.dot_general(a.astype(mx), b.astype(mx), (((1,), (1,)), ((), ())), preferred_element_type=f32)


def _mm_tn(a, b, mx=bf16):
    return lax.dot_general(a.astype(mx), b.astype(mx), (((0,), (0,)), ((), ())), preferred_element_type=f32)


def _split3(x):
    hi = x.astype(bf16)
    r1 = x - hi.astype(f32)
    mid = r1.astype(bf16)
    lo = (r1 - mid.astype(f32)).astype(bf16)
    return hi, mid, lo


def _cumsum_rows(x, tril_b):
    hi, mid, lo = _split3(x)
    return (jnp.dot(tril_b, hi, preferred_element_type=f32)
            + jnp.dot(tril_b, mid, preferred_element_type=f32)
            + jnp.dot(tril_b, lo, preferred_element_type=f32))


def _norm_proj_kernel(x_ref, g_ref, w_ref, o_ref, hn_ref):
    @pl.when(pl.program_id(1) == 0)
    def _():
        hn_ref[...] = _rms(x_ref[...], g_ref[...]).astype(bf16)

    o_ref[...] = jnp.dot(hn_ref[...], w_ref[...], preferred_element_type=f32)


def _norm_proj(x, g, w, tm, tn):
    n, d = x.shape
    e = w.shape[1]
    return pl.pallas_call(
        _norm_proj_kernel,
        grid=(n // tm, e // tn),
        in_specs=[pl.BlockSpec((tm, d), lambda i, j: (i, 0)),
                  pl.BlockSpec((1, d), lambda i, j: (0, 0)),
                  pl.BlockSpec((d, tn), lambda i, j: (0, j))],
        out_specs=pl.BlockSpec((tm, tn), lambda i, j: (i, j)),
        out_shape=jax.ShapeDtypeStruct((n, e), f32),
        scratch_shapes=[pltpu.VMEM((tm, d), bf16)],
        compiler_params=_cparams("parallel", "arbitrary"),
        name="norm_proj",
    )(x, g.reshape(1, d), w)


def _out_proj_kernel(*refs, gated):
    if gated:
        y_ref, gate_ref, w_ref, res_ref, o_ref = refs
        y = y_ref[...] * gate_ref[...]
    else:
        y_ref, w_ref, res_ref, o_ref = refs
        y = y_ref[...]
    o_ref[...] = res_ref[...] + jnp.dot(y.astype(bf16), w_ref[...], preferred_element_type=f32)


def _out_proj(y, w, res, tm, gate=None):
    n, e = y.shape
    d = w.shape[1]
    row = lambda i: (i, 0)
    args = [y] + ([gate] if gate is not None else []) + [w, res]
    in_specs = ([pl.BlockSpec((tm, e), row)] + ([pl.BlockSpec((tm, e), row)] if gate is not None else [])
                + [pl.BlockSpec((e, d), lambda i: (0, 0)), pl.BlockSpec((tm, d), row)])
    return pl.pallas_call(
        functools.partial(_out_proj_kernel, gated=gate is not None),
        grid=(n // tm,),
        in_specs=in_specs,
        out_specs=pl.BlockSpec((tm, d), row),
        out_shape=jax.ShapeDtypeStruct((n, d), f32),
        compiler_params=_cparams("parallel"),
        name="out_proj",
    )(*args)


def _ffn_kernel(x_ref, g_ref, wg_ref, wu_ref, wd_ref, o_ref, hn_ref, acc_ref):
    j = pl.program_id(1)

    @pl.when(j == 0)
    def _():
        hn_ref[...] = _rms(x_ref[...], g_ref[...]).astype(bf16)
        acc_ref[...] = jnp.zeros_like(acc_ref)

    h = hn_ref[...]
    gt = jnp.dot(h, wg_ref[...], preferred_element_type=f32)
    up = jnp.dot(h, wu_ref[...], preferred_element_type=f32)
    act = (gt * _sigmoid(gt) * up).astype(bf16)
    acc_ref[...] += jnp.dot(act, wd_ref[...], preferred_element_type=f32)

    @pl.when(j == pl.num_programs(1) - 1)
    def _():
        o_ref[...] = x_ref[...] + acc_ref[...]


def _ffn(x, g, w_gu, w_down, tm, tf):
    n, d = x.shape
    nf = FFN_HIDDEN // tf
    return pl.pallas_call(
        _ffn_kernel,
        grid=(n // tm, nf),
        in_specs=[pl.BlockSpec((tm, d), lambda i, j: (i, 0)),
                  pl.BlockSpec((1, d), lambda i, j: (0, 0)),
                  pl.BlockSpec((d, tf), lambda i, j: (0, j)),
                  pl.BlockSpec((d, tf), lambda i, j: (0, j + nf)),
                  pl.BlockSpec((tf, d), lambda i, j: (j, 0))],
        out_specs=pl.BlockSpec((tm, d), lambda i, j: (i, 0)),
        out_shape=jax.ShapeDtypeStruct((n, d), f32),
        scratch_shapes=[pltpu.VMEM((tm, d), bf16), pltpu.VMEM((tm, d), f32)],
        compiler_params=_cparams("parallel", "arbitrary"),
        name="ffn",
    )(x, g.reshape(1, d), w_gu, w_gu, w_down)


def _rmsnorm_kernel(x_ref, g_ref, o_ref):
    o_ref[...] = _rms(x_ref[...], g_ref[...])


def _rmsnorm(x, g, tm):
    n, d = x.shape
    return pl.pallas_call(
        _rmsnorm_kernel,
        grid=(n // tm,),
        in_specs=[pl.BlockSpec((tm, d), lambda i: (i, 0)), pl.BlockSpec((1, d), lambda i: (0, 0))],
        out_specs=pl.BlockSpec((tm, d), lambda i: (i, 0)),
        out_shape=jax.ShapeDtypeStruct((n, d), f32),
        compiler_params=_cparams("parallel"),
        name="rmsnorm",
    )(x, g.reshape(1, d))


def _mlstm_kernel(*refs, L, nchunks, has_init, mx):
    H, DK, DV = MLSTM_HEADS, MLSTM_DK, MLSTM_DV
    q_ref, k_ref, v_ref, o_ref, gt_ref, bias_ref, gain_ref = refs[:7]
    if has_init:
        c0_ref, n0_ref, m0_ref = refs[7:10]
        rest = refs[10:]
    else:
        rest = refs[7:]
    y_ref, c_ref, n_ref, m_ref, c_s, n_s, m_s = rest
    j = pl.program_id(1)

    @pl.when(j == 0)
    def _():
        if has_init:
            c_s[...] = c0_ref[0]
            n_s[...] = n0_ref[0]
            m_s[...] = m0_ref[0]
        else:
            c_s[...] = jnp.zeros_like(c_s)
            n_s[...] = jnp.zeros_like(n_s)
            m_s[...] = jnp.zeros_like(m_s)

    Lp = max(L, BF16_ROWS)
    rowi = lax.broadcasted_iota(jnp.int32, (Lp, Lp), 0)
    coli = lax.broadcasted_iota(jnp.int32, (Lp, Lp), 1)
    causal = coli <= rowi
    eye = coli == rowi
    scale = DK ** -0.5

    def to_row(col):
        return jnp.sum(jnp.where(eye, col, 0.0), axis=0, keepdims=True)

    def pad_rows(x, value=0.0):
        if Lp == L:
            return x
        return jnp.concatenate([x, jnp.full((Lp - L, x.shape[1]), value, f32)], axis=0)

    def chunk(c, carry):
        rows = pl.ds(pl.multiple_of(c * L, L), L)
        gts_raw = gt_ref[rows, :] + bias_ref[...]
        lsg = pad_rows(_log_sigmoid(gts_raw))
        gts = pad_rows(gts_raw, -jnp.inf)
        for h in range(H):
            i_col = gts[:, h:h + 1]
            f_col = lsg[:, H + h:H + h + 1]
            f_row = to_row(f_col)
            fc_col = jnp.sum(jnp.where(causal, f_row, 0.0), axis=1, keepdims=True)
            fc_row = to_row(fc_col)
            i_row = to_row(i_col)
            m = m_s[h]
            dm = jnp.where(causal, fc_col - fc_row + i_row, -jnp.inf)
            g = fc_col + m
            mt = jnp.maximum(g, jnp.max(dm, axis=1, keepdims=True))
            p = jnp.exp(dm - mt)
            qc = pad_rows(q_ref[rows, h * DK:(h + 1) * DK] * scale)
            kc = pad_rows(k_ref[rows, h * DK:(h + 1) * DK])
            vc = pad_rows(v_ref[rows, h * DV:(h + 1) * DV])
            s = _mm_nt(qc, kc, mx) * p
            eg = jnp.exp(g - mt)
            cst = c_s[h]
            nst = n_s[h]
            num = eg * _mm_nt(qc, cst, mx) + _mm(s, vc, mx)
            den = eg * jnp.sum(qc * nst, axis=1, keepdims=True) + jnp.sum(s, axis=1, keepdims=True)
            hh = num / jnp.maximum(jnp.abs(den), jnp.exp(-mt))
            m_new = mt[L - 1:L, :]
            f_last = fc_col[L - 1:L, :]
            w_c = jnp.exp(f_last + m - m_new)
            w_j = jnp.exp(f_last - fc_col + i_col - m_new)
            c_s[h] = w_c * cst + _mm_tn(vc * w_j, kc, mx)
            n_s[h] = w_c * nst + jnp.sum(w_j * kc, axis=0, keepdims=True)
            m_s[h] = m_new
            hh = hh[:L]
            yn = hh * lax.rsqrt(jnp.mean(hh * hh, axis=1, keepdims=True) + NORM_EPS)
            yn = yn * gain_ref[:, h * DV:(h + 1) * DV]
            y_ref[rows, h * DV:(h + 1) * DV] = _sigmoid(o_ref[rows, h * DV:(h + 1) * DV]) * yn
        return carry

    lax.fori_loop(0, nchunks, chunk, 0)

    @pl.when(j == pl.num_programs(1) - 1)
    def _():
        c_ref[0] = c_s[...]
        n_ref[0] = n_s[...]
        m_ref[0] = m_s[...]


def _mlstm_mix(pr, b_gates, gain, B, T, state, tq):
    H, DK, DV = MLSTM_HEADS, MLSTM_DK, MLSTM_DV
    L = _chunk_len(T, MLSTM_CHUNK)
    nj = T // tq
    has_init = state is not None
    mx = bf16
    rmap = lambda c: (lambda b, j: (b * nj + j, c))
    bmap4 = lambda b, j: (b, 0, 0, 0)
    bias = jnp.zeros((1, LANES), f32).at[0, :2 * H].set(b_gates)
    in_specs = [pl.BlockSpec((tq, H * DK), rmap(0)), pl.BlockSpec((tq, H * DK), rmap(1)),
                pl.BlockSpec((tq, H * DV), rmap(1)), pl.BlockSpec((tq, H * DV), rmap(2)),
                pl.BlockSpec((tq, LANES), rmap((2 * H * DK + 2 * H * DV) // LANES)),
                pl.BlockSpec((1, LANES), lambda b, j: (0, 0)),
                pl.BlockSpec((1, H * DV), lambda b, j: (0, 0))]
    args = [pr, pr, pr, pr, pr, bias, gain.reshape(1, H * DV)]
    if has_init:
        c0, n0, m0 = state
        in_specs += [pl.BlockSpec((1, H, DV, DK), bmap4), pl.BlockSpec((1, H, 1, DK), bmap4),
                     pl.BlockSpec((1, H, 1, 1), bmap4)]
        args += [c0, n0.reshape(B, H, 1, DK), m0.reshape(B, H, 1, 1)]
    y, c, n, m = pl.pallas_call(
        functools.partial(_mlstm_kernel, L=L, nchunks=tq // L, has_init=has_init, mx=mx),
        grid=(B, nj),
        in_specs=in_specs,
        out_specs=[pl.BlockSpec((tq, H * DV), rmap(0)), pl.BlockSpec((1, H, DV, DK), bmap4),
                   pl.BlockSpec((1, H, 1, DK), bmap4), pl.BlockSpec((1, H, 1, 1), bmap4)],
        out_shape=[jax.ShapeDtypeStruct((B * T, H * DV), f32), jax.ShapeDtypeStruct((B, H, DV, DK), f32),
                   jax.ShapeDtypeStruct((B, H, 1, DK), f32), jax.ShapeDtypeStruct((B, H, 1, 1), f32)],
        scratch_shapes=[pltpu.VMEM((H, DV, DK), f32), pltpu.VMEM((H, 1, DK), f32), pltpu.VMEM((H, 1, 1), f32)],
        compiler_params=_cparams("parallel", "arbitrary"),
        name="mlstm_mix",
    )(*args)
    return y, c, n.reshape(B, H, DK), m.reshape(B, H)


def _mlstm_block_kernel(q_ref, k_ref, v_ref, o_ref, gt_ref, bias_ref, gain_ref, y_ref, c_ref, n_ref, m_ref,
                        c_s, n_s, m_s, *, L, nc):
    H, DK, DV = MLSTM_HEADS, MLSTM_DK, MLSTM_DV
    P = 2 * L
    j = pl.program_id(1)

    @pl.when(j == 0)
    def _():
        c_s[...] = jnp.zeros_like(c_s)
        n_s[...] = jnp.zeros_like(n_s)
        m_s[...] = jnp.zeros_like(m_s)

    rowi = lax.broadcasted_iota(jnp.int32, (L, L), 0)
    coli = lax.broadcasted_iota(jnp.int32, (L, L), 1)
    causal = coli <= rowi
    r2 = lax.broadcasted_iota(jnp.int32, (P, P), 0)
    c2 = lax.broadcasted_iota(jnp.int32, (P, P), 1)
    pair_tril = ((c2 <= r2) & ((c2 >= L) == (r2 >= L))).astype(bf16)
    scale = DK ** -0.5

    gates = []
    for pp in range(nc // 2):
        g = gt_ref[pp * P:(pp + 1) * P, :] + bias_ref[...]
        hi, mid, lo = _split3(_log_sigmoid(g))
        fcum = (jnp.dot(pair_tril, hi, preferred_element_type=f32)
                + jnp.dot(pair_tril, mid, preferred_element_type=f32)
                + jnp.dot(pair_tril, lo, preferred_element_type=f32))
        g_t = g.T
        f_t = fcum.T
        for half in range(2):
            rs = slice(half * L, (half + 1) * L)
            gates.append((g[rs, :], fcum[rs, :], g_t[:, rs], f_t[:, rs]))

    pairs = [(c, h) for c in range(nc) for h in range(H)]
    rows_of = lambda c: slice(c * L, (c + 1) * L)
    ks_of = lambda h: slice(h * DK, (h + 1) * DK)
    vs_of = lambda h: slice(h * DV, (h + 1) * DV)
    dms, rmax, fcols, icols, qks = {}, {}, {}, {}, {}
    for c, h in pairs:
        g_c, f_c, g_r, f_r = gates[c]
        icols[c, h] = g_c[:, h:h + 1]
        fcols[c, h] = f_c[:, H + h:H + h + 1]
        dm = jnp.where(causal, fcols[c, h] - f_r[H + h:H + h + 1, :] + g_r[h:h + 1, :], -jnp.inf)
        dms[c, h] = dm
        rmax[c, h] = jnp.max(dm, axis=1, keepdims=True)
        qb = (q_ref[rows_of(c), ks_of(h)] * scale).astype(bf16)
        qks[c, h] = _mm_nt(qb, k_ref[rows_of(c), ks_of(h)], bf16)
    m_in, m_out = {}, {}
    for h in range(H):
        m = m_s[h]
        for c in range(nc):
            m_in[c, h] = m
            m = jnp.maximum(fcols[c, h][L - 1:L, :] + m, rmax[c, h][L - 1:L, :])
            m_out[c, h] = m
        m_s[h] = m
    mts, egs, ss, ssums, wcs, dcs, dns = {}, {}, {}, {}, {}, {}, {}
    for c, h in pairs:
        fc_col = fcols[c, h]
        gg = fc_col + m_in[c, h]
        mt = jnp.maximum(gg, rmax[c, h])
        mts[c, h] = mt
        egs[c, h] = jnp.exp(gg - mt)
        s = qks[c, h] * jnp.exp(dms[c, h] - mt)
        ssums[c, h] = jnp.sum(s, axis=1, keepdims=True)
        ss[c, h] = s.astype(bf16)
        f_last = fc_col[L - 1:L, :]
        wcs[c, h] = jnp.exp(f_last + m_in[c, h] - m_out[c, h])
        w_j = jnp.exp(f_last - fc_col + icols[c, h] - m_out[c, h])
        kc = k_ref[rows_of(c), ks_of(h)]
        dcs[c, h] = _mm_tn(v_ref[rows_of(c), vs_of(h)] * w_j, kc, bf16)
        dns[c, h] = jnp.sum(w_j * kc, axis=0, keepdims=True)
    c_in, n_in = {}, {}
    for h in range(H):
        cst = c_s[h]
        nst = n_s[h]
        for c in range(nc):
            c_in[c, h] = cst.astype(bf16)
            n_in[c, h] = nst
            cst = wcs[c, h] * cst + dcs[c, h]
            nst = wcs[c, h] * nst + dns[c, h]
        c_s[h] = cst
        n_s[h] = nst
    hhs = {}
    for c, h in pairs:
        rows, ks, vs = rows_of(c), ks_of(h), vs_of(h)
        qc = q_ref[rows, ks] * scale
        num = egs[c, h] * _mm_nt(qc, c_in[c, h], bf16) + _mm(ss[c, h], v_ref[rows, vs], bf16)
        den = egs[c, h] * jnp.sum(qc * n_in[c, h], axis=1, keepdims=True) + ssums[c, h]
        hhs[c, h] = num / jnp.maximum(jnp.abs(den), jnp.exp(-mts[c, h]))
    for c, h in pairs:
        rows, vs = rows_of(c), vs_of(h)
        hh = hhs[c, h]
        yn = hh * lax.rsqrt(jnp.mean(hh * hh, axis=1, keepdims=True) + NORM_EPS) * gain_ref[:, vs]
        y_ref[rows, vs] = _sigmoid(o_ref[rows, vs]) * yn

    @pl.when(j == pl.num_programs(1) - 1)
    def _():
        c_ref[0] = c_s[...]
        n_ref[0] = n_s[...]
        m_ref[0] = m_s[...]


def _mlstm_mix_fresh(pr, b_gates, gain, B, T, tq):
    H, DK, DV = MLSTM_HEADS, MLSTM_DK, MLSTM_DV
    L = MLSTM_CHUNK
    nj = T // tq
    rmap = lambda c: (lambda b, j: (b * nj + j, c))
    bmap4 = lambda b, j: (b, 0, 0, 0)
    bias = jnp.zeros((1, LANES), f32).at[0, :2 * H].set(b_gates)
    y, c, n, m = pl.pallas_call(
        functools.partial(_mlstm_block_kernel, L=L, nc=tq // L),
        grid=(B, nj),
        in_specs=[pl.BlockSpec((tq, H * DK), rmap(0)), pl.BlockSpec((tq, H * DK), rmap(1)),
                  pl.BlockSpec((tq, H * DV), rmap(1)), pl.BlockSpec((tq, H * DV), rmap(2)),
                  pl.BlockSpec((tq, LANES), rmap((2 * H * DK + 2 * H * DV) // LANES)),
                  pl.BlockSpec((1, LANES), lambda b, j: (0, 0)),
                  pl.BlockSpec((1, H * DV), lambda b, j: (0, 0))],
        out_specs=[pl.BlockSpec((tq, H * DV), rmap(0)), pl.BlockSpec((1, H, DV, DK), bmap4),
                   pl.BlockSpec((1, H, 1, DK), bmap4), pl.BlockSpec((1, H, 1, 1), bmap4)],
        out_shape=[jax.ShapeDtypeStruct((B * T, H * DV), f32), jax.ShapeDtypeStruct((B, H, DV, DK), f32),
                   jax.ShapeDtypeStruct((B, H, 1, DK), f32), jax.ShapeDtypeStruct((B, H, 1, 1), f32)],
        scratch_shapes=[pltpu.VMEM((H, DV, DK), f32), pltpu.VMEM((H, 1, DK), f32), pltpu.VMEM((H, 1, 1), f32)],
        compiler_params=_cparams("parallel", "arbitrary"),
        name="mlstm_mix_fresh",
    )(pr, pr, pr, pr, pr, bias, gain.reshape(1, H * DV))
    return y, c, n.reshape(B, H, DK), m.reshape(B, H)


def _gla_kernel(*refs, kind, layer, L, nchunks, H, DK, DV, has_init, mx):
    if kind == "hgrn":
        q_ref, f_ref, v_ref, g_ref, lbl_ref, gain_ref = refs[:6]
        rest = refs[6:]
    else:
        q_ref, k_ref, v_ref, g_ref, gr_ref, wup_ref, bgate_ref, gain_ref = refs[:8]
        rest = refs[8:]
    if has_init:
        s0_ref = rest[0]
        rest = rest[1:]
    y_ref, s_ref, st_s = rest
    j = pl.program_id(1)

    @pl.when(j == 0)
    def _():
        for h in range(H):
            if has_init:
                st_s[h] = s0_ref[0, h].T
            else:
                st_s[h] = jnp.zeros((DV, DK), f32)

    Lp = max(L, BF16_ROWS)
    rowi = lax.broadcasted_iota(jnp.int32, (Lp, Lp), 0)
    coli = lax.broadcasted_iota(jnp.int32, (Lp, Lp), 1)
    causal = coli <= rowi
    tril_b = causal.astype(bf16)

    def pad_rows(x):
        if Lp == L:
            return x
        return jnp.concatenate([x, jnp.zeros((Lp - L, x.shape[1]), f32)], axis=0)

    if kind == "hgrn":
        logits = lbl_ref[...]
        e = jnp.exp(logits - jnp.max(logits, axis=0, keepdims=True))
        sm = e / jnp.sum(e, axis=0, keepdims=True)
        lb_all = jnp.zeros((1, H * DK), f32)
        for li in range(layer):
            lb_all = lb_all + sm[li:li + 1, :]
    else:
        scale = DK ** -0.5

    def chunk(c, carry):
        rows = pl.ds(pl.multiple_of(c * L, L), L)
        if kind == "gla":
            gk = _mm(pad_rows(gr_ref[rows, :]), wup_ref[...], mx)[:L] + bgate_ref[...]
            ld_all = _log_sigmoid(gk) * (1.0 / GLA_GATE_NORMALIZER)
        for h in range(H):
            ks = slice(h * DK, (h + 1) * DK)
            vs = slice(h * DV, (h + 1) * DV)
            if kind == "hgrn":
                qraw = q_ref[rows, ks]
                fg = f_ref[rows, ks]
                lb = lb_all[:, ks]
                gc = jnp.log(lb + (1.0 - lb) * _sigmoid(fg))
                kc = (1.0 - lb) * _sigmoid(-fg)
                qc = qraw * _sigmoid(qraw)
            else:
                qc = q_ref[rows, ks] * scale
                kc = k_ref[rows, ks]
                gc = ld_all[:, ks]
            vc = pad_rows(v_ref[rows, vs])
            qc, kc, gc = pad_rows(qc), pad_rows(kc), pad_rows(gc)
            bc = _cumsum_rows(gc, tril_b)
            qe = qc * jnp.exp(bc)
            ke = kc * jnp.exp(-bc)
            a = jnp.where(causal, _mm_nt(qe, ke, mx), 0.0)
            st = st_s[h]
            o = (_mm_nt(qe, st, mx) + _mm(a, vc, mx))[:L]
            b_last = bc[L - 1:L, :]
            kdec = kc * jnp.exp(b_last - bc)
            st_s[h] = st * jnp.exp(b_last) + _mm_tn(vc, kdec, mx)
            yn = o * lax.rsqrt(jnp.mean(o * o, axis=1, keepdims=True) + NORM_EPS) * gain_ref[:, vs]
            gg = g_ref[rows, vs]
            y_ref[rows, vs] = yn * (gg * _sigmoid(gg))
        return carry

    lax.fori_loop(0, nchunks, chunk, 0)

    @pl.when(j == pl.num_programs(1) - 1)
    def _():
        for h in range(H):
            s_ref[0, h] = st_s[h].T


def _gla_mix(kind, pr, extra, gain, B, T, state, tq, layer=0):
    if kind == "hgrn":
        H, DK, DV = HGRN_HEADS, HGRN_DK, HGRN_DV
    else:
        H, DK, DV = GLA_HEADS, GLA_DK, GLA_DV
    L = _chunk_len(T, LIN_CHUNK)
    nj = T // tq
    has_init = state is not None
    mx = bf16
    rmap = lambda c: (lambda b, j: (b * nj + j, c))
    bmap4 = lambda b, j: (b, 0, 0, 0)
    const2 = lambda b, j: (0, 0)
    hk, hv = H * DK, H * DV
    if kind == "hgrn":
        (lb_logits,) = extra
        in_specs = [pl.BlockSpec((tq, hk), rmap(0)), pl.BlockSpec((tq, hk), rmap(1)),
                    pl.BlockSpec((tq, hv), rmap(2)), pl.BlockSpec((tq, hv), rmap(3)),
                    pl.BlockSpec((DEPTH, hk), const2), pl.BlockSpec((1, hv), const2)]
        args = [pr, pr, pr, pr, lb_logits, gain.reshape(1, hv)]
    else:
        w_up, b_gate = extra
        w_up_p = jnp.zeros((LANES, hk), f32).at[:GLA_GATE_RANK].set(w_up)
        in_specs = [pl.BlockSpec((tq, hk), rmap(0)), pl.BlockSpec((tq, hk), rmap(1)),
                    pl.BlockSpec((tq, hv), rmap(1)), pl.BlockSpec((tq, hv), rmap(2)),
                    pl.BlockSpec((tq, LANES), rmap((2 * hk + 2 * hv) // LANES)),
                    pl.BlockSpec((LANES, hk), const2), pl.BlockSpec((1, hk), const2),
                    pl.BlockSpec((1, hv), const2)]
        args = [pr, pr, pr, pr, pr, w_up_p, b_gate.reshape(1, hk), gain.reshape(1, hv)]
    if has_init:
        in_specs.append(pl.BlockSpec((1, H, DK, DV), bmap4))
        args.append(state)
    y, s = pl.pallas_call(
        functools.partial(_gla_kernel, kind=kind, layer=layer, L=L, nchunks=tq // L, H=H, DK=DK, DV=DV,
                          has_init=has_init, mx=mx),
        grid=(B, nj),
        in_specs=in_specs,
        out_specs=[pl.BlockSpec((tq, hv), rmap(0)), pl.BlockSpec((1, H, DK, DV), bmap4)],
        out_shape=[jax.ShapeDtypeStruct((B * T, hv), f32), jax.ShapeDtypeStruct((B, H, DK, DV), f32)],
        scratch_shapes=[pltpu.VMEM((H, DV, DK), f32)],
        compiler_params=_cparams("parallel", "arbitrary"),
        name=kind + "_mix",
    )(*args)
    return y, s


def _gla_block_kernel(*refs, kind, layer, L, nc, H, DK, DV):
    if kind == "hgrn":
        q_ref, f_ref, v_ref, g_ref, lbl_ref, gain_ref = refs[:6]
        rest = refs[6:]
    else:
        q_ref, k_ref, v_ref, g_ref, gr_ref, wup_ref, bgate_ref, gain_ref = refs[:8]
        rest = refs[8:]
    y_ref, s_ref, st_s, qe_s, ke_s, qs_s, kd_s, el_s = rest
    j = pl.program_id(1)

    @pl.when(j == 0)
    def _():
        st_s[...] = jnp.zeros_like(st_s)

    rowi = lax.broadcasted_iota(jnp.int32, (L, L), 0)
    coli = lax.broadcasted_iota(jnp.int32, (L, L), 1)
    causal = coli <= rowi
    tril_b = causal.astype(bf16)

    if kind == "hgrn":
        logits = lbl_ref[...]
        e = jnp.exp(logits - jnp.max(logits, axis=0, keepdims=True))
        sm = e / jnp.sum(e, axis=0, keepdims=True)
        lb = jnp.zeros((1, H * DK), f32)
        for li in range(layer):
            lb = lb + sm[li:li + 1, :]
    else:
        scale = DK ** -0.5

    for c in range(nc):
        rows = slice(c * L, (c + 1) * L)
        if kind == "hgrn":
            qraw = q_ref[rows, :]
            fg = f_ref[rows, :]
            gc = jnp.log(lb + (1.0 - lb) * _sigmoid(fg))
            kc = (1.0 - lb) * _sigmoid(-fg)
            qc = qraw * _sigmoid(qraw)
        else:
            gk = jnp.dot(gr_ref[rows, :].astype(bf16), wup_ref[...].astype(bf16),
                         preferred_element_type=f32) + bgate_ref[...]
            gc = _log_sigmoid(gk) * (1.0 / GLA_GATE_NORMALIZER)
            qc = q_ref[rows, :] * scale
            kc = k_ref[rows, :]
        hi, mid, lo = _split3(gc)
        bc = (jnp.dot(tril_b, hi, preferred_element_type=f32) + jnp.dot(tril_b, mid, preferred_element_type=f32)
              + jnp.dot(tril_b, lo, preferred_element_type=f32))
        b_mid = bc[L // 2 - 1:L // 2, :]
        b_last = bc[L - 1:L, :]
        qe_s[rows, :] = (qc * jnp.exp(bc - b_mid)).astype(bf16)
        ke_s[rows, :] = (kc * jnp.exp(b_mid - bc)).astype(bf16)
        qs_s[rows, :] = (qc * jnp.exp(bc)).astype(bf16)
        kd_s[rows, :] = (kc * jnp.exp(b_last - bc)).astype(bf16)
        el_s[c] = jnp.exp(b_last)

    dn_nt = (((1,), (1,)), ((), ()))
    dn_tn = (((0,), (0,)), ((), ()))
    pairs = [(c, h) for c in range(nc) for h in range(H)]
    rows_of = lambda c: slice(c * L, (c + 1) * L)
    ks_of = lambda h: slice(h * DK, (h + 1) * DK)
    vs_of = lambda h: slice(h * DV, (h + 1) * DV)
    vbs, amats, dsts = {}, {}, {}
    for c, h in pairs:
        rows, ks = rows_of(c), ks_of(h)
        vb = v_ref[rows, vs_of(h)].astype(bf16)
        a = lax.dot_general(qe_s[rows, ks], ke_s[rows, ks], dn_nt, preferred_element_type=f32)
        vbs[c, h] = vb
        amats[c, h] = jnp.where(causal, a, 0.0).astype(bf16)
        dsts[c, h] = lax.dot_general(vb, kd_s[rows, ks], dn_tn, preferred_element_type=f32)
    sts = {}
    for h in range(H):
        st = st_s[h]
        for c in range(nc):
            sts[c, h] = st.astype(bf16)
            st = st * el_s[c][:, ks_of(h)] + dsts[c, h]
        st_s[h] = st
    outs = {}
    for c, h in pairs:
        rows, ks = rows_of(c), ks_of(h)
        outs[c, h] = (lax.dot_general(qs_s[rows, ks], sts[c, h], dn_nt, preferred_element_type=f32)
                      + jnp.dot(amats[c, h], vbs[c, h], preferred_element_type=f32))
    for c, h in pairs:
        rows, vs = rows_of(c), vs_of(h)
        o = outs[c, h]
        yn = o * lax.rsqrt(jnp.mean(o * o, axis=1, keepdims=True) + NORM_EPS) * gain_ref[:, vs]
        gg = g_ref[rows, vs]
        y_ref[rows, vs] = yn * (gg * _sigmoid(gg))

    @pl.when(j == pl.num_programs(1) - 1)
    def _():
        for h in range(H):
            s_ref[0, h] = st_s[h].T


def _gla_mix_fresh(kind, pr, extra, gain, B, T, tq, layer=0):
    if kind == "hgrn":
        H, DK, DV = HGRN_HEADS, HGRN_DK, HGRN_DV
    else:
        H, DK, DV = GLA_HEADS, GLA_DK, GLA_DV
    L = 2 * LIN_CHUNK
    nc = tq // L
    nj = T // tq
    rmap = lambda c: (lambda b, j: (b * nj + j, c))
    bmap4 = lambda b, j: (b, 0, 0, 0)
    const2 = lambda b, j: (0, 0)
    hk, hv = H * DK, H * DV
    if kind == "hgrn":
        (lb_logits,) = extra
        in_specs = [pl.BlockSpec((tq, hk), rmap(0)), pl.BlockSpec((tq, hk), rmap(1)),
                    pl.BlockSpec((tq, hv), rmap(2)), pl.BlockSpec((tq, hv), rmap(3)),
                    pl.BlockSpec((DEPTH, hk), const2), pl.BlockSpec((1, hv), const2)]
        args = [pr, pr, pr, pr, lb_logits, gain.reshape(1, hv)]
    else:
        w_up, b_gate = extra
        w_up_p = jnp.zeros((LANES, hk), f32).at[:GLA_GATE_RANK].set(w_up)
        in_specs = [pl.BlockSpec((tq, hk), rmap(0)), pl.BlockSpec((tq, hk), rmap(1)),
                    pl.BlockSpec((tq, hv), rmap(1)), pl.BlockSpec((tq, hv), rmap(2)),
                    pl.BlockSpec((tq, LANES), rmap((2 * hk + 2 * hv) // LANES)),
                    pl.BlockSpec((LANES, hk), const2), pl.BlockSpec((1, hk), const2),
                    pl.BlockSpec((1, hv), const2)]
        args = [pr, pr, pr, pr, pr, w_up_p, b_gate.reshape(1, hk), gain.reshape(1, hv)]
    return pl.pallas_call(
        functools.partial(_gla_block_kernel, kind=kind, layer=layer, L=L, nc=nc, H=H, DK=DK, DV=DV),
        grid=(B, nj),
        in_specs=in_specs,
        out_specs=[pl.BlockSpec((tq, hv), rmap(0)), pl.BlockSpec((1, H, DK, DV), bmap4)],
        out_shape=[jax.ShapeDtypeStruct((B * T, hv), f32), jax.ShapeDtypeStruct((B, H, DK, DV), f32)],
        scratch_shapes=[pltpu.VMEM((H, DV, DK), f32)] + [pltpu.VMEM((tq, hk), bf16)] * 4
                       + [pltpu.VMEM((nc, 1, hk), f32)],
        compiler_params=_cparams("parallel", "arbitrary"),
        name=kind + "_mix_fresh",
    )(*args)


def _rwkv_pre_kernel(x_ref, aux_ref, sh_ref, g_ref, mu_ref, wrkv_ref, la_ref, lbw_ref, lba_ref, lbg_ref,
                     w0_ref, a0_ref, r_ref, w_ref, k_ref, v_ref, alr_ref, gate_ref, *, long_seq, blocks_per_seq,
                     seq_len):
    g = g_ref[...]
    hn = _rms(x_ref[...], g)
    tm = hn.shape[0]
    rowi = lax.broadcasted_iota(jnp.int32, (tm, 1), 0)
    rolled = pltpu.roll(hn, 1, axis=0)
    if long_seq:
        p_last = _rms(aux_ref[SUBLANES - 1:SUBLANES, :], g)
        at_start = (pl.program_id(0) % blocks_per_seq) == 0
        first = jnp.where(at_start, sh_ref[0], p_last)
        prev = jnp.where(rowi == 0, first, rolled)
    else:
        prev = jnp.where(rowi % seq_len == 0, aux_ref[...], rolled)
    xx = prev - hn

    def lerp(c):
        return (hn + xx * mu_ref[c:c + 1, :]).astype(bf16)

    r_ref[...] = jnp.dot(lerp(0), wrkv_ref[0], preferred_element_type=f32)
    k_ref[...] = jnp.dot(lerp(1), wrkv_ref[1], preferred_element_type=f32)
    v_ref[...] = jnp.dot(lerp(2), wrkv_ref[2], preferred_element_type=f32)
    lw = jnp.tanh(jnp.dot(lerp(3), la_ref[:, 0:64], preferred_element_type=f32))
    wl = -_softplus(-(w0_ref[...] + jnp.dot(lw.astype(bf16), lbw_ref[...], preferred_element_type=f32))) - 0.5
    w_ref[...] = jnp.exp(-jnp.exp(wl))
    la = jnp.dot(lerp(4), la_ref[:, 64:128], preferred_element_type=f32)
    alr_ref[...] = _sigmoid(a0_ref[...] + jnp.dot(la.astype(bf16), lba_ref[...], preferred_element_type=f32))
    lg = _sigmoid(jnp.dot(lerp(5), la_ref[:, 128:256], preferred_element_type=f32))
    gate_ref[...] = jnp.dot(lg.astype(bf16), lbg_ref[...], preferred_element_type=f32)


def _rwkv_pre(x, shift0, g, mu, wrkv, la, lbw, lba, lbg, w0, a0, B, T, tm):
    n, d = x.shape
    long_seq = T % tm == 0
    row = lambda i: (i, 0)
    const2 = lambda i: (0, 0)
    if long_seq:
        bps = T // tm
        sub = tm // SUBLANES
        aux = x
        aux_spec = pl.BlockSpec((SUBLANES, d), lambda i: (jnp.maximum(i * sub - 1, 0), 0))
        sh = shift0.reshape(B, 1, d)
        sh_spec = pl.BlockSpec((1, 1, d), lambda i: (i // bps, 0, 0))
    else:
        assert tm % T == 0
        bps = 1
        aux = jnp.repeat(shift0, T, axis=0)
        aux_spec = pl.BlockSpec((tm, d), row)
        sh = shift0.reshape(B, 1, d)
        sh_spec = pl.BlockSpec((1, 1, d), lambda i: (0, 0, 0))
    out = jax.ShapeDtypeStruct((n, d), f32)
    return pl.pallas_call(
        functools.partial(_rwkv_pre_kernel, long_seq=long_seq, blocks_per_seq=bps, seq_len=T),
        grid=(n // tm,),
        in_specs=[pl.BlockSpec((tm, d), row), aux_spec, sh_spec, pl.BlockSpec((1, d), const2),
                  pl.BlockSpec((6, d), const2), pl.BlockSpec((3, d, d), lambda i: (0, 0, 0)),
                  pl.BlockSpec((d, 256), const2), pl.BlockSpec((64, d), const2), pl.BlockSpec((64, d), const2),
                  pl.BlockSpec((128, d), const2), pl.BlockSpec((1, d), const2), pl.BlockSpec((1, d), const2)],
        out_specs=[pl.BlockSpec((tm, d), row)] * 6,
        out_shape=[out] * 6,
        compiler_params=_cparams("parallel"),
        name="rwkv_pre",
    )(x, aux, sh, g.reshape(1, d), mu, wrkv, la, lbw, lba, lbg, w0.reshape(1, d), a0.reshape(1, d))


def _rwkv_scan_kernel(*refs, tt, has_init, unroll):
    N = RWKV_N
    r_ref, w_ref, k_ref, v_ref, alr_ref, kk_ref, ka_ref, rk_ref, lnw_ref, lnb_ref = refs[:10]
    if has_init:
        s0_ref = refs[10]
        rest = refs[11:]
    else:
        rest = refs[10:]
    z_ref, sout_ref, s_s, y_s, vec_s = rest
    HP = RWKV_HEADS // 2
    j = pl.program_id(1)

    @pl.when(j == 0)
    def _():
        if has_init:
            s_s[...] = s0_ref[...]
        else:
            s_s[...] = jnp.zeros_like(s_s)

    low = lax.broadcasted_iota(jnp.int32, (N, LANES), 1) < N

    def load_pair(ref, t0):
        tiles = []
        for t in (t0, t0 + 1):
            x = ref[:, t, :]
            tiles += [x[:, hp * LANES:(hp + 1) * LANES] for hp in range(HP)]
        xt = jnp.concatenate(tiles, axis=0).T
        ev, od = xt[:N], xt[N:]
        return (jnp.where(low, ev, pltpu.roll(od, N, axis=1)),
                jnp.where(low, pltpu.roll(ev, N, axis=1), od))

    def store_pair(ref, t0, z0, z1):
        ev = jnp.where(low, z0, pltpu.roll(z1, N, axis=1))
        od = jnp.where(low, pltpu.roll(z0, N, axis=1), z1)
        x = jnp.concatenate([ev, od], axis=0).T
        for i, t in enumerate((t0, t0 + 1)):
            tiles = [x[(i * HP + hp) * SUBLANES:(i * HP + hp + 1) * SUBLANES, :] for hp in range(HP)]
            ref[:, t, :] = jnp.concatenate(tiles, axis=1)

    VA, VW, VB, VK, VR, VV = range(6)

    def stage(tk, r, w, k, v, alr):
        kkraw = k * kk_ref[...]
        nrm = jnp.sqrt(jnp.sum(kkraw * kkraw, axis=0, keepdims=True))
        kk = kkraw / jnp.maximum(nrm, 1e-12)
        vec_s[tk, VA] = -kk
        vec_s[tk, VW] = w
        vec_s[tk, VB] = kk * alr
        vec_s[tk, VK] = k * (1.0 + (alr - 1.0) * ka_ref[...])
        vec_s[tk, VR] = r
        vec_s[tk, VV] = v

    def update_state(tk, yk):
        def rows(i, carry):
            for u in range(unroll):
                vi = i * unroll + u
                sv = s_s[vi]
                sa = jnp.sum(sv * vec_s[tk, VA], axis=0, keepdims=True)
                sv = sv * vec_s[tk, VW] + sa * vec_s[tk, VB] + vec_s[tk, VV, pl.ds(vi, 1), :] * vec_s[tk, VK]
                s_s[vi] = sv
                y_s[yk, pl.ds(vi, 1), :] = jnp.sum(sv * vec_s[tk, VR], axis=0, keepdims=True)
            return carry

        lax.fori_loop(0, N // unroll, rows, 0)

    def epilogue(tk, yk):
        y = y_s[yk]
        mean = jnp.mean(y, axis=0, keepdims=True)
        yc = y - mean
        var = jnp.mean(yc * yc, axis=0, keepdims=True)
        yn = yc * lax.rsqrt(var + RWKV_GN_EPS)
        bonus = jnp.sum(vec_s[tk, VR] * vec_s[tk, VK] * rk_ref[...], axis=0, keepdims=True) * vec_s[tk, VV]
        return yn * lnw_ref[...] + lnb_ref[...] + bonus

    def stage_pair(slot, t0):
        streams = [load_pair(ref, t0) for ref in (r_ref, w_ref, k_ref, v_ref, alr_ref)]
        for i in range(2):
            stage(slot + i, *[s[i] for s in streams])

    stage_pair(0, 0)

    def step(p, carry):
        slot = 2 * (p % 2)
        update_state(slot, 0)
        update_state(slot + 1, 1)
        z0 = epilogue(slot, 0)
        z1 = epilogue(slot + 1, 1)
        stage_pair(2 - slot, 2 * jnp.minimum(p + 1, tt // 2 - 1))
        store_pair(z_ref, 2 * p, z0, z1)
        return carry

    lax.fori_loop(0, tt // 2, step, 0)

    @pl.when(j == pl.num_programs(1) - 1)
    def _():
        sout_ref[...] = s_s[...]


RWKV_SEQ_PER_STEP = LANES // RWKV_HEADS


def _rwkv_scan(r, w, k, v, alr, slabs, s0, B, T, tt):
    N = RWKV_N
    G = B // RWKV_SEQ_PER_STEP
    has_init = s0 is not None
    view = lambda a: a.reshape(B, T, D_MODEL)
    tmap = lambda g, j: (g, j, 0)
    smap = lambda g, j: (0, 0, g)
    const2 = lambda g, j: (0, 0)
    stream = pl.BlockSpec((RWKV_SEQ_PER_STEP, tt, D_MODEL), tmap)
    in_specs = [stream] * 5 + [pl.BlockSpec((N, LANES), const2)] * 5
    args = [view(r), view(w), view(k), view(v), view(alr)] + list(slabs)
    if has_init:
        in_specs.append(pl.BlockSpec((N, N, LANES), smap))
        args.append(s0)
    z, s = pl.pallas_call(
        functools.partial(_rwkv_scan_kernel, tt=tt, has_init=has_init, unroll=16),
        grid=(G, T // tt),
        in_specs=in_specs,
        out_specs=[stream, pl.BlockSpec((N, N, LANES), smap)],
        out_shape=[jax.ShapeDtypeStruct((B, T, D_MODEL), f32),
                   jax.ShapeDtypeStruct((N, N, G * LANES), f32)],
        scratch_shapes=[pltpu.VMEM((N, N, LANES), f32), pltpu.VMEM((2, N, LANES), f32),
                        pltpu.VMEM((4, 6, N, LANES), f32)],
        compiler_params=_cparams("parallel", "arbitrary"),
        name="rwkv_scan",
    )(*args)
    return z.reshape(B * T, D_MODEL), s


def _lane_slab(p):
    q = p.reshape(RWKV_HEADS // 2, 2, RWKV_N).transpose(2, 1, 0)
    q = jnp.broadcast_to(q[:, :, :, None], (RWKV_N, 2, RWKV_HEADS // 2, RWKV_SEQ_PER_STEP))
    return q.reshape(RWKV_N, LANES)


def _rwkv_mix(x, p, B, T, state, tm):
    H, N = RWKV_HEADS, RWKV_N
    G = B // RWKV_SEQ_PER_STEP
    if state is None:
        shift0 = jnp.zeros((B, D_MODEL), f32)
        s0 = None
    else:
        s_in, shift0 = state
        s0 = (s_in.reshape(G, RWKV_SEQ_PER_STEP, H // 2, 2, N, N)
              .transpose(4, 5, 0, 3, 2, 1).reshape(N, N, G * LANES))
    r, w, k, v, alr, gate = _rwkv_pre(x, shift0, p["g"], p["mu"], p["wrkv"], p["la"], p["lbw"], p["lba"], p["lbg"],
                                      p["w0"], p["a0"], B, T, tm)
    slabs = [_lane_slab(p[n]) for n in ("k_k", "k_a", "r_k", "ln_w", "ln_b")]
    z, s = _rwkv_scan(r, w, k, v, alr, slabs, s0, B, T, min(T, 32))
    s = (s.reshape(N, N, G, 2, H // 2, RWKV_SEQ_PER_STEP)
         .transpose(2, 5, 4, 3, 0, 1).reshape(B, H, N, N))
    return z, gate, s


def _pad_cols(w, to):
    return jnp.pad(w, ((0, 0), (0, to - w.shape[1])))


def _trunk(x, B, T, states, p):
    n = x.shape[0]
    tm = min(512, n)
    tmm = min(1024, n)
    tq = min(256, T)
    fresh = states is None
    block_ok = fresh and T % tq == 0 and tq % (2 * LIN_CHUNK) == 0
    if not fresh:
        m_c, m_n, m_m, h_s, g_s, r_s, r_sh = states
    new = {}
    for li in range(DEPTH):
        g_mix = p["norm_mix"][li]
        if li == 0:
            pr = _norm_proj(x, g_mix, p["mlstm_w_in"], tmm, 1664)
            if fresh and T % tq == 0 and tq % (2 * MLSTM_CHUNK) == 0:
                y, c, nn, m = _mlstm_mix_fresh(pr, p["mlstm_b_gates"], p["mlstm_norm"], B, T, tq)
            else:
                y, c, nn, m = _mlstm_mix(pr, p["mlstm_b_gates"], p["mlstm_norm"], B, T,
                                         None if fresh else (m_c[0], m_n[0], m_m[0]), tq)
            new["C"], new["n"], new["m"] = c[None], nn[None], m[None]
            x = _out_proj(y, p["mlstm_w_out"], x, tm)
        elif li == 1:
            pr = _norm_proj(x, g_mix, p["hgrn_w_in"], tmm, 2048)
            if block_ok:
                y, s = _gla_mix_fresh("hgrn", pr, (p["hgrn_lb_logits"],), p["hgrn_norm"], B, T, tq, layer=li)
            else:
                y, s = _gla_mix("hgrn", pr, (p["hgrn_lb_logits"],), p["hgrn_norm"], B, T,
                                None if fresh else h_s[0], tq, layer=li)
            new["hS"] = s[None]
            x = _out_proj(y, p["hgrn_w_out"], x, tm)
        elif li == 2:
            pr = _norm_proj(x, g_mix, p["gla_w_in"], tmm, 1664)
            if block_ok:
                y, s = _gla_mix_fresh("gla", pr, (p["gla_w_gate_up"], p["gla_b_gate"]), p["gla_norm"], B, T, tq)
            else:
                y, s = _gla_mix("gla", pr, (p["gla_w_gate_up"], p["gla_b_gate"]), p["gla_norm"], B, T,
                                None if fresh else g_s[0], tq)
            new["gS"] = s[None]
            x = _out_proj(y, p["gla_w_out"], x, tm)
        else:
            rp = dict(p["rwkv"], g=g_mix)
            z, gate, s = _rwkv_mix(x, rp, B, T, None if fresh else (r_s[0], r_sh[0]), min(256, n))
            x_last = x.reshape(B, T, D_MODEL)[:, T - 1, :]
            new["rS"] = s[None]
            new["sh"] = _rmsnorm(x_last, g_mix, B)[None]
            x = _out_proj(z, p["rwkv_w_out"], x, tm, gate=gate)
        x = _ffn(x, p["norm_ffn"][li], p["ffn_w_gate_up"][li], p["ffn_w_down"][li], tm, 1408)
    y = _rmsnorm(x, p["norm_final"], tm)
    return y.reshape(B, T, D_MODEL), (new["C"], new["n"], new["m"], new["hS"], new["gS"], new["rS"], new["sh"])


def kernel(x_prompt, x_sample, state_mlstm_C, state_mlstm_n, state_mlstm_m, state_hgrn_S, state_gla_S, state_rwkv_S, state_rwkv_shift, norm_mix, norm_ffn, norm_final, mlstm_w_in, mlstm_b_gates, mlstm_norm, mlstm_w_out, hgrn_w_in, hgrn_lb_logits, hgrn_norm, hgrn_w_out, gla_w_in, gla_w_gate_up, gla_b_gate, gla_norm, gla_w_out, rwkv_mu, rwkv_w_rkv, rwkv_w_lora_a, rwkv_w_lora_b, rwkv_w0, rwkv_a_lora_a, rwkv_a_lora_b, rwkv_a0, rwkv_g_lora_a, rwkv_g_lora_b, rwkv_k_k, rwkv_k_a, rwkv_r_k, rwkv_ln_w, rwkv_ln_b, rwkv_w_out, ffn_w_gate_up, ffn_w_down):
    cast = lambda w: w.astype(bf16)
    p = dict(
        norm_mix=norm_mix, norm_ffn=norm_ffn, norm_final=norm_final,
        mlstm_w_in=cast(_pad_cols(mlstm_w_in[0], 3328)), mlstm_b_gates=mlstm_b_gates[0], mlstm_norm=mlstm_norm[0],
        mlstm_w_out=cast(mlstm_w_out[0]),
        hgrn_w_in=cast(hgrn_w_in[0]), hgrn_lb_logits=hgrn_lb_logits, hgrn_norm=hgrn_norm[0],
        hgrn_w_out=cast(hgrn_w_out[0]),
        gla_w_in=cast(_pad_cols(gla_w_in[0], 3328)), gla_w_gate_up=gla_w_gate_up[0], gla_b_gate=gla_b_gate[0],
        gla_norm=gla_norm[0], gla_w_out=cast(gla_w_out[0]),
        rwkv=dict(mu=rwkv_mu[0], wrkv=cast(rwkv_w_rkv[0]),
                  la=cast(jnp.concatenate([rwkv_w_lora_a[0], rwkv_a_lora_a[0], rwkv_g_lora_a[0]], axis=1)),
                  lbw=cast(rwkv_w_lora_b[0]), lba=cast(rwkv_a_lora_b[0]), lbg=cast(rwkv_g_lora_b[0]),
                  w0=rwkv_w0[0], a0=rwkv_a0[0], k_k=rwkv_k_k[0], k_a=rwkv_k_a[0], r_k=rwkv_r_k[0].reshape(-1),
                  ln_w=rwkv_ln_w[0], ln_b=rwkv_ln_b[0]),
        rwkv_w_out=cast(rwkv_w_out[0]),
        ffn_w_gate_up=cast(ffn_w_gate_up), ffn_w_down=cast(ffn_w_down),
    )
    bp, tp, _ = x_prompt.shape
    bs, ts, _ = x_sample.shape
    y_p, st_p = _trunk(x_prompt.reshape(bp * tp, D_MODEL), bp, tp, None, p)
    y_s, st_s = _trunk(x_sample.reshape(bs * ts, D_MODEL), bs, ts,
                       (state_mlstm_C, state_mlstm_n, state_mlstm_m, state_hgrn_S, state_gla_S, state_rwkv_S,
                        state_rwkv_shift), p)
    return (y_p, y_s) + st_p + st_s
```

```python
import functools

import jax
import jax.numpy as jnp
from jax import lax
from jax.experimental import pallas as pl
from jax.experimental.pallas import tpu as pltpu

f32 = jnp.float32
bf16 = jnp.bfloat16

D_MODEL = 1024
DEPTH = 4
NORM_EPS = 1e-6

MLSTM_HEADS, MLSTM_DK, MLSTM_DV, MLSTM_CHUNK = 4, 128, 256, 64
HGRN_HEADS, HGRN_DK, HGRN_DV = 8, 128, 128
GLA_HEADS, GLA_DK, GLA_DV = 4, 128, 256
GLA_GATE_RANK = 16
GLA_GATE_NORMALIZER = 16.0
LIN_CHUNK = 32
RWKV_HEADS, RWKV_N = 16, 64
RWKV_GN_EPS = 64e-5
FFN_HIDDEN = 2816

LANES = 128
SUBLANES = 8
BF16_ROWS = 16
VMEM_LIMIT_BYTES = 52 * 1024 * 1024


def _cparams(*sem):
    return pltpu.CompilerParams(dimension_semantics=sem, vmem_limit_bytes=VMEM_LIMIT_BYTES)


def _chunk_len(t, cap):
    return max(d for d in range(1, min(cap, t) + 1) if t % d == 0)


def _rms(x, g):
    ms = jnp.mean(x * x, axis=-1, keepdims=True)
    return x * lax.rsqrt(ms + NORM_EPS) * g


def _sigmoid(x):
    return jax.nn.sigmoid(x)


def _softplus(x):
    return jnp.maximum(x, 0.0) + jnp.log1p(jnp.exp(-jnp.abs(x)))


def _log_sigmoid(x):
    return -_softplus(-x)


def _mm(a, b, mx=bf16):
    return jnp.dot(a.astype(mx), b.astype(mx), preferred_element_type=f32)


def _mm_nt(a, b, mx=bf16):
    return lax.dot_general(a.astype(mx), b.astype(mx), (((1,), (1,)), ((), ())), preferred_element_type=f32)


def _mm_tn(a, b, mx=bf16):
    return lax.dot_general(a.astype(mx), b.astype(mx), (((0,), (0,)), ((), ())), preferred_element_type=f32)


def _split3(x):
    hi = x.astype(bf16)
    r1 = x - hi.astype(f32)
    mid = r1.astype(bf16)
    lo = (r1 - mid.astype(f32)).astype(bf16)
    return hi, mid, lo


def _cumsum_rows(x, tril_b):
    hi, mid, lo = _split3(x)
    return (jnp.dot(tril_b, hi, preferred_element_type=f32)
            + jnp.dot(tril_b, mid, preferred_element_type=f32)
            + jnp.dot(tril_b, lo, preferred_element_type=f32))


def _norm_proj_kernel(x_ref, g_ref, w_ref, o_ref, hn_ref):
    @pl.when(pl.program_id(1) == 0)
    def _():
        hn_ref[...] = _rms(x_ref[...], g_ref[...]).astype(bf16)

    o_ref[...] = jnp.dot(hn_ref[...], w_ref[...], preferred_element_type=f32)


def _norm_proj(x, g, w, tm, tn):
    n, d = x.shape
    e = w.shape[1]
    return pl.pallas_call(
        _norm_proj_kernel,
        grid=(n // tm, e // tn),
        in_specs=[pl.BlockSpec((tm, d), lambda i, j: (i, 0)),
                  pl.BlockSpec((1, d), lambda i, j: (0, 0)),
                  pl.BlockSpec((d, tn), lambda i, j: (0, j))],
        out_specs=pl.BlockSpec((tm, tn), lambda i, j: (i, j)),
        out_shape=jax.ShapeDtypeStruct((n, e), f32),
        scratch_shapes=[pltpu.VMEM((tm, d), bf16)],
        compiler_params=_cparams("parallel", "arbitrary"),
        name="norm_proj",
    )(x, g.reshape(1, d), w)


def _out_proj_kernel(*refs, gated):
    if gated:
        y_ref, gate_ref, w_ref, res_ref, o_ref = refs
        y = y_ref[...] * gate_ref[...]
    else:
        y_ref, w_ref, res_ref, o_ref = refs
        y = y_ref[...]
    o_ref[...] = res_ref[...] + jnp.dot(y.astype(bf16), w_ref[...], preferred_element_type=f32)


def _out_proj(y, w, res, tm, gate=None):
    n, e = y.shape
    d = w.shape[1]
    row = lambda i: (i, 0)
    args = [y] + ([gate] if gate is not None else []) + [w, res]
    in_specs = ([pl.BlockSpec((tm, e), row)] + ([pl.BlockSpec((tm, e), row)] if gate is not None else [])
                + [pl.BlockSpec((e, d), lambda i: (0, 0)), pl.BlockSpec((tm, d), row)])
    return pl.pallas_call(
        functools.partial(_out_proj_kernel, gated=gate is not None),
        grid=(n // tm,),
        in_specs=in_specs,
        out_specs=pl.BlockSpec((tm, d), row),
        out_shape=jax.ShapeDtypeStruct((n, d), f32),
        compiler_params=_cparams("parallel"),
        name="out_proj",
    )(*args)


def _ffn_kernel(x_ref, g_ref, wg_ref, wu_ref, wd_ref, o_ref, hn_ref, acc_ref):
    j = pl.program_id(1)

    @pl.when(j == 0)
    def _():
        hn_ref[...] = _rms(x_ref[...], g_ref[...]).astype(bf16)
        acc_ref[...] = jnp.zeros_like(acc_ref)

    h = hn_ref[...]
    gt = jnp.dot(h, wg_ref[...], preferred_element_type=f32)
    up = jnp.dot(h, wu_ref[...], preferred_element_type=f32)
    act = (gt * _sigmoid(gt) * up).astype(bf16)
    acc_ref[...] += jnp.dot(act, wd_ref[...], preferred_element_type=f32)

    @pl.when(j == pl.num_programs(1) - 1)
    def _():
        o_ref[...] = x_ref[...] + acc_ref[...]


def _ffn(x, g, w_gu, w_down, tm, tf):
    n, d = x.shape
    nf = FFN_HIDDEN // tf
    return pl.pallas_call(
        _ffn_kernel,
        grid=(n // tm, nf),
        in_specs=[pl.BlockSpec((tm, d), lambda i, j: (i, 0)),
                  pl.BlockSpec((1, d), lambda i, j: (0, 0)),
                  pl.BlockSpec((d, tf), lambda i, j: (0, j)),
                  pl.BlockSpec((d, tf), lambda i, j: (0, j + nf)),
                  pl.BlockSpec((tf, d), lambda i, j: (j, 0))],
        out_specs=pl.BlockSpec((tm, d), lambda i, j: (i, 0)),
        out_shape=jax.ShapeDtypeStruct((n, d), f32),
        scratch_shapes=[pltpu.VMEM((tm, d), bf16), pltpu.VMEM((tm, d), f32)],
        compiler_params=_cparams("parallel", "arbitrary"),
        name="ffn",
    )(x, g.reshape(1, d), w_gu, w_gu, w_down)


def _rmsnorm_kernel(x_ref, g_ref, o_ref):
    o_ref[...] = _rms(x_ref[...], g_ref[...])


def _rmsnorm(x, g, tm):
    n, d = x.shape
    return pl.pallas_call(
        _rmsnorm_kernel,
        grid=(n // tm,),
        in_specs=[pl.BlockSpec((tm, d), lambda i: (i, 0)), pl.BlockSpec((1, d), lambda i: (0, 0))],
        out_specs=pl.BlockSpec((tm, d), lambda i: (i, 0)),
        out_shape=jax.ShapeDtypeStruct((n, d), f32),
        compiler_params=_cparams("parallel"),
        name="rmsnorm",
    )(x, g.reshape(1, d))


def _mlstm_kernel(*refs, L, nchunks, has_init, mx):
    H, DK, DV = MLSTM_HEADS, MLSTM_DK, MLSTM_DV
    q_ref, k_ref, v_ref, o_ref, gt_ref, bias_ref, gain_ref = refs[:7]
    if has_init:
        c0_ref, n0_ref, m0_ref = refs[7:10]
        rest = refs[10:]
    else:
        rest = refs[7:]
    y_ref, c_ref, n_ref, m_ref, c_s, n_s, m_s = rest
    j = pl.program_id(1)

    @pl.when(j == 0)
    def _():
        if has_init:
            c_s[...] = c0_ref[0]
            n_s[...] = n0_ref[0]
            m_s[...] = m0_ref[0]
        else:
            c_s[...] = jnp.zeros_like(c_s)
            n_s[...] = jnp.zeros_like(n_s)
            m_s[...] = jnp.zeros_like(m_s)

    Lp = max(L, BF16_ROWS)
    rowi = lax.broadcasted_iota(jnp.int32, (Lp, Lp), 0)
    coli = lax.broadcasted_iota(jnp.int32, (Lp, Lp), 1)
    causal = coli <= rowi
    eye = coli == rowi
    scale = DK ** -0.5

    def to_row(col):
        return jnp.sum(jnp.where(eye, col, 0.0), axis=0, keepdims=True)

    def pad_rows(x, value=0.0):
        if Lp == L:
            return x
        return jnp.concatenate([x, jnp.full((Lp - L, x.shape[1]), value, f32)], axis=0)

    def chunk(c, carry):
        rows = pl.ds(pl.multiple_of(c * L, L), L)
        gts_raw = gt_ref[rows, :] + bias_ref[...]
        lsg = pad_rows(_log_sigmoid(gts_raw))
        gts = pad_rows(gts_raw, -jnp.inf)
        for h in range(H):
            i_col = gts[:, h:h + 1]
            f_col = lsg[:, H + h:H + h + 1]
            f_row = to_row(f_col)
            fc_col = jnp.sum(jnp.where(causal, f_row, 0.0), axis=1, keepdims=True)
            fc_row = to_row(fc_col)
            i_row = to_row(i_col)
            m = m_s[h]
            dm = jnp.where(causal, fc_col - fc_row + i_row, -jnp.inf)
            g = fc_col + m
            mt = jnp.maximum(g, jnp.max(dm, axis=1, keepdims=True))
            p = jnp.exp(dm - mt)
            qc = pad_rows(q_ref[rows, h * DK:(h + 1) * DK] * scale)
            kc = pad_rows(k_ref[rows, h * DK:(h + 1) * DK])
            vc = pad_rows(v_ref[rows, h * DV:(h + 1) * DV])
            s = _mm_nt(qc, kc, mx) * p
            eg = jnp.exp(g - mt)
            cst = c_s[h]
            nst = n_s[h]
            num = eg * _mm_nt(qc, cst, mx) + _mm(s, vc, mx)
            den = eg * jnp.sum(qc * nst, axis=1, keepdims=True) + jnp.sum(s, axis=1, keepdims=True)
            hh = num / jnp.maximum(jnp.abs(den), jnp.exp(-mt))
            m_new = mt[L - 1:L, :]
            f_last = fc_col[L - 1:L, :]
            w_c = jnp.exp(f_last + m - m_new)
            w_j = jnp.exp(f_last - fc_col + i_col - m_new)
            c_s[h] = w_c * cst + _mm_tn(vc * w_j, kc, mx)
            n_s[h] = w_c * nst + jnp.sum(w_j * kc, axis=0, keepdims=True)
            m_s[h] = m_new
            hh = hh[:L]
            yn = hh * lax.rsqrt(jnp.mean(hh * hh, axis=1, keepdims=True) + NORM_EPS)
            yn = yn * gain_ref[:, h * DV:(h + 1) * DV]
            y_ref[rows, h * DV:(h + 1) * DV] = _sigmoid(o_ref[rows, h * DV:(h + 1) * DV]) * yn
        return carry

    lax.fori_loop(0, nchunks, chunk, 0)

    @pl.when(j == pl.num_programs(1) - 1)
    def _():
        c_ref[0] = c_s[...]
        n_ref[0] = n_s[...]
        m_ref[0] = m_s[...]


def _mlstm_mix(pr, b_gates, gain, B, T, state, tq):
    H, DK, DV = MLSTM_HEADS, MLSTM_DK, MLSTM_DV
    L = _chunk_len(T, MLSTM_CHUNK)
    nj = T // tq
    has_init = state is not None
    mx = bf16
    rmap = lambda c: (lambda b, j: (b * nj + j, c))
    bmap4 = lambda b, j: (b, 0, 0, 0)
    bias = jnp.zeros((1, LANES), f32).at[0, :2 * H].set(b_gates)
    in_specs = [pl.BlockSpec((tq, H * DK), rmap(0)), pl.BlockSpec((tq, H * DK), rmap(1)),
                pl.BlockSpec((tq, H * DV), rmap(1)), pl.BlockSpec((tq, H * DV), rmap(2)),
                pl.BlockSpec((tq, LANES), rmap((2 * H * DK + 2 * H * DV) // LANES)),
                pl.BlockSpec((1, LANES), lambda b, j: (0, 0)),
                pl.BlockSpec((1, H * DV), lambda b, j: (0, 0))]
    args = [pr, pr, pr, pr, pr, bias, gain.reshape(1, H * DV)]
    if has_init:
        c0, n0, m0 = state
        in_specs += [pl.BlockSpec((1, H, DV, DK), bmap4), pl.BlockSpec((1, H, 1, DK), bmap4),
                     pl.BlockSpec((1, H, 1, 1), bmap4)]
        args += [c0, n0.reshape(B, H, 1, DK), m0.reshape(B, H, 1, 1)]
    y, c, n, m = pl.pallas_call(
        functools.partial(_mlstm_kernel, L=L, nchunks=tq // L, has_init=has_init, mx=mx),
        grid=(B, nj),
        in_specs=in_specs,
        out_specs=[pl.BlockSpec((tq, H * DV), rmap(0)), pl.BlockSpec((1, H, DV, DK), bmap4),
                   pl.BlockSpec((1, H, 1, DK), bmap4), pl.BlockSpec((1, H, 1, 1), bmap4)],
        out_shape=[jax.ShapeDtypeStruct((B * T, H * DV), f32), jax.ShapeDtypeStruct((B, H, DV, DK), f32),
                   jax.ShapeDtypeStruct((B, H, 1, DK), f32), jax.ShapeDtypeStruct((B, H, 1, 1), f32)],
        scratch_shapes=[pltpu.VMEM((H, DV, DK), f32), pltpu.VMEM((H, 1, DK), f32), pltpu.VMEM((H, 1, 1), f32)],
        compiler_params=_cparams("parallel", "arbitrary"),
        name="mlstm_mix",
    )(*args)
    return y, c, n.reshape(B, H, DK), m.reshape(B, H)


def _mlstm_block_kernel(q_ref, k_ref, v_ref, o_ref, gt_ref, bias_ref, gain_ref, x_ref, wout_ref,
                        xo_ref, c_ref, n_ref, m_ref, c_s, n_s, m_s, y_ref, *, L, nc):
    H, DK, DV = MLSTM_HEADS, MLSTM_DK, MLSTM_DV
    P = 2 * L
    j = pl.program_id(1)

    @pl.when(j == 0)
    def _():
        c_s[...] = jnp.zeros_like(c_s)
        n_s[...] = jnp.zeros_like(n_s)
        m_s[...] = jnp.zeros_like(m_s)

    rowi = lax.broadcasted_iota(jnp.int32, (L, L), 0)
    coli = lax.broadcasted_iota(jnp.int32, (L, L), 1)
    causal = coli <= rowi
    r2 = lax.broadcasted_iota(jnp.int32, (P, P), 0)
    c2 = lax.broadcasted_iota(jnp.int32, (P, P), 1)
    pair_tril = ((c2 <= r2) & ((c2 >= L) == (r2 >= L))).astype(bf16)
    scale = DK ** -0.5

    gates = []
    for pp in range(nc // 2):
        g = gt_ref[pp * P:(pp + 1) * P, :] + bias_ref[...]
        hi, mid, lo = _split3(_log_sigmoid(g))
        fcum = (jnp.dot(pair_tril, hi, preferred_element_type=f32)
                + jnp.dot(pair_tril, mid, preferred_element_type=f32)
                + jnp.dot(pair_tril, lo, preferred_element_type=f32))
        g_t = g.T
        f_t = fcum.T
        for half in range(2):
            rs = slice(half * L, (half + 1) * L)
            gates.append((g[rs, :], fcum[rs, :], g_t[:, rs], f_t[:, rs]))

    pairs = [(c, h) for c in range(nc) for h in range(H)]
    rows_of = lambda c: slice(c * L, (c + 1) * L)
    ks_of = lambda h: slice(h * DK, (h + 1) * DK)
    vs_of = lambda h: slice(h * DV, (h + 1) * DV)
    dms, rmax, fcols, icols, qks = {}, {}, {}, {}, {}
    for c, h in pairs:
        g_c, f_c, g_r, f_r = gates[c]
        icols[c, h] = g_c[:, h:h + 1]
        fcols[c, h] = f_c[:, H + h:H + h + 1]
        dm = jnp.where(causal, fcols[c, h] - f_r[H + h:H + h + 1, :] + g_r[h:h + 1, :], -jnp.inf)
        dms[c, h] = dm
        rmax[c, h] = jnp.max(dm, axis=1, keepdims=True)
        qb = (q_ref[rows_of(c), ks_of(h)] * scale).astype(bf16)
        qks[c, h] = _mm_nt(qb, k_ref[rows_of(c), ks_of(h)], bf16)
    m_in, m_out = {}, {}
    for h in range(H):
        m = m_s[h]
        for c in range(nc):
            m_in[c, h] = m
            m = jnp.maximum(fcols[c, h][L - 1:L, :] + m, rmax[c, h][L - 1:L, :])
            m_out[c, h] = m
        m_s[h] = m
    mts, egs, ss, ssums, wcs, dcs, dns = {}, {}, {}, {}, {}, {}, {}
    for c, h in pairs:
        fc_col = fcols[c, h]
        gg = fc_col + m_in[c, h]
        mt = jnp.maximum(gg, rmax[c, h])
        mts[c, h] = mt
        egs[c, h] = jnp.exp(gg - mt)
        s = qks[c, h] * jnp.exp(dms[c, h] - mt)
        ssums[c, h] = jnp.sum(s, axis=1, keepdims=True)
        ss[c, h] = s.astype(bf16)
        f_last = fc_col[L - 1:L, :]
        wcs[c, h] = jnp.exp(f_last + m_in[c, h] - m_out[c, h])
        w_j = jnp.exp(f_last - fc_col + icols[c, h] - m_out[c, h])
        kc = k_ref[rows_of(c), ks_of(h)]
        dcs[c, h] = _mm_tn(v_ref[rows_of(c), vs_of(h)] * w_j, kc, bf16)
        dns[c, h] = jnp.sum(w_j * kc, axis=0, keepdims=True)
    c_in, n_in = {}, {}
    for h in range(H):
        cst = c_s[h]
        nst = n_s[h]
        for c in range(nc):
            c_in[c, h] = cst.astype(bf16)
            n_in[c, h] = nst
            cst = wcs[c, h] * cst + dcs[c, h]
            nst = wcs[c, h] * nst + dns[c, h]
        c_s[h] = cst
        n_s[h] = nst
    hhs = {}
    for c, h in pairs:
        rows, ks, vs = rows_of(c), ks_of(h), vs_of(h)
        qc = q_ref[rows, ks] * scale
        num = egs[c, h] * _mm_nt(qc, c_in[c, h], bf16) + _mm(ss[c, h], v_ref[rows, vs], bf16)
        den = egs[c, h] * jnp.sum(qc * n_in[c, h], axis=1, keepdims=True) + ssums[c, h]
        hhs[c, h] = num / jnp.maximum(jnp.abs(den), jnp.exp(-mts[c, h]))
    for c, h in pairs:
        rows, vs = rows_of(c), vs_of(h)
        hh = hhs[c, h]
        yn = hh * lax.rsqrt(jnp.mean(hh * hh, axis=1, keepdims=True) + NORM_EPS) * gain_ref[:, vs]
        y_ref[rows, vs] = _sigmoid(o_ref[rows, vs]) * yn
    xo_ref[...] = x_ref[...] + jnp.dot(y_ref[...].astype(bf16), wout_ref[...], preferred_element_type=f32)

    @pl.when(j == pl.num_programs(1) - 1)
    def _():
        c_ref[0] = c_s[...]
        n_ref[0] = n_s[...]
        m_ref[0] = m_s[...]


def _mlstm_mix_fresh(pr, b_gates, gain, x, w_out, B, T, tq):
    H, DK, DV = MLSTM_HEADS, MLSTM_DK, MLSTM_DV
    L = MLSTM_CHUNK
    nj = T // tq
    rmap = lambda c: (lambda b, j: (b * nj + j, c))
    bmap4 = lambda b, j: (b, 0, 0, 0)
    bias = jnp.zeros((1, LANES), f32).at[0, :2 * H].set(b_gates)
    y, c, n, m = pl.pallas_call(
        functools.partial(_mlstm_block_kernel, L=L, nc=tq // L),
        grid=(B, nj),
        in_specs=[pl.BlockSpec((tq, H * DK), rmap(0)), pl.BlockSpec((tq, H * DK), rmap(1)),
                  pl.BlockSpec((tq, H * DV), rmap(1)), pl.BlockSpec((tq, H * DV), rmap(2)),
                  pl.BlockSpec((tq, LANES), rmap((2 * H * DK + 2 * H * DV) // LANES)),
                  pl.BlockSpec((1, LANES), lambda b, j: (0, 0)),
                  pl.BlockSpec((1, H * DV), lambda b, j: (0, 0)),
                  pl.BlockSpec((tq, D_MODEL), rmap(0)),
                  pl.BlockSpec((H * DV, D_MODEL), lambda b, j: (0, 0))],
        out_specs=[pl.BlockSpec((tq, D_MODEL), rmap(0)), pl.BlockSpec((1, H, DV, DK), bmap4),
                   pl.BlockSpec((1, H, 1, DK), bmap4), pl.BlockSpec((1, H, 1, 1), bmap4)],
        out_shape=[jax.ShapeDtypeStruct((B * T, D_MODEL), f32), jax.ShapeDtypeStruct((B, H, DV, DK), f32),
                   jax.ShapeDtypeStruct((B, H, 1, DK), f32), jax.ShapeDtypeStruct((B, H, 1, 1), f32)],
        scratch_shapes=[pltpu.VMEM((H, DV, DK), f32), pltpu.VMEM((H, 1, DK), f32), pltpu.VMEM((H, 1, 1), f32),
                        pltpu.VMEM((tq, H * DV), f32)],
        compiler_params=_cparams("parallel", "arbitrary"),
        name="mlstm_mix_fresh",
    )(pr, pr, pr, pr, pr, bias, gain.reshape(1, H * DV), x, w_out)
    return y, c, n.reshape(B, H, DK), m.reshape(B, H)


def _gla_kernel(*refs, kind, layer, L, nchunks, H, DK, DV, has_init, mx):
    if kind == "hgrn":
        q_ref, f_ref, v_ref, g_ref, lbl_ref, gain_ref = refs[:6]
        rest = refs[6:]
    else:
        q_ref, k_ref, v_ref, g_ref, gr_ref, wup_ref, bgate_ref, gain_ref = refs[:8]
        rest = refs[8:]
    if has_init:
        s0_ref = rest[0]
        rest = rest[1:]
    y_ref, s_ref, st_s = rest
    j = pl.program_id(1)

    @pl.when(j == 0)
    def _():
        for h in range(H):
            if has_init:
                st_s[h] = s0_ref[0, h].T
            else:
                st_s[h] = jnp.zeros((DV, DK), f32)

    Lp = max(L, BF16_ROWS)
    rowi = lax.broadcasted_iota(jnp.int32, (Lp, Lp), 0)
    coli = lax.broadcasted_iota(jnp.int32, (Lp, Lp), 1)
    causal = coli <= rowi
    tril_b = causal.astype(bf16)

    def pad_rows(x):
        if Lp == L:
            return x
        return jnp.concatenate([x, jnp.zeros((Lp - L, x.shape[1]), f32)], axis=0)

    if kind == "hgrn":
        logits = lbl_ref[...]
        e = jnp.exp(logits - jnp.max(logits, axis=0, keepdims=True))
        sm = e / jnp.sum(e, axis=0, keepdims=True)
        lb_all = jnp.zeros((1, H * DK), f32)
        for li in range(layer):
            lb_all = lb_all + sm[li:li + 1, :]
    else:
        scale = DK ** -0.5

    def chunk(c, carry):
        rows = pl.ds(pl.multiple_of(c * L, L), L)
        if kind == "gla":
            gk = _mm(pad_rows(gr_ref[rows, :]), wup_ref[...], mx)[:L] + bgate_ref[...]
            ld_all = _log_sigmoid(gk) * (1.0 / GLA_GATE_NORMALIZER)
        for h in range(H):
            ks = slice(h * DK, (h + 1) * DK)
            vs = slice(h * DV, (h + 1) * DV)
            if kind == "hgrn":
                qraw = q_ref[rows, ks]
                fg = f_ref[rows, ks]
                lb = lb_all[:, ks]
                gc = jnp.log(lb + (1.0 - lb) * _sigmoid(fg))
                kc = (1.0 - lb) * _sigmoid(-fg)
                qc = qraw * _sigmoid(qraw)
            else:
                qc = q_ref[rows, ks] * scale
                kc = k_ref[rows, ks]
                gc = ld_all[:, ks]
            vc = pad_rows(v_ref[rows, vs])
            qc, kc, gc = pad_rows(qc), pad_rows(kc), pad_rows(gc)
            bc = _cumsum_rows(gc, tril_b)
            qe = qc * jnp.exp(bc)
            ke = kc * jnp.exp(-bc)
            a = jnp.where(causal, _mm_nt(qe, ke, mx), 0.0)
            st = st_s[h]
            o = (_mm_nt(qe, st, mx) + _mm(a, vc, mx))[:L]
            b_last = bc[L - 1:L, :]
            kdec = kc * jnp.exp(b_last - bc)
            st_s[h] = st * jnp.exp(b_last) + _mm_tn(vc, kdec, mx)
            yn = o * lax.rsqrt(jnp.mean(o * o, axis=1, keepdims=True) + NORM_EPS) * gain_ref[:, vs]
            gg = g_ref[rows, vs]
            y_ref[rows, vs] = yn * (gg * _sigmoid(gg))
        return carry

    lax.fori_loop(0, nchunks, chunk, 0)

    @pl.when(j == pl.num_programs(1) - 1)
    def _():
        for h in range(H):
            s_ref[0, h] = st_s[h].T


def _gla_mix(kind, pr, extra, gain, B, T, state, tq, layer=0):
    if kind == "hgrn":
        H, DK, DV = HGRN_HEADS, HGRN_DK, HGRN_DV
    else:
        H, DK, DV = GLA_HEADS, GLA_DK, GLA_DV
    L = _chunk_len(T, LIN_CHUNK)
    nj = T // tq
    has_init = state is not None
    mx = bf16
    rmap = lambda c: (lambda b, j: (b * nj + j, c))
    bmap4 = lambda b, j: (b, 0, 0, 0)
    const2 = lambda b, j: (0, 0)
    hk, hv = H * DK, H * DV
    if kind == "hgrn":
        (lb_logits,) = extra
        in_specs = [pl.BlockSpec((tq, hk), rmap(0)), pl.BlockSpec((tq, hk), rmap(1)),
                    pl.BlockSpec((tq, hv), rmap(2)), pl.BlockSpec((tq, hv), rmap(3)),
                    pl.BlockSpec((DEPTH, hk), const2), pl.BlockSpec((1, hv), const2)]
        args = [pr, pr, pr, pr, lb_logits, gain.reshape(1, hv)]
    else:
        w_up, b_gate = extra
        w_up_p = jnp.zeros((LANES, hk), f32).at[:GLA_GATE_RANK].set(w_up)
        in_specs = [pl.BlockSpec((tq, hk), rmap(0)), pl.BlockSpec((tq, hk), rmap(1)),
                    pl.BlockSpec((tq, hv), rmap(1)), pl.BlockSpec((tq, hv), rmap(2)),
                    pl.BlockSpec((tq, LANES), rmap((2 * hk + 2 * hv) // LANES)),
                    pl.BlockSpec((LANES, hk), const2), pl.BlockSpec((1, hk), const2),
                    pl.BlockSpec((1, hv), const2)]
        args = [pr, pr, pr, pr, pr, w_up_p, b_gate.reshape(1, hk), gain.reshape(1, hv)]
    if has_init:
        in_specs.append(pl.BlockSpec((1, H, DK, DV), bmap4))
        args.append(state)
    y, s = pl.pallas_call(
        functools.partial(_gla_kernel, kind=kind, layer=layer, L=L, nchunks=tq // L, H=H, DK=DK, DV=DV,
                          has_init=has_init, mx=mx),
        grid=(B, nj),
        in_specs=in_specs,
        out_specs=[pl.BlockSpec((tq, hv), rmap(0)), pl.BlockSpec((1, H, DK, DV), bmap4)],
        out_shape=[jax.ShapeDtypeStruct((B * T, hv), f32), jax.ShapeDtypeStruct((B, H, DK, DV), f32)],
        scratch_shapes=[pltpu.VMEM((H, DV, DK), f32)],
        compiler_params=_cparams("parallel", "arbitrary"),
        name=kind + "_mix",
    )(*args)
    return y, s


def _gla_block_kernel(*refs, kind, layer, L, nc, H, DK, DV):
    if kind == "hgrn":
        q_ref, f_ref, v_ref, g_ref, lbl_ref, gain_ref = refs[:6]
        rest = refs[6:]
    else:
        q_ref, k_ref, v_ref, g_ref, gr_ref, wup_ref, bgate_ref, gain_ref = refs[:8]
        rest = refs[8:]
    x_ref, wout_ref, xo_ref, s_ref, st_s, qe_s, ke_s, qs_s, kd_s, el_s, y_ref = rest
    j = pl.program_id(1)

    @pl.when(j == 0)
    def _():
        st_s[...] = jnp.zeros_like(st_s)

    rowi = lax.broadcasted_iota(jnp.int32, (L, L), 0)
    coli = lax.broadcasted_iota(jnp.int32, (L, L), 1)
    causal = coli <= rowi
    tril_b = causal.astype(bf16)

    if kind == "hgrn":
        logits = lbl_ref[...]
        e = jnp.exp(logits - jnp.max(logits, axis=0, keepdims=True))
        sm = e / jnp.sum(e, axis=0, keepdims=True)
        lb = jnp.zeros((1, H * DK), f32)
        for li in range(layer):
            lb = lb + sm[li:li + 1, :]
    else:
        scale = DK ** -0.5

    for c in range(nc):
        rows = slice(c * L, (c + 1) * L)
        if kind == "hgrn":
            qraw = q_ref[rows, :]
            fg = f_ref[rows, :]
            gc = jnp.log(lb + (1.0 - lb) * _sigmoid(fg))
            kc = (1.0 - lb) * _sigmoid(-fg)
            qc = qraw * _sigmoid(qraw)
        else:
            gk = jnp.dot(gr_ref[rows, :].astype(bf16), wup_ref[...].astype(bf16),
                         preferred_element_type=f32) + bgate_ref[...]
            gc = _log_sigmoid(gk) * (1.0 / GLA_GATE_NORMALIZER)
            qc = q_ref[rows, :] * scale
            kc = k_ref[rows, :]
        hi, mid, lo = _split3(gc)
        bc = (jnp.dot(tril_b, hi, preferred_element_type=f32) + jnp.dot(tril_b, mid, preferred_element_type=f32)
              + jnp.dot(tril_b, lo, preferred_element_type=f32))
        b_mid = bc[L // 2 - 1:L // 2, :]
        b_last = bc[L - 1:L, :]
        qe_s[rows, :] = (qc * jnp.exp(bc - b_mid)).astype(bf16)
        ke_s[rows, :] = (kc * jnp.exp(b_mid - bc)).astype(bf16)
        qs_s[rows, :] = (qc * jnp.exp(bc)).astype(bf16)
        kd_s[rows, :] = (kc * jnp.exp(b_last - bc)).astype(bf16)
        el_s[c] = jnp.exp(b_last)

    dn_nt = (((1,), (1,)), ((), ()))
    dn_tn = (((0,), (0,)), ((), ()))
    pairs = [(c, h) for c in range(nc) for h in range(H)]
    rows_of = lambda c: slice(c * L, (c + 1) * L)
    ks_of = lambda h: slice(h * DK, (h + 1) * DK)
    vs_of = lambda h: slice(h * DV, (h + 1) * DV)
    vbs, amats, dsts = {}, {}, {}
    for c, h in pairs:
        rows, ks = rows_of(c), ks_of(h)
        vb = v_ref[rows, vs_of(h)].astype(bf16)
        a = lax.dot_general(qe_s[rows, ks], ke_s[rows, ks], dn_nt, preferred_element_type=f32)
        vbs[c, h] = vb
        amats[c, h] = jnp.where(causal, a, 0.0).astype(bf16)
        dsts[c, h] = lax.dot_general(vb, kd_s[rows, ks], dn_tn, preferred_element_type=f32)
    sts = {}
    for h in range(H):
        st = st_s[h]
        for c in range(nc):
            sts[c, h] = st.astype(bf16)
            st = st * el_s[c][:, ks_of(h)] + dsts[c, h]
        st_s[h] = st
    outs = {}
    for c, h in pairs:
        rows, ks = rows_of(c), ks_of(h)
        outs[c, h] = (lax.dot_general(qs_s[rows, ks], sts[c, h], dn_nt, preferred_element_type=f32)
                      + jnp.dot(amats[c, h], vbs[c, h], preferred_element_type=f32))
    for c, h in pairs:
        rows, vs = rows_of(c), vs_of(h)
        o = outs[c, h]
        yn = o * lax.rsqrt(jnp.mean(o * o, axis=1, keepdims=True) + NORM_EPS) * gain_ref[:, vs]
        gg = g_ref[rows, vs]
        y_ref[rows, vs] = yn * (gg * _sigmoid(gg))
    xo_ref[...] = x_ref[...] + jnp.dot(y_ref[...].astype(bf16), wout_ref[...], preferred_element_type=f32)

    @pl.when(j == pl.num_programs(1) - 1)
    def _():
        for h in range(H):
            s_ref[0, h] = st_s[h].T


def _gla_mix_fresh(kind, pr, extra, gain, x, w_out, B, T, tq, layer=0):
    if kind == "hgrn":
        H, DK, DV = HGRN_HEADS, HGRN_DK, HGRN_DV
    else:
        H, DK, DV = GLA_HEADS, GLA_DK, GLA_DV
    L = 2 * LIN_CHUNK
    nc = tq // L
    nj = T // tq
    rmap = lambda c: (lambda b, j: (b * nj + j, c))
    bmap4 = lambda b, j: (b, 0, 0, 0)
    const2 = lambda b, j: (0, 0)
    hk, hv = H * DK, H * DV
    if kind == "hgrn":
        (lb_logits,) = extra
        in_specs = [pl.BlockSpec((tq, hk), rmap(0)), pl.BlockSpec((tq, hk), rmap(1)),
                    pl.BlockSpec((tq, hv), rmap(2)), pl.BlockSpec((tq, hv), rmap(3)),
                    pl.BlockSpec((DEPTH, hk), const2), pl.BlockSpec((1, hv), const2)]
        args = [pr, pr, pr, pr, lb_logits, gain.reshape(1, hv)]
    else:
        w_up, b_gate = extra
        w_up_p = jnp.zeros((LANES, hk), f32).at[:GLA_GATE_RANK].set(w_up)
        in_specs = [pl.BlockSpec((tq, hk), rmap(0)), pl.BlockSpec((tq, hk), rmap(1)),
                    pl.BlockSpec((tq, hv), rmap(1)), pl.BlockSpec((tq, hv), rmap(2)),
                    pl.BlockSpec((tq, LANES), rmap((2 * hk + 2 * hv) // LANES)),
                    pl.BlockSpec((LANES, hk), const2), pl.BlockSpec((1, hk), const2),
                    pl.BlockSpec((1, hv), const2)]
        args = [pr, pr, pr, pr, pr, w_up_p, b_gate.reshape(1, hk), gain.reshape(1, hv)]
    in_specs += [pl.BlockSpec((tq, D_MODEL), rmap(0)), pl.BlockSpec((hv, D_MODEL), const2)]
    args += [x, w_out]
    return pl.pallas_call(
        functools.partial(_gla_block_kernel, kind=kind, layer=layer, L=L, nc=nc, H=H, DK=DK, DV=DV),
        grid=(B, nj),
        in_specs=in_specs,
        out_specs=[pl.BlockSpec((tq, D_MODEL), rmap(0)), pl.BlockSpec((1, H, DK, DV), bmap4)],
        out_shape=[jax.ShapeDtypeStruct((B * T, D_MODEL), f32), jax.ShapeDtypeStruct((B, H, DK, DV), f32)],
        scratch_shapes=[pltpu.VMEM((H, DV, DK), f32)] + [pltpu.VMEM((tq, hk), bf16)] * 4
                       + [pltpu.VMEM((nc, 1, hk), f32), pltpu.VMEM((tq, hv), f32)],
        compiler_params=_cparams("parallel", "arbitrary"),
        name=kind + "_mix_fresh",
    )(*args)


def _rwkv_pre_kernel(x_ref, aux_ref, sh_ref, g_ref, mu_ref, wrkv_ref, la_ref, lbw_ref, lba_ref, lbg_ref,
                     w0_ref, a0_ref, r_ref, w_ref, k_ref, v_ref, alr_ref, gate_ref, *, long_seq, blocks_per_seq,
                     seq_len):
    g = g_ref[...]
    hn = _rms(x_ref[...], g)
    tm = hn.shape[0]
    rowi = lax.broadcasted_iota(jnp.int32, (tm, 1), 0)
    rolled = pltpu.roll(hn, 1, axis=0)
    if long_seq:
        p_last = _rms(aux_ref[SUBLANES - 1:SUBLANES, :], g)
        at_start = (pl.program_id(0) % blocks_per_seq) == 0
        first = jnp.where(at_start, sh_ref[0], p_last)
        prev = jnp.where(rowi == 0, first, rolled)
    else:
        prev = jnp.where(rowi % seq_len == 0, aux_ref[...], rolled)
    xx = prev - hn

    def lerp(c):
        return (hn + xx * mu_ref[c:c + 1, :]).astype(bf16)

    r_ref[...] = jnp.dot(lerp(0), wrkv_ref[0], preferred_element_type=f32)
    k_ref[...] = jnp.dot(lerp(1), wrkv_ref[1], preferred_element_type=f32)
    v_ref[...] = jnp.dot(lerp(2), wrkv_ref[2], preferred_element_type=f32)
    lw = jnp.tanh(jnp.dot(lerp(3), la_ref[:, 0:64], preferred_element_type=f32))
    wl = -_softplus(-(w0_ref[...] + jnp.dot(lw.astype(bf16), lbw_ref[...], preferred_element_type=f32))) - 0.5
    w_ref[...] = jnp.exp(-jnp.exp(wl))
    la = jnp.dot(lerp(4), la_ref[:, 64:128], preferred_element_type=f32)
    alr_ref[...] = _sigmoid(a0_ref[...] + jnp.dot(la.astype(bf16), lba_ref[...], preferred_element_type=f32))
    lg = _sigmoid(jnp.dot(lerp(5), la_ref[:, 128:256], preferred_element_type=f32))
    gate_ref[...] = jnp.dot(lg.astype(bf16), lbg_ref[...], preferred_element_type=f32)


def _rwkv_pre(x, shift0, g, mu, wrkv, la, lbw, lba, lbg, w0, a0, B, T, tm):
    n, d = x.shape
    long_seq = T % tm == 0
    row = lambda i: (i, 0)
    const2 = lambda i: (0, 0)
    if long_seq:
        bps = T // tm
        sub = tm // SUBLANES
        aux = x
        aux_spec = pl.BlockSpec((SUBLANES, d), lambda i: (jnp.maximum(i * sub - 1, 0), 0))
        sh = shift0.reshape(B, 1, d)
        sh_spec = pl.BlockSpec((1, 1, d), lambda i: (i // bps, 0, 0))
    else:
        assert tm % T == 0
        bps = 1
        aux = jnp.repeat(shift0, T, axis=0)
        aux_spec = pl.BlockSpec((tm, d), row)
        sh = shift0.reshape(B, 1, d)
        sh_spec = pl.BlockSpec((1, 1, d), lambda i: (0, 0, 0))
    out = jax.ShapeDtypeStruct((n, d), f32)
    return pl.pallas_call(
        functools.partial(_rwkv_pre_kernel, long_seq=long_seq, blocks_per_seq=bps, seq_len=T),
        grid=(n // tm,),
        in_specs=[pl.BlockSpec((tm, d), row), aux_spec, sh_spec, pl.BlockSpec((1, d), const2),
                  pl.BlockSpec((6, d), const2), pl.BlockSpec((3, d, d), lambda i: (0, 0, 0)),
                  pl.BlockSpec((d, 256), const2), pl.BlockSpec((64, d), const2), pl.BlockSpec((64, d), const2),
                  pl.BlockSpec((128, d), const2), pl.BlockSpec((1, d), const2), pl.BlockSpec((1, d), const2)],
        out_specs=[pl.BlockSpec((tm, d), row)] * 6,
        out_shape=[out] * 6,
        compiler_params=_cparams("parallel"),
        name="rwkv_pre",
    )(x, aux, sh, g.reshape(1, d), mu, wrkv, la, lbw, lba, lbg, w0.reshape(1, d), a0.reshape(1, d))


def _rwkv_scan_kernel(*refs, tt, has_init, unroll):
    N = RWKV_N
    (r_ref, w_ref, k_ref, v_ref, alr_ref, gate_ref, x_ref, wout_ref,
     kk_ref, ka_ref, rk_ref, lnw_ref, lnb_ref) = refs[:13]
    if has_init:
        s0_ref = refs[13]
        rest = refs[14:]
    else:
        rest = refs[13:]
    xo_ref, sout_ref, s_s, y_s, vec_s, z_ref = rest
    HP = RWKV_HEADS // 2
    j = pl.program_id(1)

    @pl.when(j == 0)
    def _():
        if has_init:
            s_s[...] = s0_ref[...]
        else:
            s_s[...] = jnp.zeros_like(s_s)

    low = lax.broadcasted_iota(jnp.int32, (N, LANES), 1) < N

    def load_pair(ref, t0):
        tiles = []
        for t in (t0, t0 + 1):
            x = ref[:, t, :]
            tiles += [x[:, hp * LANES:(hp + 1) * LANES] for hp in range(HP)]
        xt = jnp.concatenate(tiles, axis=0).T
        ev, od = xt[:N], xt[N:]
        return (jnp.where(low, ev, pltpu.roll(od, N, axis=1)),
                jnp.where(low, pltpu.roll(ev, N, axis=1), od))

    def store_pair(ref, t0, z0, z1):
        ev = jnp.where(low, z0, pltpu.roll(z1, N, axis=1))
        od = jnp.where(low, pltpu.roll(z0, N, axis=1), z1)
        x = jnp.concatenate([ev, od], axis=0).T
        for i, t in enumerate((t0, t0 + 1)):
            tiles = [x[(i * HP + hp) * SUBLANES:(i * HP + hp + 1) * SUBLANES, :] for hp in range(HP)]
            ref[:, t, :] = jnp.concatenate(tiles, axis=1)

    VA, VW, VB, VK, VR, VV = range(6)

    def stage(tk, r, w, k, v, alr):
        kkraw = k * kk_ref[...]
        nrm = jnp.sqrt(jnp.sum(kkraw * kkraw, axis=0, keepdims=True))
        kk = kkraw / jnp.maximum(nrm, 1e-12)
        vec_s[tk, VA] = -kk
        vec_s[tk, VW] = w
        vec_s[tk, VB] = kk * alr
        vec_s[tk, VK] = k * (1.0 + (alr - 1.0) * ka_ref[...])
        vec_s[tk, VR] = r
        vec_s[tk, VV] = v

    def update_state(tk, yk):
        def key_row(which, kc):
            return vec_s[tk, which, pl.ds(kc, 1), :]

        def project(i, sa):
            for u in range(unroll):
                kc = i * unroll + u
                sa = sa + s_s[kc] * key_row(VA, kc)
            return sa

        sa = lax.fori_loop(0, N // unroll, project, jnp.zeros((N, LANES), f32))
        vv = vec_s[tk, VV]

        def update(i, y):
            for u in range(unroll):
                kc = i * unroll + u
                sk = s_s[kc] * key_row(VW, kc) + sa * key_row(VB, kc) + vv * key_row(VK, kc)
                s_s[kc] = sk
                y = y + sk * key_row(VR, kc)
            return y

        y_s[yk] = lax.fori_loop(0, N // unroll, update, jnp.zeros((N, LANES), f32))

    def epilogue(tk, yk):
        y = y_s[yk]
        mean = jnp.mean(y, axis=0, keepdims=True)
        yc = y - mean
        var = jnp.mean(yc * yc, axis=0, keepdims=True)
        yn = yc * lax.rsqrt(var + RWKV_GN_EPS)
        bonus = jnp.sum(vec_s[tk, VR] * vec_s[tk, VK] * rk_ref[...], axis=0, keepdims=True) * vec_s[tk, VV]
        return yn * lnw_ref[...] + lnb_ref[...] + bonus

    def stage_pair(slot, t0):
        streams = [load_pair(ref, t0) for ref in (r_ref, w_ref, k_ref, v_ref, alr_ref)]
        for i in range(2):
            stage(slot + i, *[s[i] for s in streams])

    stage_pair(0, 0)

    def step(p, carry):
        slot = 2 * (p % 2)
        update_state(slot, 0)
        update_state(slot + 1, 1)
        z0 = epilogue(slot, 0)
        z1 = epilogue(slot + 1, 1)
        stage_pair(2 - slot, 2 * jnp.minimum(p + 1, tt // 2 - 1))
        store_pair(z_ref, 2 * p, z0, z1)
        return carry

    lax.fori_loop(0, tt // 2, step, 0)

    rows = RWKV_SEQ_PER_STEP * tt
    zg = (z_ref[...] * gate_ref[...]).reshape(rows, D_MODEL).astype(bf16)
    proj = jnp.dot(zg, wout_ref[...], preferred_element_type=f32)
    xo_ref[...] = x_ref[...] + proj.reshape(RWKV_SEQ_PER_STEP, tt, D_MODEL)

    @pl.when(j == pl.num_programs(1) - 1)
    def _():
        sout_ref[...] = s_s[...]


RWKV_SEQ_PER_STEP = LANES // RWKV_HEADS


def _rwkv_scan(r, w, k, v, alr, gate, x, w_out, slabs, s0, B, T, tt):
    N = RWKV_N
    G = B // RWKV_SEQ_PER_STEP
    has_init = s0 is not None
    view = lambda a: a.reshape(B, T, D_MODEL)
    tmap = lambda g, j: (g, j, 0)
    smap = lambda g, j: (0, 0, g)
    const2 = lambda g, j: (0, 0)
    stream = pl.BlockSpec((RWKV_SEQ_PER_STEP, tt, D_MODEL), tmap)
    in_specs = [stream] * 7 + [pl.BlockSpec((D_MODEL, D_MODEL), const2)] + [pl.BlockSpec((N, LANES), const2)] * 5
    args = [view(r), view(w), view(k), view(v), view(alr), view(gate), view(x), w_out] + list(slabs)
    if has_init:
        in_specs.append(pl.BlockSpec((N, N, LANES), smap))
        args.append(s0)
    z, s = pl.pallas_call(
        functools.partial(_rwkv_scan_kernel, tt=tt, has_init=has_init, unroll=16),
        grid=(G, T // tt),
        in_specs=in_specs,
        out_specs=[stream, pl.BlockSpec((N, N, LANES), smap)],
        out_shape=[jax.ShapeDtypeStruct((B, T, D_MODEL), f32),
                   jax.ShapeDtypeStruct((N, N, G * LANES), f32)],
        scratch_shapes=[pltpu.VMEM((N, N, LANES), f32), pltpu.VMEM((2, N, LANES), f32),
                        pltpu.VMEM((4, 6, N, LANES), f32), pltpu.VMEM((RWKV_SEQ_PER_STEP, tt, D_MODEL), f32)],
        compiler_params=_cparams("parallel", "arbitrary"),
        name="rwkv_scan",
    )(*args)
    return z.reshape(B * T, D_MODEL), s


def _lane_slab(p):
    q = p.reshape(RWKV_HEADS // 2, 2, RWKV_N).transpose(2, 1, 0)
    q = jnp.broadcast_to(q[:, :, :, None], (RWKV_N, 2, RWKV_HEADS // 2, RWKV_SEQ_PER_STEP))
    return q.reshape(RWKV_N, LANES)


def _rwkv_mix(x, p, B, T, state, tm):
    H, N = RWKV_HEADS, RWKV_N
    G = B // RWKV_SEQ_PER_STEP
    if state is None:
        shift0 = jnp.zeros((B, D_MODEL), f32)
        s0 = None
    else:
        s_in, shift0 = state
        s0 = (s_in.reshape(G, RWKV_SEQ_PER_STEP, H // 2, 2, N, N)
              .transpose(5, 4, 0, 3, 2, 1).reshape(N, N, G * LANES))
    r, w, k, v, alr, gate = _rwkv_pre(x, shift0, p["g"], p["mu"], p["wrkv"], p["la"], p["lbw"], p["lba"], p["lbg"],
                                      p["w0"], p["a0"], B, T, tm)
    slabs = [_lane_slab(p[n]) for n in ("k_k", "k_a", "r_k", "ln_w", "ln_b")]
    x_new, s = _rwkv_scan(r, w, k, v, alr, gate, x, p["w_out"], slabs, s0, B, T, min(T, 32))
    s = (s.reshape(N, N, G, 2, H // 2, RWKV_SEQ_PER_STEP)
         .transpose(2, 5, 4, 3, 1, 0).reshape(B, H, N, N))
    return x_new, s


def _pad_cols(w, to):
    return jnp.pad(w, ((0, 0), (0, to - w.shape[1])))


def _trunk(x, B, T, states, p):
    n = x.shape[0]
    tm = min(512, n)
    tmm = min(1024, n)
    tq = min(256, T)
    fresh = states is None
    block_ok = fresh and T % tq == 0 and tq % (2 * LIN_CHUNK) == 0
    if not fresh:
        m_c, m_n, m_m, h_s, g_s, r_s, r_sh = states
    new = {}
    for li in range(DEPTH):
        g_mix = p["norm_mix"][li]
        if li == 0:
            pr = _norm_proj(x, g_mix, p["mlstm_w_in"], tmm, 1664)
            if fresh and T % tq == 0 and tq % (2 * MLSTM_CHUNK) == 0:
                x, c, nn, m = _mlstm_mix_fresh(pr, p["mlstm_b_gates"], p["mlstm_norm"], x, p["mlstm_w_out"],
                                               B, T, tq)
            else:
                y, c, nn, m = _mlstm_mix(pr, p["mlstm_b_gates"], p["mlstm_norm"], B, T,
                                         None if fresh else (m_c[0], m_n[0], m_m[0]), tq)
                x = _out_proj(y, p["mlstm_w_out"], x, tm)
            new["C"], new["n"], new["m"] = c[None], nn[None], m[None]
        elif li == 1:
            pr = _norm_proj(x, g_mix, p["hgrn_w_in"], tmm, 2048)
            if block_ok:
                x, s = _gla_mix_fresh("hgrn", pr, (p["hgrn_lb_logits"],), p["hgrn_norm"], x, p["hgrn_w_out"],
                                      B, T, tq, layer=li)
            else:
                y, s = _gla_mix("hgrn", pr, (p["hgrn_lb_logits"],), p["hgrn_norm"], B, T,
                                None if fresh else h_s[0], tq, layer=li)
                x = _out_proj(y, p["hgrn_w_out"], x, tm)
            new["hS"] = s[None]
        elif li == 2:
            pr = _norm_proj(x, g_mix, p["gla_w_in"], tmm, 1664)
            if block_ok:
                x, s = _gla_mix_fresh("gla", pr, (p["gla_w_gate_up"], p["gla_b_gate"]), p["gla_norm"], x,
                                      p["gla_w_out"], B, T, tq)
            else:
                y, s = _gla_mix("gla", pr, (p["gla_w_gate_up"], p["gla_b_gate"]), p["gla_norm"], B, T,
                                None if fresh else g_s[0], tq)
                x = _out_proj(y, p["gla_w_out"], x, tm)
            new["gS"] = s[None]
        else:
            rp = dict(p["rwkv"], g=g_mix, w_out=p["rwkv_w_out"])
            x_last = x.reshape(B, T, D_MODEL)[:, T - 1, :]
            new["sh"] = _rmsnorm(x_last, g_mix, B)[None]
            x, s = _rwkv_mix(x, rp, B, T, None if fresh else (r_s[0], r_sh[0]), min(256, n))
            new["rS"] = s[None]
        x = _ffn(x, p["norm_ffn"][li], p["ffn_w_gate_up"][li], p["ffn_w_down"][li], tm, 1408)
    y = _rmsnorm(x, p["norm_final"], tm)
    return y.reshape(B, T, D_MODEL), (new["C"], new["n"], new["m"], new["hS"], new["gS"], new["rS"], new["sh"])


def kernel(x_prompt, x_sample, state_mlstm_C, state_mlstm_n, state_mlstm_m, state_hgrn_S, state_gla_S, state_rwkv_S, state_rwkv_shift, norm_mix, norm_ffn, norm_final, mlstm_w_in, mlstm_b_gates, mlstm_norm, mlstm_w_out, hgrn_w_in, hgrn_lb_logits, hgrn_norm, hgrn_w_out, gla_w_in, gla_w_gate_up, gla_b_gate, gla_norm, gla_w_out, rwkv_mu, rwkv_w_rkv, rwkv_w_lora_a, rwkv_w_lora_b, rwkv_w0, rwkv_a_lora_a, rwkv_a_lora_b, rwkv_a0, rwkv_g_lora_a, rwkv_g_lora_b, rwkv_k_k, rwkv_k_a, rwkv_r_k, rwkv_ln_w, rwkv_ln_b, rwkv_w_out, ffn_w_gate_up, ffn_w_down):
    cast = lambda w: w.astype(bf16)
    p = dict(
        norm_mix=norm_mix, norm_ffn=norm_ffn, norm_final=norm_final,
        mlstm_w_in=cast(_pad_cols(mlstm_w_in[0], 3328)), mlstm_b_gates=mlstm_b_gates[0], mlstm_norm=mlstm_norm[0],
        mlstm_w_out=cast(mlstm_w_out[0]),
        hgrn_w_in=cast(hgrn_w_in[0]), hgrn_lb_logits=hgrn_lb_logits, hgrn_norm=hgrn_norm[0],
        hgrn_w_out=cast(hgrn_w_out[0]),
        gla_w_in=cast(_pad_cols(gla_w_in[0], 3328)), gla_w_gate_up=gla_w_gate_up[0], gla_b_gate=gla_b_gate[0],
        gla_norm=gla_norm[0], gla_w_out=cast(gla_w_out[0]),
        rwkv=dict(mu=rwkv_mu[0], wrkv=cast(rwkv_w_rkv[0]),
                  la=cast(jnp.concatenate([rwkv_w_lora_a[0], rwkv_a_lora_a[0], rwkv_g_lora_a[0]], axis=1)),
                  lbw=cast(rwkv_w_lora_b[0]), lba=cast(rwkv_a_lora_b[0]), lbg=cast(rwkv_g_lora_b[0]),
                  w0=rwkv_w0[0], a0=rwkv_a0[0], k_k=rwkv_k_k[0], k_a=rwkv_k_a[0], r_k=rwkv_r_k[0].reshape(-1),
                  ln_w=rwkv_ln_w[0], ln_b=rwkv_ln_b[0]),
        rwkv_w_out=cast(rwkv_w_out[0]),
        ffn_w_gate_up=cast(ffn_w_gate_up), ffn_w_down=cast(ffn_w_down),
    )
    bp, tp, _ = x_prompt.shape
    bs, ts, _ = x_sample.shape
    y_p, st_p = _trunk(x_prompt.reshape(bp * tp, D_MODEL), bp, tp, None, p)
    y_s, st_s = _trunk(x_sample.reshape(bs * ts, D_MODEL), bs, ts,
                       (state_mlstm_C, state_mlstm_n, state_mlstm_m, state_hgrn_S, state_gla_S, state_rwkv_S,
                        state_rwkv_shift), p)
    return (y_p, y_s) + st_p + st_s
```

```python
import functools

import jax
import jax.numpy as jnp
from jax import lax
from jax.experimental import pallas as pl
from jax.experimental.pallas import tpu as pltpu

f32 = jnp.float32
bf16 = jnp.bfloat16

D_MODEL = 1024
DEPTH = 4
NORM_EPS = 1e-6

MLSTM_HEADS, MLSTM_DK, MLSTM_DV, MLSTM_CHUNK = 4, 128, 256, 64
HGRN_HEADS, HGRN_DK, HGRN_DV = 8, 128, 128
GLA_HEADS, GLA_DK, GLA_DV = 4, 128, 256
GLA_GATE_RANK = 16
GLA_GATE_NORMALIZER = 16.0
LIN_CHUNK = 32
RWKV_HEADS, RWKV_N = 16, 64
RWKV_GN_EPS = 64e-5
FFN_HIDDEN = 2816

LANES = 128
SUBLANES = 8
BF16_ROWS = 16
SEQS_PER_STEP = 4
VMEM_LIMIT_BYTES = 52 * 1024 * 1024


def _cparams(*sem):
    return pltpu.CompilerParams(dimension_semantics=sem, vmem_limit_bytes=VMEM_LIMIT_BYTES)


def _chunk_len(t, cap):
    return max(d for d in range(1, min(cap, t) + 1) if t % d == 0)


def _rms(x, g):
    ms = jnp.mean(x * x, axis=-1, keepdims=True)
    return x * lax.rsqrt(ms + NORM_EPS) * g


def _sigmoid(x):
    return jax.nn.sigmoid(x)


def _softplus(x):
    return jnp.maximum(x, 0.0) + jnp.log1p(jnp.exp(-jnp.abs(x)))


def _log_sigmoid(x):
    return -_softplus(-x)


def _mm(a, b, mx=bf16):
    return jnp.dot(a.astype(mx), b.astype(mx), preferred_element_type=f32)


def _mm_nt(a, b, mx=bf16):
    return lax.dot_general(a.astype(mx), b.astype(mx), (((1,), (1,)), ((), ())), preferred_element_type=f32)


def _mm_tn(a, b, mx=bf16):
    return lax.dot_general(a.astype(mx), b.astype(mx), (((0,), (0,)), ((), ())), preferred_element_type=f32)


def _split3(x):
    hi = x.astype(bf16)
    r1 = x - hi.astype(f32)
    mid = r1.astype(bf16)
    lo = (r1 - mid.astype(f32)).astype(bf16)
    return hi, mid, lo


def _cumsum_rows(x, tril_b):
    hi, mid, lo = _split3(x)
    return (jnp.dot(tril_b, hi, preferred_element_type=f32)
            + jnp.dot(tril_b, mid, preferred_element_type=f32)
            + jnp.dot(tril_b, lo, preferred_element_type=f32))


def _norm_proj_kernel(x_ref, g_ref, w_ref, o_ref, hn_ref):
    @pl.when(pl.program_id(1) == 0)
    def _():
        hn_ref[...] = _rms(x_ref[...], g_ref[...]).astype(bf16)

    o_ref[...] = jnp.dot(hn_ref[...], w_ref[...], preferred_element_type=f32)


def _norm_proj(x, g, w, tm, tn):
    n, d = x.shape
    e = w.shape[1]
    return pl.pallas_call(
        _norm_proj_kernel,
        grid=(n // tm, e // tn),
        in_specs=[pl.BlockSpec((tm, d), lambda i, j: (i, 0)),
                  pl.BlockSpec((1, d), lambda i, j: (0, 0)),
                  pl.BlockSpec((d, tn), lambda i, j: (0, j))],
        out_specs=pl.BlockSpec((tm, tn), lambda i, j: (i, j)),
        out_shape=jax.ShapeDtypeStruct((n, e), f32),
        scratch_shapes=[pltpu.VMEM((tm, d), bf16)],
        compiler_params=_cparams("parallel", "arbitrary"),
        name="norm_proj",
    )(x, g.reshape(1, d), w)


def _out_proj_kernel(*refs, gated):
    if gated:
        y_ref, gate_ref, w_ref, res_ref, o_ref = refs
        y = y_ref[...] * gate_ref[...]
    else:
        y_ref, w_ref, res_ref, o_ref = refs
        y = y_ref[...]
    o_ref[...] = res_ref[...] + jnp.dot(y.astype(bf16), w_ref[...], preferred_element_type=f32)


def _out_proj(y, w, res, tm, gate=None):
    n, e = y.shape
    d = w.shape[1]
    row = lambda i: (i, 0)
    args = [y] + ([gate] if gate is not None else []) + [w, res]
    in_specs = ([pl.BlockSpec((tm, e), row)] + ([pl.BlockSpec((tm, e), row)] if gate is not None else [])
                + [pl.BlockSpec((e, d), lambda i: (0, 0)), pl.BlockSpec((tm, d), row)])
    return pl.pallas_call(
        functools.partial(_out_proj_kernel, gated=gate is not None),
        grid=(n // tm,),
        in_specs=in_specs,
        out_specs=pl.BlockSpec((tm, d), row),
        out_shape=jax.ShapeDtypeStruct((n, d), f32),
        compiler_params=_cparams("parallel"),
        name="out_proj",
    )(*args)


def _ffn_kernel(x_ref, g_ref, wg_ref, wu_ref, wd_ref, o_ref, hn_ref, acc_ref):
    j = pl.program_id(1)

    @pl.when(j == 0)
    def _():
        hn_ref[...] = _rms(x_ref[...], g_ref[...]).astype(bf16)
        acc_ref[...] = jnp.zeros_like(acc_ref)

    h = hn_ref[...]
    gt = jnp.dot(h, wg_ref[...], preferred_element_type=f32)
    up = jnp.dot(h, wu_ref[...], preferred_element_type=f32)
    act = (gt * _sigmoid(gt) * up).astype(bf16)
    acc_ref[...] += jnp.dot(act, wd_ref[...], preferred_element_type=f32)

    @pl.when(j == pl.num_programs(1) - 1)
    def _():
        o_ref[...] = x_ref[...] + acc_ref[...]


def _ffn(x, g, w_gu, w_down, tm, tf):
    n, d = x.shape
    nf = FFN_HIDDEN // tf
    return pl.pallas_call(
        _ffn_kernel,
        grid=(n // tm, nf),
        in_specs=[pl.BlockSpec((tm, d), lambda i, j: (i, 0)),
                  pl.BlockSpec((1, d), lambda i, j: (0, 0)),
                  pl.BlockSpec((d, tf), lambda i, j: (0, j)),
                  pl.BlockSpec((d, tf), lambda i, j: (0, j + nf)),
                  pl.BlockSpec((tf, d), lambda i, j: (j, 0))],
        out_specs=pl.BlockSpec((tm, d), lambda i, j: (i, 0)),
        out_shape=jax.ShapeDtypeStruct((n, d), f32),
        scratch_shapes=[pltpu.VMEM((tm, d), bf16), pltpu.VMEM((tm, d), f32)],
        compiler_params=_cparams("parallel", "arbitrary"),
        name="ffn",
    )(x, g.reshape(1, d), w_gu, w_gu, w_down)


def _rmsnorm_kernel(x_ref, g_ref, o_ref):
    o_ref[...] = _rms(x_ref[...], g_ref[...])


def _rmsnorm(x, g, tm):
    n, d = x.shape
    return pl.pallas_call(
        _rmsnorm_kernel,
        grid=(n // tm,),
        in_specs=[pl.BlockSpec((tm, d), lambda i: (i, 0)), pl.BlockSpec((1, d), lambda i: (0, 0))],
        out_specs=pl.BlockSpec((tm, d), lambda i: (i, 0)),
        out_shape=jax.ShapeDtypeStruct((n, d), f32),
        compiler_params=_cparams("parallel"),
        name="rmsnorm",
    )(x, g.reshape(1, d))


def _mlstm_kernel(*refs, L, nchunks, has_init, mx):
    H, DK, DV = MLSTM_HEADS, MLSTM_DK, MLSTM_DV
    q_ref, k_ref, v_ref, o_ref, gt_ref, bias_ref, gain_ref = refs[:7]
    if has_init:
        c0_ref, n0_ref, m0_ref = refs[7:10]
        rest = refs[10:]
    else:
        rest = refs[7:]
    y_ref, c_ref, n_ref, m_ref, c_s, n_s, m_s = rest
    j = pl.program_id(1)

    @pl.when(j == 0)
    def _():
        if has_init:
            c_s[...] = c0_ref[0]
            n_s[...] = n0_ref[0]
            m_s[...] = m0_ref[0]
        else:
            c_s[...] = jnp.zeros_like(c_s)
            n_s[...] = jnp.zeros_like(n_s)
            m_s[...] = jnp.zeros_like(m_s)

    Lp = max(L, BF16_ROWS)
    rowi = lax.broadcasted_iota(jnp.int32, (Lp, Lp), 0)
    coli = lax.broadcasted_iota(jnp.int32, (Lp, Lp), 1)
    causal = coli <= rowi
    eye = coli == rowi
    scale = DK ** -0.5

    def to_row(col):
        return jnp.sum(jnp.where(eye, col, 0.0), axis=0, keepdims=True)

    def pad_rows(x, value=0.0):
        if Lp == L:
            return x
        return jnp.concatenate([x, jnp.full((Lp - L, x.shape[1]), value, f32)], axis=0)

    def chunk(c, carry):
        rows = pl.ds(pl.multiple_of(c * L, L), L)
        gts_raw = gt_ref[rows, :] + bias_ref[...]
        lsg = pad_rows(_log_sigmoid(gts_raw))
        gts = pad_rows(gts_raw, -jnp.inf)
        for h in range(H):
            i_col = gts[:, h:h + 1]
            f_col = lsg[:, H + h:H + h + 1]
            f_row = to_row(f_col)
            fc_col = jnp.sum(jnp.where(causal, f_row, 0.0), axis=1, keepdims=True)
            fc_row = to_row(fc_col)
            i_row = to_row(i_col)
            m = m_s[h]
            dm = jnp.where(causal, fc_col - fc_row + i_row, -jnp.inf)
            g = fc_col + m
            mt = jnp.maximum(g, jnp.max(dm, axis=1, keepdims=True))
            p = jnp.exp(dm - mt)
            qc = pad_rows(q_ref[rows, h * DK:(h + 1) * DK] * scale)
            kc = pad_rows(k_ref[rows, h * DK:(h + 1) * DK])
            vc = pad_rows(v_ref[rows, h * DV:(h + 1) * DV])
            s = _mm_nt(qc, kc, mx) * p
            eg = jnp.exp(g - mt)
            cst = c_s[h]
            nst = n_s[h]
            num = eg * _mm_nt(qc, cst, mx) + _mm(s, vc, mx)
            den = eg * jnp.sum(qc * nst, axis=1, keepdims=True) + jnp.sum(s, axis=1, keepdims=True)
            hh = num / jnp.maximum(jnp.abs(den), jnp.exp(-mt))
            m_new = mt[L - 1:L, :]
            f_last = fc_col[L - 1:L, :]
            w_c = jnp.exp(f_last + m - m_new)
            w_j = jnp.exp(f_last - fc_col + i_col - m_new)
            c_s[h] = w_c * cst + _mm_tn(vc * w_j, kc, mx)
            n_s[h] = w_c * nst + jnp.sum(w_j * kc, axis=0, keepdims=True)
            m_s[h] = m_new
            hh = hh[:L]
            yn = hh * lax.rsqrt(jnp.mean(hh * hh, axis=1, keepdims=True) + NORM_EPS)
            yn = yn * gain_ref[:, h * DV:(h + 1) * DV]
            y_ref[rows, h * DV:(h + 1) * DV] = _sigmoid(o_ref[rows, h * DV:(h + 1) * DV]) * yn
        return carry

    lax.fori_loop(0, nchunks, chunk, 0)

    @pl.when(j == pl.num_programs(1) - 1)
    def _():
        c_ref[0] = c_s[...]
        n_ref[0] = n_s[...]
        m_ref[0] = m_s[...]


def _mlstm_mix(pr, b_gates, gain, B, T, state, tq):
    H, DK, DV = MLSTM_HEADS, MLSTM_DK, MLSTM_DV
    L = _chunk_len(T, MLSTM_CHUNK)
    nj = T // tq
    has_init = state is not None
    mx = bf16
    rmap = lambda c: (lambda b, j: (b * nj + j, c))
    bmap4 = lambda b, j: (b, 0, 0, 0)
    bias = jnp.zeros((1, LANES), f32).at[0, :2 * H].set(b_gates)
    in_specs = [pl.BlockSpec((tq, H * DK), rmap(0)), pl.BlockSpec((tq, H * DK), rmap(1)),
                pl.BlockSpec((tq, H * DV), rmap(1)), pl.BlockSpec((tq, H * DV), rmap(2)),
                pl.BlockSpec((tq, LANES), rmap((2 * H * DK + 2 * H * DV) // LANES)),
                pl.BlockSpec((1, LANES), lambda b, j: (0, 0)),
                pl.BlockSpec((1, H * DV), lambda b, j: (0, 0))]
    args = [pr, pr, pr, pr, pr, bias, gain.reshape(1, H * DV)]
    if has_init:
        c0, n0, m0 = state
        in_specs += [pl.BlockSpec((1, H, DV, DK), bmap4), pl.BlockSpec((1, H, 1, DK), bmap4),
                     pl.BlockSpec((1, H, 1, 1), bmap4)]
        args += [c0, n0.reshape(B, H, 1, DK), m0.reshape(B, H, 1, 1)]
    y, c, n, m = pl.pallas_call(
        functools.partial(_mlstm_kernel, L=L, nchunks=tq // L, has_init=has_init, mx=mx),
        grid=(B, nj),
        in_specs=in_specs,
        out_specs=[pl.BlockSpec((tq, H * DV), rmap(0)), pl.BlockSpec((1, H, DV, DK), bmap4),
                   pl.BlockSpec((1, H, 1, DK), bmap4), pl.BlockSpec((1, H, 1, 1), bmap4)],
        out_shape=[jax.ShapeDtypeStruct((B * T, H * DV), f32), jax.ShapeDtypeStruct((B, H, DV, DK), f32),
                   jax.ShapeDtypeStruct((B, H, 1, DK), f32), jax.ShapeDtypeStruct((B, H, 1, 1), f32)],
        scratch_shapes=[pltpu.VMEM((H, DV, DK), f32), pltpu.VMEM((H, 1, DK), f32), pltpu.VMEM((H, 1, 1), f32)],
        compiler_params=_cparams("parallel", "arbitrary"),
        name="mlstm_mix",
    )(*args)
    return y, c, n.reshape(B, H, DK), m.reshape(B, H)


def _mlstm_block_kernel(q_ref, k_ref, v_ref, o_ref, gt_ref, bias_ref, gain_ref, x_ref, wout_ref,
                        xo_ref, c_ref, n_ref, m_ref, c_s, n_s, m_s, y_ref, *, L, nc):
    H, DK, DV = MLSTM_HEADS, MLSTM_DK, MLSTM_DV
    P = 2 * L
    j = pl.program_id(1)

    @pl.when(j == 0)
    def _():
        c_s[...] = jnp.zeros_like(c_s)
        n_s[...] = jnp.zeros_like(n_s)
        m_s[...] = jnp.zeros_like(m_s)

    rowi = lax.broadcasted_iota(jnp.int32, (L, L), 0)
    coli = lax.broadcasted_iota(jnp.int32, (L, L), 1)
    causal = coli <= rowi
    r2 = lax.broadcasted_iota(jnp.int32, (P, P), 0)
    c2 = lax.broadcasted_iota(jnp.int32, (P, P), 1)
    pair_tril = ((c2 <= r2) & ((c2 >= L) == (r2 >= L))).astype(bf16)
    scale = DK ** -0.5

    gates = []
    for pp in range(nc // 2):
        g = gt_ref[pp * P:(pp + 1) * P, :] + bias_ref[...]
        hi, mid, lo = _split3(_log_sigmoid(g))
        fcum = (jnp.dot(pair_tril, hi, preferred_element_type=f32)
                + jnp.dot(pair_tril, mid, preferred_element_type=f32)
                + jnp.dot(pair_tril, lo, preferred_element_type=f32))
        g_t = g.T
        f_t = fcum.T
        for half in range(2):
            rs = slice(half * L, (half + 1) * L)
            gates.append((g[rs, :], fcum[rs, :], g_t[:, rs], f_t[:, rs]))

    pairs = [(c, h) for c in range(nc) for h in range(H)]
    rows_of = lambda c: slice(c * L, (c + 1) * L)
    ks_of = lambda h: slice(h * DK, (h + 1) * DK)
    vs_of = lambda h: slice(h * DV, (h + 1) * DV)
    dms, rmax, fcols, icols, qks = {}, {}, {}, {}, {}
    for c, h in pairs:
        g_c, f_c, g_r, f_r = gates[c]
        icols[c, h] = g_c[:, h:h + 1]
        fcols[c, h] = f_c[:, H + h:H + h + 1]
        dm = jnp.where(causal, fcols[c, h] - f_r[H + h:H + h + 1, :] + g_r[h:h + 1, :], -jnp.inf)
        dms[c, h] = dm
        rmax[c, h] = jnp.max(dm, axis=1, keepdims=True)
        qb = (q_ref[rows_of(c), ks_of(h)] * scale).astype(bf16)
        qks[c, h] = _mm_nt(qb, k_ref[rows_of(c), ks_of(h)], bf16)
    m_in, m_out = {}, {}
    for h in range(H):
        m = m_s[h]
        for c in range(nc):
            m_in[c, h] = m
            m = jnp.maximum(fcols[c, h][L - 1:L, :] + m, rmax[c, h][L - 1:L, :])
            m_out[c, h] = m
        m_s[h] = m
    mts, egs, ss, ssums, wcs, dcs, dns = {}, {}, {}, {}, {}, {}, {}
    for c, h in pairs:
        fc_col = fcols[c, h]
        gg = fc_col + m_in[c, h]
        mt = jnp.maximum(gg, rmax[c, h])
        mts[c, h] = mt
        egs[c, h] = jnp.exp(gg - mt)
        s = qks[c, h] * jnp.exp(dms[c, h] - mt)
        ssums[c, h] = jnp.sum(s, axis=1, keepdims=True)
        ss[c, h] = s.astype(bf16)
        f_last = fc_col[L - 1:L, :]
        wcs[c, h] = jnp.exp(f_last + m_in[c, h] - m_out[c, h])
        w_j = jnp.exp(f_last - fc_col + icols[c, h] - m_out[c, h])
        kc = k_ref[rows_of(c), ks_of(h)]
        dcs[c, h] = _mm_tn(v_ref[rows_of(c), vs_of(h)] * w_j, kc, bf16)
        dns[c, h] = jnp.sum(w_j * kc, axis=0, keepdims=True)
    c_in, n_in = {}, {}
    for h in range(H):
        cst = c_s[h]
        nst = n_s[h]
        for c in range(nc):
            c_in[c, h] = cst.astype(bf16)
            n_in[c, h] = nst
            cst = wcs[c, h] * cst + dcs[c, h]
            nst = wcs[c, h] * nst + dns[c, h]
        c_s[h] = cst
        n_s[h] = nst
    hhs = {}
    for c, h in pairs:
        rows, ks, vs = rows_of(c), ks_of(h), vs_of(h)
        qc = q_ref[rows, ks] * scale
        num = egs[c, h] * _mm_nt(qc, c_in[c, h], bf16) + _mm(ss[c, h], v_ref[rows, vs], bf16)
        den = egs[c, h] * jnp.sum(qc * n_in[c, h], axis=1, keepdims=True) + ssums[c, h]
        hhs[c, h] = num / jnp.maximum(jnp.abs(den), jnp.exp(-mts[c, h]))
    for c, h in pairs:
        rows, vs = rows_of(c), vs_of(h)
        hh = hhs[c, h]
        yn = hh * lax.rsqrt(jnp.mean(hh * hh, axis=1, keepdims=True) + NORM_EPS) * gain_ref[:, vs]
        y_ref[rows, vs] = _sigmoid(o_ref[rows, vs]) * yn
    xo_ref[...] = x_ref[...] + jnp.dot(y_ref[...].astype(bf16), wout_ref[...], preferred_element_type=f32)

    @pl.when(j == pl.num_programs(1) - 1)
    def _():
        c_ref[0] = c_s[...]
        n_ref[0] = n_s[...]
        m_ref[0] = m_s[...]


def _mlstm_mix_fresh(pr, b_gates, gain, x, w_out, B, T, tq):
    H, DK, DV = MLSTM_HEADS, MLSTM_DK, MLSTM_DV
    L = MLSTM_CHUNK
    nj = T // tq
    rmap = lambda c: (lambda b, j: (b * nj + j, c))
    bmap4 = lambda b, j: (b, 0, 0, 0)
    bias = jnp.zeros((1, LANES), f32).at[0, :2 * H].set(b_gates)
    y, c, n, m = pl.pallas_call(
        functools.partial(_mlstm_block_kernel, L=L, nc=tq // L),
        grid=(B, nj),
        in_specs=[pl.BlockSpec((tq, H * DK), rmap(0)), pl.BlockSpec((tq, H * DK), rmap(1)),
                  pl.BlockSpec((tq, H * DV), rmap(1)), pl.BlockSpec((tq, H * DV), rmap(2)),
                  pl.BlockSpec((tq, LANES), rmap((2 * H * DK + 2 * H * DV) // LANES)),
                  pl.BlockSpec((1, LANES), lambda b, j: (0, 0)),
                  pl.BlockSpec((1, H * DV), lambda b, j: (0, 0)),
                  pl.BlockSpec((tq, D_MODEL), rmap(0)),
                  pl.BlockSpec((H * DV, D_MODEL), lambda b, j: (0, 0))],
        out_specs=[pl.BlockSpec((tq, D_MODEL), rmap(0)), pl.BlockSpec((1, H, DV, DK), bmap4),
                   pl.BlockSpec((1, H, 1, DK), bmap4), pl.BlockSpec((1, H, 1, 1), bmap4)],
        out_shape=[jax.ShapeDtypeStruct((B * T, D_MODEL), f32), jax.ShapeDtypeStruct((B, H, DV, DK), f32),
                   jax.ShapeDtypeStruct((B, H, 1, DK), f32), jax.ShapeDtypeStruct((B, H, 1, 1), f32)],
        scratch_shapes=[pltpu.VMEM((H, DV, DK), f32), pltpu.VMEM((H, 1, DK), f32), pltpu.VMEM((H, 1, 1), f32),
                        pltpu.VMEM((tq, H * DV), f32)],
        compiler_params=_cparams("parallel", "arbitrary"),
        name="mlstm_mix_fresh",
    )(pr, pr, pr, pr, pr, bias, gain.reshape(1, H * DV), x, w_out)
    return y, c, n.reshape(B, H, DK), m.reshape(B, H)


def _gla_kernel(*refs, kind, layer, L, nchunks, H, DK, DV, has_init, mx):
    if kind == "hgrn":
        q_ref, f_ref, v_ref, g_ref, lbl_ref, gain_ref = refs[:6]
        rest = refs[6:]
    else:
        q_ref, k_ref, v_ref, g_ref, gr_ref, wup_ref, bgate_ref, gain_ref = refs[:8]
        rest = refs[8:]
    if has_init:
        s0_ref = rest[0]
        rest = rest[1:]
    y_ref, s_ref, st_s = rest
    j = pl.program_id(1)

    @pl.when(j == 0)
    def _():
        for h in range(H):
            if has_init:
                st_s[h] = s0_ref[0, h].T
            else:
                st_s[h] = jnp.zeros((DV, DK), f32)

    Lp = max(L, BF16_ROWS)
    rowi = lax.broadcasted_iota(jnp.int32, (Lp, Lp), 0)
    coli = lax.broadcasted_iota(jnp.int32, (Lp, Lp), 1)
    causal = coli <= rowi
    tril_b = causal.astype(bf16)

    def pad_rows(x):
        if Lp == L:
            return x
        return jnp.concatenate([x, jnp.zeros((Lp - L, x.shape[1]), f32)], axis=0)

    if kind == "hgrn":
        logits = lbl_ref[...]
        e = jnp.exp(logits - jnp.max(logits, axis=0, keepdims=True))
        sm = e / jnp.sum(e, axis=0, keepdims=True)
        lb_all = jnp.zeros((1, H * DK), f32)
        for li in range(layer):
            lb_all = lb_all + sm[li:li + 1, :]
    else:
        scale = DK ** -0.5

    def chunk(c, carry):
        rows = pl.ds(pl.multiple_of(c * L, L), L)
        if kind == "gla":
            gk = _mm(pad_rows(gr_ref[rows, :]), wup_ref[...], mx)[:L] + bgate_ref[...]
            ld_all = _log_sigmoid(gk) * (1.0 / GLA_GATE_NORMALIZER)
        for h in range(H):
            ks = slice(h * DK, (h + 1) * DK)
            vs = slice(h * DV, (h + 1) * DV)
            if kind == "hgrn":
                qraw = q_ref[rows, ks]
                fg = f_ref[rows, ks]
                lb = lb_all[:, ks]
                gc = jnp.log(lb + (1.0 - lb) * _sigmoid(fg))
                kc = (1.0 - lb) * _sigmoid(-fg)
                qc = qraw * _sigmoid(qraw)
            else:
                qc = q_ref[rows, ks] * scale
                kc = k_ref[rows, ks]
                gc = ld_all[:, ks]
            vc = pad_rows(v_ref[rows, vs])
            qc, kc, gc = pad_rows(qc), pad_rows(kc), pad_rows(gc)
            bc = _cumsum_rows(gc, tril_b)
            qe = qc * jnp.exp(bc)
            ke = kc * jnp.exp(-bc)
            a = jnp.where(causal, _mm_nt(qe, ke, mx), 0.0)
            st = st_s[h]
            o = (_mm_nt(qe, st, mx) + _mm(a, vc, mx))[:L]
            b_last = bc[L - 1:L, :]
            kdec = kc * jnp.exp(b_last - bc)
            st_s[h] = st * jnp.exp(b_last) + _mm_tn(vc, kdec, mx)
            yn = o * lax.rsqrt(jnp.mean(o * o, axis=1, keepdims=True) + NORM_EPS) * gain_ref[:, vs]
            gg = g_ref[rows, vs]
            y_ref[rows, vs] = yn * (gg * _sigmoid(gg))
        return carry

    lax.fori_loop(0, nchunks, chunk, 0)

    @pl.when(j == pl.num_programs(1) - 1)
    def _():
        for h in range(H):
            s_ref[0, h] = st_s[h].T


def _gla_mix(kind, pr, extra, gain, B, T, state, tq, layer=0):
    if kind == "hgrn":
        H, DK, DV = HGRN_HEADS, HGRN_DK, HGRN_DV
    else:
        H, DK, DV = GLA_HEADS, GLA_DK, GLA_DV
    L = _chunk_len(T, LIN_CHUNK)
    nj = T // tq
    has_init = state is not None
    mx = bf16
    rmap = lambda c: (lambda b, j: (b * nj + j, c))
    bmap4 = lambda b, j: (b, 0, 0, 0)
    const2 = lambda b, j: (0, 0)
    hk, hv = H * DK, H * DV
    if kind == "hgrn":
        (lb_logits,) = extra
        in_specs = [pl.BlockSpec((tq, hk), rmap(0)), pl.BlockSpec((tq, hk), rmap(1)),
                    pl.BlockSpec((tq, hv), rmap(2)), pl.BlockSpec((tq, hv), rmap(3)),
                    pl.BlockSpec((DEPTH, hk), const2), pl.BlockSpec((1, hv), const2)]
        args = [pr, pr, pr, pr, lb_logits, gain.reshape(1, hv)]
    else:
        w_up, b_gate = extra
        w_up_p = jnp.zeros((LANES, hk), f32).at[:GLA_GATE_RANK].set(w_up)
        in_specs = [pl.BlockSpec((tq, hk), rmap(0)), pl.BlockSpec((tq, hk), rmap(1)),
                    pl.BlockSpec((tq, hv), rmap(1)), pl.BlockSpec((tq, hv), rmap(2)),
                    pl.BlockSpec((tq, LANES), rmap((2 * hk + 2 * hv) // LANES)),
                    pl.BlockSpec((LANES, hk), const2), pl.BlockSpec((1, hk), const2),
                    pl.BlockSpec((1, hv), const2)]
        args = [pr, pr, pr, pr, pr, w_up_p, b_gate.reshape(1, hk), gain.reshape(1, hv)]
    if has_init:
        in_specs.append(pl.BlockSpec((1, H, DK, DV), bmap4))
        args.append(state)
    y, s = pl.pallas_call(
        functools.partial(_gla_kernel, kind=kind, layer=layer, L=L, nchunks=tq // L, H=H, DK=DK, DV=DV,
                          has_init=has_init, mx=mx),
        grid=(B, nj),
        in_specs=in_specs,
        out_specs=[pl.BlockSpec((tq, hv), rmap(0)), pl.BlockSpec((1, H, DK, DV), bmap4)],
        out_shape=[jax.ShapeDtypeStruct((B * T, hv), f32), jax.ShapeDtypeStruct((B, H, DK, DV), f32)],
        scratch_shapes=[pltpu.VMEM((H, DV, DK), f32)],
        compiler_params=_cparams("parallel", "arbitrary"),
        name=kind + "_mix",
    )(*args)
    return y, s


def _gla_seq_kernel(*refs, kind, layer, T, bs, H, DK, DV):
    if kind == "hgrn":
        q_ref, f_ref, v_ref, g_ref, lbl_ref, gain_ref = refs[:6]
        rest = refs[6:]
    else:
        q_ref, k_ref, v_ref, g_ref, gr_ref, wup_ref, bgate_ref, gain_ref = refs[:8]
        rest = refs[8:]
    s0_ref, y_ref, s_ref, qe_s, ke_s, kd_s, el_s, v_s = rest
    P = BF16_ROWS
    R = bs * T
    hk = H * DK

    ri = lax.broadcasted_iota(jnp.int32, (R, R), 0)
    ci = lax.broadcasted_iota(jnp.int32, (R, R), 1)
    same_seq = (ri // T) == (ci // T)
    tril_b = (same_seq & (ci <= ri)).astype(bf16)
    seq_b = same_seq.astype(bf16)
    rp = lax.broadcasted_iota(jnp.int32, (P, P), 0)
    cp = lax.broadcasted_iota(jnp.int32, (P, P), 1)
    causal = cp <= rp

    if kind == "hgrn":
        logits = lbl_ref[...]
        e = jnp.exp(logits - jnp.max(logits, axis=0, keepdims=True))
        sm = e / jnp.sum(e, axis=0, keepdims=True)
        lb = jnp.zeros((1, hk), f32)
        for li in range(layer):
            lb = lb + sm[li:li + 1, :]
        qraw = q_ref[...]
        fg = f_ref[...]
        gc = jnp.log(lb + (1.0 - lb) * _sigmoid(fg))
        kc = (1.0 - lb) * _sigmoid(-fg)
        qc = qraw * _sigmoid(qraw)
    else:
        gk = _mm(gr_ref[...], wup_ref[...]) + bgate_ref[...]
        gc = _log_sigmoid(gk) * (1.0 / GLA_GATE_NORMALIZER)
        qc = q_ref[...] * (DK ** -0.5)
        kc = k_ref[...]
    pieces = _split3(gc)
    bc = sum(jnp.dot(tril_b, x, preferred_element_type=f32) for x in pieces)
    bl = sum(jnp.dot(seq_b, x, preferred_element_type=f32) for x in pieces)
    qe = qc * jnp.exp(bc)
    ke = kc * jnp.exp(-bc)
    kd = kc * jnp.exp(bl - bc)
    e_hi, e_mid, e_lo = [x.astype(f32) for x in _split3(jnp.exp(bl))]
    vv = v_ref[...]
    zpad_k = jnp.zeros((P - T, hk), f32)
    zpad_v = jnp.zeros((P - T, vv.shape[1]), f32)
    prow = lax.broadcasted_iota(jnp.int32, (P, hk), 0)
    for c in range(bs):
        src = slice(c * T, (c + 1) * T)
        dst = slice(c * P, (c + 1) * P)
        qe_s[dst, :] = jnp.concatenate([qe[src], zpad_k], axis=0)
        ke_s[dst, :] = jnp.concatenate([ke[src], zpad_k], axis=0)
        kd_s[dst, :] = jnp.concatenate([kd[src], zpad_k], axis=0)
        v_s[dst, :] = jnp.concatenate([vv[src], zpad_v], axis=0)
        first = slice(c * T, c * T + 1)
        el_s[dst, :] = jnp.where(prow == 0, e_hi[first],
                                 jnp.where(prow == 1, e_mid[first], jnp.where(prow == 2, e_lo[first], 0.0)))

    ones_b = jnp.ones((P, LANES), bf16)
    pairs = [(c, h) for c in range(bs) for h in range(H)]
    rows_of = lambda c: slice(c * P, (c + 1) * P)
    ks_of = lambda h: slice(h * DK, (h + 1) * DK)
    vs_of = lambda h: slice(h * DV, (h + 1) * DV)
    qes, vbs, amats, news = {}, {}, {}, {}
    for c, h in pairs:
        rows, ks = rows_of(c), ks_of(h)
        qes[c, h] = qe_s[rows, ks].astype(bf16)
        vbs[c, h] = v_s[rows, vs_of(h)].astype(bf16)
        a = _mm_nt(qes[c, h], ke_s[rows, ks])
        amats[c, h] = jnp.where(causal, a, 0.0).astype(bf16)
        decay = _mm_tn(el_s[rows, ks], ones_b)
        decay = jnp.concatenate([decay] * (DV // LANES), axis=1)
        news[c, h] = s0_ref[c, h] * decay + _mm_tn(kd_s[rows, ks], vbs[c, h])
    for c, h in pairs:
        s_ref[c, h] = news[c, h]
    outs = {}
    for c, h in pairs:
        outs[c, h] = (_mm(qes[c, h], s0_ref[c, h]) + _mm(amats[c, h], vbs[c, h]))[:T]
    for c, h in pairs:
        vs = vs_of(h)
        rows = slice(c * T, (c + 1) * T)
        o = outs[c, h]
        yn = o * lax.rsqrt(jnp.mean(o * o, axis=1, keepdims=True) + NORM_EPS) * gain_ref[:, vs]
        gg = g_ref[rows, vs]
        y_ref[rows, vs] = yn * (gg * _sigmoid(gg))


def _gla_mix_seq(kind, pr, extra, gain, B, T, state, bs, layer=0):
    if kind == "hgrn":
        H, DK, DV = HGRN_HEADS, HGRN_DK, HGRN_DV
    else:
        H, DK, DV = GLA_HEADS, GLA_DK, GLA_DV
    R = bs * T
    rmap = lambda c: (lambda i: (i, c))
    bmap4 = lambda i: (i, 0, 0, 0)
    const2 = lambda i: (0, 0)
    hk, hv = H * DK, H * DV
    if kind == "hgrn":
        (lb_logits,) = extra
        in_specs = [pl.BlockSpec((R, hk), rmap(0)), pl.BlockSpec((R, hk), rmap(1)),
                    pl.BlockSpec((R, hv), rmap(2)), pl.BlockSpec((R, hv), rmap(3)),
                    pl.BlockSpec((DEPTH, hk), const2), pl.BlockSpec((1, hv), const2)]
        args = [pr, pr, pr, pr, lb_logits, gain.reshape(1, hv)]
    else:
        w_up, b_gate = extra
        w_up_p = jnp.zeros((LANES, hk), f32).at[:GLA_GATE_RANK].set(w_up)
        in_specs = [pl.BlockSpec((R, hk), rmap(0)), pl.BlockSpec((R, hk), rmap(1)),
                    pl.BlockSpec((R, hv), rmap(1)), pl.BlockSpec((R, hv), rmap(2)),
                    pl.BlockSpec((R, LANES), rmap((2 * hk + 2 * hv) // LANES)),
                    pl.BlockSpec((LANES, hk), const2), pl.BlockSpec((1, hk), const2),
                    pl.BlockSpec((1, hv), const2)]
        args = [pr, pr, pr, pr, pr, w_up_p, b_gate.reshape(1, hk), gain.reshape(1, hv)]
    in_specs.append(pl.BlockSpec((bs, H, DK, DV), bmap4))
    args.append(state)
    P = BF16_ROWS
    return pl.pallas_call(
        functools.partial(_gla_seq_kernel, kind=kind, layer=layer, T=T, bs=bs, H=H, DK=DK, DV=DV),
        grid=(B // bs,),
        in_specs=in_specs,
        out_specs=[pl.BlockSpec((R, hv), rmap(0)), pl.BlockSpec((bs, H, DK, DV), bmap4)],
        out_shape=[jax.ShapeDtypeStruct((B * T, hv), f32), jax.ShapeDtypeStruct((B, H, DK, DV), f32)],
        scratch_shapes=[pltpu.VMEM((bs * P, hk), f32)] * 4 + [pltpu.VMEM((bs * P, hv), f32)],
        compiler_params=_cparams("parallel"),
        name=kind + "_mix_seq",
    )(*args)


def _gla_block_kernel(*refs, kind, layer, L, nc, H, DK, DV):
    if kind == "hgrn":
        q_ref, f_ref, v_ref, g_ref, lbl_ref, gain_ref = refs[:6]
        rest = refs[6:]
    else:
        q_ref, k_ref, v_ref, g_ref, gr_ref, wup_ref, bgate_ref, gain_ref = refs[:8]
        rest = refs[8:]
    x_ref, wout_ref, xo_ref, s_ref, st_s, qe_s, ke_s, qs_s, kd_s, el_s, y_ref = rest
    j = pl.program_id(1)

    @pl.when(j == 0)
    def _():
        st_s[...] = jnp.zeros_like(st_s)

    rowi = lax.broadcasted_iota(jnp.int32, (L, L), 0)
    coli = lax.broadcasted_iota(jnp.int32, (L, L), 1)
    causal = coli <= rowi
    tril_b = causal.astype(bf16)

    if kind == "hgrn":
        logits = lbl_ref[...]
        e = jnp.exp(logits - jnp.max(logits, axis=0, keepdims=True))
        sm = e / jnp.sum(e, axis=0, keepdims=True)
        lb = jnp.zeros((1, H * DK), f32)
        for li in range(layer):
            lb = lb + sm[li:li + 1, :]
    else:
        scale = DK ** -0.5

    for c in range(nc):
        rows = slice(c * L, (c + 1) * L)
        if kind == "hgrn":
            qraw = q_ref[rows, :]
            fg = f_ref[rows, :]
            gc = jnp.log(lb + (1.0 - lb) * _sigmoid(fg))
            kc = (1.0 - lb) * _sigmoid(-fg)
            qc = qraw * _sigmoid(qraw)
        else:
            gk = jnp.dot(gr_ref[rows, :].astype(bf16), wup_ref[...].astype(bf16),
                         preferred_element_type=f32) + bgate_ref[...]
            gc = _log_sigmoid(gk) * (1.0 / GLA_GATE_NORMALIZER)
            qc = q_ref[rows, :] * scale
            kc = k_ref[rows, :]
        hi, mid, lo = _split3(gc)
        bc = (jnp.dot(tril_b, hi, preferred_element_type=f32) + jnp.dot(tril_b, mid, preferred_element_type=f32)
              + jnp.dot(tril_b, lo, preferred_element_type=f32))
        b_mid = bc[L // 2 - 1:L // 2, :]
        b_last = bc[L - 1:L, :]
        qe_s[rows, :] = (qc * jnp.exp(bc - b_mid)).astype(bf16)
        ke_s[rows, :] = (kc * jnp.exp(b_mid - bc)).astype(bf16)
        qs_s[rows, :] = (qc * jnp.exp(bc)).astype(bf16)
        kd_s[rows, :] = (kc * jnp.exp(b_last - bc)).astype(bf16)
        el_s[c] = jnp.exp(b_last)

    dn_nt = (((1,), (1,)), ((), ()))
    dn_tn = (((0,), (0,)), ((), ()))
    pairs = [(c, h) for c in range(nc) for h in range(H)]
    rows_of = lambda c: slice(c * L, (c + 1) * L)
    ks_of = lambda h: slice(h * DK, (h + 1) * DK)
    vs_of = lambda h: slice(h * DV, (h + 1) * DV)
    vbs, amats, dsts = {}, {}, {}
    for c, h in pairs:
        rows, ks = rows_of(c), ks_of(h)
        vb = v_ref[rows, vs_of(h)].astype(bf16)
        a = lax.dot_general(qe_s[rows, ks], ke_s[rows, ks], dn_nt, preferred_element_type=f32)
        vbs[c, h] = vb
        amats[c, h] = jnp.where(causal, a, 0.0).astype(bf16)
        dsts[c, h] = lax.dot_general(vb, kd_s[rows, ks], dn_tn, preferred_element_type=f32)
    sts = {}
    for h in range(H):
        st = st_s[h]
        for c in range(nc):
            sts[c, h] = st.astype(bf16)
            st = st * el_s[c][:, ks_of(h)] + dsts[c, h]
        st_s[h] = st
    outs = {}
    for c, h in pairs:
        rows, ks = rows_of(c), ks_of(h)
        outs[c, h] = (lax.dot_general(qs_s[rows, ks], sts[c, h], dn_nt, preferred_element_type=f32)
                      + jnp.dot(amats[c, h], vbs[c, h], preferred_element_type=f32))
    for c, h in pairs:
        rows, vs = rows_of(c), vs_of(h)
        o = outs[c, h]
        yn = o * lax.rsqrt(jnp.mean(o * o, axis=1, keepdims=True) + NORM_EPS) * gain_ref[:, vs]
        gg = g_ref[rows, vs]
        y_ref[rows, vs] = yn * (gg * _sigmoid(gg))
    xo_ref[...] = x_ref[...] + jnp.dot(y_ref[...].astype(bf16), wout_ref[...], preferred_element_type=f32)

    @pl.when(j == pl.num_programs(1) - 1)
    def _():
        for h in range(H):
            s_ref[0, h] = st_s[h].T


def _gla_mix_fresh(kind, pr, extra, gain, x, w_out, B, T, tq, layer=0):
    if kind == "hgrn":
        H, DK, DV = HGRN_HEADS, HGRN_DK, HGRN_DV
    else:
        H, DK, DV = GLA_HEADS, GLA_DK, GLA_DV
    L = 2 * LIN_CHUNK
    nc = tq // L
    nj = T // tq
    rmap = lambda c: (lambda b, j: (b * nj + j, c))
    bmap4 = lambda b, j: (b, 0, 0, 0)
    const2 = lambda b, j: (0, 0)
    hk, hv = H * DK, H * DV
    if kind == "hgrn":
        (lb_logits,) = extra
        in_specs = [pl.BlockSpec((tq, hk), rmap(0)), pl.BlockSpec((tq, hk), rmap(1)),
                    pl.BlockSpec((tq, hv), rmap(2)), pl.BlockSpec((tq, hv), rmap(3)),
                    pl.BlockSpec((DEPTH, hk), const2), pl.BlockSpec((1, hv), const2)]
        args = [pr, pr, pr, pr, lb_logits, gain.reshape(1, hv)]
    else:
        w_up, b_gate = extra
        w_up_p = jnp.zeros((LANES, hk), f32).at[:GLA_GATE_RANK].set(w_up)
        in_specs = [pl.BlockSpec((tq, hk), rmap(0)), pl.BlockSpec((tq, hk), rmap(1)),
                    pl.BlockSpec((tq, hv), rmap(1)), pl.BlockSpec((tq, hv), rmap(2)),
                    pl.BlockSpec((tq, LANES), rmap((2 * hk + 2 * hv) // LANES)),
                    pl.BlockSpec((LANES, hk), const2), pl.BlockSpec((1, hk), const2),
                    pl.BlockSpec((1, hv), const2)]
        args = [pr, pr, pr, pr, pr, w_up_p, b_gate.reshape(1, hk), gain.reshape(1, hv)]
    in_specs += [pl.BlockSpec((tq, D_MODEL), rmap(0)), pl.BlockSpec((hv, D_MODEL), const2)]
    args += [x, w_out]
    return pl.pallas_call(
        functools.partial(_gla_block_kernel, kind=kind, layer=layer, L=L, nc=nc, H=H, DK=DK, DV=DV),
        grid=(B, nj),
        in_specs=in_specs,
        out_specs=[pl.BlockSpec((tq, D_MODEL), rmap(0)), pl.BlockSpec((1, H, DK, DV), bmap4)],
        out_shape=[jax.ShapeDtypeStruct((B * T, D_MODEL), f32), jax.ShapeDtypeStruct((B, H, DK, DV), f32)],
        scratch_shapes=[pltpu.VMEM((H, DV, DK), f32)] + [pltpu.VMEM((tq, hk), bf16)] * 4
                       + [pltpu.VMEM((nc, 1, hk), f32), pltpu.VMEM((tq, hv), f32)],
        compiler_params=_cparams("parallel", "arbitrary"),
        name=kind + "_mix_fresh",
    )(*args)


def _rwkv_pre_kernel(x_ref, aux_ref, sh_ref, g_ref, mu_ref, wrkv_ref, la_ref, lbw_ref, lba_ref, lbg_ref,
                     w0_ref, a0_ref, r_ref, w_ref, k_ref, v_ref, alr_ref, gate_ref, *, long_seq, blocks_per_seq,
                     seq_len):
    g = g_ref[...]
    hn = _rms(x_ref[...], g)
    tm = hn.shape[0]
    rowi = lax.broadcasted_iota(jnp.int32, (tm, 1), 0)
    rolled = pltpu.roll(hn, 1, axis=0)
    if long_seq:
        p_last = _rms(aux_ref[SUBLANES - 1:SUBLANES, :], g)
        at_start = (pl.program_id(0) % blocks_per_seq) == 0
        first = jnp.where(at_start, sh_ref[0], p_last)
        prev = jnp.where(rowi == 0, first, rolled)
    else:
        prev = jnp.where(rowi % seq_len == 0, aux_ref[...], rolled)
    xx = prev - hn

    def lerp(c):
        return (hn + xx * mu_ref[c:c + 1, :]).astype(bf16)

    r_ref[...] = jnp.dot(lerp(0), wrkv_ref[0], preferred_element_type=f32)
    k_ref[...] = jnp.dot(lerp(1), wrkv_ref[1], preferred_element_type=f32)
    v_ref[...] = jnp.dot(lerp(2), wrkv_ref[2], preferred_element_type=f32)
    lw = jnp.tanh(jnp.dot(lerp(3), la_ref[:, 0:64], preferred_element_type=f32))
    wl = -_softplus(-(w0_ref[...] + jnp.dot(lw.astype(bf16), lbw_ref[...], preferred_element_type=f32))) - 0.5
    w_ref[...] = jnp.exp(-jnp.exp(wl))
    la = jnp.dot(lerp(4), la_ref[:, 64:128], preferred_element_type=f32)
    alr_ref[...] = _sigmoid(a0_ref[...] + jnp.dot(la.astype(bf16), lba_ref[...], preferred_element_type=f32))
    lg = _sigmoid(jnp.dot(lerp(5), la_ref[:, 128:256], preferred_element_type=f32))
    gate_ref[...] = jnp.dot(lg.astype(bf16), lbg_ref[...], preferred_element_type=f32)


def _rwkv_pre(x, shift0, g, mu, wrkv, la, lbw, lba, lbg, w0, a0, B, T, tm):
    n, d = x.shape
    long_seq = T % tm == 0
    row = lambda i: (i, 0)
    const2 = lambda i: (0, 0)
    if long_seq:
        bps = T // tm
        sub = tm // SUBLANES
        aux = x
        aux_spec = pl.BlockSpec((SUBLANES, d), lambda i: (jnp.maximum(i * sub - 1, 0), 0))
        sh = shift0.reshape(B, 1, d)
        sh_spec = pl.BlockSpec((1, 1, d), lambda i: (i // bps, 0, 0))
    else:
        assert tm % T == 0
        bps = 1
        aux = jnp.repeat(shift0, T, axis=0)
        aux_spec = pl.BlockSpec((tm, d), row)
        sh = shift0.reshape(B, 1, d)
        sh_spec = pl.BlockSpec((1, 1, d), lambda i: (0, 0, 0))
    out = jax.ShapeDtypeStruct((n, d), f32)
    return pl.pallas_call(
        functools.partial(_rwkv_pre_kernel, long_seq=long_seq, blocks_per_seq=bps, seq_len=T),
        grid=(n // tm,),
        in_specs=[pl.BlockSpec((tm, d), row), aux_spec, sh_spec, pl.BlockSpec((1, d), const2),
                  pl.BlockSpec((6, d), const2), pl.BlockSpec((3, d, d), lambda i: (0, 0, 0)),
                  pl.BlockSpec((d, 256), const2), pl.BlockSpec((64, d), const2), pl.BlockSpec((64, d), const2),
                  pl.BlockSpec((128, d), const2), pl.BlockSpec((1, d), const2), pl.BlockSpec((1, d), const2)],
        out_specs=[pl.BlockSpec((tm, d), row)] * 6,
        out_shape=[out] * 6,
        compiler_params=_cparams("parallel"),
        name="rwkv_pre",
    )(x, aux, sh, g.reshape(1, d), mu, wrkv, la, lbw, lba, lbg, w0.reshape(1, d), a0.reshape(1, d))


def _rwkv_scan_kernel(*refs, tt, has_init, unroll):
    N = RWKV_N
    (r_ref, w_ref, k_ref, v_ref, alr_ref, gate_ref, x_ref, wout_ref,
     kk_ref, ka_ref, rk_ref, lnw_ref, lnb_ref) = refs[:13]
    if has_init:
        s0_ref = refs[13]
        rest = refs[14:]
    else:
        rest = refs[13:]
    xo_ref, sout_ref, s_s, y_s, vec_s, z_ref, zz_s = rest
    HP = RWKV_HEADS // 2
    j = pl.program_id(1)

    @pl.when(j == 0)
    def _():
        if has_init:
            s_s[...] = s0_ref[...]
        else:
            s_s[...] = jnp.zeros_like(s_s)

    low = lax.broadcasted_iota(jnp.int32, (N, LANES), 1) < N

    def load_pair(ref, t0):
        tiles = []
        for t in (t0, t0 + 1):
            x = ref[:, t, :]
            tiles += [x[:, hp * LANES:(hp + 1) * LANES] for hp in range(HP)]
        xt = jnp.concatenate(tiles, axis=0).T
        ev, od = xt[:N], xt[N:]
        return (jnp.where(low, ev, pltpu.roll(od, N, axis=1)),
                jnp.where(low, pltpu.roll(ev, N, axis=1), od))

    def store_pair(ref, t0, z0, z1):
        ev = jnp.where(low, z0, pltpu.roll(z1, N, axis=1))
        od = jnp.where(low, pltpu.roll(z0, N, axis=1), z1)
        x = jnp.concatenate([ev, od], axis=0).T
        for i, t in enumerate((t0, t0 + 1)):
            tiles = [x[(i * HP + hp) * SUBLANES:(i * HP + hp + 1) * SUBLANES, :] for hp in range(HP)]
            ref[:, t, :] = jnp.concatenate(tiles, axis=1)

    VA, VW, VB, VK, VR, VV = range(6)

    def stage(tk, r, w, k, v, alr):
        kkraw = k * kk_ref[...]
        nrm = jnp.sqrt(jnp.sum(kkraw * kkraw, axis=0, keepdims=True))
        kk = kkraw / jnp.maximum(nrm, 1e-12)
        vec_s[tk, VA] = -kk
        vec_s[tk, VW] = w
        vec_s[tk, VB] = kk * alr
        vec_s[tk, VK] = k * (1.0 + (alr - 1.0) * ka_ref[...])
        vec_s[tk, VR] = r
        vec_s[tk, VV] = v

    def key_row(tk, which, kc):
        return vec_s[tk, which, pl.ds(kc, 1), :]

    def project(tk):
        def body(i, sa):
            for u in range(unroll):
                kc = i * unroll + u
                sa = sa + s_s[kc] * key_row(tk, VA, kc)
            return sa

        return lax.fori_loop(0, N // unroll, body, jnp.zeros((N, LANES), f32))

    def update(tk, sa):
        vv = vec_s[tk, VV]
        y = jnp.zeros((N, LANES), f32)
        for kc in range(N):
            sk = s_s[kc] * key_row(tk, VW, kc) + sa * key_row(tk, VB, kc) + vv * key_row(tk, VK, kc)
            s_s[kc] = sk
            y = y + sk * key_row(tk, VR, kc)
        y_s[tk] = y

    def epilogue(tk):
        y = y_s[tk]
        mean = jnp.mean(y, axis=0, keepdims=True)
        yc = y - mean
        var = jnp.mean(yc * yc, axis=0, keepdims=True)
        yn = yc * lax.rsqrt(var + RWKV_GN_EPS)
        bonus = jnp.sum(vec_s[tk, VR] * vec_s[tk, VK] * rk_ref[...], axis=0, keepdims=True) * vec_s[tk, VV]
        return yn * lnw_ref[...] + lnb_ref[...] + bonus

    def stage_pair(slot, t0):
        streams = [load_pair(ref, t0) for ref in (r_ref, w_ref, k_ref, v_ref, alr_ref)]
        for i in range(2):
            stage(slot + i, *[s[i] for s in streams])

    @pl.when(j == 0)
    def _():
        vec_s[...] = jnp.zeros_like(vec_s)
        y_s[...] = jnp.zeros_like(y_s)

    npairs = tt // 2
    stage_pair(0, 0)

    def step(p, carry):
        slot = 2 * (p % 2)
        prev = 2 - slot
        sa = project(slot)
        update(slot, sa)
        zz_s[0] = epilogue(prev)
        zz_s[1] = epilogue(prev + 1)
        stage_pair(prev, 2 * jnp.minimum(p + 1, npairs - 1))
        sa = project(slot + 1)
        update(slot + 1, sa)
        store_pair(z_ref, 2 * jnp.maximum(p - 1, 0), zz_s[0], zz_s[1])
        return carry

    lax.fori_loop(0, npairs, step, 0)
    last = 2 * ((npairs - 1) % 2)
    store_pair(z_ref, tt - 2, epilogue(last), epilogue(last + 1))

    rows = RWKV_SEQ_PER_STEP * tt
    zg = (z_ref[...] * gate_ref[...]).reshape(rows, D_MODEL).astype(bf16)
    proj = jnp.dot(zg, wout_ref[...], preferred_element_type=f32)
    xo_ref[...] = x_ref[...] + proj.reshape(RWKV_SEQ_PER_STEP, tt, D_MODEL)

    @pl.when(j == pl.num_programs(1) - 1)
    def _():
        sout_ref[...] = s_s[...]


RWKV_SEQ_PER_STEP = LANES // RWKV_HEADS


def _rwkv_scan(r, w, k, v, alr, gate, x, w_out, slabs, s0, B, T, tt):
    N = RWKV_N
    G = B // RWKV_SEQ_PER_STEP
    has_init = s0 is not None
    view = lambda a: a.reshape(B, T, D_MODEL)
    tmap = lambda g, j: (g, j, 0)
    smap = lambda g, j: (0, 0, g)
    const2 = lambda g, j: (0, 0)
    stream = pl.BlockSpec((RWKV_SEQ_PER_STEP, tt, D_MODEL), tmap)
    in_specs = [stream] * 7 + [pl.BlockSpec((D_MODEL, D_MODEL), const2)] + [pl.BlockSpec((N, LANES), const2)] * 5
    args = [view(r), view(w), view(k), view(v), view(alr), view(gate), view(x), w_out] + list(slabs)
    if has_init:
        in_specs.append(pl.BlockSpec((N, N, LANES), smap))
        args.append(s0)
    z, s = pl.pallas_call(
        functools.partial(_rwkv_scan_kernel, tt=tt, has_init=has_init, unroll=16),
        grid=(G, T // tt),
        in_specs=in_specs,
        out_specs=[stream, pl.BlockSpec((N, N, LANES), smap)],
        out_shape=[jax.ShapeDtypeStruct((B, T, D_MODEL), f32),
                   jax.ShapeDtypeStruct((N, N, G * LANES), f32)],
        scratch_shapes=[pltpu.VMEM((N, N, LANES), f32), pltpu.VMEM((4, N, LANES), f32),
                        pltpu.VMEM((4, 6, N, LANES), f32), pltpu.VMEM((RWKV_SEQ_PER_STEP, tt, D_MODEL), f32),
                        pltpu.VMEM((2, N, LANES), f32)],
        compiler_params=_cparams("parallel", "arbitrary"),
        name="rwkv_scan",
    )(*args)
    return z.reshape(B * T, D_MODEL), s


def _lane_slab(p):
    q = p.reshape(RWKV_HEADS // 2, 2, RWKV_N).transpose(2, 1, 0)
    q = jnp.broadcast_to(q[:, :, :, None], (RWKV_N, 2, RWKV_HEADS // 2, RWKV_SEQ_PER_STEP))
    return q.reshape(RWKV_N, LANES)


def _rwkv_mix(x, p, B, T, state, tm):
    H, N = RWKV_HEADS, RWKV_N
    G = B // RWKV_SEQ_PER_STEP
    if state is None:
        shift0 = jnp.zeros((B, D_MODEL), f32)
        s0 = None
    else:
        s_in, shift0 = state
        s0 = (s_in.reshape(G, RWKV_SEQ_PER_STEP, H // 2, 2, N, N)
              .transpose(5, 4, 0, 3, 2, 1).reshape(N, N, G * LANES))
    r, w, k, v, alr, gate = _rwkv_pre(x, shift0, p["g"], p["mu"], p["wrkv"], p["la"], p["lbw"], p["lba"], p["lbg"],
                                      p["w0"], p["a0"], B, T, tm)
    slabs = [_lane_slab(p[n]) for n in ("k_k", "k_a", "r_k", "ln_w", "ln_b")]
    x_new, s = _rwkv_scan(r, w, k, v, alr, gate, x, p["w_out"], slabs, s0, B, T, min(T, 32))
    s = (s.reshape(N, N, G, 2, H // 2, RWKV_SEQ_PER_STEP)
         .transpose(2, 5, 4, 3, 1, 0).reshape(B, H, N, N))
    return x_new, s


def _pad_cols(w, to):
    return jnp.pad(w, ((0, 0), (0, to - w.shape[1])))


def _trunk(x, B, T, states, p):
    n = x.shape[0]
    tm = min(512, n)
    tmm = min(1024, n)
    tq = min(256, T)
    fresh = states is None
    block_ok = fresh and T % tq == 0 and tq % (2 * LIN_CHUNK) == 0
    seq_ok = (not fresh) and T == SUBLANES and T <= LIN_CHUNK and B % SEQS_PER_STEP == 0
    if not fresh:
        m_c, m_n, m_m, h_s, g_s, r_s, r_sh = states
    new = {}
    for li in range(DEPTH):
        g_mix = p["norm_mix"][li]
        if li == 0:
            pr = _norm_proj(x, g_mix, p["mlstm_w_in"], tmm, 1664)
            if fresh and T % tq == 0 and tq % (2 * MLSTM_CHUNK) == 0:
                x, c, nn, m = _mlstm_mix_fresh(pr, p["mlstm_b_gates"], p["mlstm_norm"], x, p["mlstm_w_out"],
                                               B, T, tq)
            else:
                y, c, nn, m = _mlstm_mix(pr, p["mlstm_b_gates"], p["mlstm_norm"], B, T,
                                         None if fresh else (m_c[0], m_n[0], m_m[0]), tq)
                x = _out_proj(y, p["mlstm_w_out"], x, tm)
            new["C"], new["n"], new["m"] = c[None], nn[None], m[None]
        elif li == 1:
            pr = _norm_proj(x, g_mix, p["hgrn_w_in"], tmm, 2048)
            if block_ok:
                x, s = _gla_mix_fresh("hgrn", pr, (p["hgrn_lb_logits"],), p["hgrn_norm"], x, p["hgrn_w_out"],
                                      B, T, tq, layer=li)
            else:
                if seq_ok:
                    y, s = _gla_mix_seq("hgrn", pr, (p["hgrn_lb_logits"],), p["hgrn_norm"], B, T, h_s[0],
                                        SEQS_PER_STEP, layer=li)
                else:
                    y, s = _gla_mix("hgrn", pr, (p["hgrn_lb_logits"],), p["hgrn_norm"], B, T,
                                    None if fresh else h_s[0], tq, layer=li)
                x = _out_proj(y, p["hgrn_w_out"], x, tm)
            new["hS"] = s[None]
        elif li == 2:
            pr = _norm_proj(x, g_mix, p["gla_w_in"], tmm, 1664)
            if block_ok:
                x, s = _gla_mix_fresh("gla", pr, (p["gla_w_gate_up"], p["gla_b_gate"]), p["gla_norm"], x,
                                      p["gla_w_out"], B, T, tq)
            else:
                if seq_ok:
                    y, s = _gla_mix_seq("gla", pr, (p["gla_w_gate_up"], p["gla_b_gate"]), p["gla_norm"], B, T,
                                        g_s[0], SEQS_PER_STEP)
                else:
                    y, s = _gla_mix("gla", pr, (p["gla_w_gate_up"], p["gla_b_gate"]), p["gla_norm"], B, T,
                                    None if fresh else g_s[0], tq)
                x = _out_proj(y, p["gla_w_out"], x, tm)
            new["gS"] = s[None]
        else:
            rp = dict(p["rwkv"], g=g_mix, w_out=p["rwkv_w_out"])
            x_last = x.reshape(B, T, D_MODEL)[:, T - 1, :]
            new["sh"] = _rmsnorm(x_last, g_mix, B)[None]
            x, s = _rwkv_mix(x, rp, B, T, None if fresh else (r_s[0], r_sh[0]), min(256, n))
            new["rS"] = s[None]
        x = _ffn(x, p["norm_ffn"][li], p["ffn_w_gate_up"][li], p["ffn_w_down"][li], tm, 1408)
    y = _rmsnorm(x, p["norm_final"], tm)
    return y.reshape(B, T, D_MODEL), (new["C"], new["n"], new["m"], new["hS"], new["gS"], new["rS"], new["sh"])


def kernel(x_prompt, x_sample, state_mlstm_C, state_mlstm_n, state_mlstm_m, state_hgrn_S, state_gla_S, state_rwkv_S, state_rwkv_shift, norm_mix, norm_ffn, norm_final, mlstm_w_in, mlstm_b_gates, mlstm_norm, mlstm_w_out, hgrn_w_in, hgrn_lb_logits, hgrn_norm, hgrn_w_out, gla_w_in, gla_w_gate_up, gla_b_gate, gla_norm, gla_w_out, rwkv_mu, rwkv_w_rkv, rwkv_w_lora_a, rwkv_w_lora_b, rwkv_w0, rwkv_a_lora_a, rwkv_a_lora_b, rwkv_a0, rwkv_g_lora_a, rwkv_g_lora_b, rwkv_k_k, rwkv_k_a, rwkv_r_k, rwkv_ln_w, rwkv_ln_b, rwkv_w_out, ffn_w_gate_up, ffn_w_down):
    cast = lambda w: w.astype(bf16)
    p = dict(
        norm_mix=norm_mix, norm_ffn=norm_ffn, norm_final=norm_final,
        mlstm_w_in=cast(_pad_cols(mlstm_w_in[0], 3328)), mlstm_b_gates=mlstm_b_gates[0], mlstm_norm=mlstm_norm[0],
        mlstm_w_out=cast(mlstm_w_out[0]),
        hgrn_w_in=cast(hgrn_w_in[0]), hgrn_lb_logits=hgrn_lb_logits, hgrn_norm=hgrn_norm[0],
        hgrn_w_out=cast(hgrn_w_out[0]),
        gla_w_in=cast(_pad_cols(gla_w_in[0], 3328)), gla_w_gate_up=gla_w_gate_up[0], gla_b_gate=gla_b_gate[0],
        gla_norm=gla_norm[0], gla_w_out=cast(gla_w_out[0]),
        rwkv=dict(mu=rwkv_mu[0], wrkv=cast(rwkv_w_rkv[0]),
                  la=cast(jnp.concatenate([rwkv_w_lora_a[0], rwkv_a_lora_a[0], rwkv_g_lora_a[0]], axis=1)),
                  lbw=cast(rwkv_w_lora_b[0]), lba=cast(rwkv_a_lora_b[0]), lbg=cast(rwkv_g_lora_b[0]),
                  w0=rwkv_w0[0], a0=rwkv_a0[0], k_k=rwkv_k_k[0], k_a=rwkv_k_a[0], r_k=rwkv_r_k[0].reshape(-1),
                  ln_w=rwkv_ln_w[0], ln_b=rwkv_ln_b[0]),
        rwkv_w_out=cast(rwkv_w_out[0]),
        ffn_w_gate_up=cast(ffn_w_gate_up), ffn_w_down=cast(ffn_w_down),
    )
    bp, tp, _ = x_prompt.shape
    bs, ts, _ = x_sample.shape
    y_p, st_p = _trunk(x_prompt.reshape(bp * tp, D_MODEL), bp, tp, None, p)
    y_s, st_s = _trunk(x_sample.reshape(bs * ts, D_MODEL), bs, ts,
                       (state_mlstm_C, state_mlstm_n, state_mlstm_m, state_hgrn_S, state_gla_S, state_rwkv_S,
                        state_rwkv_shift), p)
    return (y_p, y_s) + st_p + st_s
```

```python
import functools

import jax
import jax.numpy as jnp
from jax import lax
from jax.experimental import pallas as pl
from jax.experimental.pallas import tpu as pltpu

f32 = jnp.float32
bf16 = jnp.bfloat16

D_MODEL = 1024
DEPTH = 4
NORM_EPS = 1e-6

MLSTM_HEADS, MLSTM_DK, MLSTM_DV, MLSTM_CHUNK = 4, 128, 256, 64
HGRN_HEADS, HGRN_DK, HGRN_DV = 8, 128, 128
GLA_HEADS, GLA_DK, GLA_DV = 4, 128, 256
GLA_GATE_RANK = 16
GLA_GATE_NORMALIZER = 16.0
LIN_CHUNK = 32
RWKV_HEADS, RWKV_N = 16, 64
RWKV_GN_EPS = 64e-5
FFN_HIDDEN = 2816

LANES = 128
SUBLANES = 8
BF16_ROWS = 16
SEQS_PER_STEP = 4
VMEM_LIMIT_BYTES = 52 * 1024 * 1024


def _cparams(*sem):
    return pltpu.CompilerParams(dimension_semantics=sem, vmem_limit_bytes=VMEM_LIMIT_BYTES)


def _chunk_len(t, cap):
    return max(d for d in range(1, min(cap, t) + 1) if t % d == 0)


def _rms(x, g):
    ms = jnp.mean(x * x, axis=-1, keepdims=True)
    return x * lax.rsqrt(ms + NORM_EPS) * g


def _sigmoid(x):
    return jax.nn.sigmoid(x)


def _softplus(x):
    return jnp.maximum(x, 0.0) + jnp.log1p(jnp.exp(-jnp.abs(x)))


def _log_sigmoid(x):
    return -_softplus(-x)


def _mm(a, b, mx=bf16):
    return jnp.dot(a.astype(mx), b.astype(mx), preferred_element_type=f32)


def _mm_nt(a, b, mx=bf16):
    return lax.dot_general(a.astype(mx), b.astype(mx), (((1,), (1,)), ((), ())), preferred_element_type=f32)


def _mm_tn(a, b, mx=bf16):
    return lax.dot_general(a.astype(mx), b.astype(mx), (((0,), (0,)), ((), ())), preferred_element_type=f32)


def _split3(x):
    hi = x.astype(bf16)
    r1 = x - hi.astype(f32)
    mid = r1.astype(bf16)
    lo = (r1 - mid.astype(f32)).astype(bf16)
    return hi, mid, lo


def _cumsum_rows(x, tril_b):
    hi, mid, lo = _split3(x)
    return (jnp.dot(tril_b, hi, preferred_element_type=f32)
            + jnp.dot(tril_b, mid, preferred_element_type=f32)
            + jnp.dot(tril_b, lo, preferred_element_type=f32))


def _norm_proj_kernel(x_ref, g_ref, w_ref, o_ref, hn_ref):
    @pl.when(pl.program_id(1) == 0)
    def _():
        hn_ref[...] = _rms(x_ref[...], g_ref[...]).astype(bf16)

    o_ref[...] = jnp.dot(hn_ref[...], w_ref[...], preferred_element_type=f32)


def _norm_proj(x, g, w, tm, tn):
    n, d = x.shape
    e = w.shape[1]
    return pl.pallas_call(
        _norm_proj_kernel,
        grid=(n // tm, e // tn),
        in_specs=[pl.BlockSpec((tm, d), lambda i, j: (i, 0)),
                  pl.BlockSpec((1, d), lambda i, j: (0, 0)),
                  pl.BlockSpec((d, tn), lambda i, j: (0, j))],
        out_specs=pl.BlockSpec((tm, tn), lambda i, j: (i, j)),
        out_shape=jax.ShapeDtypeStruct((n, e), f32),
        scratch_shapes=[pltpu.VMEM((tm, d), bf16)],
        compiler_params=_cparams("parallel", "arbitrary"),
        name="norm_proj",
    )(x, g.reshape(1, d), w)


def _out_proj_kernel(*refs, gated):
    if gated:
        y_ref, gate_ref, w_ref, res_ref, o_ref = refs
        y = y_ref[...] * gate_ref[...]
    else:
        y_ref, w_ref, res_ref, o_ref = refs
        y = y_ref[...]
    o_ref[...] = res_ref[...] + jnp.dot(y.astype(bf16), w_ref[...], preferred_element_type=f32)


def _out_proj(y, w, res, tm, gate=None):
    n, e = y.shape
    d = w.shape[1]
    row = lambda i: (i, 0)
    args = [y] + ([gate] if gate is not None else []) + [w, res]
    in_specs = ([pl.BlockSpec((tm, e), row)] + ([pl.BlockSpec((tm, e), row)] if gate is not None else [])
                + [pl.BlockSpec((e, d), lambda i: (0, 0)), pl.BlockSpec((tm, d), row)])
    return pl.pallas_call(
        functools.partial(_out_proj_kernel, gated=gate is not None),
        grid=(n // tm,),
        in_specs=in_specs,
        out_specs=pl.BlockSpec((tm, d), row),
        out_shape=jax.ShapeDtypeStruct((n, d), f32),
        compiler_params=_cparams("parallel"),
        name="out_proj",
    )(*args)


def _ffn_kernel(x_ref, g_ref, wg_ref, wu_ref, wd_ref, o_ref, hn_ref, acc_ref):
    j = pl.program_id(1)

    @pl.when(j == 0)
    def _():
        hn_ref[...] = _rms(x_ref[...], g_ref[...]).astype(bf16)
        acc_ref[...] = jnp.zeros_like(acc_ref)

    h = hn_ref[...]
    gt = jnp.dot(h, wg_ref[...], preferred_element_type=f32)
    up = jnp.dot(h, wu_ref[...], preferred_element_type=f32)
    act = (gt * _sigmoid(gt) * up).astype(bf16)
    acc_ref[...] += jnp.dot(act, wd_ref[...], preferred_element_type=f32)

    @pl.when(j == pl.num_programs(1) - 1)
    def _():
        o_ref[...] = x_ref[...] + acc_ref[...]


def _ffn(x, g, w_gu, w_down, tm, tf):
    n, d = x.shape
    nf = FFN_HIDDEN // tf
    return pl.pallas_call(
        _ffn_kernel,
        grid=(n // tm, nf),
        in_specs=[pl.BlockSpec((tm, d), lambda i, j: (i, 0)),
                  pl.BlockSpec((1, d), lambda i, j: (0, 0)),
                  pl.BlockSpec((d, tf), lambda i, j: (0, j)),
                  pl.BlockSpec((d, tf), lambda i, j: (0, j + nf)),
                  pl.BlockSpec((tf, d), lambda i, j: (j, 0))],
        out_specs=pl.BlockSpec((tm, d), lambda i, j: (i, 0)),
        out_shape=jax.ShapeDtypeStruct((n, d), f32),
        scratch_shapes=[pltpu.VMEM((tm, d), bf16), pltpu.VMEM((tm, d), f32)],
        compiler_params=_cparams("parallel", "arbitrary"),
        name="ffn",
    )(x, g.reshape(1, d), w_gu, w_gu, w_down)


def _rmsnorm_kernel(x_ref, g_ref, o_ref):
    o_ref[...] = _rms(x_ref[...], g_ref[...])


def _rmsnorm(x, g, tm):
    n, d = x.shape
    return pl.pallas_call(
        _rmsnorm_kernel,
        grid=(n // tm,),
        in_specs=[pl.BlockSpec((tm, d), lambda i: (i, 0)), pl.BlockSpec((1, d), lambda i: (0, 0))],
        out_specs=pl.BlockSpec((tm, d), lambda i: (i, 0)),
        out_shape=jax.ShapeDtypeStruct((n, d), f32),
        compiler_params=_cparams("parallel"),
        name="rmsnorm",
    )(x, g.reshape(1, d))


def _mlstm_kernel(*refs, L, nchunks, has_init, mx):
    H, DK, DV = MLSTM_HEADS, MLSTM_DK, MLSTM_DV
    q_ref, k_ref, v_ref, o_ref, gt_ref, bias_ref, gain_ref = refs[:7]
    if has_init:
        c0_ref, n0_ref, m0_ref = refs[7:10]
        rest = refs[10:]
    else:
        rest = refs[7:]
    y_ref, c_ref, n_ref, m_ref, c_s, n_s, m_s = rest
    j = pl.program_id(1)

    @pl.when(j == 0)
    def _():
        if has_init:
            c_s[...] = c0_ref[0]
            n_s[...] = n0_ref[0]
            m_s[...] = m0_ref[0]
        else:
            c_s[...] = jnp.zeros_like(c_s)
            n_s[...] = jnp.zeros_like(n_s)
            m_s[...] = jnp.zeros_like(m_s)

    Lp = max(L, BF16_ROWS)
    rowi = lax.broadcasted_iota(jnp.int32, (Lp, Lp), 0)
    coli = lax.broadcasted_iota(jnp.int32, (Lp, Lp), 1)
    causal = coli <= rowi
    eye = coli == rowi
    scale = DK ** -0.5

    def to_row(col):
        return jnp.sum(jnp.where(eye, col, 0.0), axis=0, keepdims=True)

    def pad_rows(x, value=0.0):
        if Lp == L:
            return x
        return jnp.concatenate([x, jnp.full((Lp - L, x.shape[1]), value, f32)], axis=0)

    def chunk(c, carry):
        rows = pl.ds(pl.multiple_of(c * L, L), L)
        gts_raw = gt_ref[rows, :] + bias_ref[...]
        lsg = pad_rows(_log_sigmoid(gts_raw))
        gts = pad_rows(gts_raw, -jnp.inf)
        for h in range(H):
            i_col = gts[:, h:h + 1]
            f_col = lsg[:, H + h:H + h + 1]
            f_row = to_row(f_col)
            fc_col = jnp.sum(jnp.where(causal, f_row, 0.0), axis=1, keepdims=True)
            fc_row = to_row(fc_col)
            i_row = to_row(i_col)
            m = m_s[h]
            dm = jnp.where(causal, fc_col - fc_row + i_row, -jnp.inf)
            g = fc_col + m
            mt = jnp.maximum(g, jnp.max(dm, axis=1, keepdims=True))
            p = jnp.exp(dm - mt)
            qc = pad_rows(q_ref[rows, h * DK:(h + 1) * DK] * scale)
            kc = pad_rows(k_ref[rows, h * DK:(h + 1) * DK])
            vc = pad_rows(v_ref[rows, h * DV:(h + 1) * DV])
            s = _mm_nt(qc, kc, mx) * p
            eg = jnp.exp(g - mt)
            cst = c_s[h]
            nst = n_s[h]
            num = eg * _mm_nt(qc, cst, mx) + _mm(s, vc, mx)
            den = eg * jnp.sum(qc * nst, axis=1, keepdims=True) + jnp.sum(s, axis=1, keepdims=True)
            hh = num / jnp.maximum(jnp.abs(den), jnp.exp(-mt))
            m_new = mt[L - 1:L, :]
            f_last = fc_col[L - 1:L, :]
            w_c = jnp.exp(f_last + m - m_new)
            w_j = jnp.exp(f_last - fc_col + i_col - m_new)
            c_s[h] = w_c * cst + _mm_tn(vc * w_j, kc, mx)
            n_s[h] = w_c * nst + jnp.sum(w_j * kc, axis=0, keepdims=True)
            m_s[h] = m_new
            hh = hh[:L]
            yn = hh * lax.rsqrt(jnp.mean(hh * hh, axis=1, keepdims=True) + NORM_EPS)
            yn = yn * gain_ref[:, h * DV:(h + 1) * DV]
            y_ref[rows, h * DV:(h + 1) * DV] = _sigmoid(o_ref[rows, h * DV:(h + 1) * DV]) * yn
        return carry

    lax.fori_loop(0, nchunks, chunk, 0)

    @pl.when(j == pl.num_programs(1) - 1)
    def _():
        c_ref[0] = c_s[...]
        n_ref[0] = n_s[...]
        m_ref[0] = m_s[...]


def _mlstm_mix(pr, b_gates, gain, B, T, state, tq):
    H, DK, DV = MLSTM_HEADS, MLSTM_DK, MLSTM_DV
    L = _chunk_len(T, MLSTM_CHUNK)
    nj = T // tq
    has_init = state is not None
    mx = bf16
    rmap = lambda c: (lambda b, j: (b * nj + j, c))
    bmap4 = lambda b, j: (b, 0, 0, 0)
    bias = jnp.zeros((1, LANES), f32).at[0, :2 * H].set(b_gates)
    in_specs = [pl.BlockSpec((tq, H * DK), rmap(0)), pl.BlockSpec((tq, H * DK), rmap(1)),
                pl.BlockSpec((tq, H * DV), rmap(1)), pl.BlockSpec((tq, H * DV), rmap(2)),
                pl.BlockSpec((tq, LANES), rmap((2 * H * DK + 2 * H * DV) // LANES)),
                pl.BlockSpec((1, LANES), lambda b, j: (0, 0)),
                pl.BlockSpec((1, H * DV), lambda b, j: (0, 0))]
    args = [pr, pr, pr, pr, pr, bias, gain.reshape(1, H * DV)]
    if has_init:
        c0, n0, m0 = state
        in_specs += [pl.BlockSpec((1, H, DV, DK), bmap4), pl.BlockSpec((1, H, 1, DK), bmap4),
                     pl.BlockSpec((1, H, 1, 1), bmap4)]
        args += [c0, n0.reshape(B, H, 1, DK), m0.reshape(B, H, 1, 1)]
    y, c, n, m = pl.pallas_call(
        functools.partial(_mlstm_kernel, L=L, nchunks=tq // L, has_init=has_init, mx=mx),
        grid=(B, nj),
        in_specs=in_specs,
        out_specs=[pl.BlockSpec((tq, H * DV), rmap(0)), pl.BlockSpec((1, H, DV, DK), bmap4),
                   pl.BlockSpec((1, H, 1, DK), bmap4), pl.BlockSpec((1, H, 1, 1), bmap4)],
        out_shape=[jax.ShapeDtypeStruct((B * T, H * DV), f32), jax.ShapeDtypeStruct((B, H, DV, DK), f32),
                   jax.ShapeDtypeStruct((B, H, 1, DK), f32), jax.ShapeDtypeStruct((B, H, 1, 1), f32)],
        scratch_shapes=[pltpu.VMEM((H, DV, DK), f32), pltpu.VMEM((H, 1, DK), f32), pltpu.VMEM((H, 1, 1), f32)],
        compiler_params=_cparams("parallel", "arbitrary"),
        name="mlstm_mix",
    )(*args)
    return y, c, n.reshape(B, H, DK), m.reshape(B, H)


def _mlstm_block_kernel(q_ref, k_ref, v_ref, o_ref, gt_ref, bias_ref, gain_ref, x_ref, wout_ref,
                        xo_ref, c_ref, n_ref, m_ref, c_s, n_s, m_s, y_ref, *, L, nc):
    H, DK, DV = MLSTM_HEADS, MLSTM_DK, MLSTM_DV
    P = 2 * L
    j = pl.program_id(1)

    @pl.when(j == 0)
    def _():
        c_s[...] = jnp.zeros_like(c_s)
        n_s[...] = jnp.zeros_like(n_s)
        m_s[...] = jnp.zeros_like(m_s)

    rowi = lax.broadcasted_iota(jnp.int32, (L, L), 0)
    coli = lax.broadcasted_iota(jnp.int32, (L, L), 1)
    causal = coli <= rowi
    r2 = lax.broadcasted_iota(jnp.int32, (P, P), 0)
    c2 = lax.broadcasted_iota(jnp.int32, (P, P), 1)
    pair_tril = ((c2 <= r2) & ((c2 >= L) == (r2 >= L))).astype(bf16)
    scale = DK ** -0.5

    gates = []
    for pp in range(nc // 2):
        g = gt_ref[pp * P:(pp + 1) * P, :] + bias_ref[...]
        hi, mid, lo = _split3(_log_sigmoid(g))
        fcum = (jnp.dot(pair_tril, hi, preferred_element_type=f32)
                + jnp.dot(pair_tril, mid, preferred_element_type=f32)
                + jnp.dot(pair_tril, lo, preferred_element_type=f32))
        g_t = g.T
        f_t = fcum.T
        for half in range(2):
            rs = slice(half * L, (half + 1) * L)
            gates.append((g[rs, :], fcum[rs, :], g_t[:, rs], f_t[:, rs]))

    pairs = [(c, h) for c in range(nc) for h in range(H)]
    rows_of = lambda c: slice(c * L, (c + 1) * L)
    ks_of = lambda h: slice(h * DK, (h + 1) * DK)
    vs_of = lambda h: slice(h * DV, (h + 1) * DV)
    ones_b = jnp.ones((L, LANES), bf16)
    wide = lambda x: jnp.concatenate([x] * (DV // LANES), axis=1)
    dms, rmax, fcols, icols, qks, qbs = {}, {}, {}, {}, {}, {}
    for c, h in pairs:
        g_c, f_c, g_r, f_r = gates[c]
        icols[c, h] = jnp.broadcast_to(g_c[:, h:h + 1], (L, LANES))
        fcols[c, h] = jnp.broadcast_to(f_c[:, H + h:H + h + 1], (L, LANES))
        dm = jnp.where(causal, fcols[c, h][:, :L] - f_r[H + h:H + h + 1, :] + g_r[h:h + 1, :], -jnp.inf)
        dms[c, h] = dm
        rmax[c, h] = jnp.broadcast_to(jnp.max(dm, axis=1, keepdims=True), (L, LANES))
        qbs[c, h] = (q_ref[rows_of(c), ks_of(h)] * scale).astype(bf16)
        qks[c, h] = _mm_nt(qbs[c, h], k_ref[rows_of(c), ks_of(h)])
    m_in, m_out = {}, {}
    for h in range(H):
        m = jnp.broadcast_to(m_s[h], (1, LANES))
        for c in range(nc):
            m_in[c, h] = m
            m = jnp.maximum(fcols[c, h][L - 1:L, :] + m, rmax[c, h][L - 1:L, :])
            m_out[c, h] = m
        m_s[h] = m[:, :1]
    mts, egs, ss, ssums, wcs, dcs, dns = {}, {}, {}, {}, {}, {}, {}
    for c, h in pairs:
        fc_col = fcols[c, h]
        gg = fc_col + m_in[c, h]
        mt = jnp.maximum(gg, rmax[c, h])
        mts[c, h] = mt
        egs[c, h] = jnp.exp(gg - mt)
        s = (qks[c, h] * jnp.exp(dms[c, h] - mt[:, :L])).astype(bf16)
        ss[c, h] = s
        ssums[c, h] = jnp.dot(s, ones_b, preferred_element_type=f32)
        f_last = fc_col[L - 1:L, :]
        wcs[c, h] = jnp.exp(f_last + m_in[c, h] - m_out[c, h])
        w_j = jnp.exp(f_last - fc_col + icols[c, h] - m_out[c, h])
        kc = k_ref[rows_of(c), ks_of(h)]
        dcs[c, h] = _mm_tn(v_ref[rows_of(c), vs_of(h)] * wide(w_j), kc)
        dns[c, h] = jnp.sum(w_j * kc, axis=0, keepdims=True)
    c_in, n_in = {}, {}
    for h in range(H):
        cst = c_s[h]
        nst = n_s[h]
        for c in range(nc):
            c_in[c, h] = cst.astype(bf16)
            n_in[c, h] = nst
            cst = wcs[c, h] * cst + dcs[c, h]
            nst = wcs[c, h] * nst + dns[c, h]
        c_s[h] = cst
        n_s[h] = nst
    hhs, msq = {}, {}
    mean_b = jnp.full((DV, LANES), 1.0 / DV, bf16)
    for c, h in pairs:
        rows, vs = rows_of(c), vs_of(h)
        qn = _mm_nt(qbs[c, h], jnp.broadcast_to(n_in[c, h], (LANES, DK)))
        num = wide(egs[c, h]) * _mm_nt(qbs[c, h], c_in[c, h]) + _mm(ss[c, h], v_ref[rows, vs])
        den = egs[c, h] * qn + ssums[c, h]
        hh = num / wide(jnp.maximum(jnp.abs(den), jnp.exp(-mts[c, h])))
        hhs[c, h] = hh
        msq[c, h] = jnp.dot((hh * hh).astype(bf16), mean_b, preferred_element_type=f32)
    for c, h in pairs:
        rows, vs = rows_of(c), vs_of(h)
        yn = hhs[c, h] * wide(lax.rsqrt(msq[c, h] + NORM_EPS)) * gain_ref[:, vs]
        y_ref[rows, vs] = _sigmoid(o_ref[rows, vs]) * yn
    xo_ref[...] = x_ref[...] + jnp.dot(y_ref[...].astype(bf16), wout_ref[...], preferred_element_type=f32)

    @pl.when(j == pl.num_programs(1) - 1)
    def _():
        c_ref[0] = c_s[...]
        n_ref[0] = n_s[...]
        m_ref[0] = m_s[...]


def _mlstm_mix_fresh(pr, b_gates, gain, x, w_out, B, T, tq):
    H, DK, DV = MLSTM_HEADS, MLSTM_DK, MLSTM_DV
    L = MLSTM_CHUNK
    nj = T // tq
    rmap = lambda c: (lambda b, j: (b * nj + j, c))
    bmap4 = lambda b, j: (b, 0, 0, 0)
    bias = jnp.zeros((1, LANES), f32).at[0, :2 * H].set(b_gates)
    y, c, n, m = pl.pallas_call(
        functools.partial(_mlstm_block_kernel, L=L, nc=tq // L),
        grid=(B, nj),
        in_specs=[pl.BlockSpec((tq, H * DK), rmap(0)), pl.BlockSpec((tq, H * DK), rmap(1)),
                  pl.BlockSpec((tq, H * DV), rmap(1)), pl.BlockSpec((tq, H * DV), rmap(2)),
                  pl.BlockSpec((tq, LANES), rmap((2 * H * DK + 2 * H * DV) // LANES)),
                  pl.BlockSpec((1, LANES), lambda b, j: (0, 0)),
                  pl.BlockSpec((1, H * DV), lambda b, j: (0, 0)),
                  pl.BlockSpec((tq, D_MODEL), rmap(0)),
                  pl.BlockSpec((H * DV, D_MODEL), lambda b, j: (0, 0))],
        out_specs=[pl.BlockSpec((tq, D_MODEL), rmap(0)), pl.BlockSpec((1, H, DV, DK), bmap4),
                   pl.BlockSpec((1, H, 1, DK), bmap4), pl.BlockSpec((1, H, 1, 1), bmap4)],
        out_shape=[jax.ShapeDtypeStruct((B * T, D_MODEL), f32), jax.ShapeDtypeStruct((B, H, DV, DK), f32),
                   jax.ShapeDtypeStruct((B, H, 1, DK), f32), jax.ShapeDtypeStruct((B, H, 1, 1), f32)],
        scratch_shapes=[pltpu.VMEM((H, DV, DK), f32), pltpu.VMEM((H, 1, DK), f32), pltpu.VMEM((H, 1, 1), f32),
                        pltpu.VMEM((tq, H * DV), f32)],
        compiler_params=_cparams("parallel", "arbitrary"),
        name="mlstm_mix_fresh",
    )(pr, pr, pr, pr, pr, bias, gain.reshape(1, H * DV), x, w_out)
    return y, c, n.reshape(B, H, DK), m.reshape(B, H)


def _mlstm_seq_kernel(q_ref, k_ref, v_ref, o_ref, gt_ref, bias_ref, gain_ref, c0_ref, n0_ref, m0_ref,
                      y_ref, c_ref, n_ref, m_ref, *, T, bs):
    H, DK, DV = MLSTM_HEADS, MLSTM_DK, MLSTM_DV
    P = BF16_ROWS
    R = bs * T
    ri = lax.broadcasted_iota(jnp.int32, (R, R), 0)
    ci = lax.broadcasted_iota(jnp.int32, (R, R), 1)
    tril_b = (((ri // T) == (ci // T)) & (ci <= ri)).astype(bf16)
    rp = lax.broadcasted_iota(jnp.int32, (P, P), 0)
    cp = lax.broadcasted_iota(jnp.int32, (P, P), 1)
    causal = cp <= rp
    real_col = lax.broadcasted_iota(jnp.int32, (1, P), 1) < T
    scale = DK ** -0.5
    ones_b = jnp.ones((P, LANES), bf16)
    mean_b = jnp.full((DV, LANES), 1.0 / DV, bf16)
    wide = lambda x: jnp.concatenate([x] * (DV // LANES), axis=1)

    g = gt_ref[...] + bias_ref[...]
    fcum = _cumsum_rows(_log_sigmoid(g), tril_b)
    zrows = jnp.zeros((LANES - R, LANES), f32)
    g_t = jnp.concatenate([g, zrows], axis=0).T
    f_t = jnp.concatenate([fcum, zrows], axis=0).T

    def pad_rows(x, value=0.0):
        return jnp.concatenate([x, jnp.full((P - T, x.shape[1]), value, f32)], axis=0)

    pairs = [(c, h) for c in range(bs) for h in range(H)]
    ks_of = lambda h: slice(h * DK, (h + 1) * DK)
    vs_of = lambda h: slice(h * DV, (h + 1) * DV)
    st = {}
    for c, h in pairs:
        rows = slice(c * T, (c + 1) * T)
        last = slice((c + 1) * T - 1, (c + 1) * T)
        i_col = jnp.broadcast_to(pad_rows(g[rows, h:h + 1], -jnp.inf), (P, LANES))
        f_last = jnp.broadcast_to(fcum[last, H + h:H + h + 1], (1, LANES))
        fc_col = jnp.concatenate([jnp.broadcast_to(fcum[rows, H + h:H + h + 1], (T, LANES)),
                                  jnp.broadcast_to(f_last, (P - T, LANES))], axis=0)
        i_row = jnp.where(real_col, g_t[h:h + 1, c * T:c * T + P], -jnp.inf)
        fc_row = f_t[H + h:H + h + 1, c * T:c * T + P]
        dm = jnp.where(causal, fc_col[:, :P] - fc_row + i_row, -jnp.inf)
        rmax = jnp.broadcast_to(jnp.max(dm, axis=1, keepdims=True), (P, LANES))
        qb = pad_rows(q_ref[rows, ks_of(h)] * scale).astype(bf16)
        kc = pad_rows(k_ref[rows, ks_of(h)])
        vc = pad_rows(v_ref[rows, vs_of(h)])
        st[c, h] = dict(i_col=i_col, f_last=f_last, fc_col=fc_col, dm=dm, rmax=rmax, qb=qb, kc=kc, vc=vc,
                        qk=_mm_nt(qb, kc))
    for c, h in pairs:
        d = st[c, h]
        m = jnp.broadcast_to(m0_ref[c, h], (1, LANES))
        gg = d["fc_col"] + m
        mt = jnp.maximum(gg, d["rmax"])
        m_new = mt[T - 1:T, :]
        d["mt"] = mt
        d["eg"] = jnp.exp(gg - mt)
        s = (d["qk"] * jnp.exp(d["dm"] - mt[:, :P])).astype(bf16)
        d["s"] = s
        d["ssum"] = jnp.dot(s, ones_b, preferred_element_type=f32)
        w_c = jnp.exp(d["f_last"] + m - m_new)
        w_j = jnp.exp(d["f_last"] - d["fc_col"] + d["i_col"] - m_new)
        c_ref[c, h] = w_c * c0_ref[c, h] + _mm_tn(d["vc"] * wide(w_j), d["kc"])
        n_ref[c, h] = w_c * n0_ref[c, h] + jnp.sum(w_j * d["kc"], axis=0, keepdims=True)
        m_ref[c, h] = m_new[:, :1]
    for c, h in pairs:
        d = st[c, h]
        qn = _mm_nt(d["qb"], jnp.broadcast_to(n0_ref[c, h], (LANES, DK)))
        num = wide(d["eg"]) * _mm_nt(d["qb"], c0_ref[c, h]) + _mm(d["s"], d["vc"])
        den = d["eg"] * qn + d["ssum"]
        hh = num / wide(jnp.maximum(jnp.abs(den), jnp.exp(-d["mt"])))
        d["hh"] = hh
        d["msq"] = jnp.dot((hh * hh).astype(bf16), mean_b, preferred_element_type=f32)
    for c, h in pairs:
        d = st[c, h]
        rows, vs = slice(c * T, (c + 1) * T), vs_of(h)
        yn = (d["hh"] * wide(lax.rsqrt(d["msq"] + NORM_EPS)))[:T] * gain_ref[:, vs]
        y_ref[rows, vs] = _sigmoid(o_ref[rows, vs]) * yn


def _mlstm_mix_seq(pr, b_gates, gain, B, T, state, bs):
    H, DK, DV = MLSTM_HEADS, MLSTM_DK, MLSTM_DV
    R = bs * T
    rmap = lambda c: (lambda i: (i, c))
    bmap4 = lambda i: (i, 0, 0, 0)
    const2 = lambda i: (0, 0)
    bias = jnp.zeros((1, LANES), f32).at[0, :2 * H].set(b_gates)
    c0, n0, m0 = state
    y, c, n, m = pl.pallas_call(
        functools.partial(_mlstm_seq_kernel, T=T, bs=bs),
        grid=(B // bs,),
        in_specs=[pl.BlockSpec((R, H * DK), rmap(0)), pl.BlockSpec((R, H * DK), rmap(1)),
                  pl.BlockSpec((R, H * DV), rmap(1)), pl.BlockSpec((R, H * DV), rmap(2)),
                  pl.BlockSpec((R, LANES), rmap((2 * H * DK + 2 * H * DV) // LANES)),
                  pl.BlockSpec((1, LANES), const2), pl.BlockSpec((1, H * DV), const2),
                  pl.BlockSpec((bs, H, DV, DK), bmap4), pl.BlockSpec((bs, H, 1, DK), bmap4),
                  pl.BlockSpec((bs, H, 1, 1), bmap4)],
        out_specs=[pl.BlockSpec((R, H * DV), rmap(0)), pl.BlockSpec((bs, H, DV, DK), bmap4),
                   pl.BlockSpec((bs, H, 1, DK), bmap4), pl.BlockSpec((bs, H, 1, 1), bmap4)],
        out_shape=[jax.ShapeDtypeStruct((B * T, H * DV), f32), jax.ShapeDtypeStruct((B, H, DV, DK), f32),
                   jax.ShapeDtypeStruct((B, H, 1, DK), f32), jax.ShapeDtypeStruct((B, H, 1, 1), f32)],
        compiler_params=_cparams("parallel"),
        name="mlstm_mix_seq",
    )(pr, pr, pr, pr, pr, bias, gain.reshape(1, H * DV), c0, n0.reshape(B, H, 1, DK), m0.reshape(B, H, 1, 1))
    return y, c, n.reshape(B, H, DK), m.reshape(B, H)


def _gla_kernel(*refs, kind, layer, L, nchunks, H, DK, DV, has_init, mx):
    if kind == "hgrn":
        q_ref, f_ref, v_ref, g_ref, lbl_ref, gain_ref = refs[:6]
        rest = refs[6:]
    else:
        q_ref, k_ref, v_ref, g_ref, gr_ref, wup_ref, bgate_ref, gain_ref = refs[:8]
        rest = refs[8:]
    if has_init:
        s0_ref = rest[0]
        rest = rest[1:]
    y_ref, s_ref, st_s = rest
    j = pl.program_id(1)

    @pl.when(j == 0)
    def _():
        for h in range(H):
            if has_init:
                st_s[h] = s0_ref[0, h].T
            else:
                st_s[h] = jnp.zeros((DV, DK), f32)

    Lp = max(L, BF16_ROWS)
    rowi = lax.broadcasted_iota(jnp.int32, (Lp, Lp), 0)
    coli = lax.broadcasted_iota(jnp.int32, (Lp, Lp), 1)
    causal = coli <= rowi
    tril_b = causal.astype(bf16)

    def pad_rows(x):
        if Lp == L:
            return x
        return jnp.concatenate([x, jnp.zeros((Lp - L, x.shape[1]), f32)], axis=0)

    if kind == "hgrn":
        logits = lbl_ref[...]
        e = jnp.exp(logits - jnp.max(logits, axis=0, keepdims=True))
        sm = e / jnp.sum(e, axis=0, keepdims=True)
        lb_all = jnp.zeros((1, H * DK), f32)
        for li in range(layer):
            lb_all = lb_all + sm[li:li + 1, :]
    else:
        scale = DK ** -0.5

    def chunk(c, carry):
        rows = pl.ds(pl.multiple_of(c * L, L), L)
        if kind == "gla":
            gk = _mm(pad_rows(gr_ref[rows, :]), wup_ref[...], mx)[:L] + bgate_ref[...]
            ld_all = _log_sigmoid(gk) * (1.0 / GLA_GATE_NORMALIZER)
        for h in range(H):
            ks = slice(h * DK, (h + 1) * DK)
            vs = slice(h * DV, (h + 1) * DV)
            if kind == "hgrn":
                qraw = q_ref[rows, ks]
                fg = f_ref[rows, ks]
                lb = lb_all[:, ks]
                gc = jnp.log(lb + (1.0 - lb) * _sigmoid(fg))
                kc = (1.0 - lb) * _sigmoid(-fg)
                qc = qraw * _sigmoid(qraw)
            else:
                qc = q_ref[rows, ks] * scale
                kc = k_ref[rows, ks]
                gc = ld_all[:, ks]
            vc = pad_rows(v_ref[rows, vs])
            qc, kc, gc = pad_rows(qc), pad_rows(kc), pad_rows(gc)
            bc = _cumsum_rows(gc, tril_b)
            qe = qc * jnp.exp(bc)
            ke = kc * jnp.exp(-bc)
            a = jnp.where(causal, _mm_nt(qe, ke, mx), 0.0)
            st = st_s[h]
            o = (_mm_nt(qe, st, mx) + _mm(a, vc, mx))[:L]
            b_last = bc[L - 1:L, :]
            kdec = kc * jnp.exp(b_last - bc)
            st_s[h] = st * jnp.exp(b_last) + _mm_tn(vc, kdec, mx)
            yn = o * lax.rsqrt(jnp.mean(o * o, axis=1, keepdims=True) + NORM_EPS) * gain_ref[:, vs]
            gg = g_ref[rows, vs]
            y_ref[rows, vs] = yn * (gg * _sigmoid(gg))
        return carry

    lax.fori_loop(0, nchunks, chunk, 0)

    @pl.when(j == pl.num_programs(1) - 1)
    def _():
        for h in range(H):
            s_ref[0, h] = st_s[h].T


def _gla_mix(kind, pr, extra, gain, B, T, state, tq, layer=0):
    if kind == "hgrn":
        H, DK, DV = HGRN_HEADS, HGRN_DK, HGRN_DV
    else:
        H, DK, DV = GLA_HEADS, GLA_DK, GLA_DV
    L = _chunk_len(T, LIN_CHUNK)
    nj = T // tq
    has_init = state is not None
    mx = bf16
    rmap = lambda c: (lambda b, j: (b * nj + j, c))
    bmap4 = lambda b, j: (b, 0, 0, 0)
    const2 = lambda b, j: (0, 0)
    hk, hv = H * DK, H * DV
    if kind == "hgrn":
        (lb_logits,) = extra
        in_specs = [pl.BlockSpec((tq, hk), rmap(0)), pl.BlockSpec((tq, hk), rmap(1)),
                    pl.BlockSpec((tq, hv), rmap(2)), pl.BlockSpec((tq, hv), rmap(3)),
                    pl.BlockSpec((DEPTH, hk), const2), pl.BlockSpec((1, hv), const2)]
        args = [pr, pr, pr, pr, lb_logits, gain.reshape(1, hv)]
    else:
        w_up, b_gate = extra
        w_up_p = jnp.zeros((LANES, hk), f32).at[:GLA_GATE_RANK].set(w_up)
        in_specs = [pl.BlockSpec((tq, hk), rmap(0)), pl.BlockSpec((tq, hk), rmap(1)),
                    pl.BlockSpec((tq, hv), rmap(1)), pl.BlockSpec((tq, hv), rmap(2)),
                    pl.BlockSpec((tq, LANES), rmap((2 * hk + 2 * hv) // LANES)),
                    pl.BlockSpec((LANES, hk), const2), pl.BlockSpec((1, hk), const2),
                    pl.BlockSpec((1, hv), const2)]
        args = [pr, pr, pr, pr, pr, w_up_p, b_gate.reshape(1, hk), gain.reshape(1, hv)]
    if has_init:
        in_specs.append(pl.BlockSpec((1, H, DK, DV), bmap4))
        args.append(state)
    y, s = pl.pallas_call(
        functools.partial(_gla_kernel, kind=kind, layer=layer, L=L, nchunks=tq // L, H=H, DK=DK, DV=DV,
                          has_init=has_init, mx=mx),
        grid=(B, nj),
        in_specs=in_specs,
        out_specs=[pl.BlockSpec((tq, hv), rmap(0)), pl.BlockSpec((1, H, DK, DV), bmap4)],
        out_shape=[jax.ShapeDtypeStruct((B * T, hv), f32), jax.ShapeDtypeStruct((B, H, DK, DV), f32)],
        scratch_shapes=[pltpu.VMEM((H, DV, DK), f32)],
        compiler_params=_cparams("parallel", "arbitrary"),
        name=kind + "_mix",
    )(*args)
    return y, s


def _gla_seq_kernel(*refs, kind, layer, T, bs, H, DK, DV):
    if kind == "hgrn":
        q_ref, f_ref, v_ref, g_ref, lbl_ref, gain_ref = refs[:6]
        rest = refs[6:]
    else:
        q_ref, k_ref, v_ref, g_ref, gr_ref, wup_ref, bgate_ref, gain_ref = refs[:8]
        rest = refs[8:]
    s0_ref, y_ref, s_ref, qe_s, ke_s, kd_s, el_s, v_s = rest
    P = BF16_ROWS
    R = bs * T
    hk = H * DK

    ri = lax.broadcasted_iota(jnp.int32, (R, R), 0)
    ci = lax.broadcasted_iota(jnp.int32, (R, R), 1)
    same_seq = (ri // T) == (ci // T)
    tril_b = (same_seq & (ci <= ri)).astype(bf16)
    seq_b = same_seq.astype(bf16)
    rp = lax.broadcasted_iota(jnp.int32, (P, P), 0)
    cp = lax.broadcasted_iota(jnp.int32, (P, P), 1)
    causal = cp <= rp

    if kind == "hgrn":
        logits = lbl_ref[...]
        e = jnp.exp(logits - jnp.max(logits, axis=0, keepdims=True))
        sm = e / jnp.sum(e, axis=0, keepdims=True)
        lb = jnp.zeros((1, hk), f32)
        for li in range(layer):
            lb = lb + sm[li:li + 1, :]
        qraw = q_ref[...]
        fg = f_ref[...]
        gc = jnp.log(lb + (1.0 - lb) * _sigmoid(fg))
        kc = (1.0 - lb) * _sigmoid(-fg)
        qc = qraw * _sigmoid(qraw)
    else:
        gk = _mm(gr_ref[...], wup_ref[...]) + bgate_ref[...]
        gc = _log_sigmoid(gk) * (1.0 / GLA_GATE_NORMALIZER)
        qc = q_ref[...] * (DK ** -0.5)
        kc = k_ref[...]
    pieces = _split3(gc)
    bc = sum(jnp.dot(tril_b, x, preferred_element_type=f32) for x in pieces)
    bl = sum(jnp.dot(seq_b, x, preferred_element_type=f32) for x in pieces)
    qe = qc * jnp.exp(bc)
    ke = kc * jnp.exp(-bc)
    kd = kc * jnp.exp(bl - bc)
    e_hi, e_mid, e_lo = [x.astype(f32) for x in _split3(jnp.exp(bl))]
    vv = v_ref[...]
    zpad_k = jnp.zeros((P - T, hk), f32)
    zpad_v = jnp.zeros((P - T, vv.shape[1]), f32)
    prow = lax.broadcasted_iota(jnp.int32, (P, hk), 0)
    for c in range(bs):
        src = slice(c * T, (c + 1) * T)
        dst = slice(c * P, (c + 1) * P)
        qe_s[dst, :] = jnp.concatenate([qe[src], zpad_k], axis=0)
        ke_s[dst, :] = jnp.concatenate([ke[src], zpad_k], axis=0)
        kd_s[dst, :] = jnp.concatenate([kd[src], zpad_k], axis=0)
        v_s[dst, :] = jnp.concatenate([vv[src], zpad_v], axis=0)
        first = slice(c * T, c * T + 1)
        el_s[dst, :] = jnp.where(prow == 0, e_hi[first],
                                 jnp.where(prow == 1, e_mid[first], jnp.where(prow == 2, e_lo[first], 0.0)))

    ones_b = jnp.ones((P, LANES), bf16)
    pairs = [(c, h) for c in range(bs) for h in range(H)]
    rows_of = lambda c: slice(c * P, (c + 1) * P)
    ks_of = lambda h: slice(h * DK, (h + 1) * DK)
    vs_of = lambda h: slice(h * DV, (h + 1) * DV)
    qes, vbs, amats, news = {}, {}, {}, {}
    for c, h in pairs:
        rows, ks = rows_of(c), ks_of(h)
        qes[c, h] = qe_s[rows, ks].astype(bf16)
        vbs[c, h] = v_s[rows, vs_of(h)].astype(bf16)
        a = _mm_nt(qes[c, h], ke_s[rows, ks])
        amats[c, h] = jnp.where(causal, a, 0.0).astype(bf16)
        decay = _mm_tn(el_s[rows, ks], ones_b)
        decay = jnp.concatenate([decay] * (DV // LANES), axis=1)
        news[c, h] = s0_ref[c, h] * decay + _mm_tn(kd_s[rows, ks], vbs[c, h])
    for c, h in pairs:
        s_ref[c, h] = news[c, h]
    outs = {}
    for c, h in pairs:
        outs[c, h] = (_mm(qes[c, h], s0_ref[c, h]) + _mm(amats[c, h], vbs[c, h]))[:T]
    for c, h in pairs:
        vs = vs_of(h)
        rows = slice(c * T, (c + 1) * T)
        o = outs[c, h]
        yn = o * lax.rsqrt(jnp.mean(o * o, axis=1, keepdims=True) + NORM_EPS) * gain_ref[:, vs]
        gg = g_ref[rows, vs]
        y_ref[rows, vs] = yn * (gg * _sigmoid(gg))


def _gla_mix_seq(kind, pr, extra, gain, B, T, state, bs, layer=0):
    if kind == "hgrn":
        H, DK, DV = HGRN_HEADS, HGRN_DK, HGRN_DV
    else:
        H, DK, DV = GLA_HEADS, GLA_DK, GLA_DV
    R = bs * T
    rmap = lambda c: (lambda i: (i, c))
    bmap4 = lambda i: (i, 0, 0, 0)
    const2 = lambda i: (0, 0)
    hk, hv = H * DK, H * DV
    if kind == "hgrn":
        (lb_logits,) = extra
        in_specs = [pl.BlockSpec((R, hk), rmap(0)), pl.BlockSpec((R, hk), rmap(1)),
                    pl.BlockSpec((R, hv), rmap(2)), pl.BlockSpec((R, hv), rmap(3)),
                    pl.BlockSpec((DEPTH, hk), const2), pl.BlockSpec((1, hv), const2)]
        args = [pr, pr, pr, pr, lb_logits, gain.reshape(1, hv)]
    else:
        w_up, b_gate = extra
        w_up_p = jnp.zeros((LANES, hk), f32).at[:GLA_GATE_RANK].set(w_up)
        in_specs = [pl.BlockSpec((R, hk), rmap(0)), pl.BlockSpec((R, hk), rmap(1)),
                    pl.BlockSpec((R, hv), rmap(1)), pl.BlockSpec((R, hv), rmap(2)),
                    pl.BlockSpec((R, LANES), rmap((2 * hk + 2 * hv) // LANES)),
                    pl.BlockSpec((LANES, hk), const2), pl.BlockSpec((1, hk), const2),
                    pl.BlockSpec((1, hv), const2)]
        args = [pr, pr, pr, pr, pr, w_up_p, b_gate.reshape(1, hk), gain.reshape(1, hv)]
    in_specs.append(pl.BlockSpec((bs, H, DK, DV), bmap4))
    args.append(state)
    P = BF16_ROWS
    return pl.pallas_call(
        functools.partial(_gla_seq_kernel, kind=kind, layer=layer, T=T, bs=bs, H=H, DK=DK, DV=DV),
        grid=(B // bs,),
        in_specs=in_specs,
        out_specs=[pl.BlockSpec((R, hv), rmap(0)), pl.BlockSpec((bs, H, DK, DV), bmap4)],
        out_shape=[jax.ShapeDtypeStruct((B * T, hv), f32), jax.ShapeDtypeStruct((B, H, DK, DV), f32)],
        scratch_shapes=[pltpu.VMEM((bs * P, hk), f32)] * 4 + [pltpu.VMEM((bs * P, hv), f32)],
        compiler_params=_cparams("parallel"),
        name=kind + "_mix_seq",
    )(*args)


def _gla_block_kernel(*refs, kind, layer, L, nc, H, DK, DV):
    if kind == "hgrn":
        q_ref, f_ref, v_ref, g_ref, lbl_ref, gain_ref = refs[:6]
        rest = refs[6:]
    else:
        q_ref, k_ref, v_ref, g_ref, gr_ref, wup_ref, bgate_ref, gain_ref = refs[:8]
        rest = refs[8:]
    x_ref, wout_ref, xo_ref, s_ref, st_s, qe_s, ke_s, qs_s, kd_s, el_s, y_ref = rest
    j = pl.program_id(1)

    @pl.when(j == 0)
    def _():
        st_s[...] = jnp.zeros_like(st_s)

    rowi = lax.broadcasted_iota(jnp.int32, (L, L), 0)
    coli = lax.broadcasted_iota(jnp.int32, (L, L), 1)
    causal = coli <= rowi
    tril_b = causal.astype(bf16)

    if kind == "hgrn":
        logits = lbl_ref[...]
        e = jnp.exp(logits - jnp.max(logits, axis=0, keepdims=True))
        sm = e / jnp.sum(e, axis=0, keepdims=True)
        lb = jnp.zeros((1, H * DK), f32)
        for li in range(layer):
            lb = lb + sm[li:li + 1, :]
    else:
        scale = DK ** -0.5

    for c in range(nc):
        rows = slice(c * L, (c + 1) * L)
        if kind == "hgrn":
            qraw = q_ref[rows, :]
            fg = f_ref[rows, :]
            gc = jnp.log(lb + (1.0 - lb) * _sigmoid(fg))
            kc = (1.0 - lb) * _sigmoid(-fg)
            qc = qraw * _sigmoid(qraw)
        else:
            gk = jnp.dot(gr_ref[rows, :].astype(bf16), wup_ref[...].astype(bf16),
                         preferred_element_type=f32) + bgate_ref[...]
            gc = _log_sigmoid(gk) * (1.0 / GLA_GATE_NORMALIZER)
            qc = q_ref[rows, :] * scale
            kc = k_ref[rows, :]
        hi, mid, lo = _split3(gc)
        bc = (jnp.dot(tril_b, hi, preferred_element_type=f32) + jnp.dot(tril_b, mid, preferred_element_type=f32)
              + jnp.dot(tril_b, lo, preferred_element_type=f32))
        b_mid = bc[L // 2 - 1:L // 2, :]
        b_last = bc[L - 1:L, :]
        qe_s[rows, :] = (qc * jnp.exp(bc - b_mid)).astype(bf16)
        ke_s[rows, :] = (kc * jnp.exp(b_mid - bc)).astype(bf16)
        qs_s[rows, :] = (qc * jnp.exp(bc)).astype(bf16)
        kd_s[rows, :] = (kc * jnp.exp(b_last - bc)).astype(bf16)
        el_s[c] = jnp.exp(b_last)

    dn_nt = (((1,), (1,)), ((), ()))
    dn_tn = (((0,), (0,)), ((), ()))
    pairs = [(c, h) for c in range(nc) for h in range(H)]
    rows_of = lambda c: slice(c * L, (c + 1) * L)
    ks_of = lambda h: slice(h * DK, (h + 1) * DK)
    vs_of = lambda h: slice(h * DV, (h + 1) * DV)
    vbs, amats, dsts = {}, {}, {}
    for c, h in pairs:
        rows, ks = rows_of(c), ks_of(h)
        vb = v_ref[rows, vs_of(h)].astype(bf16)
        a = lax.dot_general(qe_s[rows, ks], ke_s[rows, ks], dn_nt, preferred_element_type=f32)
        vbs[c, h] = vb
        amats[c, h] = jnp.where(causal, a, 0.0).astype(bf16)
        dsts[c, h] = lax.dot_general(vb, kd_s[rows, ks], dn_tn, preferred_element_type=f32)
    sts = {}
    for h in range(H):
        st = st_s[h]
        for c in range(nc):
            sts[c, h] = st.astype(bf16)
            st = st * el_s[c][:, ks_of(h)] + dsts[c, h]
        st_s[h] = st
    outs = {}
    for c, h in pairs:
        rows, ks = rows_of(c), ks_of(h)
        outs[c, h] = (lax.dot_general(qs_s[rows, ks], sts[c, h], dn_nt, preferred_element_type=f32)
                      + jnp.dot(amats[c, h], vbs[c, h], preferred_element_type=f32))
    for c, h in pairs:
        rows, vs = rows_of(c), vs_of(h)
        o = outs[c, h]
        yn = o * lax.rsqrt(jnp.mean(o * o, axis=1, keepdims=True) + NORM_EPS) * gain_ref[:, vs]
        gg = g_ref[rows, vs]
        y_ref[rows, vs] = yn * (gg * _sigmoid(gg))
    xo_ref[...] = x_ref[...] + jnp.dot(y_ref[...].astype(bf16), wout_ref[...], preferred_element_type=f32)

    @pl.when(j == pl.num_programs(1) - 1)
    def _():
        for h in range(H):
            s_ref[0, h] = st_s[h].T


def _gla_mix_fresh(kind, pr, extra, gain, x, w_out, B, T, tq, layer=0):
    if kind == "hgrn":
        H, DK, DV = HGRN_HEADS, HGRN_DK, HGRN_DV
    else:
        H, DK, DV = GLA_HEADS, GLA_DK, GLA_DV
    L = 2 * LIN_CHUNK
    nc = tq // L
    nj = T // tq
    rmap = lambda c: (lambda b, j: (b * nj + j, c))
    bmap4 = lambda b, j: (b, 0, 0, 0)
    const2 = lambda b, j: (0, 0)
    hk, hv = H * DK, H * DV
    if kind == "hgrn":
        (lb_logits,) = extra
        in_specs = [pl.BlockSpec((tq, hk), rmap(0)), pl.BlockSpec((tq, hk), rmap(1)),
                    pl.BlockSpec((tq, hv), rmap(2)), pl.BlockSpec((tq, hv), rmap(3)),
                    pl.BlockSpec((DEPTH, hk), const2), pl.BlockSpec((1, hv), const2)]
        args = [pr, pr, pr, pr, lb_logits, gain.reshape(1, hv)]
    else:
        w_up, b_gate = extra
        w_up_p = jnp.zeros((LANES, hk), f32).at[:GLA_GATE_RANK].set(w_up)
        in_specs = [pl.BlockSpec((tq, hk), rmap(0)), pl.BlockSpec((tq, hk), rmap(1)),
                    pl.BlockSpec((tq, hv), rmap(1)), pl.BlockSpec((tq, hv), rmap(2)),
                    pl.BlockSpec((tq, LANES), rmap((2 * hk + 2 * hv) // LANES)),
                    pl.BlockSpec((LANES, hk), const2), pl.BlockSpec((1, hk), const2),
                    pl.BlockSpec((1, hv), const2)]
        args = [pr, pr, pr, pr, pr, w_up_p, b_gate.reshape(1, hk), gain.reshape(1, hv)]
    in_specs += [pl.BlockSpec((tq, D_MODEL), rmap(0)), pl.BlockSpec((hv, D_MODEL), const2)]
    args += [x, w_out]
    return pl.pallas_call(
        functools.partial(_gla_block_kernel, kind=kind, layer=layer, L=L, nc=nc, H=H, DK=DK, DV=DV),
        grid=(B, nj),
        in_specs=in_specs,
        out_specs=[pl.BlockSpec((tq, D_MODEL), rmap(0)), pl.BlockSpec((1, H, DK, DV), bmap4)],
        out_shape=[jax.ShapeDtypeStruct((B * T, D_MODEL), f32), jax.ShapeDtypeStruct((B, H, DK, DV), f32)],
        scratch_shapes=[pltpu.VMEM((H, DV, DK), f32)] + [pltpu.VMEM((tq, hk), bf16)] * 4
                       + [pltpu.VMEM((nc, 1, hk), f32), pltpu.VMEM((tq, hv), f32)],
        compiler_params=_cparams("parallel", "arbitrary"),
        name=kind + "_mix_fresh",
    )(*args)


def _rwkv_pre_kernel(x_ref, aux_ref, sh_ref, g_ref, mu_ref, wrkv_ref, la_ref, lbw_ref, lba_ref, lbg_ref,
                     w0_ref, a0_ref, r_ref, w_ref, k_ref, v_ref, alr_ref, gate_ref, *, long_seq, blocks_per_seq,
                     seq_len):
    g = g_ref[...]
    hn = _rms(x_ref[...], g)
    tm = hn.shape[0]
    rowi = lax.broadcasted_iota(jnp.int32, (tm, 1), 0)
    rolled = pltpu.roll(hn, 1, axis=0)
    if long_seq:
        p_last = _rms(aux_ref[SUBLANES - 1:SUBLANES, :], g)
        at_start = (pl.program_id(0) % blocks_per_seq) == 0
        first = jnp.where(at_start, sh_ref[0], p_last)
        prev = jnp.where(rowi == 0, first, rolled)
    else:
        prev = jnp.where(rowi % seq_len == 0, aux_ref[...], rolled)
    xx = prev - hn

    def lerp(c):
        return (hn + xx * mu_ref[c:c + 1, :]).astype(bf16)

    r_ref[...] = jnp.dot(lerp(0), wrkv_ref[0], preferred_element_type=f32)
    k_ref[...] = jnp.dot(lerp(1), wrkv_ref[1], preferred_element_type=f32)
    v_ref[...] = jnp.dot(lerp(2), wrkv_ref[2], preferred_element_type=f32)
    lw = jnp.tanh(jnp.dot(lerp(3), la_ref[:, 0:64], preferred_element_type=f32))
    wl = -_softplus(-(w0_ref[...] + jnp.dot(lw.astype(bf16), lbw_ref[...], preferred_element_type=f32))) - 0.5
    w_ref[...] = jnp.exp(-jnp.exp(wl))
    la = jnp.dot(lerp(4), la_ref[:, 64:128], preferred_element_type=f32)
    alr_ref[...] = _sigmoid(a0_ref[...] + jnp.dot(la.astype(bf16), lba_ref[...], preferred_element_type=f32))
    lg = _sigmoid(jnp.dot(lerp(5), la_ref[:, 128:256], preferred_element_type=f32))
    gate_ref[...] = jnp.dot(lg.astype(bf16), lbg_ref[...], preferred_element_type=f32)


def _rwkv_pre(x, shift0, g, mu, wrkv, la, lbw, lba, lbg, w0, a0, B, T, tm):
    n, d = x.shape
    long_seq = T % tm == 0
    row = lambda i: (i, 0)
    const2 = lambda i: (0, 0)
    if long_seq:
        bps = T // tm
        sub = tm // SUBLANES
        aux = x
        aux_spec = pl.BlockSpec((SUBLANES, d), lambda i: (jnp.maximum(i * sub - 1, 0), 0))
        sh = shift0.reshape(B, 1, d)
        sh_spec = pl.BlockSpec((1, 1, d), lambda i: (i // bps, 0, 0))
    else:
        assert tm % T == 0
        bps = 1
        aux = jnp.repeat(shift0, T, axis=0)
        aux_spec = pl.BlockSpec((tm, d), row)
        sh = shift0.reshape(B, 1, d)
        sh_spec = pl.BlockSpec((1, 1, d), lambda i: (0, 0, 0))
    out = jax.ShapeDtypeStruct((n, d), f32)
    return pl.pallas_call(
        functools.partial(_rwkv_pre_kernel, long_seq=long_seq, blocks_per_seq=bps, seq_len=T),
        grid=(n // tm,),
        in_specs=[pl.BlockSpec((tm, d), row), aux_spec, sh_spec, pl.BlockSpec((1, d), const2),
                  pl.BlockSpec((6, d), const2), pl.BlockSpec((3, d, d), lambda i: (0, 0, 0)),
                  pl.BlockSpec((d, 256), const2), pl.BlockSpec((64, d), const2), pl.BlockSpec((64, d), const2),
                  pl.BlockSpec((128, d), const2), pl.BlockSpec((1, d), const2), pl.BlockSpec((1, d), const2)],
        out_specs=[pl.BlockSpec((tm, d), row)] * 6,
        out_shape=[out] * 6,
        compiler_params=_cparams("parallel"),
        name="rwkv_pre",
    )(x, aux, sh, g.reshape(1, d), mu, wrkv, la, lbw, lba, lbg, w0.reshape(1, d), a0.reshape(1, d))


def _rwkv_scan_kernel(*refs, tt, has_init, unroll):
    N = RWKV_N
    (r_ref, w_ref, k_ref, v_ref, alr_ref, gate_ref, x_ref, wout_ref,
     kk_ref, ka_ref, rk_ref, lnw_ref, lnb_ref) = refs[:13]
    if has_init:
        s0_ref = refs[13]
        rest = refs[14:]
    else:
        rest = refs[13:]
    xo_ref, sout_ref, s_s, y_s, vec_s, z_ref, zz_s = rest
    HP = RWKV_HEADS // 2
    j = pl.program_id(1)

    @pl.when(j == 0)
    def _():
        if has_init:
            s_s[...] = s0_ref[...]
        else:
            s_s[...] = jnp.zeros_like(s_s)

    low = lax.broadcasted_iota(jnp.int32, (N, LANES), 1) < N

    def load_pair(ref, t0):
        tiles = []
        for t in (t0, t0 + 1):
            x = ref[:, t, :]
            tiles += [x[:, hp * LANES:(hp + 1) * LANES] for hp in range(HP)]
        xt = jnp.concatenate(tiles, axis=0).T
        ev, od = xt[:N], xt[N:]
        return (jnp.where(low, ev, pltpu.roll(od, N, axis=1)),
                jnp.where(low, pltpu.roll(ev, N, axis=1), od))

    def store_pair(ref, t0, z0, z1):
        ev = jnp.where(low, z0, pltpu.roll(z1, N, axis=1))
        od = jnp.where(low, pltpu.roll(z0, N, axis=1), z1)
        x = jnp.concatenate([ev, od], axis=0).T
        for i, t in enumerate((t0, t0 + 1)):
            tiles = [x[(i * HP + hp) * SUBLANES:(i * HP + hp + 1) * SUBLANES, :] for hp in range(HP)]
            ref[:, t, :] = jnp.concatenate(tiles, axis=1)

    VA, VW, VB, VK, VR, VV = range(6)

    def stage(tk, r, w, k, v, alr):
        kkraw = k * kk_ref[...]
        nrm = jnp.sqrt(jnp.sum(kkraw * kkraw, axis=0, keepdims=True))
        kk = kkraw / jnp.maximum(nrm, 1e-12)
        vec_s[tk, VA] = -kk
        vec_s[tk, VW] = w
        vec_s[tk, VB] = kk * alr
        vec_s[tk, VK] = k * (1.0 + (alr - 1.0) * ka_ref[...])
        vec_s[tk, VR] = r
        vec_s[tk, VV] = v

    def key_row(tk, which, kc):
        return vec_s[tk, which, pl.ds(kc, 1), :]

    def project(tk):
        def body(i, sa):
            for u in range(unroll):
                kc = i * unroll + u
                sa = sa + s_s[kc] * key_row(tk, VA, kc)
            return sa

        return lax.fori_loop(0, N // unroll, body, jnp.zeros((N, LANES), f32))

    def update(tk, sa):
        vv = vec_s[tk, VV]
        y = jnp.zeros((N, LANES), f32)
        for kc in range(N):
            sk = s_s[kc] * key_row(tk, VW, kc) + sa * key_row(tk, VB, kc) + vv * key_row(tk, VK, kc)
            s_s[kc] = sk
            y = y + sk * key_row(tk, VR, kc)
        y_s[tk] = y

    def epilogue(tk):
        y = y_s[tk]
        mean = jnp.mean(y, axis=0, keepdims=True)
        yc = y - mean
        var = jnp.mean(yc * yc, axis=0, keepdims=True)
        yn = yc * lax.rsqrt(var + RWKV_GN_EPS)
        bonus = jnp.sum(vec_s[tk, VR] * vec_s[tk, VK] * rk_ref[...], axis=0, keepdims=True) * vec_s[tk, VV]
        return yn * lnw_ref[...] + lnb_ref[...] + bonus

    def stage_pair(slot, t0):
        streams = [load_pair(ref, t0) for ref in (r_ref, w_ref, k_ref, v_ref, alr_ref)]
        for i in range(2):
            stage(slot + i, *[s[i] for s in streams])

    @pl.when(j == 0)
    def _():
        vec_s[...] = jnp.zeros_like(vec_s)
        y_s[...] = jnp.zeros_like(y_s)

    npairs = tt // 2
    stage_pair(0, 0)

    def step(p, carry):
        slot = 2 * (p % 2)
        prev = 2 - slot
        sa = project(slot)
        update(slot, sa)
        zz_s[0] = epilogue(prev)
        zz_s[1] = epilogue(prev + 1)
        stage_pair(prev, 2 * jnp.minimum(p + 1, npairs - 1))
        sa = project(slot + 1)
        update(slot + 1, sa)
        store_pair(z_ref, 2 * jnp.maximum(p - 1, 0), zz_s[0], zz_s[1])
        return carry

    lax.fori_loop(0, npairs, step, 0)
    last = 2 * ((npairs - 1) % 2)
    store_pair(z_ref, tt - 2, epilogue(last), epilogue(last + 1))

    rows = RWKV_SEQ_PER_STEP * tt
    zg = (z_ref[...] * gate_ref[...]).reshape(rows, D_MODEL).astype(bf16)
    proj = jnp.dot(zg, wout_ref[...], preferred_element_type=f32)
    xo_ref[...] = x_ref[...] + proj.reshape(RWKV_SEQ_PER_STEP, tt, D_MODEL)

    @pl.when(j == pl.num_programs(1) - 1)
    def _():
        sout_ref[...] = s_s[...]


RWKV_SEQ_PER_STEP = LANES // RWKV_HEADS


def _rwkv_scan(r, w, k, v, alr, gate, x, w_out, slabs, s0, B, T, tt):
    N = RWKV_N
    G = B // RWKV_SEQ_PER_STEP
    has_init = s0 is not None
    view = lambda a: a.reshape(B, T, D_MODEL)
    tmap = lambda g, j: (g, j, 0)
    smap = lambda g, j: (0, 0, g)
    const2 = lambda g, j: (0, 0)
    stream = pl.BlockSpec((RWKV_SEQ_PER_STEP, tt, D_MODEL), tmap)
    in_specs = [stream] * 7 + [pl.BlockSpec((D_MODEL, D_MODEL), const2)] + [pl.BlockSpec((N, LANES), const2)] * 5
    args = [view(r), view(w), view(k), view(v), view(alr), view(gate), view(x), w_out] + list(slabs)
    if has_init:
        in_specs.append(pl.BlockSpec((N, N, LANES), smap))
        args.append(s0)
    z, s = pl.pallas_call(
        functools.partial(_rwkv_scan_kernel, tt=tt, has_init=has_init, unroll=16),
        grid=(G, T // tt),
        in_specs=in_specs,
        out_specs=[stream, pl.BlockSpec((N, N, LANES), smap)],
        out_shape=[jax.ShapeDtypeStruct((B, T, D_MODEL), f32),
                   jax.ShapeDtypeStruct((N, N, G * LANES), f32)],
        scratch_shapes=[pltpu.VMEM((N, N, LANES), f32), pltpu.VMEM((4, N, LANES), f32),
                        pltpu.VMEM((4, 6, N, LANES), f32), pltpu.VMEM((RWKV_SEQ_PER_STEP, tt, D_MODEL), f32),
                        pltpu.VMEM((2, N, LANES), f32)],
        compiler_params=_cparams("parallel", "arbitrary"),
        name="rwkv_scan",
    )(*args)
    return z.reshape(B * T, D_MODEL), s


def _lane_slab(p):
    q = p.reshape(RWKV_HEADS // 2, 2, RWKV_N).transpose(2, 1, 0)
    q = jnp.broadcast_to(q[:, :, :, None], (RWKV_N, 2, RWKV_HEADS // 2, RWKV_SEQ_PER_STEP))
    return q.reshape(RWKV_N, LANES)


def _rwkv_mix(x, p, B, T, state, tm):
    H, N = RWKV_HEADS, RWKV_N
    G = B // RWKV_SEQ_PER_STEP
    if state is None:
        shift0 = jnp.zeros((B, D_MODEL), f32)
        s0 = None
    else:
        s_in, shift0 = state
        s0 = (s_in.reshape(G, RWKV_SEQ_PER_STEP, H // 2, 2, N, N)
              .transpose(5, 4, 0, 3, 2, 1).reshape(N, N, G * LANES))
    r, w, k, v, alr, gate = _rwkv_pre(x, shift0, p["g"], p["mu"], p["wrkv"], p["la"], p["lbw"], p["lba"], p["lbg"],
                                      p["w0"], p["a0"], B, T, tm)
    slabs = [_lane_slab(p[n]) for n in ("k_k", "k_a", "r_k", "ln_w", "ln_b")]
    x_new, s = _rwkv_scan(r, w, k, v, alr, gate, x, p["w_out"], slabs, s0, B, T, min(T, 32))
    s = (s.reshape(N, N, G, 2, H // 2, RWKV_SEQ_PER_STEP)
         .transpose(2, 5, 4, 3, 1, 0).reshape(B, H, N, N))
    return x_new, s


def _pad_cols(w, to):
    return jnp.pad(w, ((0, 0), (0, to - w.shape[1])))


def _trunk(x, B, T, states, p):
    n = x.shape[0]
    tm = min(512, n)
    tmm = min(1024, n)
    tq = min(256, T)
    fresh = states is None
    block_ok = fresh and T % tq == 0 and tq % (2 * LIN_CHUNK) == 0
    seq_ok = (not fresh) and T == SUBLANES and T <= LIN_CHUNK and B % SEQS_PER_STEP == 0
    if not fresh:
        m_c, m_n, m_m, h_s, g_s, r_s, r_sh = states
    new = {}
    for li in range(DEPTH):
        g_mix = p["norm_mix"][li]
        if li == 0:
            pr = _norm_proj(x, g_mix, p["mlstm_w_in"], tmm, 1664)
            if fresh and T % tq == 0 and tq % (2 * MLSTM_CHUNK) == 0:
                x, c, nn, m = _mlstm_mix_fresh(pr, p["mlstm_b_gates"], p["mlstm_norm"], x, p["mlstm_w_out"],
                                               B, T, tq)
            else:
                if seq_ok:
                    y, c, nn, m = _mlstm_mix_seq(pr, p["mlstm_b_gates"], p["mlstm_norm"], B, T,
                                                 (m_c[0], m_n[0], m_m[0]), SEQS_PER_STEP)
                else:
                    y, c, nn, m = _mlstm_mix(pr, p["mlstm_b_gates"], p["mlstm_norm"], B, T,
                                             None if fresh else (m_c[0], m_n[0], m_m[0]), tq)
                x = _out_proj(y, p["mlstm_w_out"], x, tm)
            new["C"], new["n"], new["m"] = c[None], nn[None], m[None]
        elif li == 1:
            pr = _norm_proj(x, g_mix, p["hgrn_w_in"], tmm, 2048)
            if block_ok:
                x, s = _gla_mix_fresh("hgrn", pr, (p["hgrn_lb_logits"],), p["hgrn_norm"], x, p["hgrn_w_out"],
                                      B, T, tq, layer=li)
            else:
                if seq_ok:
                    y, s = _gla_mix_seq("hgrn", pr, (p["hgrn_lb_logits"],), p["hgrn_norm"], B, T, h_s[0],
                                        SEQS_PER_STEP, layer=li)
                else:
                    y, s = _gla_mix("hgrn", pr, (p["hgrn_lb_logits"],), p["hgrn_norm"], B, T,
                                    None if fresh else h_s[0], tq, layer=li)
                x = _out_proj(y, p["hgrn_w_out"], x, tm)
            new["hS"] = s[None]
        elif li == 2:
            pr = _norm_proj(x, g_mix, p["gla_w_in"], tmm, 1664)
            if block_ok:
                x, s = _gla_mix_fresh("gla", pr, (p["gla_w_gate_up"], p["gla_b_gate"]), p["gla_norm"], x,
                                      p["gla_w_out"], B, T, tq)
            else:
                if seq_ok:
                    y, s = _gla_mix_seq("gla", pr, (p["gla_w_gate_up"], p["gla_b_gate"]), p["gla_norm"], B, T,
                                        g_s[0], SEQS_PER_STEP)
                else:
                    y, s = _gla_mix("gla", pr, (p["gla_w_gate_up"], p["gla_b_gate"]), p["gla_norm"], B, T,
                                    None if fresh else g_s[0], tq)
                x = _out_proj(y, p["gla_w_out"], x, tm)
            new["gS"] = s[None]
        else:
            rp = dict(p["rwkv"], g=g_mix, w_out=p["rwkv_w_out"])
            x_last = x.reshape(B, T, D_MODEL)[:, T - 1, :]
            new["sh"] = _rmsnorm(x_last, g_mix, B)[None]
            x, s = _rwkv_mix(x, rp, B, T, None if fresh else (r_s[0], r_sh[0]), min(256, n))
            new["rS"] = s[None]
        x = _ffn(x, p["norm_ffn"][li], p["ffn_w_gate_up"][li], p["ffn_w_down"][li], tm, 1408)
    y = _rmsnorm(x, p["norm_final"], tm)
    return y.reshape(B, T, D_MODEL), (new["C"], new["n"], new["m"], new["hS"], new["gS"], new["rS"], new["sh"])


def kernel(x_prompt, x_sample, state_mlstm_C, state_mlstm_n, state_mlstm_m, state_hgrn_S, state_gla_S, state_rwkv_S, state_rwkv_shift, norm_mix, norm_ffn, norm_final, mlstm_w_in, mlstm_b_gates, mlstm_norm, mlstm_w_out, hgrn_w_in, hgrn_lb_logits, hgrn_norm, hgrn_w_out, gla_w_in, gla_w_gate_up, gla_b_gate, gla_norm, gla_w_out, rwkv_mu, rwkv_w_rkv, rwkv_w_lora_a, rwkv_w_lora_b, rwkv_w0, rwkv_a_lora_a, rwkv_a_lora_b, rwkv_a0, rwkv_g_lora_a, rwkv_g_lora_b, rwkv_k_k, rwkv_k_a, rwkv_r_k, rwkv_ln_w, rwkv_ln_b, rwkv_w_out, ffn_w_gate_up, ffn_w_down):
    cast = lambda w: w.astype(bf16)
    p = dict(
        norm_mix=norm_mix, norm_ffn=norm_ffn, norm_final=norm_final,
        mlstm_w_in=cast(_pad_cols(mlstm_w_in[0], 3328)), mlstm_b_gates=mlstm_b_gates[0], mlstm_norm=mlstm_norm[0],
        mlstm_w_out=cast(mlstm_w_out[0]),
        hgrn_w_in=cast(hgrn_w_in[0]), hgrn_lb_logits=hgrn_lb_logits, hgrn_norm=hgrn_norm[0],
        hgrn_w_out=cast(hgrn_w_out[0]),
        gla_w_in=cast(_pad_cols(gla_w_in[0], 3328)), gla_w_gate_up=gla_w_gate_up[0], gla_b_gate=gla_b_gate[0],
        gla_norm=gla_norm[0], gla_w_out=cast(gla_w_out[0]),
        rwkv=dict(mu=rwkv_mu[0], wrkv=cast(rwkv_w_rkv[0]),
                  la=cast(jnp.concatenate([rwkv_w_lora_a[0], rwkv_a_lora_a[0], rwkv_g_lora_a[0]], axis=1)),
                  lbw=cast(rwkv_w_lora_b[0]), lba=cast(rwkv_a_lora_b[0]), lbg=cast(rwkv_g_lora_b[0]),
                  w0=rwkv_w0[0], a0=rwkv_a0[0], k_k=rwkv_k_k[0], k_a=rwkv_k_a[0], r_k=rwkv_r_k[0].reshape(-1),
                  ln_w=rwkv_ln_w[0], ln_b=rwkv_ln_b[0]),
        rwkv_w_out=cast(rwkv_w_out[0]),
        ffn_w_gate_up=cast(ffn_w_gate_up), ffn_w_down=cast(ffn_w_down),
    )
    bp, tp, _ = x_prompt.shape
    bs, ts, _ = x_sample.shape
    y_p, st_p = _trunk(x_prompt.reshape(bp * tp, D_MODEL), bp, tp, None, p)
    y_s, st_s = _trunk(x_sample.reshape(bs * ts, D_MODEL), bs, ts,
                       (state_mlstm_C, state_mlstm_n, state_mlstm_m, state_hgrn_S, state_gla_S, state_rwkv_S,
                        state_rwkv_shift), p)
    return (y_p, y_s) + st_p + st_s
```

```python
import functools

import jax
import jax.numpy as jnp
from jax import lax
from jax.experimental import pallas as pl
from jax.experimental.pallas import tpu as pltpu

f32 = jnp.float32
bf16 = jnp.bfloat16

D_MODEL = 1024
DEPTH = 4
NORM_EPS = 1e-6

MLSTM_HEADS, MLSTM_DK, MLSTM_DV, MLSTM_CHUNK = 4, 128, 256, 64
HGRN_HEADS, HGRN_DK, HGRN_DV = 8, 128, 128
GLA_HEADS, GLA_DK, GLA_DV = 4, 128, 256
GLA_GATE_RANK = 16
GLA_GATE_NORMALIZER = 16.0
LIN_CHUNK = 32
RWKV_HEADS, RWKV_N = 16, 64
RWKV_GN_EPS = 64e-5
FFN_HIDDEN = 2816

LANES = 128
SUBLANES = 8
BF16_ROWS = 16
SEQS_PER_STEP = 4
VMEM_LIMIT_BYTES = 52 * 1024 * 1024


def _cparams(*sem):
    return pltpu.CompilerParams(dimension_semantics=sem, vmem_limit_bytes=VMEM_LIMIT_BYTES)


def _chunk_len(t, cap):
    return max(d for d in range(1, min(cap, t) + 1) if t % d == 0)


def _rms(x, g):
    ms = jnp.mean(x * x, axis=-1, keepdims=True)
    return x * lax.rsqrt(ms + NORM_EPS) * g


def _sigmoid(x):
    return jax.nn.sigmoid(x)


def _softplus(x):
    return jnp.maximum(x, 0.0) + jnp.log1p(jnp.exp(-jnp.abs(x)))


def _log_sigmoid(x):
    return -_softplus(-x)


def _mm(a, b, mx=bf16):
    return jnp.dot(a.astype(mx), b.astype(mx), preferred_element_type=f32)


def _mm_nt(a, b, mx=bf16):
    return lax.dot_general(a.astype(mx), b.astype(mx), (((1,), (1,)), ((), ())), preferred_element_type=f32)


def _mm_tn(a, b, mx=bf16):
    return lax.dot_general(a.astype(mx), b.astype(mx), (((0,), (0,)), ((), ())), preferred_element_type=f32)


def _split3(x):
    hi = x.astype(bf16)
    r1 = x - hi.astype(f32)
    mid = r1.astype(bf16)
    lo = (r1 - mid.astype(f32)).astype(bf16)
    return hi, mid, lo


def _cumsum_rows(x, tril_b):
    hi, mid, lo = _split3(x)
    return (jnp.dot(tril_b, hi, preferred_element_type=f32)
            + jnp.dot(tril_b, mid, preferred_element_type=f32)
            + jnp.dot(tril_b, lo, preferred_element_type=f32))


def _norm_proj_kernel(x_ref, g_ref, w_ref, o_ref, hn_ref):
    @pl.when(pl.program_id(1) == 0)
    def _():
        hn_ref[...] = _rms(x_ref[...], g_ref[...]).astype(bf16)

    o_ref[...] = jnp.dot(hn_ref[...], w_ref[...], preferred_element_type=f32)


def _norm_proj(x, g, w, tm, tn):
    n, d = x.shape
    e = w.shape[1]
    return pl.pallas_call(
        _norm_proj_kernel,
        grid=(n // tm, e // tn),
        in_specs=[pl.BlockSpec((tm, d), lambda i, j: (i, 0)),
                  pl.BlockSpec((1, d), lambda i, j: (0, 0)),
                  pl.BlockSpec((d, tn), lambda i, j: (0, j))],
        out_specs=pl.BlockSpec((tm, tn), lambda i, j: (i, j)),
        out_shape=jax.ShapeDtypeStruct((n, e), f32),
        scratch_shapes=[pltpu.VMEM((tm, d), bf16)],
        compiler_params=_cparams("parallel", "arbitrary"),
        name="norm_proj",
    )(x, g.reshape(1, d), w)


def _out_proj_kernel(*refs, gated):
    if gated:
        y_ref, gate_ref, w_ref, res_ref, o_ref = refs
        y = y_ref[...] * gate_ref[...]
    else:
        y_ref, w_ref, res_ref, o_ref = refs
        y = y_ref[...]
    o_ref[...] = res_ref[...] + jnp.dot(y.astype(bf16), w_ref[...], preferred_element_type=f32)


def _out_proj(y, w, res, tm, gate=None):
    n, e = y.shape
    d = w.shape[1]
    row = lambda i: (i, 0)
    args = [y] + ([gate] if gate is not None else []) + [w, res]
    in_specs = ([pl.BlockSpec((tm, e), row)] + ([pl.BlockSpec((tm, e), row)] if gate is not None else [])
                + [pl.BlockSpec((e, d), lambda i: (0, 0)), pl.BlockSpec((tm, d), row)])
    return pl.pallas_call(
        functools.partial(_out_proj_kernel, gated=gate is not None),
        grid=(n // tm,),
        in_specs=in_specs,
        out_specs=pl.BlockSpec((tm, d), row),
        out_shape=jax.ShapeDtypeStruct((n, d), f32),
        compiler_params=_cparams("parallel"),
        name="out_proj",
    )(*args)


def _ffn_kernel(*refs, final_norm):
    if final_norm:
        x_ref, g_ref, wg_ref, wu_ref, wd_ref, gf_ref, o_ref, hn_ref, acc_ref = refs
    else:
        x_ref, g_ref, wg_ref, wu_ref, wd_ref, o_ref, hn_ref, acc_ref = refs
    j = pl.program_id(1)

    @pl.when(j == 0)
    def _():
        hn_ref[...] = _rms(x_ref[...], g_ref[...]).astype(bf16)
        acc_ref[...] = jnp.zeros_like(acc_ref)

    h = hn_ref[...]
    gt = jnp.dot(h, wg_ref[...], preferred_element_type=f32)
    up = jnp.dot(h, wu_ref[...], preferred_element_type=f32)
    act = (gt * _sigmoid(gt) * up).astype(bf16)
    acc_ref[...] += jnp.dot(act, wd_ref[...], preferred_element_type=f32)

    @pl.when(j == pl.num_programs(1) - 1)
    def _():
        out = x_ref[...] + acc_ref[...]
        o_ref[...] = _rms(out, gf_ref[...]) if final_norm else out


def _ffn(x, g, w_gu, w_down, tm, tf, final_gain=None):
    n, d = x.shape
    nf = FFN_HIDDEN // tf
    final_norm = final_gain is not None
    in_specs = [pl.BlockSpec((tm, d), lambda i, j: (i, 0)),
                pl.BlockSpec((1, d), lambda i, j: (0, 0)),
                pl.BlockSpec((d, tf), lambda i, j: (0, j)),
                pl.BlockSpec((d, tf), lambda i, j: (0, j + nf)),
                pl.BlockSpec((tf, d), lambda i, j: (j, 0))]
    args = [x, g.reshape(1, d), w_gu, w_gu, w_down]
    if final_norm:
        in_specs.append(pl.BlockSpec((1, d), lambda i, j: (0, 0)))
        args.append(final_gain.reshape(1, d))
    return pl.pallas_call(
        functools.partial(_ffn_kernel, final_norm=final_norm),
        grid=(n // tm, nf),
        in_specs=in_specs,
        out_specs=pl.BlockSpec((tm, d), lambda i, j: (i, 0)),
        out_shape=jax.ShapeDtypeStruct((n, d), f32),
        scratch_shapes=[pltpu.VMEM((tm, d), bf16), pltpu.VMEM((tm, d), f32)],
        compiler_params=_cparams("parallel", "arbitrary"),
        name="ffn",
    )(*args)


def _rmsnorm_kernel(x_ref, g_ref, o_ref):
    o_ref[...] = _rms(x_ref[...], g_ref[...])


def _rmsnorm(x, g, tm):
    n, d = x.shape
    return pl.pallas_call(
        _rmsnorm_kernel,
        grid=(n // tm,),
        in_specs=[pl.BlockSpec((tm, d), lambda i: (i, 0)), pl.BlockSpec((1, d), lambda i: (0, 0))],
        out_specs=pl.BlockSpec((tm, d), lambda i: (i, 0)),
        out_shape=jax.ShapeDtypeStruct((n, d), f32),
        compiler_params=_cparams("parallel"),
        name="rmsnorm",
    )(x, g.reshape(1, d))


def _mlstm_kernel(*refs, L, nchunks, has_init, mx):
    H, DK, DV = MLSTM_HEADS, MLSTM_DK, MLSTM_DV
    q_ref, k_ref, v_ref, o_ref, gt_ref, bias_ref, gain_ref = refs[:7]
    if has_init:
        c0_ref, n0_ref, m0_ref = refs[7:10]
        rest = refs[10:]
    else:
        rest = refs[7:]
    y_ref, c_ref, n_ref, m_ref, c_s, n_s, m_s = rest
    j = pl.program_id(1)

    @pl.when(j == 0)
    def _():
        if has_init:
            c_s[...] = c0_ref[0]
            n_s[...] = n0_ref[0]
            m_s[...] = m0_ref[0]
        else:
            c_s[...] = jnp.zeros_like(c_s)
            n_s[...] = jnp.zeros_like(n_s)
            m_s[...] = jnp.zeros_like(m_s)

    Lp = max(L, BF16_ROWS)
    rowi = lax.broadcasted_iota(jnp.int32, (Lp, Lp), 0)
    coli = lax.broadcasted_iota(jnp.int32, (Lp, Lp), 1)
    causal = coli <= rowi
    eye = coli == rowi
    scale = DK ** -0.5

    def to_row(col):
        return jnp.sum(jnp.where(eye, col, 0.0), axis=0, keepdims=True)

    def pad_rows(x, value=0.0):
        if Lp == L:
            return x
        return jnp.concatenate([x, jnp.full((Lp - L, x.shape[1]), value, f32)], axis=0)

    def chunk(c, carry):
        rows = pl.ds(pl.multiple_of(c * L, L), L)
        gts_raw = gt_ref[rows, :] + bias_ref[...]
        lsg = pad_rows(_log_sigmoid(gts_raw))
        gts = pad_rows(gts_raw, -jnp.inf)
        for h in range(H):
            i_col = gts[:, h:h + 1]
            f_col = lsg[:, H + h:H + h + 1]
            f_row = to_row(f_col)
            fc_col = jnp.sum(jnp.where(causal, f_row, 0.0), axis=1, keepdims=True)
            fc_row = to_row(fc_col)
            i_row = to_row(i_col)
            m = m_s[h]
            dm = jnp.where(causal, fc_col - fc_row + i_row, -jnp.inf)
            g = fc_col + m
            mt = jnp.maximum(g, jnp.max(dm, axis=1, keepdims=True))
            p = jnp.exp(dm - mt)
            qc = pad_rows(q_ref[rows, h * DK:(h + 1) * DK] * scale)
            kc = pad_rows(k_ref[rows, h * DK:(h + 1) * DK])
            vc = pad_rows(v_ref[rows, h * DV:(h + 1) * DV])
            s = _mm_nt(qc, kc, mx) * p
            eg = jnp.exp(g - mt)
            cst = c_s[h]
            nst = n_s[h]
            num = eg * _mm_nt(qc, cst, mx) + _mm(s, vc, mx)
            den = eg * jnp.sum(qc * nst, axis=1, keepdims=True) + jnp.sum(s, axis=1, keepdims=True)
            hh = num / jnp.maximum(jnp.abs(den), jnp.exp(-mt))
            m_new = mt[L - 1:L, :]
            f_last = fc_col[L - 1:L, :]
            w_c = jnp.exp(f_last + m - m_new)
            w_j = jnp.exp(f_last - fc_col + i_col - m_new)
            c_s[h] = w_c * cst + _mm_tn(vc * w_j, kc, mx)
            n_s[h] = w_c * nst + jnp.sum(w_j * kc, axis=0, keepdims=True)
            m_s[h] = m_new
            hh = hh[:L]
            yn = hh * lax.rsqrt(jnp.mean(hh * hh, axis=1, keepdims=True) + NORM_EPS)
            yn = yn * gain_ref[:, h * DV:(h + 1) * DV]
            y_ref[rows, h * DV:(h + 1) * DV] = _sigmoid(o_ref[rows, h * DV:(h + 1) * DV]) * yn
        return carry

    lax.fori_loop(0, nchunks, chunk, 0)

    @pl.when(j == pl.num_programs(1) - 1)
    def _():
        c_ref[0] = c_s[...]
        n_ref[0] = n_s[...]
        m_ref[0] = m_s[...]


def _mlstm_mix(pr, b_gates, gain, B, T, state, tq):
    H, DK, DV = MLSTM_HEADS, MLSTM_DK, MLSTM_DV
    L = _chunk_len(T, MLSTM_CHUNK)
    nj = T // tq
    has_init = state is not None
    mx = bf16
    rmap = lambda c: (lambda b, j: (b * nj + j, c))
    bmap4 = lambda b, j: (b, 0, 0, 0)
    bias = jnp.zeros((1, LANES), f32).at[0, :2 * H].set(b_gates)
    in_specs = [pl.BlockSpec((tq, H * DK), rmap(0)), pl.BlockSpec((tq, H * DK), rmap(1)),
                pl.BlockSpec((tq, H * DV), rmap(1)), pl.BlockSpec((tq, H * DV), rmap(2)),
                pl.BlockSpec((tq, LANES), rmap((2 * H * DK + 2 * H * DV) // LANES)),
                pl.BlockSpec((1, LANES), lambda b, j: (0, 0)),
                pl.BlockSpec((1, H * DV), lambda b, j: (0, 0))]
    args = [pr, pr, pr, pr, pr, bias, gain.reshape(1, H * DV)]
    if has_init:
        c0, n0, m0 = state
        in_specs += [pl.BlockSpec((1, H, DV, DK), bmap4), pl.BlockSpec((1, H, 1, DK), bmap4),
                     pl.BlockSpec((1, H, 1, 1), bmap4)]
        args += [c0, n0.reshape(B, H, 1, DK), m0.reshape(B, H, 1, 1)]
    y, c, n, m = pl.pallas_call(
        functools.partial(_mlstm_kernel, L=L, nchunks=tq // L, has_init=has_init, mx=mx),
        grid=(B, nj),
        in_specs=in_specs,
        out_specs=[pl.BlockSpec((tq, H * DV), rmap(0)), pl.BlockSpec((1, H, DV, DK), bmap4),
                   pl.BlockSpec((1, H, 1, DK), bmap4), pl.BlockSpec((1, H, 1, 1), bmap4)],
        out_shape=[jax.ShapeDtypeStruct((B * T, H * DV), f32), jax.ShapeDtypeStruct((B, H, DV, DK), f32),
                   jax.ShapeDtypeStruct((B, H, 1, DK), f32), jax.ShapeDtypeStruct((B, H, 1, 1), f32)],
        scratch_shapes=[pltpu.VMEM((H, DV, DK), f32), pltpu.VMEM((H, 1, DK), f32), pltpu.VMEM((H, 1, 1), f32)],
        compiler_params=_cparams("parallel", "arbitrary"),
        name="mlstm_mix",
    )(*args)
    return y, c, n.reshape(B, H, DK), m.reshape(B, H)


def _mlstm_block_kernel(q_ref, k_ref, v_ref, o_ref, gt_ref, bias_ref, gain_ref, x_ref, wout_ref,
                        xo_ref, c_ref, n_ref, m_ref, c_s, n_s, m_s, y_ref, *, L, nc):
    H, DK, DV = MLSTM_HEADS, MLSTM_DK, MLSTM_DV
    P = 2 * L
    j = pl.program_id(1)

    @pl.when(j == 0)
    def _():
        c_s[...] = jnp.zeros_like(c_s)
        n_s[...] = jnp.zeros_like(n_s)
        m_s[...] = jnp.zeros_like(m_s)

    rowi = lax.broadcasted_iota(jnp.int32, (L, L), 0)
    coli = lax.broadcasted_iota(jnp.int32, (L, L), 1)
    causal = coli <= rowi
    r2 = lax.broadcasted_iota(jnp.int32, (P, P), 0)
    c2 = lax.broadcasted_iota(jnp.int32, (P, P), 1)
    pair_tril = ((c2 <= r2) & ((c2 >= L) == (r2 >= L))).astype(bf16)
    scale = DK ** -0.5

    gates = []
    for pp in range(nc // 2):
        g = gt_ref[pp * P:(pp + 1) * P, :] + bias_ref[...]
        hi, mid, lo = _split3(_log_sigmoid(g))
        fcum = (jnp.dot(pair_tril, hi, preferred_element_type=f32)
                + jnp.dot(pair_tril, mid, preferred_element_type=f32)
                + jnp.dot(pair_tril, lo, preferred_element_type=f32))
        g_t = g.T
        f_t = fcum.T
        for half in range(2):
            rs = slice(half * L, (half + 1) * L)
            gates.append((g[rs, :], fcum[rs, :], g_t[:, rs], f_t[:, rs]))

    pairs = [(c, h) for c in range(nc) for h in range(H)]
    rows_of = lambda c: slice(c * L, (c + 1) * L)
    ks_of = lambda h: slice(h * DK, (h + 1) * DK)
    vs_of = lambda h: slice(h * DV, (h + 1) * DV)
    ones_b = jnp.ones((L, LANES), bf16)
    wide = lambda x: jnp.concatenate([x] * (DV // LANES), axis=1)
    dms, rmax, fcols, icols, qks, qbs = {}, {}, {}, {}, {}, {}
    for c, h in pairs:
        g_c, f_c, g_r, f_r = gates[c]
        icols[c, h] = jnp.broadcast_to(g_c[:, h:h + 1], (L, LANES))
        fcols[c, h] = jnp.broadcast_to(f_c[:, H + h:H + h + 1], (L, LANES))
        dm = jnp.where(causal, fcols[c, h][:, :L] - f_r[H + h:H + h + 1, :] + g_r[h:h + 1, :], -jnp.inf)
        dms[c, h] = dm
        rmax[c, h] = jnp.broadcast_to(jnp.max(dm, axis=1, keepdims=True), (L, LANES))
        qbs[c, h] = (q_ref[rows_of(c), ks_of(h)] * scale).astype(bf16)
        qks[c, h] = _mm_nt(qbs[c, h], k_ref[rows_of(c), ks_of(h)])
    m_in, m_out = {}, {}
    for h in range(H):
        m = jnp.broadcast_to(m_s[h], (1, LANES))
        for c in range(nc):
            m_in[c, h] = m
            m = jnp.maximum(fcols[c, h][L - 1:L, :] + m, rmax[c, h][L - 1:L, :])
            m_out[c, h] = m
        m_s[h] = m[:, :1]
    mts, egs, ss, ssums, wcs, dcs, dns = {}, {}, {}, {}, {}, {}, {}
    for c, h in pairs:
        fc_col = fcols[c, h]
        gg = fc_col + m_in[c, h]
        mt = jnp.maximum(gg, rmax[c, h])
        mts[c, h] = mt
        egs[c, h] = jnp.exp(gg - mt)
        s = (qks[c, h] * jnp.exp(dms[c, h] - mt[:, :L])).astype(bf16)
        ss[c, h] = s
        ssums[c, h] = jnp.dot(s, ones_b, preferred_element_type=f32)
        f_last = fc_col[L - 1:L, :]
        wcs[c, h] = jnp.exp(f_last + m_in[c, h] - m_out[c, h])
        w_j = jnp.exp(f_last - fc_col + icols[c, h] - m_out[c, h])
        kc = k_ref[rows_of(c), ks_of(h)]
        dcs[c, h] = _mm_tn(v_ref[rows_of(c), vs_of(h)] * wide(w_j), kc)
        dns[c, h] = jnp.sum(w_j * kc, axis=0, keepdims=True)
    c_in, n_in = {}, {}
    for h in range(H):
        cst = c_s[h]
        nst = n_s[h]
        for c in range(nc):
            c_in[c, h] = cst.astype(bf16)
            n_in[c, h] = nst
            cst = wcs[c, h] * cst + dcs[c, h]
            nst = wcs[c, h] * nst + dns[c, h]
        c_s[h] = cst
        n_s[h] = nst
    hhs, msq = {}, {}
    mean_b = jnp.full((DV, LANES), 1.0 / DV, bf16)
    for c, h in pairs:
        rows, vs = rows_of(c), vs_of(h)
        qn = _mm_nt(qbs[c, h], jnp.broadcast_to(n_in[c, h], (LANES, DK)))
        num = wide(egs[c, h]) * _mm_nt(qbs[c, h], c_in[c, h]) + _mm(ss[c, h], v_ref[rows, vs])
        den = egs[c, h] * qn + ssums[c, h]
        hh = num / wide(jnp.maximum(jnp.abs(den), jnp.exp(-mts[c, h])))
        hhs[c, h] = hh
        msq[c, h] = jnp.dot((hh * hh).astype(bf16), mean_b, preferred_element_type=f32)
    for c, h in pairs:
        rows, vs = rows_of(c), vs_of(h)
        yn = hhs[c, h] * wide(lax.rsqrt(msq[c, h] + NORM_EPS)) * gain_ref[:, vs]
        y_ref[rows, vs] = _sigmoid(o_ref[rows, vs]) * yn
    xo_ref[...] = x_ref[...] + jnp.dot(y_ref[...].astype(bf16), wout_ref[...], preferred_element_type=f32)

    @pl.when(j == pl.num_programs(1) - 1)
    def _():
        c_ref[0] = c_s[...]
        n_ref[0] = n_s[...]
        m_ref[0] = m_s[...]


def _mlstm_mix_fresh(pr, b_gates, gain, x, w_out, B, T, tq):
    H, DK, DV = MLSTM_HEADS, MLSTM_DK, MLSTM_DV
    L = MLSTM_CHUNK
    nj = T // tq
    rmap = lambda c: (lambda b, j: (b * nj + j, c))
    bmap4 = lambda b, j: (b, 0, 0, 0)
    bias = jnp.zeros((1, LANES), f32).at[0, :2 * H].set(b_gates)
    y, c, n, m = pl.pallas_call(
        functools.partial(_mlstm_block_kernel, L=L, nc=tq // L),
        grid=(B, nj),
        in_specs=[pl.BlockSpec((tq, H * DK), rmap(0)), pl.BlockSpec((tq, H * DK), rmap(1)),
                  pl.BlockSpec((tq, H * DV), rmap(1)), pl.BlockSpec((tq, H * DV), rmap(2)),
                  pl.BlockSpec((tq, LANES), rmap((2 * H * DK + 2 * H * DV) // LANES)),
                  pl.BlockSpec((1, LANES), lambda b, j: (0, 0)),
                  pl.BlockSpec((1, H * DV), lambda b, j: (0, 0)),
                  pl.BlockSpec((tq, D_MODEL), rmap(0)),
                  pl.BlockSpec((H * DV, D_MODEL), lambda b, j: (0, 0))],
        out_specs=[pl.BlockSpec((tq, D_MODEL), rmap(0)), pl.BlockSpec((1, H, DV, DK), bmap4),
                   pl.BlockSpec((1, H, 1, DK), bmap4), pl.BlockSpec((1, H, 1, 1), bmap4)],
        out_shape=[jax.ShapeDtypeStruct((B * T, D_MODEL), f32), jax.ShapeDtypeStruct((B, H, DV, DK), f32),
                   jax.ShapeDtypeStruct((B, H, 1, DK), f32), jax.ShapeDtypeStruct((B, H, 1, 1), f32)],
        scratch_shapes=[pltpu.VMEM((H, DV, DK), f32), pltpu.VMEM((H, 1, DK), f32), pltpu.VMEM((H, 1, 1), f32),
                        pltpu.VMEM((tq, H * DV), f32)],
        compiler_params=_cparams("parallel", "arbitrary"),
        name="mlstm_mix_fresh",
    )(pr, pr, pr, pr, pr, bias, gain.reshape(1, H * DV), x, w_out)
    return y, c, n.reshape(B, H, DK), m.reshape(B, H)


def _mlstm_seq_kernel(q_ref, k_ref, v_ref, o_ref, gt_ref, bias_ref, gain_ref, c0_ref, n0_ref, m0_ref,
                      y_ref, c_ref, n_ref, m_ref, *, T, bs):
    H, DK, DV = MLSTM_HEADS, MLSTM_DK, MLSTM_DV
    P = BF16_ROWS
    R = bs * T
    ri = lax.broadcasted_iota(jnp.int32, (R, R), 0)
    ci = lax.broadcasted_iota(jnp.int32, (R, R), 1)
    tril_b = (((ri // T) == (ci // T)) & (ci <= ri)).astype(bf16)
    rp = lax.broadcasted_iota(jnp.int32, (P, P), 0)
    cp = lax.broadcasted_iota(jnp.int32, (P, P), 1)
    causal = cp <= rp
    real_col = lax.broadcasted_iota(jnp.int32, (1, P), 1) < T
    scale = DK ** -0.5
    ones_b = jnp.ones((P, LANES), bf16)
    mean_b = jnp.full((DV, LANES), 1.0 / DV, bf16)
    wide = lambda x: jnp.concatenate([x] * (DV // LANES), axis=1)

    g = gt_ref[...] + bias_ref[...]
    fcum = _cumsum_rows(_log_sigmoid(g), tril_b)
    zrows = jnp.zeros((LANES - R, LANES), f32)
    g_t = jnp.concatenate([g, zrows], axis=0).T
    f_t = jnp.concatenate([fcum, zrows], axis=0).T

    def pad_rows(x, value=0.0):
        return jnp.concatenate([x, jnp.full((P - T, x.shape[1]), value, f32)], axis=0)

    pairs = [(c, h) for c in range(bs) for h in range(H)]
    ks_of = lambda h: slice(h * DK, (h + 1) * DK)
    vs_of = lambda h: slice(h * DV, (h + 1) * DV)
    st = {}
    for c, h in pairs:
        rows = slice(c * T, (c + 1) * T)
        last = slice((c + 1) * T - 1, (c + 1) * T)
        i_col = jnp.broadcast_to(pad_rows(g[rows, h:h + 1], -jnp.inf), (P, LANES))
        f_last = jnp.broadcast_to(fcum[last, H + h:H + h + 1], (1, LANES))
        fc_col = jnp.concatenate([jnp.broadcast_to(fcum[rows, H + h:H + h + 1], (T, LANES)),
                                  jnp.broadcast_to(f_last, (P - T, LANES))], axis=0)
        i_row = jnp.where(real_col, g_t[h:h + 1, c * T:c * T + P], -jnp.inf)
        fc_row = f_t[H + h:H + h + 1, c * T:c * T + P]
        dm = jnp.where(causal, fc_col[:, :P] - fc_row + i_row, -jnp.inf)
        rmax = jnp.broadcast_to(jnp.max(dm, axis=1, keepdims=True), (P, LANES))
        qb = pad_rows(q_ref[rows, ks_of(h)] * scale).astype(bf16)
        kc = pad_rows(k_ref[rows, ks_of(h)])
        vc = pad_rows(v_ref[rows, vs_of(h)])
        st[c, h] = dict(i_col=i_col, f_last=f_last, fc_col=fc_col, dm=dm, rmax=rmax, qb=qb, kc=kc, vc=vc,
                        qk=_mm_nt(qb, kc))
    for c, h in pairs:
        d = st[c, h]
        m = jnp.broadcast_to(m0_ref[c, h], (1, LANES))
        gg = d["fc_col"] + m
        mt = jnp.maximum(gg, d["rmax"])
        m_new = mt[T - 1:T, :]
        d["mt"] = mt
        d["eg"] = jnp.exp(gg - mt)
        s = (d["qk"] * jnp.exp(d["dm"] - mt[:, :P])).astype(bf16)
        d["s"] = s
        d["ssum"] = jnp.dot(s, ones_b, preferred_element_type=f32)
        w_c = jnp.exp(d["f_last"] + m - m_new)
        w_j = jnp.exp(d["f_last"] - d["fc_col"] + d["i_col"] - m_new)
        c_ref[c, h] = w_c * c0_ref[c, h] + _mm_tn(d["vc"] * wide(w_j), d["kc"])
        n_ref[c, h] = w_c * n0_ref[c, h] + jnp.sum(w_j * d["kc"], axis=0, keepdims=True)
        m_ref[c, h] = m_new[:, :1]
    for c, h in pairs:
        d = st[c, h]
        qn = _mm_nt(d["qb"], jnp.broadcast_to(n0_ref[c, h], (LANES, DK)))
        num = wide(d["eg"]) * _mm_nt(d["qb"], c0_ref[c, h]) + _mm(d["s"], d["vc"])
        den = d["eg"] * qn + d["ssum"]
        hh = num / wide(jnp.maximum(jnp.abs(den), jnp.exp(-d["mt"])))
        d["hh"] = hh
        d["msq"] = jnp.dot((hh * hh).astype(bf16), mean_b, preferred_element_type=f32)
    for c, h in pairs:
        d = st[c, h]
        rows, vs = slice(c * T, (c + 1) * T), vs_of(h)
        yn = (d["hh"] * wide(lax.rsqrt(d["msq"] + NORM_EPS)))[:T] * gain_ref[:, vs]
        y_ref[rows, vs] = _sigmoid(o_ref[rows, vs]) * yn


def _mlstm_mix_seq(pr, b_gates, gain, B, T, state, bs):
    H, DK, DV = MLSTM_HEADS, MLSTM_DK, MLSTM_DV
    R = bs * T
    rmap = lambda c: (lambda i: (i, c))
    bmap4 = lambda i: (i, 0, 0, 0)
    const2 = lambda i: (0, 0)
    bias = jnp.zeros((1, LANES), f32).at[0, :2 * H].set(b_gates)
    c0, n0, m0 = state
    y, c, n, m = pl.pallas_call(
        functools.partial(_mlstm_seq_kernel, T=T, bs=bs),
        grid=(B // bs,),
        in_specs=[pl.BlockSpec((R, H * DK), rmap(0)), pl.BlockSpec((R, H * DK), rmap(1)),
                  pl.BlockSpec((R, H * DV), rmap(1)), pl.BlockSpec((R, H * DV), rmap(2)),
                  pl.BlockSpec((R, LANES), rmap((2 * H * DK + 2 * H * DV) // LANES)),
                  pl.BlockSpec((1, LANES), const2), pl.BlockSpec((1, H * DV), const2),
                  pl.BlockSpec((bs, H, DV, DK), bmap4), pl.BlockSpec((bs, H, 1, DK), bmap4),
                  pl.BlockSpec((bs, H, 1, 1), bmap4)],
        out_specs=[pl.BlockSpec((R, H * DV), rmap(0)), pl.BlockSpec((bs, H, DV, DK), bmap4),
                   pl.BlockSpec((bs, H, 1, DK), bmap4), pl.BlockSpec((bs, H, 1, 1), bmap4)],
        out_shape=[jax.ShapeDtypeStruct((B * T, H * DV), f32), jax.ShapeDtypeStruct((B, H, DV, DK), f32),
                   jax.ShapeDtypeStruct((B, H, 1, DK), f32), jax.ShapeDtypeStruct((B, H, 1, 1), f32)],
        compiler_params=_cparams("parallel"),
        name="mlstm_mix_seq",
    )(pr, pr, pr, pr, pr, bias, gain.reshape(1, H * DV), c0, n0.reshape(B, H, 1, DK), m0.reshape(B, H, 1, 1))
    return y, c, n.reshape(B, H, DK), m.reshape(B, H)


def _gla_kernel(*refs, kind, layer, L, nchunks, H, DK, DV, has_init, mx):
    if kind == "hgrn":
        q_ref, f_ref, v_ref, g_ref, lbl_ref, gain_ref = refs[:6]
        rest = refs[6:]
    else:
        q_ref, k_ref, v_ref, g_ref, gr_ref, wup_ref, bgate_ref, gain_ref = refs[:8]
        rest = refs[8:]
    if has_init:
        s0_ref = rest[0]
        rest = rest[1:]
    y_ref, s_ref, st_s = rest
    j = pl.program_id(1)

    @pl.when(j == 0)
    def _():
        for h in range(H):
            if has_init:
                st_s[h] = s0_ref[0, h].T
            else:
                st_s[h] = jnp.zeros((DV, DK), f32)

    Lp = max(L, BF16_ROWS)
    rowi = lax.broadcasted_iota(jnp.int32, (Lp, Lp), 0)
    coli = lax.broadcasted_iota(jnp.int32, (Lp, Lp), 1)
    causal = coli <= rowi
    tril_b = causal.astype(bf16)

    def pad_rows(x):
        if Lp == L:
            return x
        return jnp.concatenate([x, jnp.zeros((Lp - L, x.shape[1]), f32)], axis=0)

    if kind == "hgrn":
        logits = lbl_ref[...]
        e = jnp.exp(logits - jnp.max(logits, axis=0, keepdims=True))
        sm = e / jnp.sum(e, axis=0, keepdims=True)
        lb_all = jnp.zeros((1, H * DK), f32)
        for li in range(layer):
            lb_all = lb_all + sm[li:li + 1, :]
    else:
        scale = DK ** -0.5

    def chunk(c, carry):
        rows = pl.ds(pl.multiple_of(c * L, L), L)
        if kind == "gla":
            gk = _mm(pad_rows(gr_ref[rows, :]), wup_ref[...], mx)[:L] + bgate_ref[...]
            ld_all = _log_sigmoid(gk) * (1.0 / GLA_GATE_NORMALIZER)
        for h in range(H):
            ks = slice(h * DK, (h + 1) * DK)
            vs = slice(h * DV, (h + 1) * DV)
            if kind == "hgrn":
                qraw = q_ref[rows, ks]
                fg = f_ref[rows, ks]
                lb = lb_all[:, ks]
                gc = jnp.log(lb + (1.0 - lb) * _sigmoid(fg))
                kc = (1.0 - lb) * _sigmoid(-fg)
                qc = qraw * _sigmoid(qraw)
            else:
                qc = q_ref[rows, ks] * scale
                kc = k_ref[rows, ks]
                gc = ld_all[:, ks]
            vc = pad_rows(v_ref[rows, vs])
            qc, kc, gc = pad_rows(qc), pad_rows(kc), pad_rows(gc)
            bc = _cumsum_rows(gc, tril_b)
            qe = qc * jnp.exp(bc)
            ke = kc * jnp.exp(-bc)
            a = jnp.where(causal, _mm_nt(qe, ke, mx), 0.0)
            st = st_s[h]
            o = (_mm_nt(qe, st, mx) + _mm(a, vc, mx))[:L]
            b_last = bc[L - 1:L, :]
            kdec = kc * jnp.exp(b_last - bc)
            st_s[h] = st * jnp.exp(b_last) + _mm_tn(vc, kdec, mx)
            yn = o * lax.rsqrt(jnp.mean(o * o, axis=1, keepdims=True) + NORM_EPS) * gain_ref[:, vs]
            gg = g_ref[rows, vs]
            y_ref[rows, vs] = yn * (gg * _sigmoid(gg))
        return carry

    lax.fori_loop(0, nchunks, chunk, 0)

    @pl.when(j == pl.num_programs(1) - 1)
    def _():
        for h in range(H):
            s_ref[0, h] = st_s[h].T


def _gla_mix(kind, pr, extra, gain, B, T, state, tq, layer=0):
    if kind == "hgrn":
        H, DK, DV = HGRN_HEADS, HGRN_DK, HGRN_DV
    else:
        H, DK, DV = GLA_HEADS, GLA_DK, GLA_DV
    L = _chunk_len(T, LIN_CHUNK)
    nj = T // tq
    has_init = state is not None
    mx = bf16
    rmap = lambda c: (lambda b, j: (b * nj + j, c))
    bmap4 = lambda b, j: (b, 0, 0, 0)
    const2 = lambda b, j: (0, 0)
    hk, hv = H * DK, H * DV
    if kind == "hgrn":
        (lb_logits,) = extra
        in_specs = [pl.BlockSpec((tq, hk), rmap(0)), pl.BlockSpec((tq, hk), rmap(1)),
                    pl.BlockSpec((tq, hv), rmap(2)), pl.BlockSpec((tq, hv), rmap(3)),
                    pl.BlockSpec((DEPTH, hk), const2), pl.BlockSpec((1, hv), const2)]
        args = [pr, pr, pr, pr, lb_logits, gain.reshape(1, hv)]
    else:
        w_up, b_gate = extra
        w_up_p = jnp.zeros((LANES, hk), f32).at[:GLA_GATE_RANK].set(w_up)
        in_specs = [pl.BlockSpec((tq, hk), rmap(0)), pl.BlockSpec((tq, hk), rmap(1)),
                    pl.BlockSpec((tq, hv), rmap(1)), pl.BlockSpec((tq, hv), rmap(2)),
                    pl.BlockSpec((tq, LANES), rmap((2 * hk + 2 * hv) // LANES)),
                    pl.BlockSpec((LANES, hk), const2), pl.BlockSpec((1, hk), const2),
                    pl.BlockSpec((1, hv), const2)]
        args = [pr, pr, pr, pr, pr, w_up_p, b_gate.reshape(1, hk), gain.reshape(1, hv)]
    if has_init:
        in_specs.append(pl.BlockSpec((1, H, DK, DV), bmap4))
        args.append(state)
    y, s = pl.pallas_call(
        functools.partial(_gla_kernel, kind=kind, layer=layer, L=L, nchunks=tq // L, H=H, DK=DK, DV=DV,
                          has_init=has_init, mx=mx),
        grid=(B, nj),
        in_specs=in_specs,
        out_specs=[pl.BlockSpec((tq, hv), rmap(0)), pl.BlockSpec((1, H, DK, DV), bmap4)],
        out_shape=[jax.ShapeDtypeStruct((B * T, hv), f32), jax.ShapeDtypeStruct((B, H, DK, DV), f32)],
        scratch_shapes=[pltpu.VMEM((H, DV, DK), f32)],
        compiler_params=_cparams("parallel", "arbitrary"),
        name=kind + "_mix",
    )(*args)
    return y, s


def _gla_seq_kernel(*refs, kind, layer, T, bs, H, DK, DV):
    if kind == "hgrn":
        q_ref, f_ref, v_ref, g_ref, lbl_ref, gain_ref = refs[:6]
        rest = refs[6:]
    else:
        q_ref, k_ref, v_ref, g_ref, gr_ref, wup_ref, bgate_ref, gain_ref = refs[:8]
        rest = refs[8:]
    s0_ref, y_ref, s_ref, qe_s, ke_s, kd_s, el_s, v_s = rest
    P = BF16_ROWS
    R = bs * T
    hk = H * DK

    ri = lax.broadcasted_iota(jnp.int32, (R, R), 0)
    ci = lax.broadcasted_iota(jnp.int32, (R, R), 1)
    same_seq = (ri // T) == (ci // T)
    tril_b = (same_seq & (ci <= ri)).astype(bf16)
    seq_b = same_seq.astype(bf16)
    rp = lax.broadcasted_iota(jnp.int32, (P, P), 0)
    cp = lax.broadcasted_iota(jnp.int32, (P, P), 1)
    causal = cp <= rp

    if kind == "hgrn":
        logits = lbl_ref[...]
        e = jnp.exp(logits - jnp.max(logits, axis=0, keepdims=True))
        sm = e / jnp.sum(e, axis=0, keepdims=True)
        lb = jnp.zeros((1, hk), f32)
        for li in range(layer):
            lb = lb + sm[li:li + 1, :]
        qraw = q_ref[...]
        fg = f_ref[...]
        gc = jnp.log(lb + (1.0 - lb) * _sigmoid(fg))
        kc = (1.0 - lb) * _sigmoid(-fg)
        qc = qraw * _sigmoid(qraw)
    else:
        gk = _mm(gr_ref[...], wup_ref[...]) + bgate_ref[...]
        gc = _log_sigmoid(gk) * (1.0 / GLA_GATE_NORMALIZER)
        qc = q_ref[...] * (DK ** -0.5)
        kc = k_ref[...]
    pieces = _split3(gc)
    bc = sum(jnp.dot(tril_b, x, preferred_element_type=f32) for x in pieces)
    bl = sum(jnp.dot(seq_b, x, preferred_element_type=f32) for x in pieces)
    qe = qc * jnp.exp(bc)
    ke = kc * jnp.exp(-bc)
    kd = kc * jnp.exp(bl - bc)
    e_hi, e_mid, e_lo = [x.astype(f32) for x in _split3(jnp.exp(bl))]
    vv = v_ref[...]
    zpad_k = jnp.zeros((P - T, hk), f32)
    zpad_v = jnp.zeros((P - T, vv.shape[1]), f32)
    prow = lax.broadcasted_iota(jnp.int32, (P, hk), 0)
    for c in range(bs):
        src = slice(c * T, (c + 1) * T)
        dst = slice(c * P, (c + 1) * P)
        qe_s[dst, :] = jnp.concatenate([qe[src], zpad_k], axis=0)
        ke_s[dst, :] = jnp.concatenate([ke[src], zpad_k], axis=0)
        kd_s[dst, :] = jnp.concatenate([kd[src], zpad_k], axis=0)
        v_s[dst, :] = jnp.concatenate([vv[src], zpad_v], axis=0)
        first = slice(c * T, c * T + 1)
        el_s[dst, :] = jnp.where(prow == 0, e_hi[first],
                                 jnp.where(prow == 1, e_mid[first], jnp.where(prow == 2, e_lo[first], 0.0)))

    ones_b = jnp.ones((P, LANES), bf16)
    pairs = [(c, h) for c in range(bs) for h in range(H)]
    rows_of = lambda c: slice(c * P, (c + 1) * P)
    ks_of = lambda h: slice(h * DK, (h + 1) * DK)
    vs_of = lambda h: slice(h * DV, (h + 1) * DV)
    qes, vbs, amats, news = {}, {}, {}, {}
    for c, h in pairs:
        rows, ks = rows_of(c), ks_of(h)
        qes[c, h] = qe_s[rows, ks].astype(bf16)
        vbs[c, h] = v_s[rows, vs_of(h)].astype(bf16)
        a = _mm_nt(qes[c, h], ke_s[rows, ks])
        amats[c, h] = jnp.where(causal, a, 0.0).astype(bf16)
        decay = _mm_tn(el_s[rows, ks], ones_b)
        decay = jnp.concatenate([decay] * (DV // LANES), axis=1)
        news[c, h] = s0_ref[c, h] * decay + _mm_tn(kd_s[rows, ks], vbs[c, h])
    for c, h in pairs:
        s_ref[c, h] = news[c, h]
    outs = {}
    for c, h in pairs:
        outs[c, h] = (_mm(qes[c, h], s0_ref[c, h]) + _mm(amats[c, h], vbs[c, h]))[:T]
    for c, h in pairs:
        vs = vs_of(h)
        rows = slice(c * T, (c + 1) * T)
        o = outs[c, h]
        yn = o * lax.rsqrt(jnp.mean(o * o, axis=1, keepdims=True) + NORM_EPS) * gain_ref[:, vs]
        gg = g_ref[rows, vs]
        y_ref[rows, vs] = yn * (gg * _sigmoid(gg))


def _gla_mix_seq(kind, pr, extra, gain, B, T, state, bs, layer=0):
    if kind == "hgrn":
        H, DK, DV = HGRN_HEADS, HGRN_DK, HGRN_DV
    else:
        H, DK, DV = GLA_HEADS, GLA_DK, GLA_DV
    R = bs * T
    rmap = lambda c: (lambda i: (i, c))
    bmap4 = lambda i: (i, 0, 0, 0)
    const2 = lambda i: (0, 0)
    hk, hv = H * DK, H * DV
    if kind == "hgrn":
        (lb_logits,) = extra
        in_specs = [pl.BlockSpec((R, hk), rmap(0)), pl.BlockSpec((R, hk), rmap(1)),
                    pl.BlockSpec((R, hv), rmap(2)), pl.BlockSpec((R, hv), rmap(3)),
                    pl.BlockSpec((DEPTH, hk), const2), pl.BlockSpec((1, hv), const2)]
        args = [pr, pr, pr, pr, lb_logits, gain.reshape(1, hv)]
    else:
        w_up, b_gate = extra
        w_up_p = jnp.zeros((LANES, hk), f32).at[:GLA_GATE_RANK].set(w_up)
        in_specs = [pl.BlockSpec((R, hk), rmap(0)), pl.BlockSpec((R, hk), rmap(1)),
                    pl.BlockSpec((R, hv), rmap(1)), pl.BlockSpec((R, hv), rmap(2)),
                    pl.BlockSpec((R, LANES), rmap((2 * hk + 2 * hv) // LANES)),
                    pl.BlockSpec((LANES, hk), const2), pl.BlockSpec((1, hk), const2),
                    pl.BlockSpec((1, hv), const2)]
        args = [pr, pr, pr, pr, pr, w_up_p, b_gate.reshape(1, hk), gain.reshape(1, hv)]
    in_specs.append(pl.BlockSpec((bs, H, DK, DV), bmap4))
    args.append(state)
    P = BF16_ROWS
    return pl.pallas_call(
        functools.partial(_gla_seq_kernel, kind=kind, layer=layer, T=T, bs=bs, H=H, DK=DK, DV=DV),
        grid=(B // bs,),
        in_specs=in_specs,
        out_specs=[pl.BlockSpec((R, hv), rmap(0)), pl.BlockSpec((bs, H, DK, DV), bmap4)],
        out_shape=[jax.ShapeDtypeStruct((B * T, hv), f32), jax.ShapeDtypeStruct((B, H, DK, DV), f32)],
        scratch_shapes=[pltpu.VMEM((bs * P, hk), f32)] * 4 + [pltpu.VMEM((bs * P, hv), f32)],
        compiler_params=_cparams("parallel"),
        name=kind + "_mix_seq",
    )(*args)


def _gla_block_kernel(*refs, kind, layer, L, nc, H, DK, DV):
    if kind == "hgrn":
        q_ref, f_ref, v_ref, g_ref, lbl_ref, gain_ref = refs[:6]
        rest = refs[6:]
    else:
        q_ref, k_ref, v_ref, g_ref, gr_ref, wup_ref, bgate_ref, gain_ref = refs[:8]
        rest = refs[8:]
    x_ref, wout_ref, xo_ref, s_ref, st_s, qe_s, ke_s, qs_s, kd_s, el_s, y_ref = rest
    j = pl.program_id(1)

    @pl.when(j == 0)
    def _():
        st_s[...] = jnp.zeros_like(st_s)

    rowi = lax.broadcasted_iota(jnp.int32, (L, L), 0)
    coli = lax.broadcasted_iota(jnp.int32, (L, L), 1)
    causal = coli <= rowi
    tril_b = causal.astype(bf16)

    if kind == "hgrn":
        logits = lbl_ref[...]
        e = jnp.exp(logits - jnp.max(logits, axis=0, keepdims=True))
        sm = e / jnp.sum(e, axis=0, keepdims=True)
        lb = jnp.zeros((1, H * DK), f32)
        for li in range(layer):
            lb = lb + sm[li:li + 1, :]
    else:
        scale = DK ** -0.5

    for c in range(nc):
        rows = slice(c * L, (c + 1) * L)
        if kind == "hgrn":
            qraw = q_ref[rows, :]
            fg = f_ref[rows, :]
            gc = jnp.log(lb + (1.0 - lb) * _sigmoid(fg))
            kc = (1.0 - lb) * _sigmoid(-fg)
            qc = qraw * _sigmoid(qraw)
        else:
            gk = jnp.dot(gr_ref[rows, :].astype(bf16), wup_ref[...].astype(bf16),
                         preferred_element_type=f32) + bgate_ref[...]
            gc = _log_sigmoid(gk) * (1.0 / GLA_GATE_NORMALIZER)
            qc = q_ref[rows, :] * scale
            kc = k_ref[rows, :]
        hi, mid, lo = _split3(gc)
        bc = (jnp.dot(tril_b, hi, preferred_element_type=f32) + jnp.dot(tril_b, mid, preferred_element_type=f32)
              + jnp.dot(tril_b, lo, preferred_element_type=f32))
        b_mid = bc[L // 2 - 1:L // 2, :]
        b_last = bc[L - 1:L, :]
        qe = qc * jnp.exp(bc - b_mid)
        ke = kc * jnp.exp(b_mid - bc)
        qe_s[rows, :] = qe.astype(bf16)
        ke_s[rows, :] = ke.astype(bf16)
        qs_s[rows, :] = (qe * jnp.exp(b_mid)).astype(bf16)
        kd_s[rows, :] = (ke * jnp.exp(b_last - b_mid)).astype(bf16)
        el_s[c] = jnp.exp(b_last)

    dn_nt = (((1,), (1,)), ((), ()))
    dn_tn = (((0,), (0,)), ((), ()))
    pairs = [(c, h) for c in range(nc) for h in range(H)]
    rows_of = lambda c: slice(c * L, (c + 1) * L)
    ks_of = lambda h: slice(h * DK, (h + 1) * DK)
    vs_of = lambda h: slice(h * DV, (h + 1) * DV)
    vbs, amats, dsts = {}, {}, {}
    for c, h in pairs:
        rows, ks = rows_of(c), ks_of(h)
        vb = v_ref[rows, vs_of(h)].astype(bf16)
        a = lax.dot_general(qe_s[rows, ks], ke_s[rows, ks], dn_nt, preferred_element_type=f32)
        vbs[c, h] = vb
        amats[c, h] = jnp.where(causal, a, 0.0).astype(bf16)
        dsts[c, h] = lax.dot_general(vb, kd_s[rows, ks], dn_tn, preferred_element_type=f32)
    sts = {}
    for h in range(H):
        st = st_s[h]
        for c in range(nc):
            sts[c, h] = st.astype(bf16)
            st = st * el_s[c][:, ks_of(h)] + dsts[c, h]
        st_s[h] = st
    outs = {}
    for c, h in pairs:
        rows, ks = rows_of(c), ks_of(h)
        outs[c, h] = (lax.dot_general(qs_s[rows, ks], sts[c, h], dn_nt, preferred_element_type=f32)
                      + jnp.dot(amats[c, h], vbs[c, h], preferred_element_type=f32))
    for c, h in pairs:
        rows, vs = rows_of(c), vs_of(h)
        o = outs[c, h]
        yn = o * lax.rsqrt(jnp.mean(o * o, axis=1, keepdims=True) + NORM_EPS) * gain_ref[:, vs]
        gg = g_ref[rows, vs]
        y_ref[rows, vs] = yn * (gg * _sigmoid(gg))
    xo_ref[...] = x_ref[...] + jnp.dot(y_ref[...].astype(bf16), wout_ref[...], preferred_element_type=f32)

    @pl.when(j == pl.num_programs(1) - 1)
    def _():
        for h in range(H):
            s_ref[0, h] = st_s[h].T


def _gla_mix_fresh(kind, pr, extra, gain, x, w_out, B, T, tq, layer=0):
    if kind == "hgrn":
        H, DK, DV = HGRN_HEADS, HGRN_DK, HGRN_DV
    else:
        H, DK, DV = GLA_HEADS, GLA_DK, GLA_DV
    L = 2 * LIN_CHUNK
    nc = tq // L
    nj = T // tq
    rmap = lambda c: (lambda b, j: (b * nj + j, c))
    bmap4 = lambda b, j: (b, 0, 0, 0)
    const2 = lambda b, j: (0, 0)
    hk, hv = H * DK, H * DV
    if kind == "hgrn":
        (lb_logits,) = extra
        in_specs = [pl.BlockSpec((tq, hk), rmap(0)), pl.BlockSpec((tq, hk), rmap(1)),
                    pl.BlockSpec((tq, hv), rmap(2)), pl.BlockSpec((tq, hv), rmap(3)),
                    pl.BlockSpec((DEPTH, hk), const2), pl.BlockSpec((1, hv), const2)]
        args = [pr, pr, pr, pr, lb_logits, gain.reshape(1, hv)]
    else:
        w_up, b_gate = extra
        w_up_p = jnp.zeros((LANES, hk), f32).at[:GLA_GATE_RANK].set(w_up)
        in_specs = [pl.BlockSpec((tq, hk), rmap(0)), pl.BlockSpec((tq, hk), rmap(1)),
                    pl.BlockSpec((tq, hv), rmap(1)), pl.BlockSpec((tq, hv), rmap(2)),
                    pl.BlockSpec((tq, LANES), rmap((2 * hk + 2 * hv) // LANES)),
                    pl.BlockSpec((LANES, hk), const2), pl.BlockSpec((1, hk), const2),
                    pl.BlockSpec((1, hv), const2)]
        args = [pr, pr, pr, pr, pr, w_up_p, b_gate.reshape(1, hk), gain.reshape(1, hv)]
    in_specs += [pl.BlockSpec((tq, D_MODEL), rmap(0)), pl.BlockSpec((hv, D_MODEL), const2)]
    args += [x, w_out]
    return pl.pallas_call(
        functools.partial(_gla_block_kernel, kind=kind, layer=layer, L=L, nc=nc, H=H, DK=DK, DV=DV),
        grid=(B, nj),
        in_specs=in_specs,
        out_specs=[pl.BlockSpec((tq, D_MODEL), rmap(0)), pl.BlockSpec((1, H, DK, DV), bmap4)],
        out_shape=[jax.ShapeDtypeStruct((B * T, D_MODEL), f32), jax.ShapeDtypeStruct((B, H, DK, DV), f32)],
        scratch_shapes=[pltpu.VMEM((H, DV, DK), f32)] + [pltpu.VMEM((tq, hk), bf16)] * 4
                       + [pltpu.VMEM((nc, 1, hk), f32), pltpu.VMEM((tq, hv), f32)],
        compiler_params=_cparams("parallel", "arbitrary"),
        name=kind + "_mix_fresh",
    )(*args)


def _rwkv_pre_kernel(x_ref, aux_ref, sh_ref, g_ref, mu_ref, wrkv_ref, la_ref, lbw_ref, lba_ref, lbg_ref,
                     w0_ref, a0_ref, r_ref, w_ref, k_ref, v_ref, alr_ref, gate_ref, *, long_seq, blocks_per_seq,
                     seq_len):
    g = g_ref[...]
    hn = _rms(x_ref[...], g)
    tm = hn.shape[0]
    rowi = lax.broadcasted_iota(jnp.int32, (tm, 1), 0)
    rolled = pltpu.roll(hn, 1, axis=0)
    if long_seq:
        p_last = _rms(aux_ref[SUBLANES - 1:SUBLANES, :], g)
        at_start = (pl.program_id(0) % blocks_per_seq) == 0
        first = jnp.where(at_start, sh_ref[0], p_last)
        prev = jnp.where(rowi == 0, first, rolled)
    else:
        prev = jnp.where(rowi % seq_len == 0, aux_ref[...], rolled)
    xx = prev - hn

    def lerp(c):
        return (hn + xx * mu_ref[c:c + 1, :]).astype(bf16)

    r_ref[...] = jnp.dot(lerp(0), wrkv_ref[0], preferred_element_type=f32)
    k_ref[...] = jnp.dot(lerp(1), wrkv_ref[1], preferred_element_type=f32)
    v_ref[...] = jnp.dot(lerp(2), wrkv_ref[2], preferred_element_type=f32)
    lw = jnp.tanh(jnp.dot(lerp(3), la_ref[:, 0:64], preferred_element_type=f32))
    wl = -_softplus(-(w0_ref[...] + jnp.dot(lw.astype(bf16), lbw_ref[...], preferred_element_type=f32))) - 0.5
    w_ref[...] = jnp.exp(-jnp.exp(wl))
    la = jnp.dot(lerp(4), la_ref[:, 64:128], preferred_element_type=f32)
    alr_ref[...] = _sigmoid(a0_ref[...] + jnp.dot(la.astype(bf16), lba_ref[...], preferred_element_type=f32))
    lg = _sigmoid(jnp.dot(lerp(5), la_ref[:, 128:256], preferred_element_type=f32))
    gate_ref[...] = jnp.dot(lg.astype(bf16), lbg_ref[...], preferred_element_type=f32)


def _rwkv_pre(x, shift0, g, mu, wrkv, la, lbw, lba, lbg, w0, a0, B, T, tm):
    n, d = x.shape
    long_seq = T % tm == 0
    row = lambda i: (i, 0)
    const2 = lambda i: (0, 0)
    if long_seq:
        bps = T // tm
        sub = tm // SUBLANES
        aux = x
        aux_spec = pl.BlockSpec((SUBLANES, d), lambda i: (jnp.maximum(i * sub - 1, 0), 0))
        sh = shift0.reshape(B, 1, d)
        sh_spec = pl.BlockSpec((1, 1, d), lambda i: (i // bps, 0, 0))
    else:
        assert tm % T == 0
        bps = 1
        aux = jnp.repeat(shift0, T, axis=0)
        aux_spec = pl.BlockSpec((tm, d), row)
        sh = shift0.reshape(B, 1, d)
        sh_spec = pl.BlockSpec((1, 1, d), lambda i: (0, 0, 0))
    out = jax.ShapeDtypeStruct((n, d), f32)
    return pl.pallas_call(
        functools.partial(_rwkv_pre_kernel, long_seq=long_seq, blocks_per_seq=bps, seq_len=T),
        grid=(n // tm,),
        in_specs=[pl.BlockSpec((tm, d), row), aux_spec, sh_spec, pl.BlockSpec((1, d), const2),
                  pl.BlockSpec((6, d), const2),
                  pl.BlockSpec((3, d, d), lambda i: (0, 0, 0), pipeline_mode=pl.Buffered(1)),
                  pl.BlockSpec((d, 256), const2), pl.BlockSpec((64, d), const2), pl.BlockSpec((64, d), const2),
                  pl.BlockSpec((128, d), const2), pl.BlockSpec((1, d), const2), pl.BlockSpec((1, d), const2)],
        out_specs=[pl.BlockSpec((tm, d), row)] * 6,
        out_shape=[out] * 6,
        compiler_params=_cparams("parallel"),
        name="rwkv_pre",
    )(x, aux, sh, g.reshape(1, d), mu, wrkv, la, lbw, lba, lbg, w0.reshape(1, d), a0.reshape(1, d))


def _rwkv_scan_kernel(*refs, tt, has_init, unroll):
    N = RWKV_N
    (r_ref, w_ref, k_ref, v_ref, alr_ref, gate_ref, x_ref, wout_ref,
     kk_ref, ka_ref, rk_ref, lnw_ref, lnb_ref) = refs[:13]
    if has_init:
        s0_ref = refs[13]
        rest = refs[14:]
    else:
        rest = refs[13:]
    xo_ref, sout_ref, s_s, y_s, vec_s, z_ref, zz_s = rest
    HP = RWKV_HEADS // 2
    j = pl.program_id(1)

    @pl.when(j == 0)
    def _():
        if has_init:
            s_s[...] = s0_ref[...]
        else:
            s_s[...] = jnp.zeros_like(s_s)

    low = lax.broadcasted_iota(jnp.int32, (N, LANES), 1) < N

    def load_pair(ref, t0):
        tiles = []
        for t in (t0, t0 + 1):
            x = ref[:, t, :]
            tiles += [x[:, hp * LANES:(hp + 1) * LANES] for hp in range(HP)]
        xt = jnp.concatenate(tiles, axis=0).T
        ev, od = xt[:N], xt[N:]
        return (jnp.where(low, ev, pltpu.roll(od, N, axis=1)),
                jnp.where(low, pltpu.roll(ev, N, axis=1), od))

    def store_pair(ref, t0, z0, z1):
        ev = jnp.where(low, z0, pltpu.roll(z1, N, axis=1))
        od = jnp.where(low, pltpu.roll(z0, N, axis=1), z1)
        x = jnp.concatenate([ev, od], axis=0).T
        for i, t in enumerate((t0, t0 + 1)):
            tiles = [x[(i * HP + hp) * SUBLANES:(i * HP + hp + 1) * SUBLANES, :] for hp in range(HP)]
            ref[:, t, :] = jnp.concatenate(tiles, axis=1)

    VA, VW, VB, VK, VR, VV = range(6)

    def stage(tk, r, w, k, v, alr):
        kkraw = k * kk_ref[...]
        nrm = jnp.sqrt(jnp.sum(kkraw * kkraw, axis=0, keepdims=True))
        kk = kkraw / jnp.maximum(nrm, 1e-12)
        vec_s[tk, VA] = -kk
        vec_s[tk, VW] = w
        vec_s[tk, VB] = kk * alr
        vec_s[tk, VK] = k * (1.0 + (alr - 1.0) * ka_ref[...])
        vec_s[tk, VR] = r
        vec_s[tk, VV] = v

    def key_row(tk, which, kc):
        return vec_s[tk, which, pl.ds(kc, 1), :]

    def project(tk):
        def body(i, sa):
            for u in range(unroll):
                kc = i * unroll + u
                sa = sa + s_s[kc] * key_row(tk, VA, kc)
            return sa

        return lax.fori_loop(0, N // unroll, body, jnp.zeros((N, LANES), f32))

    def update(tk, sa):
        vv = vec_s[tk, VV]
        y = jnp.zeros((N, LANES), f32)
        for kc in range(N):
            sk = s_s[kc] * key_row(tk, VW, kc) + sa * key_row(tk, VB, kc) + vv * key_row(tk, VK, kc)
            s_s[kc] = sk
            y = y + sk * key_row(tk, VR, kc)
        y_s[tk] = y

    def epilogue(tk):
        y = y_s[tk]
        mean = jnp.mean(y, axis=0, keepdims=True)
        yc = y - mean
        var = jnp.mean(yc * yc, axis=0, keepdims=True)
        yn = yc * lax.rsqrt(var + RWKV_GN_EPS)
        bonus = jnp.sum(vec_s[tk, VR] * vec_s[tk, VK] * rk_ref[...], axis=0, keepdims=True) * vec_s[tk, VV]
        return yn * lnw_ref[...] + lnb_ref[...] + bonus

    def stage_pair(slot, t0):
        streams = [load_pair(ref, t0) for ref in (r_ref, w_ref, k_ref, v_ref, alr_ref)]
        for i in range(2):
            stage(slot + i, *[s[i] for s in streams])

    @pl.when(j == 0)
    def _():
        vec_s[...] = jnp.zeros_like(vec_s)
        y_s[...] = jnp.zeros_like(y_s)

    npairs = tt // 2
    stage_pair(0, 0)

    def step(p, carry):
        slot = 2 * (p % 2)
        prev = 2 - slot
        sa = project(slot)
        update(slot, sa)
        zz_s[0] = epilogue(prev)
        zz_s[1] = epilogue(prev + 1)
        stage_pair(prev, 2 * jnp.minimum(p + 1, npairs - 1))
        sa = project(slot + 1)
        update(slot + 1, sa)
        store_pair(z_ref, 2 * jnp.maximum(p - 1, 0), zz_s[0], zz_s[1])
        return carry

    lax.fori_loop(0, npairs, step, 0)
    last = 2 * ((npairs - 1) % 2)
    store_pair(z_ref, tt - 2, epilogue(last), epilogue(last + 1))

    rows = RWKV_SEQ_PER_STEP * tt
    zg = (z_ref[...] * gate_ref[...]).reshape(rows, D_MODEL).astype(bf16)
    proj = jnp.dot(zg, wout_ref[...], preferred_element_type=f32)
    xo_ref[...] = x_ref[...] + proj.reshape(RWKV_SEQ_PER_STEP, tt, D_MODEL)

    @pl.when(j == pl.num_programs(1) - 1)
    def _():
        sout_ref[...] = s_s[...]


RWKV_SEQ_PER_STEP = LANES // RWKV_HEADS


def _rwkv_scan(r, w, k, v, alr, gate, x, w_out, slabs, s0, B, T, tt):
    N = RWKV_N
    G = B // RWKV_SEQ_PER_STEP
    has_init = s0 is not None
    view = lambda a: a.reshape(B, T, D_MODEL)
    tmap = lambda g, j: (g, j, 0)
    smap = lambda g, j: (0, 0, g)
    const2 = lambda g, j: (0, 0)
    stream = pl.BlockSpec((RWKV_SEQ_PER_STEP, tt, D_MODEL), tmap)
    in_specs = [stream] * 7 + [pl.BlockSpec((D_MODEL, D_MODEL), const2)] + [pl.BlockSpec((N, LANES), const2)] * 5
    args = [view(r), view(w), view(k), view(v), view(alr), view(gate), view(x), w_out] + list(slabs)
    if has_init:
        in_specs.append(pl.BlockSpec((N, N, LANES), smap))
        args.append(s0)
    z, s = pl.pallas_call(
        functools.partial(_rwkv_scan_kernel, tt=tt, has_init=has_init, unroll=16),
        grid=(G, T // tt),
        in_specs=in_specs,
        out_specs=[stream, pl.BlockSpec((N, N, LANES), smap)],
        out_shape=[jax.ShapeDtypeStruct((B, T, D_MODEL), f32),
                   jax.ShapeDtypeStruct((N, N, G * LANES), f32)],
        scratch_shapes=[pltpu.VMEM((N, N, LANES), f32), pltpu.VMEM((4, N, LANES), f32),
                        pltpu.VMEM((4, 6, N, LANES), f32), pltpu.VMEM((RWKV_SEQ_PER_STEP, tt, D_MODEL), f32),
                        pltpu.VMEM((2, N, LANES), f32)],
        compiler_params=_cparams("parallel", "arbitrary"),
        name="rwkv_scan",
    )(*args)
    return z.reshape(B * T, D_MODEL), s


def _lane_slab(p):
    q = p.reshape(RWKV_HEADS // 2, 2, RWKV_N).transpose(2, 1, 0)
    q = jnp.broadcast_to(q[:, :, :, None], (RWKV_N, 2, RWKV_HEADS // 2, RWKV_SEQ_PER_STEP))
    return q.reshape(RWKV_N, LANES)


def _rwkv_mix(x, p, B, T, state, tm):
    H, N = RWKV_HEADS, RWKV_N
    G = B // RWKV_SEQ_PER_STEP
    if state is None:
        shift0 = jnp.zeros((B, D_MODEL), f32)
        s0 = None
    else:
        s_in, shift0 = state
        s0 = (s_in.reshape(G, RWKV_SEQ_PER_STEP, H // 2, 2, N, N)
              .transpose(5, 4, 0, 3, 2, 1).reshape(N, N, G * LANES))
    r, w, k, v, alr, gate = _rwkv_pre(x, shift0, p["g"], p["mu"], p["wrkv"], p["la"], p["lbw"], p["lba"], p["lbg"],
                                      p["w0"], p["a0"], B, T, tm)
    slabs = [_lane_slab(p[n]) for n in ("k_k", "k_a", "r_k", "ln_w", "ln_b")]
    x_new, s = _rwkv_scan(r, w, k, v, alr, gate, x, p["w_out"], slabs, s0, B, T, min(T, 32))
    s = (s.reshape(N, N, G, 2, H // 2, RWKV_SEQ_PER_STEP)
         .transpose(2, 5, 4, 3, 1, 0).reshape(B, H, N, N))
    return x_new, s


def _pad_cols(w, to):
    return jnp.pad(w, ((0, 0), (0, to - w.shape[1])))


def _trunk(x, B, T, states, p):
    n = x.shape[0]
    tm = min(512, n)
    tmm = min(1024, n)
    tq = min(256, T)
    fresh = states is None
    block_ok = fresh and T % tq == 0 and tq % (2 * LIN_CHUNK) == 0
    seq_ok = (not fresh) and T == SUBLANES and T <= LIN_CHUNK and B % SEQS_PER_STEP == 0
    if not fresh:
        m_c, m_n, m_m, h_s, g_s, r_s, r_sh = states
    new = {}
    for li in range(DEPTH):
        g_mix = p["norm_mix"][li]
        if li == 0:
            pr = _norm_proj(x, g_mix, p["mlstm_w_in"], tmm, 1664)
            if fresh and T % tq == 0 and tq % (2 * MLSTM_CHUNK) == 0:
                x, c, nn, m = _mlstm_mix_fresh(pr, p["mlstm_b_gates"], p["mlstm_norm"], x, p["mlstm_w_out"],
                                               B, T, tq)
            else:
                if seq_ok:
                    y, c, nn, m = _mlstm_mix_seq(pr, p["mlstm_b_gates"], p["mlstm_norm"], B, T,
                                                 (m_c[0], m_n[0], m_m[0]), SEQS_PER_STEP)
                else:
                    y, c, nn, m = _mlstm_mix(pr, p["mlstm_b_gates"], p["mlstm_norm"], B, T,
                                             None if fresh else (m_c[0], m_n[0], m_m[0]), tq)
                x = _out_proj(y, p["mlstm_w_out"], x, tm)
            new["C"], new["n"], new["m"] = c[None], nn[None], m[None]
        elif li == 1:
            pr = _norm_proj(x, g_mix, p["hgrn_w_in"], tmm, 2048)
            if block_ok:
                x, s = _gla_mix_fresh("hgrn", pr, (p["hgrn_lb_logits"],), p["hgrn_norm"], x, p["hgrn_w_out"],
                                      B, T, tq, layer=li)
            else:
                if seq_ok:
                    y, s = _gla_mix_seq("hgrn", pr, (p["hgrn_lb_logits"],), p["hgrn_norm"], B, T, h_s[0],
                                        SEQS_PER_STEP, layer=li)
                else:
                    y, s = _gla_mix("hgrn", pr, (p["hgrn_lb_logits"],), p["hgrn_norm"], B, T,
                                    None if fresh else h_s[0], tq, layer=li)
                x = _out_proj(y, p["hgrn_w_out"], x, tm)
            new["hS"] = s[None]
        elif li == 2:
            pr = _norm_proj(x, g_mix, p["gla_w_in"], tmm, 1664)
            if block_ok:
                x, s = _gla_mix_fresh("gla", pr, (p["gla_w_gate_up"], p["gla_b_gate"]), p["gla_norm"], x,
                                      p["gla_w_out"], B, T, tq)
            else:
                if seq_ok:
                    y, s = _gla_mix_seq("gla", pr, (p["gla_w_gate_up"], p["gla_b_gate"]), p["gla_norm"], B, T,
                                        g_s[0], SEQS_PER_STEP)
                else:
                    y, s = _gla_mix("gla", pr, (p["gla_w_gate_up"], p["gla_b_gate"]), p["gla_norm"], B, T,
                                    None if fresh else g_s[0], tq)
                x = _out_proj(y, p["gla_w_out"], x, tm)
            new["gS"] = s[None]
        else:
            rp = dict(p["rwkv"], g=g_mix, w_out=p["rwkv_w_out"])
            x_last = x.reshape(B, T, D_MODEL)[:, T - 1, :]
            new["sh"] = _rmsnorm(x_last, g_mix, B)[None]
            x, s = _rwkv_mix(x, rp, B, T, None if fresh else (r_s[0], r_sh[0]), tm)
            new["rS"] = s[None]
        x = _ffn(x, p["norm_ffn"][li], p["ffn_w_gate_up"][li], p["ffn_w_down"][li], tm, 1408,
                 final_gain=p["norm_final"] if li == DEPTH - 1 else None)
    return x.reshape(B, T, D_MODEL), (new["C"], new["n"], new["m"], new["hS"], new["gS"], new["rS"], new["sh"])


def kernel(x_prompt, x_sample, state_mlstm_C, state_mlstm_n, state_mlstm_m, state_hgrn_S, state_gla_S, state_rwkv_S, state_rwkv_shift, norm_mix, norm_ffn, norm_final, mlstm_w_in, mlstm_b_gates, mlstm_norm, mlstm_w_out, hgrn_w_in, hgrn_lb_logits, hgrn_norm, hgrn_w_out, gla_w_in, gla_w_gate_up, gla_b_gate, gla_norm, gla_w_out, rwkv_mu, rwkv_w_rkv, rwkv_w_lora_a, rwkv_w_lora_b, rwkv_w0, rwkv_a_lora_a, rwkv_a_lora_b, rwkv_a0, rwkv_g_lora_a, rwkv_g_lora_b, rwkv_k_k, rwkv_k_a, rwkv_r_k, rwkv_ln_w, rwkv_ln_b, rwkv_w_out, ffn_w_gate_up, ffn_w_down):
    cast = lambda w: w.astype(bf16)
    p = dict(
        norm_mix=norm_mix, norm_ffn=norm_ffn, norm_final=norm_final,
        mlstm_w_in=cast(_pad_cols(mlstm_w_in[0], 3328)), mlstm_b_gates=mlstm_b_gates[0], mlstm_norm=mlstm_norm[0],
        mlstm_w_out=cast(mlstm_w_out[0]),
        hgrn_w_in=cast(hgrn_w_in[0]), hgrn_lb_logits=hgrn_lb_logits, hgrn_norm=hgrn_norm[0],
        hgrn_w_out=cast(hgrn_w_out[0]),
        gla_w_in=cast(_pad_cols(gla_w_in[0], 3328)), gla_w_gate_up=gla_w_gate_up[0], gla_b_gate=gla_b_gate[0],
        gla_norm=gla_norm[0], gla_w_out=cast(gla_w_out[0]),
        rwkv=dict(mu=rwkv_mu[0], wrkv=cast(rwkv_w_rkv[0]),
                  la=cast(jnp.concatenate([rwkv_w_lora_a[0], rwkv_a_lora_a[0], rwkv_g_lora_a[0]], axis=1)),
                  lbw=cast(rwkv_w_lora_b[0]), lba=cast(rwkv_a_lora_b[0]), lbg=cast(rwkv_g_lora_b[0]),
                  w0=rwkv_w0[0], a0=rwkv_a0[0], k_k=rwkv_k_k[0], k_a=rwkv_k_a[0], r_k=rwkv_r_k[0].reshape(-1),
                  ln_w=rwkv_ln_w[0], ln_b=rwkv_ln_b[0]),
        rwkv_w_out=cast(rwkv_w_out[0]),
        ffn_w_gate_up=cast(ffn_w_gate_up), ffn_w_down=cast(ffn_w_down),
    )
    bp, tp, _ = x_prompt.shape
    bs, ts, _ = x_sample.shape
    y_p, st_p = _trunk(x_prompt.reshape(bp * tp, D_MODEL), bp, tp, None, p)
    y_s, st_s = _trunk(x_sample.reshape(bs * ts, D_MODEL), bs, ts,
                       (state_mlstm_C, state_mlstm_n, state_mlstm_m, state_hgrn_S, state_gla_S, state_rwkv_S,
                        state_rwkv_shift), p)
    return (y_p, y_s) + st_p + st_s
```

```python
import functools

import jax
import jax.numpy as jnp
from jax import lax
from jax.experimental import pallas as pl
from jax.experimental.pallas import tpu as pltpu

f32 = jnp.float32
bf16 = jnp.bfloat16

D_MODEL = 1024
DEPTH = 4
NORM_EPS = 1e-6

MLSTM_HEADS, MLSTM_DK, MLSTM_DV, MLSTM_CHUNK = 4, 128, 256, 64
HGRN_HEADS, HGRN_DK, HGRN_DV = 8, 128, 128
GLA_HEADS, GLA_DK, GLA_DV = 4, 128, 256
GLA_GATE_RANK = 16
GLA_GATE_NORMALIZER = 16.0
LIN_CHUNK = 32
RWKV_HEADS, RWKV_N = 16, 64
RWKV_GN_EPS = 64e-5
FFN_HIDDEN = 2816

LANES = 128
SUBLANES = 8
BF16_ROWS = 16
SEQS_PER_STEP = 4
VMEM_LIMIT_BYTES = 52 * 1024 * 1024


def _cparams(*sem):
    return pltpu.CompilerParams(dimension_semantics=sem, vmem_limit_bytes=VMEM_LIMIT_BYTES)


def _chunk_len(t, cap):
    return max(d for d in range(1, min(cap, t) + 1) if t % d == 0)


def _rms(x, g):
    ms = jnp.mean(x * x, axis=-1, keepdims=True)
    return x * lax.rsqrt(ms + NORM_EPS) * g


def _sigmoid(x):
    return jax.nn.sigmoid(x)


def _softplus(x):
    return jnp.maximum(x, 0.0) + jnp.log1p(jnp.exp(-jnp.abs(x)))


def _log_sigmoid(x):
    return -_softplus(-x)


def _mm(a, b, mx=bf16):
    return jnp.dot(a.astype(mx), b.astype(mx), preferred_element_type=f32)


def _mm_nt(a, b, mx=bf16):
    return lax.dot_general(a.astype(mx), b.astype(mx), (((1,), (1,)), ((), ())), preferred_element_type=f32)


def _mm_tn(a, b, mx=bf16):
    return lax.dot_general(a.astype(mx), b.astype(mx), (((0,), (0,)), ((), ())), preferred_element_type=f32)


def _split3(x):
    hi = x.astype(bf16)
    r1 = x - hi.astype(f32)
    mid = r1.astype(bf16)
    lo = (r1 - mid.astype(f32)).astype(bf16)
    return hi, mid, lo


def _cumsum_rows(x, tril_b):
    hi, mid, lo = _split3(x)
    return (jnp.dot(tril_b, hi, preferred_element_type=f32)
            + jnp.dot(tril_b, mid, preferred_element_type=f32)
            + jnp.dot(tril_b, lo, preferred_element_type=f32))


def _norm_proj_kernel(x_ref, g_ref, w_ref, o_ref, hn_ref):
    @pl.when(pl.program_id(1) == 0)
    def _():
        hn_ref[...] = _rms(x_ref[...], g_ref[...]).astype(bf16)

    o_ref[...] = jnp.dot(hn_ref[...], w_ref[...], preferred_element_type=f32)


def _norm_proj(x, g, w, tm, tn):
    n, d = x.shape
    e = w.shape[1]
    return pl.pallas_call(
        _norm_proj_kernel,
        grid=(n // tm, e // tn),
        in_specs=[pl.BlockSpec((tm, d), lambda i, j: (i, 0)),
                  pl.BlockSpec((1, d), lambda i, j: (0, 0)),
                  pl.BlockSpec((d, tn), lambda i, j: (0, j))],
        out_specs=pl.BlockSpec((tm, tn), lambda i, j: (i, j)),
        out_shape=jax.ShapeDtypeStruct((n, e), f32),
        scratch_shapes=[pltpu.VMEM((tm, d), bf16)],
        compiler_params=_cparams("parallel", "arbitrary"),
        name="norm_proj",
    )(x, g.reshape(1, d), w)


def _out_proj_kernel(*refs, gated):
    if gated:
        y_ref, gate_ref, w_ref, res_ref, o_ref = refs
        y = y_ref[...] * gate_ref[...]
    else:
        y_ref, w_ref, res_ref, o_ref = refs
        y = y_ref[...]
    o_ref[...] = res_ref[...] + jnp.dot(y.astype(bf16), w_ref[...], preferred_element_type=f32)


def _out_proj(y, w, res, tm, gate=None):
    n, e = y.shape
    d = w.shape[1]
    row = lambda i: (i, 0)
    args = [y] + ([gate] if gate is not None else []) + [w, res]
    in_specs = ([pl.BlockSpec((tm, e), row)] + ([pl.BlockSpec((tm, e), row)] if gate is not None else [])
                + [pl.BlockSpec((e, d), lambda i: (0, 0)), pl.BlockSpec((tm, d), row)])
    return pl.pallas_call(
        functools.partial(_out_proj_kernel, gated=gate is not None),
        grid=(n // tm,),
        in_specs=in_specs,
        out_specs=pl.BlockSpec((tm, d), row),
        out_shape=jax.ShapeDtypeStruct((n, d), f32),
        compiler_params=_cparams("parallel"),
        name="out_proj",
    )(*args)


def _ffn_kernel(*refs, final_norm):
    if final_norm:
        x_ref, g_ref, wg_ref, wu_ref, wd_ref, gf_ref, o_ref, hn_ref, acc_ref = refs
    else:
        x_ref, g_ref, wg_ref, wu_ref, wd_ref, o_ref, hn_ref, acc_ref = refs
    j = pl.program_id(1)

    @pl.when(j == 0)
    def _():
        hn_ref[...] = _rms(x_ref[...], g_ref[...]).astype(bf16)
        acc_ref[...] = jnp.zeros_like(acc_ref)

    h = hn_ref[...]
    gt = jnp.dot(h, wg_ref[...], preferred_element_type=f32)
    up = jnp.dot(h, wu_ref[...], preferred_element_type=f32)
    act = (gt * _sigmoid(gt) * up).astype(bf16)
    acc_ref[...] += jnp.dot(act, wd_ref[...], preferred_element_type=f32)

    @pl.when(j == pl.num_programs(1) - 1)
    def _():
        out = x_ref[...] + acc_ref[...]
        o_ref[...] = _rms(out, gf_ref[...]) if final_norm else out


def _ffn(x, g, w_gu, w_down, tm, tf, final_gain=None):
    n, d = x.shape
    nf = FFN_HIDDEN // tf
    final_norm = final_gain is not None
    in_specs = [pl.BlockSpec((tm, d), lambda i, j: (i, 0)),
                pl.BlockSpec((1, d), lambda i, j: (0, 0)),
                pl.BlockSpec((d, tf), lambda i, j: (0, j)),
                pl.BlockSpec((d, tf), lambda i, j: (0, j + nf)),
                pl.BlockSpec((tf, d), lambda i, j: (j, 0))]
    args = [x, g.reshape(1, d), w_gu, w_gu, w_down]
    if final_norm:
        in_specs.append(pl.BlockSpec((1, d), lambda i, j: (0, 0)))
        args.append(final_gain.reshape(1, d))
    return pl.pallas_call(
        functools.partial(_ffn_kernel, final_norm=final_norm),
        grid=(n // tm, nf),
        in_specs=in_specs,
        out_specs=pl.BlockSpec((tm, d), lambda i, j: (i, 0)),
        out_shape=jax.ShapeDtypeStruct((n, d), f32),
        scratch_shapes=[pltpu.VMEM((tm, d), bf16), pltpu.VMEM((tm, d), f32)],
        compiler_params=_cparams("parallel", "arbitrary"),
        name="ffn",
    )(*args)


def _rmsnorm_kernel(x_ref, g_ref, o_ref):
    o_ref[...] = _rms(x_ref[...], g_ref[...])


def _rmsnorm(x, g, tm):
    n, d = x.shape
    return pl.pallas_call(
        _rmsnorm_kernel,
        grid=(n // tm,),
        in_specs=[pl.BlockSpec((tm, d), lambda i: (i, 0)), pl.BlockSpec((1, d), lambda i: (0, 0))],
        out_specs=pl.BlockSpec((tm, d), lambda i: (i, 0)),
        out_shape=jax.ShapeDtypeStruct((n, d), f32),
        compiler_params=_cparams("parallel"),
        name="rmsnorm",
    )(x, g.reshape(1, d))


def _mlstm_kernel(*refs, L, nchunks, has_init, mx):
    H, DK, DV = MLSTM_HEADS, MLSTM_DK, MLSTM_DV
    q_ref, k_ref, v_ref, o_ref, gt_ref, bias_ref, gain_ref = refs[:7]
    if has_init:
        c0_ref, n0_ref, m0_ref = refs[7:10]
        rest = refs[10:]
    else:
        rest = refs[7:]
    y_ref, c_ref, n_ref, m_ref, c_s, n_s, m_s = rest
    j = pl.program_id(1)

    @pl.when(j == 0)
    def _():
        if has_init:
            c_s[...] = c0_ref[0]
            n_s[...] = n0_ref[0]
            m_s[...] = m0_ref[0]
        else:
            c_s[...] = jnp.zeros_like(c_s)
            n_s[...] = jnp.zeros_like(n_s)
            m_s[...] = jnp.zeros_like(m_s)

    Lp = max(L, BF16_ROWS)
    rowi = lax.broadcasted_iota(jnp.int32, (Lp, Lp), 0)
    coli = lax.broadcasted_iota(jnp.int32, (Lp, Lp), 1)
    causal = coli <= rowi
    eye = coli == rowi
    scale = DK ** -0.5

    def to_row(col):
        return jnp.sum(jnp.where(eye, col, 0.0), axis=0, keepdims=True)

    def pad_rows(x, value=0.0):
        if Lp == L:
            return x
        return jnp.concatenate([x, jnp.full((Lp - L, x.shape[1]), value, f32)], axis=0)

    def chunk(c, carry):
        rows = pl.ds(pl.multiple_of(c * L, L), L)
        gts_raw = gt_ref[rows, :] + bias_ref[...]
        lsg = pad_rows(_log_sigmoid(gts_raw))
        gts = pad_rows(gts_raw, -jnp.inf)
        for h in range(H):
            i_col = gts[:, h:h + 1]
            f_col = lsg[:, H + h:H + h + 1]
            f_row = to_row(f_col)
            fc_col = jnp.sum(jnp.where(causal, f_row, 0.0), axis=1, keepdims=True)
            fc_row = to_row(fc_col)
            i_row = to_row(i_col)
            m = m_s[h]
            dm = jnp.where(causal, fc_col - fc_row + i_row, -jnp.inf)
            g = fc_col + m
            mt = jnp.maximum(g, jnp.max(dm, axis=1, keepdims=True))
            p = jnp.exp(dm - mt)
            qc = pad_rows(q_ref[rows, h * DK:(h + 1) * DK] * scale)
            kc = pad_rows(k_ref[rows, h * DK:(h + 1) * DK])
            vc = pad_rows(v_ref[rows, h * DV:(h + 1) * DV])
            s = _mm_nt(qc, kc, mx) * p
            eg = jnp.exp(g - mt)
            cst = c_s[h]
            nst = n_s[h]
            num = eg * _mm_nt(qc, cst, mx) + _mm(s, vc, mx)
            den = eg * jnp.sum(qc * nst, axis=1, keepdims=True) + jnp.sum(s, axis=1, keepdims=True)
            hh = num / jnp.maximum(jnp.abs(den), jnp.exp(-mt))
            m_new = mt[L - 1:L, :]
            f_last = fc_col[L - 1:L, :]
            w_c = jnp.exp(f_last + m - m_new)
            w_j = jnp.exp(f_last - fc_col + i_col - m_new)
            c_s[h] = w_c * cst + _mm_tn(vc * w_j, kc, mx)
            n_s[h] = w_c * nst + jnp.sum(w_j * kc, axis=0, keepdims=True)
            m_s[h] = m_new
            hh = hh[:L]
            yn = hh * lax.rsqrt(jnp.mean(hh * hh, axis=1, keepdims=True) + NORM_EPS)
            yn = yn * gain_ref[:, h * DV:(h + 1) * DV]
            y_ref[rows, h * DV:(h + 1) * DV] = _sigmoid(o_ref[rows, h * DV:(h + 1) * DV]) * yn
        return carry

    lax.fori_loop(0, nchunks, chunk, 0)

    @pl.when(j == pl.num_programs(1) - 1)
    def _():
        c_ref[0] = c_s[...]
        n_ref[0] = n_s[...]
        m_ref[0] = m_s[...]


def _mlstm_mix(pr, b_gates, gain, B, T, state, tq):
    H, DK, DV = MLSTM_HEADS, MLSTM_DK, MLSTM_DV
    L = _chunk_len(T, MLSTM_CHUNK)
    nj = T // tq
    has_init = state is not None
    mx = bf16
    rmap = lambda c: (lambda b, j: (b * nj + j, c))
    bmap4 = lambda b, j: (b, 0, 0, 0)
    bias = jnp.zeros((1, LANES), f32).at[0, :2 * H].set(b_gates)
    in_specs = [pl.BlockSpec((tq, H * DK), rmap(0)), pl.BlockSpec((tq, H * DK), rmap(1)),
                pl.BlockSpec((tq, H * DV), rmap(1)), pl.BlockSpec((tq, H * DV), rmap(2)),
                pl.BlockSpec((tq, LANES), rmap((2 * H * DK + 2 * H * DV) // LANES)),
                pl.BlockSpec((1, LANES), lambda b, j: (0, 0)),
                pl.BlockSpec((1, H * DV), lambda b, j: (0, 0))]
    args = [pr, pr, pr, pr, pr, bias, gain.reshape(1, H * DV)]
    if has_init:
        c0, n0, m0 = state
        in_specs += [pl.BlockSpec((1, H, DV, DK), bmap4), pl.BlockSpec((1, H, 1, DK), bmap4),
                     pl.BlockSpec((1, H, 1, 1), bmap4)]
        args += [c0, n0.reshape(B, H, 1, DK), m0.reshape(B, H, 1, 1)]
    y, c, n, m = pl.pallas_call(
        functools.partial(_mlstm_kernel, L=L, nchunks=tq // L, has_init=has_init, mx=mx),
        grid=(B, nj),
        in_specs=in_specs,
        out_specs=[pl.BlockSpec((tq, H * DV), rmap(0)), pl.BlockSpec((1, H, DV, DK), bmap4),
                   pl.BlockSpec((1, H, 1, DK), bmap4), pl.BlockSpec((1, H, 1, 1), bmap4)],
        out_shape=[jax.ShapeDtypeStruct((B * T, H * DV), f32), jax.ShapeDtypeStruct((B, H, DV, DK), f32),
                   jax.ShapeDtypeStruct((B, H, 1, DK), f32), jax.ShapeDtypeStruct((B, H, 1, 1), f32)],
        scratch_shapes=[pltpu.VMEM((H, DV, DK), f32), pltpu.VMEM((H, 1, DK), f32), pltpu.VMEM((H, 1, 1), f32)],
        compiler_params=_cparams("parallel", "arbitrary"),
        name="mlstm_mix",
    )(*args)
    return y, c, n.reshape(B, H, DK), m.reshape(B, H)


def _mlstm_block_kernel(q_ref, k_ref, v_ref, o_ref, gt_ref, bias_ref, gain_ref, x_ref, wout_ref,
                        xo_ref, c_ref, n_ref, m_ref, c_s, n_s, m_s, y_ref, *, L, nc):
    H, DK, DV = MLSTM_HEADS, MLSTM_DK, MLSTM_DV
    P = 2 * L
    j = pl.program_id(1)

    @pl.when(j == 0)
    def _():
        c_s[...] = jnp.zeros_like(c_s)
        n_s[...] = jnp.zeros_like(n_s)
        m_s[...] = jnp.zeros_like(m_s)

    rowi = lax.broadcasted_iota(jnp.int32, (L, L), 0)
    coli = lax.broadcasted_iota(jnp.int32, (L, L), 1)
    causal = coli <= rowi
    r2 = lax.broadcasted_iota(jnp.int32, (P, P), 0)
    c2 = lax.broadcasted_iota(jnp.int32, (P, P), 1)
    pair_tril = ((c2 <= r2) & ((c2 >= L) == (r2 >= L))).astype(bf16)
    scale = DK ** -0.5

    gates = []
    for pp in range(nc // 2):
        g = gt_ref[pp * P:(pp + 1) * P, :] + bias_ref[...]
        hi, mid, lo = _split3(_log_sigmoid(g))
        fcum = (jnp.dot(pair_tril, hi, preferred_element_type=f32)
                + jnp.dot(pair_tril, mid, preferred_element_type=f32)
                + jnp.dot(pair_tril, lo, preferred_element_type=f32))
        g_t = g.T
        f_t = fcum.T
        for half in range(2):
            rs = slice(half * L, (half + 1) * L)
            gates.append((g[rs, :], fcum[rs, :], g_t[:, rs], f_t[:, rs]))

    pairs = [(c, h) for c in range(nc) for h in range(H)]
    rows_of = lambda c: slice(c * L, (c + 1) * L)
    ks_of = lambda h: slice(h * DK, (h + 1) * DK)
    vs_of = lambda h: slice(h * DV, (h + 1) * DV)
    ones_b = jnp.ones((L, LANES), bf16)
    wide = lambda x: jnp.concatenate([x] * (DV // LANES), axis=1)
    dms, rmax, fcols, icols, qks, qbs = {}, {}, {}, {}, {}, {}
    for c, h in pairs:
        g_c, f_c, g_r, f_r = gates[c]
        icols[c, h] = jnp.broadcast_to(g_c[:, h:h + 1], (L, LANES))
        fcols[c, h] = jnp.broadcast_to(f_c[:, H + h:H + h + 1], (L, LANES))
        dm = jnp.where(causal, fcols[c, h][:, :L] - f_r[H + h:H + h + 1, :] + g_r[h:h + 1, :], -jnp.inf)
        dms[c, h] = dm
        rmax[c, h] = jnp.broadcast_to(jnp.max(dm, axis=1, keepdims=True), (L, LANES))
        qbs[c, h] = (q_ref[rows_of(c), ks_of(h)] * scale).astype(bf16)
        qks[c, h] = _mm_nt(qbs[c, h], k_ref[rows_of(c), ks_of(h)])
    m_in, m_out = {}, {}
    for h in range(H):
        m = jnp.broadcast_to(m_s[h], (1, LANES))
        for c in range(nc):
            m_in[c, h] = m
            m = jnp.maximum(fcols[c, h][L - 1:L, :] + m, rmax[c, h][L - 1:L, :])
            m_out[c, h] = m
        m_s[h] = m[:, :1]
    mts, egs, ss, ssums, wcs, dcs, dns = {}, {}, {}, {}, {}, {}, {}
    for c, h in pairs:
        fc_col = fcols[c, h]
        gg = fc_col + m_in[c, h]
        mt = jnp.maximum(gg, rmax[c, h])
        mts[c, h] = mt
        egs[c, h] = jnp.exp(gg - mt)
        s = (qks[c, h] * jnp.exp(dms[c, h] - mt[:, :L])).astype(bf16)
        ss[c, h] = s
        ssums[c, h] = jnp.dot(s, ones_b, preferred_element_type=f32)
        f_last = fc_col[L - 1:L, :]
        wcs[c, h] = jnp.exp(f_last + m_in[c, h] - m_out[c, h])
        w_j = jnp.exp(f_last - fc_col + icols[c, h] - m_out[c, h])
        kc = k_ref[rows_of(c), ks_of(h)]
        dcs[c, h] = _mm_tn(v_ref[rows_of(c), vs_of(h)] * wide(w_j), kc)
        dns[c, h] = jnp.sum(w_j * kc, axis=0, keepdims=True)
    c_in, n_in = {}, {}
    for h in range(H):
        cst = c_s[h]
        nst = n_s[h]
        for c in range(nc):
            c_in[c, h] = cst.astype(bf16)
            n_in[c, h] = nst
            cst = wcs[c, h] * cst + dcs[c, h]
            nst = wcs[c, h] * nst + dns[c, h]
        c_s[h] = cst
        n_s[h] = nst
    hhs, msq = {}, {}
    mean_b = jnp.full((DV, LANES), 1.0 / DV, bf16)
    for c, h in pairs:
        rows, vs = rows_of(c), vs_of(h)
        qn = _mm_nt(qbs[c, h], jnp.broadcast_to(n_in[c, h], (LANES, DK)))
        num = wide(egs[c, h]) * _mm_nt(qbs[c, h], c_in[c, h]) + _mm(ss[c, h], v_ref[rows, vs])
        den = egs[c, h] * qn + ssums[c, h]
        hh = num / wide(jnp.maximum(jnp.abs(den), jnp.exp(-mts[c, h])))
        hhs[c, h] = hh
        msq[c, h] = jnp.dot((hh * hh).astype(bf16), mean_b, preferred_element_type=f32)
    for c, h in pairs:
        rows, vs = rows_of(c), vs_of(h)
        yn = hhs[c, h] * wide(lax.rsqrt(msq[c, h] + NORM_EPS)) * gain_ref[:, vs]
        y_ref[rows, vs] = _sigmoid(o_ref[rows, vs]) * yn
    xo_ref[...] = x_ref[...] + jnp.dot(y_ref[...].astype(bf16), wout_ref[...], preferred_element_type=f32)

    @pl.when(j == pl.num_programs(1) - 1)
    def _():
        c_ref[0] = c_s[...]
        n_ref[0] = n_s[...]
        m_ref[0] = m_s[...]


def _mlstm_mix_fresh(pr, b_gates, gain, x, w_out, B, T, tq):
    H, DK, DV = MLSTM_HEADS, MLSTM_DK, MLSTM_DV
    L = MLSTM_CHUNK
    nj = T // tq
    rmap = lambda c: (lambda b, j: (b * nj + j, c))
    bmap4 = lambda b, j: (b, 0, 0, 0)
    bias = jnp.zeros((1, LANES), f32).at[0, :2 * H].set(b_gates)
    y, c, n, m = pl.pallas_call(
        functools.partial(_mlstm_block_kernel, L=L, nc=tq // L),
        grid=(B, nj),
        in_specs=[pl.BlockSpec((tq, H * DK), rmap(0)), pl.BlockSpec((tq, H * DK), rmap(1)),
                  pl.BlockSpec((tq, H * DV), rmap(1)), pl.BlockSpec((tq, H * DV), rmap(2)),
                  pl.BlockSpec((tq, LANES), rmap((2 * H * DK + 2 * H * DV) // LANES)),
                  pl.BlockSpec((1, LANES), lambda b, j: (0, 0)),
                  pl.BlockSpec((1, H * DV), lambda b, j: (0, 0)),
                  pl.BlockSpec((tq, D_MODEL), rmap(0)),
                  pl.BlockSpec((H * DV, D_MODEL), lambda b, j: (0, 0))],
        out_specs=[pl.BlockSpec((tq, D_MODEL), rmap(0)), pl.BlockSpec((1, H, DV, DK), bmap4),
                   pl.BlockSpec((1, H, 1, DK), bmap4), pl.BlockSpec((1, H, 1, 1), bmap4)],
        out_shape=[jax.ShapeDtypeStruct((B * T, D_MODEL), f32), jax.ShapeDtypeStruct((B, H, DV, DK), f32),
                   jax.ShapeDtypeStruct((B, H, 1, DK), f32), jax.ShapeDtypeStruct((B, H, 1, 1), f32)],
        scratch_shapes=[pltpu.VMEM((H, DV, DK), f32), pltpu.VMEM((H, 1, DK), f32), pltpu.VMEM((H, 1, 1), f32),
                        pltpu.VMEM((tq, H * DV), f32)],
        compiler_params=_cparams("parallel", "arbitrary"),
        name="mlstm_mix_fresh",
    )(pr, pr, pr, pr, pr, bias, gain.reshape(1, H * DV), x, w_out)
    return y, c, n.reshape(B, H, DK), m.reshape(B, H)


def _mlstm_seq_kernel(q_ref, k_ref, v_ref, o_ref, gt_ref, bias_ref, gain_ref, c0_ref, n0_ref, m0_ref,
                      y_ref, c_ref, n_ref, m_ref, *, T, bs):
    H, DK, DV = MLSTM_HEADS, MLSTM_DK, MLSTM_DV
    P = BF16_ROWS
    R = bs * T
    ri = lax.broadcasted_iota(jnp.int32, (R, R), 0)
    ci = lax.broadcasted_iota(jnp.int32, (R, R), 1)
    tril_b = (((ri // T) == (ci // T)) & (ci <= ri)).astype(bf16)
    rp = lax.broadcasted_iota(jnp.int32, (P, P), 0)
    cp = lax.broadcasted_iota(jnp.int32, (P, P), 1)
    causal = cp <= rp
    real_col = lax.broadcasted_iota(jnp.int32, (1, P), 1) < T
    scale = DK ** -0.5
    ones_b = jnp.ones((P, LANES), bf16)
    mean_b = jnp.full((DV, LANES), 1.0 / DV, bf16)
    wide = lambda x: jnp.concatenate([x] * (DV // LANES), axis=1)

    g = gt_ref[...] + bias_ref[...]
    fcum = _cumsum_rows(_log_sigmoid(g), tril_b)
    zrows = jnp.zeros((LANES - R, LANES), f32)
    g_t = jnp.concatenate([g, zrows], axis=0).T
    f_t = jnp.concatenate([fcum, zrows], axis=0).T

    def pad_rows(x, value=0.0):
        return jnp.concatenate([x, jnp.full((P - T, x.shape[1]), value, f32)], axis=0)

    pairs = [(c, h) for c in range(bs) for h in range(H)]
    ks_of = lambda h: slice(h * DK, (h + 1) * DK)
    vs_of = lambda h: slice(h * DV, (h + 1) * DV)
    st = {}
    for c, h in pairs:
        rows = slice(c * T, (c + 1) * T)
        last = slice((c + 1) * T - 1, (c + 1) * T)
        i_col = jnp.broadcast_to(pad_rows(g[rows, h:h + 1], -jnp.inf), (P, LANES))
        f_last = jnp.broadcast_to(fcum[last, H + h:H + h + 1], (1, LANES))
        fc_col = jnp.concatenate([jnp.broadcast_to(fcum[rows, H + h:H + h + 1], (T, LANES)),
                                  jnp.broadcast_to(f_last, (P - T, LANES))], axis=0)
        i_row = jnp.where(real_col, g_t[h:h + 1, c * T:c * T + P], -jnp.inf)
        fc_row = f_t[H + h:H + h + 1, c * T:c * T + P]
        dm = jnp.where(causal, fc_col[:, :P] - fc_row + i_row, -jnp.inf)
        rmax = jnp.broadcast_to(jnp.max(dm, axis=1, keepdims=True), (P, LANES))
        qb = pad_rows(q_ref[rows, ks_of(h)] * scale).astype(bf16)
        kc = pad_rows(k_ref[rows, ks_of(h)])
        vc = pad_rows(v_ref[rows, vs_of(h)])
        st[c, h] = dict(i_col=i_col, f_last=f_last, fc_col=fc_col, dm=dm, rmax=rmax, qb=qb, kc=kc, vc=vc,
                        qk=_mm_nt(qb, kc))
    for c, h in pairs:
        d = st[c, h]
        m = jnp.broadcast_to(m0_ref[c, h], (1, LANES))
        gg = d["fc_col"] + m
        mt = jnp.maximum(gg, d["rmax"])
        m_new = mt[T - 1:T, :]
        d["mt"] = mt
        d["eg"] = jnp.exp(gg - mt)
        s = (d["qk"] * jnp.exp(d["dm"] - mt[:, :P])).astype(bf16)
        d["s"] = s
        d["ssum"] = jnp.dot(s, ones_b, preferred_element_type=f32)
        w_c = jnp.exp(d["f_last"] + m - m_new)
        w_j = jnp.exp(d["f_last"] - d["fc_col"] + d["i_col"] - m_new)
        c_ref[c, h] = w_c * c0_ref[c, h] + _mm_tn(d["vc"] * wide(w_j), d["kc"])
        n_ref[c, h] = w_c * n0_ref[c, h] + jnp.sum(w_j * d["kc"], axis=0, keepdims=True)
        m_ref[c, h] = m_new[:, :1]
    for c, h in pairs:
        d = st[c, h]
        qn = _mm_nt(d["qb"], jnp.broadcast_to(n0_ref[c, h], (LANES, DK)))
        num = wide(d["eg"]) * _mm_nt(d["qb"], c0_ref[c, h]) + _mm(d["s"], d["vc"])
        den = d["eg"] * qn + d["ssum"]
        hh = num / wide(jnp.maximum(jnp.abs(den), jnp.exp(-d["mt"])))
        d["hh"] = hh
        d["msq"] = jnp.dot((hh * hh).astype(bf16), mean_b, preferred_element_type=f32)
    for c, h in pairs:
        d = st[c, h]
        rows, vs = slice(c * T, (c + 1) * T), vs_of(h)
        yn = (d["hh"] * wide(lax.rsqrt(d["msq"] + NORM_EPS)))[:T] * gain_ref[:, vs]
        y_ref[rows, vs] = _sigmoid(o_ref[rows, vs]) * yn


def _mlstm_mix_seq(pr, b_gates, gain, B, T, state, bs):
    H, DK, DV = MLSTM_HEADS, MLSTM_DK, MLSTM_DV
    R = bs * T
    rmap = lambda c: (lambda i: (i, c))
    bmap4 = lambda i: (i, 0, 0, 0)
    const2 = lambda i: (0, 0)
    bias = jnp.zeros((1, LANES), f32).at[0, :2 * H].set(b_gates)
    c0, n0, m0 = state
    y, c, n, m = pl.pallas_call(
        functools.partial(_mlstm_seq_kernel, T=T, bs=bs),
        grid=(B // bs,),
        in_specs=[pl.BlockSpec((R, H * DK), rmap(0)), pl.BlockSpec((R, H * DK), rmap(1)),
                  pl.BlockSpec((R, H * DV), rmap(1)), pl.BlockSpec((R, H * DV), rmap(2)),
                  pl.BlockSpec((R, LANES), rmap((2 * H * DK + 2 * H * DV) // LANES)),
                  pl.BlockSpec((1, LANES), const2), pl.BlockSpec((1, H * DV), const2),
                  pl.BlockSpec((bs, H, DV, DK), bmap4), pl.BlockSpec((bs, H, 1, DK), bmap4),
                  pl.BlockSpec((bs, H, 1, 1), bmap4)],
        out_specs=[pl.BlockSpec((R, H * DV), rmap(0)), pl.BlockSpec((bs, H, DV, DK), bmap4),
                   pl.BlockSpec((bs, H, 1, DK), bmap4), pl.BlockSpec((bs, H, 1, 1), bmap4)],
        out_shape=[jax.ShapeDtypeStruct((B * T, H * DV), f32), jax.ShapeDtypeStruct((B, H, DV, DK), f32),
                   jax.ShapeDtypeStruct((B, H, 1, DK), f32), jax.ShapeDtypeStruct((B, H, 1, 1), f32)],
        compiler_params=_cparams("parallel"),
        name="mlstm_mix_seq",
    )(pr, pr, pr, pr, pr, bias, gain.reshape(1, H * DV), c0, n0.reshape(B, H, 1, DK), m0.reshape(B, H, 1, 1))
    return y, c, n.reshape(B, H, DK), m.reshape(B, H)


def _gla_kernel(*refs, kind, layer, L, nchunks, H, DK, DV, has_init, mx):
    if kind == "hgrn":
        q_ref, f_ref, v_ref, g_ref, lbl_ref, gain_ref = refs[:6]
        rest = refs[6:]
    else:
        q_ref, k_ref, v_ref, g_ref, gr_ref, wup_ref, bgate_ref, gain_ref = refs[:8]
        rest = refs[8:]
    if has_init:
        s0_ref = rest[0]
        rest = rest[1:]
    y_ref, s_ref, st_s = rest
    j = pl.program_id(1)

    @pl.when(j == 0)
    def _():
        for h in range(H):
            if has_init:
                st_s[h] = s0_ref[0, h].T
            else:
                st_s[h] = jnp.zeros((DV, DK), f32)

    Lp = max(L, BF16_ROWS)
    rowi = lax.broadcasted_iota(jnp.int32, (Lp, Lp), 0)
    coli = lax.broadcasted_iota(jnp.int32, (Lp, Lp), 1)
    causal = coli <= rowi
    tril_b = causal.astype(bf16)

    def pad_rows(x):
        if Lp == L:
            return x
        return jnp.concatenate([x, jnp.zeros((Lp - L, x.shape[1]), f32)], axis=0)

    if kind == "hgrn":
        logits = lbl_ref[...]
        e = jnp.exp(logits - jnp.max(logits, axis=0, keepdims=True))
        sm = e / jnp.sum(e, axis=0, keepdims=True)
        lb_all = jnp.zeros((1, H * DK), f32)
        for li in range(layer):
            lb_all = lb_all + sm[li:li + 1, :]
    else:
        scale = DK ** -0.5

    def chunk(c, carry):
        rows = pl.ds(pl.multiple_of(c * L, L), L)
        if kind == "gla":
            gk = _mm(pad_rows(gr_ref[rows, :]), wup_ref[...], mx)[:L] + bgate_ref[...]
            ld_all = _log_sigmoid(gk) * (1.0 / GLA_GATE_NORMALIZER)
        for h in range(H):
            ks = slice(h * DK, (h + 1) * DK)
            vs = slice(h * DV, (h + 1) * DV)
            if kind == "hgrn":
                qraw = q_ref[rows, ks]
                fg = f_ref[rows, ks]
                lb = lb_all[:, ks]
                gc = jnp.log(lb + (1.0 - lb) * _sigmoid(fg))
                kc = (1.0 - lb) * _sigmoid(-fg)
                qc = qraw * _sigmoid(qraw)
            else:
                qc = q_ref[rows, ks] * scale
                kc = k_ref[rows, ks]
                gc = ld_all[:, ks]
            vc = pad_rows(v_ref[rows, vs])
            qc, kc, gc = pad_rows(qc), pad_rows(kc), pad_rows(gc)
            bc = _cumsum_rows(gc, tril_b)
            qe = qc * jnp.exp(bc)
            ke = kc * jnp.exp(-bc)
            a = jnp.where(causal, _mm_nt(qe, ke, mx), 0.0)
            st = st_s[h]
            o = (_mm_nt(qe, st, mx) + _mm(a, vc, mx))[:L]
            b_last = bc[L - 1:L, :]
            kdec = kc * jnp.exp(b_last - bc)
            st_s[h] = st * jnp.exp(b_last) + _mm_tn(vc, kdec, mx)
            yn = o * lax.rsqrt(jnp.mean(o * o, axis=1, keepdims=True) + NORM_EPS) * gain_ref[:, vs]
            gg = g_ref[rows, vs]
            y_ref[rows, vs] = yn * (gg * _sigmoid(gg))
        return carry

    lax.fori_loop(0, nchunks, chunk, 0)

    @pl.when(j == pl.num_programs(1) - 1)
    def _():
        for h in range(H):
            s_ref[0, h] = st_s[h].T


def _gla_mix(kind, pr, extra, gain, B, T, state, tq, layer=0):
    if kind == "hgrn":
        H, DK, DV = HGRN_HEADS, HGRN_DK, HGRN_DV
    else:
        H, DK, DV = GLA_HEADS, GLA_DK, GLA_DV
    L = _chunk_len(T, LIN_CHUNK)
    nj = T // tq
    has_init = state is not None
    mx = bf16
    rmap = lambda c: (lambda b, j: (b * nj + j, c))
    bmap4 = lambda b, j: (b, 0, 0, 0)
    const2 = lambda b, j: (0, 0)
    hk, hv = H * DK, H * DV
    if kind == "hgrn":
        (lb_logits,) = extra
        in_specs = [pl.BlockSpec((tq, hk), rmap(0)), pl.BlockSpec((tq, hk), rmap(1)),
                    pl.BlockSpec((tq, hv), rmap(2)), pl.BlockSpec((tq, hv), rmap(3)),
                    pl.BlockSpec((DEPTH, hk), const2), pl.BlockSpec((1, hv), const2)]
        args = [pr, pr, pr, pr, lb_logits, gain.reshape(1, hv)]
    else:
        w_up, b_gate = extra
        w_up_p = jnp.zeros((LANES, hk), f32).at[:GLA_GATE_RANK].set(w_up)
        in_specs = [pl.BlockSpec((tq, hk), rmap(0)), pl.BlockSpec((tq, hk), rmap(1)),
                    pl.BlockSpec((tq, hv), rmap(1)), pl.BlockSpec((tq, hv), rmap(2)),
                    pl.BlockSpec((tq, LANES), rmap((2 * hk + 2 * hv) // LANES)),
                    pl.BlockSpec((LANES, hk), const2), pl.BlockSpec((1, hk), const2),
                    pl.BlockSpec((1, hv), const2)]
        args = [pr, pr, pr, pr, pr, w_up_p, b_gate.reshape(1, hk), gain.reshape(1, hv)]
    if has_init:
        in_specs.append(pl.BlockSpec((1, H, DK, DV), bmap4))
        args.append(state)
    y, s = pl.pallas_call(
        functools.partial(_gla_kernel, kind=kind, layer=layer, L=L, nchunks=tq // L, H=H, DK=DK, DV=DV,
                          has_init=has_init, mx=mx),
        grid=(B, nj),
        in_specs=in_specs,
        out_specs=[pl.BlockSpec((tq, hv), rmap(0)), pl.BlockSpec((1, H, DK, DV), bmap4)],
        out_shape=[jax.ShapeDtypeStruct((B * T, hv), f32), jax.ShapeDtypeStruct((B, H, DK, DV), f32)],
        scratch_shapes=[pltpu.VMEM((H, DV, DK), f32)],
        compiler_params=_cparams("parallel", "arbitrary"),
        name=kind + "_mix",
    )(*args)
    return y, s


def _gla_seq_kernel(*refs, kind, layer, T, bs, H, DK, DV):
    if kind == "hgrn":
        q_ref, f_ref, v_ref, g_ref, lbl_ref, gain_ref = refs[:6]
        rest = refs[6:]
    else:
        q_ref, k_ref, v_ref, g_ref, gr_ref, wup_ref, bgate_ref, gain_ref = refs[:8]
        rest = refs[8:]
    s0_ref, y_ref, s_ref, qe_s, ke_s, kd_s, el_s, v_s = rest
    P = BF16_ROWS
    R = bs * T
    hk = H * DK

    ri = lax.broadcasted_iota(jnp.int32, (R, R), 0)
    ci = lax.broadcasted_iota(jnp.int32, (R, R), 1)
    same_seq = (ri // T) == (ci // T)
    tril_b = (same_seq & (ci <= ri)).astype(bf16)
    seq_b = same_seq.astype(bf16)
    rp = lax.broadcasted_iota(jnp.int32, (P, P), 0)
    cp = lax.broadcasted_iota(jnp.int32, (P, P), 1)
    causal = cp <= rp

    if kind == "hgrn":
        logits = lbl_ref[...]
        e = jnp.exp(logits - jnp.max(logits, axis=0, keepdims=True))
        sm = e / jnp.sum(e, axis=0, keepdims=True)
        lb = jnp.zeros((1, hk), f32)
        for li in range(layer):
            lb = lb + sm[li:li + 1, :]
        qraw = q_ref[...]
        fg = f_ref[...]
        gc = jnp.log(lb + (1.0 - lb) * _sigmoid(fg))
        kc = (1.0 - lb) * _sigmoid(-fg)
        qc = qraw * _sigmoid(qraw)
    else:
        gk = _mm(gr_ref[...], wup_ref[...]) + bgate_ref[...]
        gc = _log_sigmoid(gk) * (1.0 / GLA_GATE_NORMALIZER)
        qc = q_ref[...] * (DK ** -0.5)
        kc = k_ref[...]
    pieces = _split3(gc)
    bc = sum(jnp.dot(tril_b, x, preferred_element_type=f32) for x in pieces)
    bl = sum(jnp.dot(seq_b, x, preferred_element_type=f32) for x in pieces)
    qe = qc * jnp.exp(bc)
    ke = kc * jnp.exp(-bc)
    kd = kc * jnp.exp(bl - bc)
    e_hi, e_mid, e_lo = [x.astype(f32) for x in _split3(jnp.exp(bl))]
    vv = v_ref[...]
    zpad_k = jnp.zeros((P - T, hk), f32)
    zpad_v = jnp.zeros((P - T, vv.shape[1]), f32)
    prow = lax.broadcasted_iota(jnp.int32, (P, hk), 0)
    for c in range(bs):
        src = slice(c * T, (c + 1) * T)
        dst = slice(c * P, (c + 1) * P)
        qe_s[dst, :] = jnp.concatenate([qe[src], zpad_k], axis=0)
        ke_s[dst, :] = jnp.concatenate([ke[src], zpad_k], axis=0)
        kd_s[dst, :] = jnp.concatenate([kd[src], zpad_k], axis=0)
        v_s[dst, :] = jnp.concatenate([vv[src], zpad_v], axis=0)
        first = slice(c * T, c * T + 1)
        el_s[dst, :] = jnp.where(prow == 0, e_hi[first],
                                 jnp.where(prow == 1, e_mid[first], jnp.where(prow == 2, e_lo[first], 0.0)))

    ones_b = jnp.ones((P, LANES), bf16)
    pairs = [(c, h) for c in range(bs) for h in range(H)]
    rows_of = lambda c: slice(c * P, (c + 1) * P)
    ks_of = lambda h: slice(h * DK, (h + 1) * DK)
    vs_of = lambda h: slice(h * DV, (h + 1) * DV)
    qes, vbs, amats, news = {}, {}, {}, {}
    for c, h in pairs:
        rows, ks = rows_of(c), ks_of(h)
        qes[c, h] = qe_s[rows, ks].astype(bf16)
        vbs[c, h] = v_s[rows, vs_of(h)].astype(bf16)
        a = _mm_nt(qes[c, h], ke_s[rows, ks])
        amats[c, h] = jnp.where(causal, a, 0.0).astype(bf16)
        decay = _mm_tn(el_s[rows, ks], ones_b)
        decay = jnp.concatenate([decay] * (DV // LANES), axis=1)
        news[c, h] = s0_ref[c, h] * decay + _mm_tn(kd_s[rows, ks], vbs[c, h])
    for c, h in pairs:
        s_ref[c, h] = news[c, h]
    outs = {}
    for c, h in pairs:
        outs[c, h] = (_mm(qes[c, h], s0_ref[c, h]) + _mm(amats[c, h], vbs[c, h]))[:T]
    for c, h in pairs:
        vs = vs_of(h)
        rows = slice(c * T, (c + 1) * T)
        o = outs[c, h]
        yn = o * lax.rsqrt(jnp.mean(o * o, axis=1, keepdims=True) + NORM_EPS) * gain_ref[:, vs]
        gg = g_ref[rows, vs]
        y_ref[rows, vs] = yn * (gg * _sigmoid(gg))


def _gla_mix_seq(kind, pr, extra, gain, B, T, state, bs, layer=0):
    if kind == "hgrn":
        H, DK, DV = HGRN_HEADS, HGRN_DK, HGRN_DV
    else:
        H, DK, DV = GLA_HEADS, GLA_DK, GLA_DV
    R = bs * T
    rmap = lambda c: (lambda i: (i, c))
    bmap4 = lambda i: (i, 0, 0, 0)
    const2 = lambda i: (0, 0)
    hk, hv = H * DK, H * DV
    if kind == "hgrn":
        (lb_logits,) = extra
        in_specs = [pl.BlockSpec((R, hk), rmap(0)), pl.BlockSpec((R, hk), rmap(1)),
                    pl.BlockSpec((R, hv), rmap(2)), pl.BlockSpec((R, hv), rmap(3)),
                    pl.BlockSpec((DEPTH, hk), const2), pl.BlockSpec((1, hv), const2)]
        args = [pr, pr, pr, pr, lb_logits, gain.reshape(1, hv)]
    else:
        w_up, b_gate = extra
        w_up_p = jnp.zeros((LANES, hk), f32).at[:GLA_GATE_RANK].set(w_up)
        in_specs = [pl.BlockSpec((R, hk), rmap(0)), pl.BlockSpec((R, hk), rmap(1)),
                    pl.BlockSpec((R, hv), rmap(1)), pl.BlockSpec((R, hv), rmap(2)),
                    pl.BlockSpec((R, LANES), rmap((2 * hk + 2 * hv) // LANES)),
                    pl.BlockSpec((LANES, hk), const2), pl.BlockSpec((1, hk), const2),
                    pl.BlockSpec((1, hv), const2)]
        args = [pr, pr, pr, pr, pr, w_up_p, b_gate.reshape(1, hk), gain.reshape(1, hv)]
    in_specs.append(pl.BlockSpec((bs, H, DK, DV), bmap4))
    args.append(state)
    P = BF16_ROWS
    return pl.pallas_call(
        functools.partial(_gla_seq_kernel, kind=kind, layer=layer, T=T, bs=bs, H=H, DK=DK, DV=DV),
        grid=(B // bs,),
        in_specs=in_specs,
        out_specs=[pl.BlockSpec((R, hv), rmap(0)), pl.BlockSpec((bs, H, DK, DV), bmap4)],
        out_shape=[jax.ShapeDtypeStruct((B * T, hv), f32), jax.ShapeDtypeStruct((B, H, DK, DV), f32)],
        scratch_shapes=[pltpu.VMEM((bs * P, hk), f32)] * 4 + [pltpu.VMEM((bs * P, hv), f32)],
        compiler_params=_cparams("parallel"),
        name=kind + "_mix_seq",
    )(*args)


def _gla_block_kernel(*refs, kind, layer, L, nc, H, DK, DV):
    if kind == "hgrn":
        q_ref, f_ref, v_ref, g_ref, lbl_ref, gain_ref = refs[:6]
        rest = refs[6:]
    else:
        q_ref, k_ref, v_ref, g_ref, gr_ref, wup_ref, bgate_ref, gain_ref = refs[:8]
        rest = refs[8:]
    x_ref, wout_ref, xo_ref, s_ref, st_s, qe_s, ke_s, qs_s, kd_s, el_s, y_ref = rest
    j = pl.program_id(1)

    @pl.when(j == 0)
    def _():
        st_s[...] = jnp.zeros_like(st_s)

    rowi = lax.broadcasted_iota(jnp.int32, (L, L), 0)
    coli = lax.broadcasted_iota(jnp.int32, (L, L), 1)
    causal = coli <= rowi
    tril_b = causal.astype(bf16)

    if kind == "hgrn":
        logits = lbl_ref[...]
        e = jnp.exp(logits - jnp.max(logits, axis=0, keepdims=True))
        sm = e / jnp.sum(e, axis=0, keepdims=True)
        lb = jnp.zeros((1, H * DK), f32)
        for li in range(layer):
            lb = lb + sm[li:li + 1, :]
    else:
        scale = DK ** -0.5

    for c in range(nc):
        rows = slice(c * L, (c + 1) * L)
        if kind == "hgrn":
            qraw = q_ref[rows, :]
            fg = f_ref[rows, :]
            gc = jnp.log(lb + (1.0 - lb) * _sigmoid(fg))
            kc = (1.0 - lb) * _sigmoid(-fg)
            qc = qraw * _sigmoid(qraw)
        else:
            gk = jnp.dot(gr_ref[rows, :].astype(bf16), wup_ref[...].astype(bf16),
                         preferred_element_type=f32) + bgate_ref[...]
            gc = _log_sigmoid(gk) * (1.0 / GLA_GATE_NORMALIZER)
            qc = q_ref[rows, :] * scale
            kc = k_ref[rows, :]
        hi, mid, lo = _split3(gc)
        bc = (jnp.dot(tril_b, hi, preferred_element_type=f32) + jnp.dot(tril_b, mid, preferred_element_type=f32)
              + jnp.dot(tril_b, lo, preferred_element_type=f32))
        b_mid = bc[L // 2 - 1:L // 2, :]
        b_last = bc[L - 1:L, :]
        qe = qc * jnp.exp(bc - b_mid)
        ke = kc * jnp.exp(b_mid - bc)
        qe_s[rows, :] = qe.astype(bf16)
        ke_s[rows, :] = ke.astype(bf16)
        qs_s[rows, :] = (qe * jnp.exp(b_mid)).astype(bf16)
        kd_s[rows, :] = (ke * jnp.exp(b_last - b_mid)).astype(bf16)
        el_s[c] = jnp.exp(b_last)

    dn_nt = (((1,), (1,)), ((), ()))
    dn_tn = (((0,), (0,)), ((), ()))
    pairs = [(c, h) for c in range(nc) for h in range(H)]
    rows_of = lambda c: slice(c * L, (c + 1) * L)
    ks_of = lambda h: slice(h * DK, (h + 1) * DK)
    vs_of = lambda h: slice(h * DV, (h + 1) * DV)
    vbs, amats, dsts = {}, {}, {}
    for c, h in pairs:
        rows, ks = rows_of(c), ks_of(h)
        vb = v_ref[rows, vs_of(h)].astype(bf16)
        a = lax.dot_general(qe_s[rows, ks], ke_s[rows, ks], dn_nt, preferred_element_type=f32)
        vbs[c, h] = vb
        amats[c, h] = jnp.where(causal, a, 0.0).astype(bf16)
        dsts[c, h] = lax.dot_general(vb, kd_s[rows, ks], dn_tn, preferred_element_type=f32)
    sts = {}
    for h in range(H):
        st = st_s[h]
        for c in range(nc):
            sts[c, h] = st.astype(bf16)
            st = st * el_s[c][:, ks_of(h)] + dsts[c, h]
        st_s[h] = st
    outs = {}
    for c, h in pairs:
        rows, ks = rows_of(c), ks_of(h)
        outs[c, h] = (lax.dot_general(qs_s[rows, ks], sts[c, h], dn_nt, preferred_element_type=f32)
                      + jnp.dot(amats[c, h], vbs[c, h], preferred_element_type=f32))
    for c, h in pairs:
        rows, vs = rows_of(c), vs_of(h)
        o = outs[c, h]
        yn = o * lax.rsqrt(jnp.mean(o * o, axis=1, keepdims=True) + NORM_EPS) * gain_ref[:, vs]
        gg = g_ref[rows, vs]
        y_ref[rows, vs] = yn * (gg * _sigmoid(gg))
    xo_ref[...] = x_ref[...] + jnp.dot(y_ref[...].astype(bf16), wout_ref[...], preferred_element_type=f32)

    @pl.when(j == pl.num_programs(1) - 1)
    def _():
        for h in range(H):
            s_ref[0, h] = st_s[h].T


def _gla_mix_fresh(kind, pr, extra, gain, x, w_out, B, T, tq, layer=0):
    if kind == "hgrn":
        H, DK, DV = HGRN_HEADS, HGRN_DK, HGRN_DV
    else:
        H, DK, DV = GLA_HEADS, GLA_DK, GLA_DV
    L = 2 * LIN_CHUNK
    nc = tq // L
    nj = T // tq
    rmap = lambda c: (lambda b, j: (b * nj + j, c))
    bmap4 = lambda b, j: (b, 0, 0, 0)
    const2 = lambda b, j: (0, 0)
    hk, hv = H * DK, H * DV
    if kind == "hgrn":
        (lb_logits,) = extra
        in_specs = [pl.BlockSpec((tq, hk), rmap(0)), pl.BlockSpec((tq, hk), rmap(1)),
                    pl.BlockSpec((tq, hv), rmap(2)), pl.BlockSpec((tq, hv), rmap(3)),
                    pl.BlockSpec((DEPTH, hk), const2), pl.BlockSpec((1, hv), const2)]
        args = [pr, pr, pr, pr, lb_logits, gain.reshape(1, hv)]
    else:
        w_up, b_gate = extra
        w_up_p = jnp.zeros((LANES, hk), f32).at[:GLA_GATE_RANK].set(w_up)
        in_specs = [pl.BlockSpec((tq, hk), rmap(0)), pl.BlockSpec((tq, hk), rmap(1)),
                    pl.BlockSpec((tq, hv), rmap(1)), pl.BlockSpec((tq, hv), rmap(2)),
                    pl.BlockSpec((tq, LANES), rmap((2 * hk + 2 * hv) // LANES)),
                    pl.BlockSpec((LANES, hk), const2), pl.BlockSpec((1, hk), const2),
                    pl.BlockSpec((1, hv), const2)]
        args = [pr, pr, pr, pr, pr, w_up_p, b_gate.reshape(1, hk), gain.reshape(1, hv)]
    in_specs += [pl.BlockSpec((tq, D_MODEL), rmap(0)), pl.BlockSpec((hv, D_MODEL), const2)]
    args += [x, w_out]
    return pl.pallas_call(
        functools.partial(_gla_block_kernel, kind=kind, layer=layer, L=L, nc=nc, H=H, DK=DK, DV=DV),
        grid=(B, nj),
        in_specs=in_specs,
        out_specs=[pl.BlockSpec((tq, D_MODEL), rmap(0)), pl.BlockSpec((1, H, DK, DV), bmap4)],
        out_shape=[jax.ShapeDtypeStruct((B * T, D_MODEL), f32), jax.ShapeDtypeStruct((B, H, DK, DV), f32)],
        scratch_shapes=[pltpu.VMEM((H, DV, DK), f32)] + [pltpu.VMEM((tq, hk), bf16)] * 4
                       + [pltpu.VMEM((nc, 1, hk), f32), pltpu.VMEM((tq, hv), f32)],
        compiler_params=_cparams("parallel", "arbitrary"),
        name=kind + "_mix_fresh",
    )(*args)


def _rwkv_pre_kernel(x_ref, aux_ref, sh_ref, g_ref, mu_ref, wrkv_ref, la_ref, lbw_ref, lba_ref, lbg_ref,
                     w0_ref, a0_ref, r_ref, w_ref, k_ref, v_ref, alr_ref, gate_ref, *, long_seq, blocks_per_seq,
                     seq_len):
    g = g_ref[...]
    hn = _rms(x_ref[...], g)
    tm = hn.shape[0]
    rowi = lax.broadcasted_iota(jnp.int32, (tm, 1), 0)
    rolled = pltpu.roll(hn, 1, axis=0)
    if long_seq:
        p_last = _rms(aux_ref[SUBLANES - 1:SUBLANES, :], g)
        at_start = (pl.program_id(0) % blocks_per_seq) == 0
        first = jnp.where(at_start, sh_ref[0], p_last)
        prev = jnp.where(rowi == 0, first, rolled)
    else:
        prev = jnp.where(rowi % seq_len == 0, aux_ref[...], rolled)
    xx = prev - hn

    def lerp(c):
        return (hn + xx * mu_ref[c:c + 1, :]).astype(bf16)

    r_ref[...] = jnp.dot(lerp(0), wrkv_ref[0], preferred_element_type=f32)
    k_ref[...] = jnp.dot(lerp(1), wrkv_ref[1], preferred_element_type=f32)
    v_ref[...] = jnp.dot(lerp(2), wrkv_ref[2], preferred_element_type=f32)
    lw = jnp.tanh(jnp.dot(lerp(3), la_ref[:, 0:64], preferred_element_type=f32))
    wl = -_softplus(-(w0_ref[...] + jnp.dot(lw.astype(bf16), lbw_ref[...], preferred_element_type=f32))) - 0.5
    w_ref[...] = jnp.exp(-jnp.exp(wl))
    la = jnp.dot(lerp(4), la_ref[:, 64:128], preferred_element_type=f32)
    alr_ref[...] = _sigmoid(a0_ref[...] + jnp.dot(la.astype(bf16), lba_ref[...], preferred_element_type=f32))
    lg = _sigmoid(jnp.dot(lerp(5), la_ref[:, 128:256], preferred_element_type=f32))
    gate_ref[...] = jnp.dot(lg.astype(bf16), lbg_ref[...], preferred_element_type=f32)


def _rwkv_pre(x, shift0, g, mu, wrkv, la, lbw, lba, lbg, w0, a0, B, T, tm):
    n, d = x.shape
    long_seq = T % tm == 0
    row = lambda i: (i, 0)
    const2 = lambda i: (0, 0)
    if long_seq:
        bps = T // tm
        sub = tm // SUBLANES
        aux = x
        aux_spec = pl.BlockSpec((SUBLANES, d), lambda i: (jnp.maximum(i * sub - 1, 0), 0))
        sh = shift0.reshape(B, 1, d)
        sh_spec = pl.BlockSpec((1, 1, d), lambda i: (i // bps, 0, 0))
    else:
        assert tm % T == 0
        bps = 1
        aux = jnp.repeat(shift0, T, axis=0)
        aux_spec = pl.BlockSpec((tm, d), row)
        sh = shift0.reshape(B, 1, d)
        sh_spec = pl.BlockSpec((1, 1, d), lambda i: (0, 0, 0))
    out = jax.ShapeDtypeStruct((n, d), f32)
    return pl.pallas_call(
        functools.partial(_rwkv_pre_kernel, long_seq=long_seq, blocks_per_seq=bps, seq_len=T),
        grid=(n // tm,),
        in_specs=[pl.BlockSpec((tm, d), row), aux_spec, sh_spec, pl.BlockSpec((1, d), const2),
                  pl.BlockSpec((6, d), const2),
                  pl.BlockSpec((3, d, d), lambda i: (0, 0, 0), pipeline_mode=pl.Buffered(1)),
                  pl.BlockSpec((d, 256), const2), pl.BlockSpec((64, d), const2), pl.BlockSpec((64, d), const2),
                  pl.BlockSpec((128, d), const2), pl.BlockSpec((1, d), const2), pl.BlockSpec((1, d), const2)],
        out_specs=[pl.BlockSpec((tm, d), row)] * 6,
        out_shape=[out] * 6,
        compiler_params=_cparams("parallel"),
        name="rwkv_pre",
    )(x, aux, sh, g.reshape(1, d), mu, wrkv, la, lbw, lba, lbg, w0.reshape(1, d), a0.reshape(1, d))


def _rwkv_scan_kernel(*refs, tt, has_init, unroll):
    N = RWKV_N
    (r_ref, w_ref, k_ref, v_ref, alr_ref, gate_ref, x_ref, wout_ref,
     kk_ref, ka_ref, rk_ref, lnw_ref, lnb_ref) = refs[:13]
    if has_init:
        s0_ref = refs[13]
        rest = refs[14:]
    else:
        rest = refs[13:]
    xo_ref, sout_ref, s_s, y_s, vec_s, z_ref, zz_s = rest
    HP = RWKV_HEADS // 2
    j = pl.program_id(1)

    lane_heads = [2 * hp + par for par in range(2) for hp in range(HP)]

    @pl.when(j == 0)
    def _():
        if has_init:
            def load_value(vc, carry):
                rows = jnp.concatenate([s0_ref[:, h, vc, :] for h in lane_heads], axis=0)
                s_s[:, vc, :] = rows.T
                return carry

            lax.fori_loop(0, N, load_value, 0)
        else:
            s_s[...] = jnp.zeros_like(s_s)

    low = lax.broadcasted_iota(jnp.int32, (N, LANES), 1) < N

    def load_pair(ref, t0):
        tiles = []
        for t in (t0, t0 + 1):
            x = ref[:, t, :]
            tiles += [x[:, hp * LANES:(hp + 1) * LANES] for hp in range(HP)]
        xt = jnp.concatenate(tiles, axis=0).T
        ev, od = xt[:N], xt[N:]
        return (jnp.where(low, ev, pltpu.roll(od, N, axis=1)),
                jnp.where(low, pltpu.roll(ev, N, axis=1), od))

    def store_pair(ref, t0, z0, z1):
        ev = jnp.where(low, z0, pltpu.roll(z1, N, axis=1))
        od = jnp.where(low, pltpu.roll(z0, N, axis=1), z1)
        x = jnp.concatenate([ev, od], axis=0).T
        for i, t in enumerate((t0, t0 + 1)):
            tiles = [x[(i * HP + hp) * SUBLANES:(i * HP + hp + 1) * SUBLANES, :] for hp in range(HP)]
            ref[:, t, :] = jnp.concatenate(tiles, axis=1)

    VA, VW, VB, VK, VR, VV = range(6)

    def stage(tk, r, w, k, v, alr):
        kkraw = k * kk_ref[...]
        nrm = jnp.sqrt(jnp.sum(kkraw * kkraw, axis=0, keepdims=True))
        kk = kkraw / jnp.maximum(nrm, 1e-12)
        vec_s[tk, VA] = -kk
        vec_s[tk, VW] = w
        vec_s[tk, VB] = kk * alr
        vec_s[tk, VK] = k * (1.0 + (alr - 1.0) * ka_ref[...])
        vec_s[tk, VR] = r
        vec_s[tk, VV] = v

    def key_row(tk, which, kc):
        return vec_s[tk, which, pl.ds(kc, 1), :]

    def project(tk):
        def body(i, sa):
            for u in range(unroll):
                kc = i * unroll + u
                sa = sa + s_s[kc] * key_row(tk, VA, kc)
            return sa

        return lax.fori_loop(0, N // unroll, body, jnp.zeros((N, LANES), f32))

    def update(tk, sa):
        vv = vec_s[tk, VV]
        y = jnp.zeros((N, LANES), f32)
        for kc in range(N):
            sk = s_s[kc] * key_row(tk, VW, kc) + sa * key_row(tk, VB, kc) + vv * key_row(tk, VK, kc)
            s_s[kc] = sk
            y = y + sk * key_row(tk, VR, kc)
        y_s[tk] = y

    def epilogue(tk):
        y = y_s[tk]
        mean = jnp.mean(y, axis=0, keepdims=True)
        yc = y - mean
        var = jnp.mean(yc * yc, axis=0, keepdims=True)
        yn = yc * lax.rsqrt(var + RWKV_GN_EPS)
        bonus = jnp.sum(vec_s[tk, VR] * vec_s[tk, VK] * rk_ref[...], axis=0, keepdims=True) * vec_s[tk, VV]
        return yn * lnw_ref[...] + lnb_ref[...] + bonus

    def stage_pair(slot, t0):
        streams = [load_pair(ref, t0) for ref in (r_ref, w_ref, k_ref, v_ref, alr_ref)]
        for i in range(2):
            stage(slot + i, *[s[i] for s in streams])

    @pl.when(j == 0)
    def _():
        vec_s[...] = jnp.zeros_like(vec_s)
        y_s[...] = jnp.zeros_like(y_s)

    npairs = tt // 2
    stage_pair(0, 0)

    def step(p, carry):
        slot = 2 * (p % 2)
        prev = 2 - slot
        sa = project(slot)
        update(slot, sa)
        zz_s[0] = epilogue(prev)
        zz_s[1] = epilogue(prev + 1)
        stage_pair(prev, 2 * jnp.minimum(p + 1, npairs - 1))
        sa = project(slot + 1)
        update(slot + 1, sa)
        store_pair(z_ref, 2 * jnp.maximum(p - 1, 0), zz_s[0], zz_s[1])
        return carry

    lax.fori_loop(0, npairs, step, 0)
    last = 2 * ((npairs - 1) % 2)
    store_pair(z_ref, tt - 2, epilogue(last), epilogue(last + 1))

    rows = RWKV_SEQ_PER_STEP * tt
    zg = (z_ref[...] * gate_ref[...]).reshape(rows, D_MODEL).astype(bf16)
    proj = jnp.dot(zg, wout_ref[...], preferred_element_type=f32)
    xo_ref[...] = x_ref[...] + proj.reshape(RWKV_SEQ_PER_STEP, tt, D_MODEL)

    @pl.when(j == pl.num_programs(1) - 1)
    def _():
        def store_value(vc, carry):
            rows = s_s[:, vc, :].T
            for i, h in enumerate(lane_heads):
                sout_ref[:, h, vc, :] = rows[i * RWKV_SEQ_PER_STEP:(i + 1) * RWKV_SEQ_PER_STEP, :]
            return carry

        lax.fori_loop(0, N, store_value, 0)


RWKV_SEQ_PER_STEP = LANES // RWKV_HEADS


def _rwkv_scan(r, w, k, v, alr, gate, x, w_out, slabs, s0, B, T, tt):
    N = RWKV_N
    G = B // RWKV_SEQ_PER_STEP
    has_init = s0 is not None
    view = lambda a: a.reshape(B, T, D_MODEL)
    tmap = lambda g, j: (g, j, 0)
    smap = lambda g, j: (g, 0, 0, 0)
    const2 = lambda g, j: (0, 0)
    stream = pl.BlockSpec((RWKV_SEQ_PER_STEP, tt, D_MODEL), tmap)
    state_spec = pl.BlockSpec((RWKV_SEQ_PER_STEP, RWKV_HEADS, N, N), smap)
    in_specs = [stream] * 7 + [pl.BlockSpec((D_MODEL, D_MODEL), const2)] + [pl.BlockSpec((N, LANES), const2)] * 5
    args = [view(r), view(w), view(k), view(v), view(alr), view(gate), view(x), w_out] + list(slabs)
    if has_init:
        in_specs.append(state_spec)
        args.append(s0)
    z, s = pl.pallas_call(
        functools.partial(_rwkv_scan_kernel, tt=tt, has_init=has_init, unroll=16),
        grid=(G, T // tt),
        in_specs=in_specs,
        out_specs=[stream, state_spec],
        out_shape=[jax.ShapeDtypeStruct((B, T, D_MODEL), f32),
                   jax.ShapeDtypeStruct((B, RWKV_HEADS, N, N), f32)],
        scratch_shapes=[pltpu.VMEM((N, N, LANES), f32), pltpu.VMEM((4, N, LANES), f32),
                        pltpu.VMEM((4, 6, N, LANES), f32), pltpu.VMEM((RWKV_SEQ_PER_STEP, tt, D_MODEL), f32),
                        pltpu.VMEM((2, N, LANES), f32)],
        compiler_params=_cparams("parallel", "arbitrary"),
        name="rwkv_scan",
    )(*args)
    return z.reshape(B * T, D_MODEL), s


def _lane_slab(p):
    q = p.reshape(RWKV_HEADS // 2, 2, RWKV_N).transpose(2, 1, 0)
    q = jnp.broadcast_to(q[:, :, :, None], (RWKV_N, 2, RWKV_HEADS // 2, RWKV_SEQ_PER_STEP))
    return q.reshape(RWKV_N, LANES)


def _rwkv_mix(x, p, B, T, state, tm):
    if state is None:
        shift0 = jnp.zeros((B, D_MODEL), f32)
        s0 = None
    else:
        s0, shift0 = state
    r, w, k, v, alr, gate = _rwkv_pre(x, shift0, p["g"], p["mu"], p["wrkv"], p["la"], p["lbw"], p["lba"], p["lbg"],
                                      p["w0"], p["a0"], B, T, tm)
    slabs = [_lane_slab(p[n]) for n in ("k_k", "k_a", "r_k", "ln_w", "ln_b")]
    return _rwkv_scan(r, w, k, v, alr, gate, x, p["w_out"], slabs, s0, B, T, min(T, 32))


def _pad_cols(w, to):
    return jnp.pad(w, ((0, 0), (0, to - w.shape[1])))


def _trunk(x, B, T, states, p):
    n = x.shape[0]
    tm = min(512, n)
    tmm = min(1024, n)
    tq = min(256, T)
    fresh = states is None
    block_ok = fresh and T % tq == 0 and tq % (2 * LIN_CHUNK) == 0
    seq_ok = (not fresh) and T == SUBLANES and T <= LIN_CHUNK and B % SEQS_PER_STEP == 0
    if not fresh:
        m_c, m_n, m_m, h_s, g_s, r_s, r_sh = states
    new = {}
    for li in range(DEPTH):
        g_mix = p["norm_mix"][li]
        if li == 0:
            pr = _norm_proj(x, g_mix, p["mlstm_w_in"], tmm, 1664)
            if fresh and T % tq == 0 and tq % (2 * MLSTM_CHUNK) == 0:
                x, c, nn, m = _mlstm_mix_fresh(pr, p["mlstm_b_gates"], p["mlstm_norm"], x, p["mlstm_w_out"],
                                               B, T, tq)
            else:
                if seq_ok:
                    y, c, nn, m = _mlstm_mix_seq(pr, p["mlstm_b_gates"], p["mlstm_norm"], B, T,
                                                 (m_c[0], m_n[0], m_m[0]), SEQS_PER_STEP)
                else:
                    y, c, nn, m = _mlstm_mix(pr, p["mlstm_b_gates"], p["mlstm_norm"], B, T,
                                             None if fresh else (m_c[0], m_n[0], m_m[0]), tq)
                x = _out_proj(y, p["mlstm_w_out"], x, tm)
            new["C"], new["n"], new["m"] = c[None], nn[None], m[None]
        elif li == 1:
            pr = _norm_proj(x, g_mix, p["hgrn_w_in"], tmm, 2048)
            if block_ok:
                x, s = _gla_mix_fresh("hgrn", pr, (p["hgrn_lb_logits"],), p["hgrn_norm"], x, p["hgrn_w_out"],
                                      B, T, tq, layer=li)
            else:
                if seq_ok:
                    y, s = _gla_mix_seq("hgrn", pr, (p["hgrn_lb_logits"],), p["hgrn_norm"], B, T, h_s[0],
                                        SEQS_PER_STEP, layer=li)
                else:
                    y, s = _gla_mix("hgrn", pr, (p["hgrn_lb_logits"],), p["hgrn_norm"], B, T,
                                    None if fresh else h_s[0], tq, layer=li)
                x = _out_proj(y, p["hgrn_w_out"], x, tm)
            new["hS"] = s[None]
        elif li == 2:
            pr = _norm_proj(x, g_mix, p["gla_w_in"], tmm, 1664)
            if block_ok:
                x, s = _gla_mix_fresh("gla", pr, (p["gla_w_gate_up"], p["gla_b_gate"]), p["gla_norm"], x,
                                      p["gla_w_out"], B, T, tq)
            else:
                if seq_ok:
                    y, s = _gla_mix_seq("gla", pr, (p["gla_w_gate_up"], p["gla_b_gate"]), p["gla_norm"], B, T,
                                        g_s[0], SEQS_PER_STEP)
                else:
                    y, s = _gla_mix("gla", pr, (p["gla_w_gate_up"], p["gla_b_gate"]), p["gla_norm"], B, T,
                                    None if fresh else g_s[0], tq)
                x = _out_proj(y, p["gla_w_out"], x, tm)
            new["gS"] = s[None]
        else:
            rp = dict(p["rwkv"], g=g_mix, w_out=p["rwkv_w_out"])
            x_last = x.reshape(B, T, D_MODEL)[:, T - 1, :]
            new["sh"] = _rmsnorm(x_last, g_mix, B)[None]
            x, s = _rwkv_mix(x, rp, B, T, None if fresh else (r_s[0], r_sh[0]), tm)
            new["rS"] = s[None]
        x = _ffn(x, p["norm_ffn"][li], p["ffn_w_gate_up"][li], p["ffn_w_down"][li], tm, 1408,
                 final_gain=p["norm_final"] if li == DEPTH - 1 else None)
    return x.reshape(B, T, D_MODEL), (new["C"], new["n"], new["m"], new["hS"], new["gS"], new["rS"], new["sh"])


def kernel(x_prompt, x_sample, state_mlstm_C, state_mlstm_n, state_mlstm_m, state_hgrn_S, state_gla_S, state_rwkv_S, state_rwkv_shift, norm_mix, norm_ffn, norm_final, mlstm_w_in, mlstm_b_gates, mlstm_norm, mlstm_w_out, hgrn_w_in, hgrn_lb_logits, hgrn_norm, hgrn_w_out, gla_w_in, gla_w_gate_up, gla_b_gate, gla_norm, gla_w_out, rwkv_mu, rwkv_w_rkv, rwkv_w_lora_a, rwkv_w_lora_b, rwkv_w0, rwkv_a_lora_a, rwkv_a_lora_b, rwkv_a0, rwkv_g_lora_a, rwkv_g_lora_b, rwkv_k_k, rwkv_k_a, rwkv_r_k, rwkv_ln_w, rwkv_ln_b, rwkv_w_out, ffn_w_gate_up, ffn_w_down):
    cast = lambda w: w.astype(bf16)
    p = dict(
        norm_mix=norm_mix, norm_ffn=norm_ffn, norm_final=norm_final,
        mlstm_w_in=cast(_pad_cols(mlstm_w_in[0], 3328)), mlstm_b_gates=mlstm_b_gates[0], mlstm_norm=mlstm_norm[0],
        mlstm_w_out=cast(mlstm_w_out[0]),
        hgrn_w_in=cast(hgrn_w_in[0]), hgrn_lb_logits=hgrn_lb_logits, hgrn_norm=hgrn_norm[0],
        hgrn_w_out=cast(hgrn_w_out[0]),
        gla_w_in=cast(_pad_cols(gla_w_in[0], 3328)), gla_w_gate_up=gla_w_gate_up[0], gla_b_gate=gla_b_gate[0],
        gla_norm=gla_norm[0], gla_w_out=cast(gla_w_out[0]),
        rwkv=dict(mu=rwkv_mu[0], wrkv=cast(rwkv_w_rkv[0]),
                  la=cast(jnp.concatenate([rwkv_w_lora_a[0], rwkv_a_lora_a[0], rwkv_g_lora_a[0]], axis=1)),
                  lbw=cast(rwkv_w_lora_b[0]), lba=cast(rwkv_a_lora_b[0]), lbg=cast(rwkv_g_lora_b[0]),
                  w0=rwkv_w0[0], a0=rwkv_a0[0], k_k=rwkv_k_k[0], k_a=rwkv_k_a[0], r_k=rwkv_r_k[0].reshape(-1),
                  ln_w=rwkv_ln_w[0], ln_b=rwkv_ln_b[0]),
        rwkv_w_out=cast(rwkv_w_out[0]),
        ffn_w_gate_up=cast(ffn_w_gate_up), ffn_w_down=cast(ffn_w_down),
    )
    bp, tp, _ = x_prompt.shape
    bs, ts, _ = x_sample.shape
    y_p, st_p = _trunk(x_prompt.reshape(bp * tp, D_MODEL), bp, tp, None, p)
    y_s, st_s = _trunk(x_sample.reshape(bs * ts, D_MODEL), bs, ts,
                       (state_mlstm_C, state_mlstm_n, state_mlstm_m, state_hgrn_S, state_gla_S, state_rwkv_S,
                        state_rwkv_shift), p)
    return (y_p, y_s) + st_p + st_s
```

```python
import functools

import jax
import jax.numpy as jnp
from jax import lax
from jax.experimental import pallas as pl
from jax.experimental.pallas import tpu as pltpu

f32 = jnp.float32
bf16 = jnp.bfloat16

D_MODEL = 1024
DEPTH = 4
NORM_EPS = 1e-6

MLSTM_HEADS, MLSTM_DK, MLSTM_DV, MLSTM_CHUNK = 4, 128, 256, 64
HGRN_HEADS, HGRN_DK, HGRN_DV = 8, 128, 128
GLA_HEADS, GLA_DK, GLA_DV = 4, 128, 256
GLA_GATE_RANK = 16
GLA_GATE_NORMALIZER = 16.0
LIN_CHUNK = 32
RWKV_HEADS, RWKV_N = 16, 64
RWKV_GN_EPS = 64e-5
FFN_HIDDEN = 2816

LANES = 128
SUBLANES = 8
BF16_ROWS = 16
SEQS_PER_STEP = 4
VMEM_LIMIT_BYTES = 52 * 1024 * 1024


def _cparams(*sem):
    return pltpu.CompilerParams(dimension_semantics=sem, vmem_limit_bytes=VMEM_LIMIT_BYTES)


def _chunk_len(t, cap):
    return max(d for d in range(1, min(cap, t) + 1) if t % d == 0)


def _rms(x, g):
    ms = jnp.mean(x * x, axis=-1, keepdims=True)
    return x * lax.rsqrt(ms + NORM_EPS) * g


def _sigmoid(x):
    return jax.nn.sigmoid(x)


def _softplus(x):
    return jnp.maximum(x, 0.0) + jnp.log1p(jnp.exp(-jnp.abs(x)))


def _log_sigmoid(x):
    return -_softplus(-x)


def _mm(a, b, mx=bf16):
    return jnp.dot(a.astype(mx), b.astype(mx), preferred_element_type=f32)


def _mm_nt(a, b, mx=bf16):
    return lax.dot_general(a.astype(mx), b.astype(mx), (((1,), (1,)), ((), ())), preferred_element_type=f32)


def _mm_tn(a, b, mx=bf16):
    return lax.dot_general(a.astype(mx), b.astype(mx), (((0,), (0,)), ((), ())), preferred_element_type=f32)


def _split3(x):
    hi = x.astype(bf16)
    r1 = x - hi.astype(f32)
    mid = r1.astype(bf16)
    lo = (r1 - mid.astype(f32)).astype(bf16)
    return hi, mid, lo


def _cumsum_rows(x, tril_b):
    hi, mid, lo = _split3(x)
    return (jnp.dot(tril_b, hi, preferred_element_type=f32)
            + jnp.dot(tril_b, mid, preferred_element_type=f32)
            + jnp.dot(tril_b, lo, preferred_element_type=f32))


def _norm_proj_kernel(x_ref, g_ref, w_ref, o_ref, hn_ref):
    @pl.when(pl.program_id(1) == 0)
    def _():
        hn_ref[...] = _rms(x_ref[...], g_ref[...]).astype(bf16)

    o_ref[...] = jnp.dot(hn_ref[...], w_ref[...], preferred_element_type=f32)


def _norm_proj(x, g, w, tm, tn):
    n, d = x.shape
    e = w.shape[1]
    return pl.pallas_call(
        _norm_proj_kernel,
        grid=(n // tm, e // tn),
        in_specs=[pl.BlockSpec((tm, d), lambda i, j: (i, 0)),
                  pl.BlockSpec((1, d), lambda i, j: (0, 0)),
                  pl.BlockSpec((d, tn), lambda i, j: (0, j))],
        out_specs=pl.BlockSpec((tm, tn), lambda i, j: (i, j)),
        out_shape=jax.ShapeDtypeStruct((n, e), f32),
        scratch_shapes=[pltpu.VMEM((tm, d), bf16)],
        compiler_params=_cparams("parallel", "arbitrary"),
        name="norm_proj",
    )(x, g.reshape(1, d), w)


def _out_proj_kernel(*refs, gated):
    if gated:
        y_ref, gate_ref, w_ref, res_ref, o_ref = refs
        y = y_ref[...] * gate_ref[...]
    else:
        y_ref, w_ref, res_ref, o_ref = refs
        y = y_ref[...]
    o_ref[...] = res_ref[...] + jnp.dot(y.astype(bf16), w_ref[...], preferred_element_type=f32)


def _out_proj(y, w, res, tm, gate=None):
    n, e = y.shape
    d = w.shape[1]
    row = lambda i: (i, 0)
    args = [y] + ([gate] if gate is not None else []) + [w, res]
    in_specs = ([pl.BlockSpec((tm, e), row)] + ([pl.BlockSpec((tm, e), row)] if gate is not None else [])
                + [pl.BlockSpec((e, d), lambda i: (0, 0)), pl.BlockSpec((tm, d), row)])
    return pl.pallas_call(
        functools.partial(_out_proj_kernel, gated=gate is not None),
        grid=(n // tm,),
        in_specs=in_specs,
        out_specs=pl.BlockSpec((tm, d), row),
        out_shape=jax.ShapeDtypeStruct((n, d), f32),
        compiler_params=_cparams("parallel"),
        name="out_proj",
    )(*args)


def _ffn_kernel(*refs, final_norm):
    if final_norm:
        x_ref, g_ref, wg_ref, wu_ref, wd_ref, gf_ref, o_ref, hn_ref, acc_ref = refs
    else:
        x_ref, g_ref, wg_ref, wu_ref, wd_ref, o_ref, hn_ref, acc_ref = refs
    j = pl.program_id(1)

    @pl.when(j == 0)
    def _():
        hn_ref[...] = _rms(x_ref[...], g_ref[...]).astype(bf16)
        acc_ref[...] = jnp.zeros_like(acc_ref)

    h = hn_ref[...]
    gt = jnp.dot(h, wg_ref[...], preferred_element_type=f32)
    up = jnp.dot(h, wu_ref[...], preferred_element_type=f32)
    act = (gt * _sigmoid(gt) * up).astype(bf16)
    acc_ref[...] += jnp.dot(act, wd_ref[...], preferred_element_type=f32)

    @pl.when(j == pl.num_programs(1) - 1)
    def _():
        out = x_ref[...] + acc_ref[...]
        o_ref[...] = _rms(out, gf_ref[...]) if final_norm else out


def _ffn(x, g, w_gu, w_down, tm, tf, final_gain=None):
    n, d = x.shape
    nf = FFN_HIDDEN // tf
    final_norm = final_gain is not None
    in_specs = [pl.BlockSpec((tm, d), lambda i, j: (i, 0)),
                pl.BlockSpec((1, d), lambda i, j: (0, 0)),
                pl.BlockSpec((d, tf), lambda i, j: (0, j)),
                pl.BlockSpec((d, tf), lambda i, j: (0, j + nf)),
                pl.BlockSpec((tf, d), lambda i, j: (j, 0))]
    args = [x, g.reshape(1, d), w_gu, w_gu, w_down]
    if final_norm:
        in_specs.append(pl.BlockSpec((1, d), lambda i, j: (0, 0)))
        args.append(final_gain.reshape(1, d))
    return pl.pallas_call(
        functools.partial(_ffn_kernel, final_norm=final_norm),
        grid=(n // tm, nf),
        in_specs=in_specs,
        out_specs=pl.BlockSpec((tm, d), lambda i, j: (i, 0)),
        out_shape=jax.ShapeDtypeStruct((n, d), f32),
        scratch_shapes=[pltpu.VMEM((tm, d), bf16), pltpu.VMEM((tm, d), f32)],
        compiler_params=_cparams("parallel", "arbitrary"),
        name="ffn",
    )(*args)


def _rmsnorm_kernel(x_ref, g_ref, o_ref):
    o_ref[...] = _rms(x_ref[...], g_ref[...])


def _rmsnorm(x, g, tm):
    n, d = x.shape
    return pl.pallas_call(
        _rmsnorm_kernel,
        grid=(n // tm,),
        in_specs=[pl.BlockSpec((tm, d), lambda i: (i, 0)), pl.BlockSpec((1, d), lambda i: (0, 0))],
        out_specs=pl.BlockSpec((tm, d), lambda i: (i, 0)),
        out_shape=jax.ShapeDtypeStruct((n, d), f32),
        compiler_params=_cparams("parallel"),
        name="rmsnorm",
    )(x, g.reshape(1, d))


def _mlstm_kernel(*refs, L, nchunks, has_init, mx):
    H, DK, DV = MLSTM_HEADS, MLSTM_DK, MLSTM_DV
    q_ref, k_ref, v_ref, o_ref, gt_ref, bias_ref, gain_ref = refs[:7]
    if has_init:
        c0_ref, n0_ref, m0_ref = refs[7:10]
        rest = refs[10:]
    else:
        rest = refs[7:]
    y_ref, c_ref, n_ref, m_ref, c_s, n_s, m_s = rest
    j = pl.program_id(1)

    @pl.when(j == 0)
    def _():
        if has_init:
            c_s[...] = c0_ref[0]
            n_s[...] = n0_ref[0]
            m_s[...] = m0_ref[0]
        else:
            c_s[...] = jnp.zeros_like(c_s)
            n_s[...] = jnp.zeros_like(n_s)
            m_s[...] = jnp.zeros_like(m_s)

    Lp = max(L, BF16_ROWS)
    rowi = lax.broadcasted_iota(jnp.int32, (Lp, Lp), 0)
    coli = lax.broadcasted_iota(jnp.int32, (Lp, Lp), 1)
    causal = coli <= rowi
    eye = coli == rowi
    scale = DK ** -0.5

    def to_row(col):
        return jnp.sum(jnp.where(eye, col, 0.0), axis=0, keepdims=True)

    def pad_rows(x, value=0.0):
        if Lp == L:
            return x
        return jnp.concatenate([x, jnp.full((Lp - L, x.shape[1]), value, f32)], axis=0)

    def chunk(c, carry):
        rows = pl.ds(pl.multiple_of(c * L, L), L)
        gts_raw = gt_ref[rows, :] + bias_ref[...]
        lsg = pad_rows(_log_sigmoid(gts_raw))
        gts = pad_rows(gts_raw, -jnp.inf)
        for h in range(H):
            i_col = gts[:, h:h + 1]
            f_col = lsg[:, H + h:H + h + 1]
            f_row = to_row(f_col)
            fc_col = jnp.sum(jnp.where(causal, f_row, 0.0), axis=1, keepdims=True)
            fc_row = to_row(fc_col)
            i_row = to_row(i_col)
            m = m_s[h]
            dm = jnp.where(causal, fc_col - fc_row + i_row, -jnp.inf)
            g = fc_col + m
            mt = jnp.maximum(g, jnp.max(dm, axis=1, keepdims=True))
            p = jnp.exp(dm - mt)
            qc = pad_rows(q_ref[rows, h * DK:(h + 1) * DK] * scale)
            kc = pad_rows(k_ref[rows, h * DK:(h + 1) * DK])
            vc = pad_rows(v_ref[rows, h * DV:(h + 1) * DV])
            s = _mm_nt(qc, kc, mx) * p
            eg = jnp.exp(g - mt)
            cst = c_s[h]
            nst = n_s[h]
            num = eg * _mm_nt(qc, cst, mx) + _mm(s, vc, mx)
            den = eg * jnp.sum(qc * nst, axis=1, keepdims=True) + jnp.sum(s, axis=1, keepdims=True)
            hh = num / jnp.maximum(jnp.abs(den), jnp.exp(-mt))
            m_new = mt[L - 1:L, :]
            f_last = fc_col[L - 1:L, :]
            w_c = jnp.exp(f_last + m - m_new)
            w_j = jnp.exp(f_last - fc_col + i_col - m_new)
            c_s[h] = w_c * cst + _mm_tn(vc * w_j, kc, mx)
            n_s[h] = w_c * nst + jnp.sum(w_j * kc, axis=0, keepdims=True)
            m_s[h] = m_new
            hh = hh[:L]
            yn = hh * lax.rsqrt(jnp.mean(hh * hh, axis=1, keepdims=True) + NORM_EPS)
            yn = yn * gain_ref[:, h * DV:(h + 1) * DV]
            y_ref[rows, h * DV:(h + 1) * DV] = _sigmoid(o_ref[rows, h * DV:(h + 1) * DV]) * yn
        return carry

    lax.fori_loop(0, nchunks, chunk, 0)

    @pl.when(j == pl.num_programs(1) - 1)
    def _():
        c_ref[0] = c_s[...]
        n_ref[0] = n_s[...]
        m_ref[0] = m_s[...]


def _mlstm_mix(pr, b_gates, gain, B, T, state, tq):
    H, DK, DV = MLSTM_HEADS, MLSTM_DK, MLSTM_DV
    L = _chunk_len(T, MLSTM_CHUNK)
    nj = T // tq
    has_init = state is not None
    mx = bf16
    rmap = lambda c: (lambda b, j: (b * nj + j, c))
    bmap4 = lambda b, j: (b, 0, 0, 0)
    bias = jnp.zeros((1, LANES), f32).at[0, :2 * H].set(b_gates)
    in_specs = [pl.BlockSpec((tq, H * DK), rmap(0)), pl.BlockSpec((tq, H * DK), rmap(1)),
                pl.BlockSpec((tq, H * DV), rmap(1)), pl.BlockSpec((tq, H * DV), rmap(2)),
                pl.BlockSpec((tq, LANES), rmap((2 * H * DK + 2 * H * DV) // LANES)),
                pl.BlockSpec((1, LANES), lambda b, j: (0, 0)),
                pl.BlockSpec((1, H * DV), lambda b, j: (0, 0))]
    args = [pr, pr, pr, pr, pr, bias, gain.reshape(1, H * DV)]
    if has_init:
        c0, n0, m0 = state
        in_specs += [pl.BlockSpec((1, H, DV, DK), bmap4), pl.BlockSpec((1, H, 1, DK), bmap4),
                     pl.BlockSpec((1, H, 1, 1), bmap4)]
        args += [c0, n0.reshape(B, H, 1, DK), m0.reshape(B, H, 1, 1)]
    y, c, n, m = pl.pallas_call(
        functools.partial(_mlstm_kernel, L=L, nchunks=tq // L, has_init=has_init, mx=mx),
        grid=(B, nj),
        in_specs=in_specs,
        out_specs=[pl.BlockSpec((tq, H * DV), rmap(0)), pl.BlockSpec((1, H, DV, DK), bmap4),
                   pl.BlockSpec((1, H, 1, DK), bmap4), pl.BlockSpec((1, H, 1, 1), bmap4)],
        out_shape=[jax.ShapeDtypeStruct((B * T, H * DV), f32), jax.ShapeDtypeStruct((B, H, DV, DK), f32),
                   jax.ShapeDtypeStruct((B, H, 1, DK), f32), jax.ShapeDtypeStruct((B, H, 1, 1), f32)],
        scratch_shapes=[pltpu.VMEM((H, DV, DK), f32), pltpu.VMEM((H, 1, DK), f32), pltpu.VMEM((H, 1, 1), f32)],
        compiler_params=_cparams("parallel", "arbitrary"),
        name="mlstm_mix",
    )(*args)
    return y, c, n.reshape(B, H, DK), m.reshape(B, H)


def _mlstm_block_kernel(q_ref, k_ref, v_ref, o_ref, gt_ref, bias_ref, gain_ref, x_ref, wout_ref,
                        xo_ref, c_ref, n_ref, m_ref, c_s, n_s, m_s, y_ref, *, L, nc):
    H, DK, DV = MLSTM_HEADS, MLSTM_DK, MLSTM_DV
    P = 2 * L
    j = pl.program_id(1)

    @pl.when(j == 0)
    def _():
        c_s[...] = jnp.zeros_like(c_s)
        n_s[...] = jnp.zeros_like(n_s)
        m_s[...] = jnp.zeros_like(m_s)

    rowi = lax.broadcasted_iota(jnp.int32, (L, L), 0)
    coli = lax.broadcasted_iota(jnp.int32, (L, L), 1)
    causal = coli <= rowi
    r2 = lax.broadcasted_iota(jnp.int32, (P, P), 0)
    c2 = lax.broadcasted_iota(jnp.int32, (P, P), 1)
    pair_tril = ((c2 <= r2) & ((c2 >= L) == (r2 >= L))).astype(bf16)
    scale = DK ** -0.5

    gates = []
    for pp in range(nc // 2):
        g = gt_ref[pp * P:(pp + 1) * P, :] + bias_ref[...]
        hi, mid, lo = _split3(_log_sigmoid(g))
        fcum = (jnp.dot(pair_tril, hi, preferred_element_type=f32)
                + jnp.dot(pair_tril, mid, preferred_element_type=f32)
                + jnp.dot(pair_tril, lo, preferred_element_type=f32))
        g_t = g.T
        f_t = fcum.T
        for half in range(2):
            rs = slice(half * L, (half + 1) * L)
            gates.append((g[rs, :], fcum[rs, :], g_t[:, rs], f_t[:, rs]))

    pairs = [(c, h) for c in range(nc) for h in range(H)]
    rows_of = lambda c: slice(c * L, (c + 1) * L)
    ks_of = lambda h: slice(h * DK, (h + 1) * DK)
    vs_of = lambda h: slice(h * DV, (h + 1) * DV)
    ones_b = jnp.ones((L, LANES), bf16)
    wide = lambda x: jnp.concatenate([x] * (DV // LANES), axis=1)
    dms, rmax, fcols, icols, qks, qbs = {}, {}, {}, {}, {}, {}
    for c, h in pairs:
        g_c, f_c, g_r, f_r = gates[c]
        icols[c, h] = jnp.broadcast_to(g_c[:, h:h + 1], (L, LANES))
        fcols[c, h] = jnp.broadcast_to(f_c[:, H + h:H + h + 1], (L, LANES))
        dm = jnp.where(causal, fcols[c, h][:, :L] - f_r[H + h:H + h + 1, :] + g_r[h:h + 1, :], -jnp.inf)
        dms[c, h] = dm
        rmax[c, h] = jnp.broadcast_to(jnp.max(dm, axis=1, keepdims=True), (L, LANES))
        qbs[c, h] = (q_ref[rows_of(c), ks_of(h)] * scale).astype(bf16)
        qks[c, h] = _mm_nt(qbs[c, h], k_ref[rows_of(c), ks_of(h)])
    m_in, m_out = {}, {}
    for h in range(H):
        m = jnp.broadcast_to(m_s[h], (1, LANES))
        for c in range(nc):
            m_in[c, h] = m
            m = jnp.maximum(fcols[c, h][L - 1:L, :] + m, rmax[c, h][L - 1:L, :])
            m_out[c, h] = m
        m_s[h] = m[:, :1]
    mts, egs, ss, ssums, wcs, dcs, dns = {}, {}, {}, {}, {}, {}, {}
    for c, h in pairs:
        fc_col = fcols[c, h]
        gg = fc_col + m_in[c, h]
        mt = jnp.maximum(gg, rmax[c, h])
        mts[c, h] = mt
        egs[c, h] = jnp.exp(gg - mt)
        s = (qks[c, h] * jnp.exp(dms[c, h] - mt[:, :L])).astype(bf16)
        ss[c, h] = s
        ssums[c, h] = jnp.dot(s, ones_b, preferred_element_type=f32)
        f_last = fc_col[L - 1:L, :]
        wcs[c, h] = jnp.exp(f_last + m_in[c, h] - m_out[c, h])
        w_j = jnp.exp(f_last - fc_col + icols[c, h] - m_out[c, h])
        kc = k_ref[rows_of(c), ks_of(h)]
        dcs[c, h] = _mm_tn(v_ref[rows_of(c), vs_of(h)] * wide(w_j), kc)
        dns[c, h] = jnp.sum(w_j * kc, axis=0, keepdims=True)
    c_in, n_in = {}, {}
    for h in range(H):
        cst = c_s[h]
        nst = n_s[h]
        for c in range(nc):
            c_in[c, h] = cst.astype(bf16)
            n_in[c, h] = nst
            cst = wcs[c, h] * cst + dcs[c, h]
            nst = wcs[c, h] * nst + dns[c, h]
        c_s[h] = cst
        n_s[h] = nst
    hhs, msq = {}, {}
    mean_b = jnp.full((DV, LANES), 1.0 / DV, bf16)
    for c, h in pairs:
        rows, vs = rows_of(c), vs_of(h)
        qn = _mm_nt(qbs[c, h], jnp.broadcast_to(n_in[c, h], (LANES, DK)))
        num = wide(egs[c, h]) * _mm_nt(qbs[c, h], c_in[c, h]) + _mm(ss[c, h], v_ref[rows, vs])
        den = egs[c, h] * qn + ssums[c, h]
        hh = num / wide(jnp.maximum(jnp.abs(den), jnp.exp(-mts[c, h])))
        hhs[c, h] = hh
        msq[c, h] = jnp.dot((hh * hh).astype(bf16), mean_b, preferred_element_type=f32)
    for c, h in pairs:
        rows, vs = rows_of(c), vs_of(h)
        yn = hhs[c, h] * wide(lax.rsqrt(msq[c, h] + NORM_EPS)) * gain_ref[:, vs]
        y_ref[rows, vs] = _sigmoid(o_ref[rows, vs]) * yn
    xo_ref[...] = x_ref[...] + jnp.dot(y_ref[...].astype(bf16), wout_ref[...], preferred_element_type=f32)

    @pl.when(j == pl.num_programs(1) - 1)
    def _():
        c_ref[0] = c_s[...]
        n_ref[0] = n_s[...]
        m_ref[0] = m_s[...]


def _mlstm_mix_fresh(pr, b_gates, gain, x, w_out, B, T, tq):
    H, DK, DV = MLSTM_HEADS, MLSTM_DK, MLSTM_DV
    L = MLSTM_CHUNK
    nj = T // tq
    rmap = lambda c: (lambda b, j: (b * nj + j, c))
    bmap4 = lambda b, j: (b, 0, 0, 0)
    bias = jnp.zeros((1, LANES), f32).at[0, :2 * H].set(b_gates)
    y, c, n, m = pl.pallas_call(
        functools.partial(_mlstm_block_kernel, L=L, nc=tq // L),
        grid=(B, nj),
        in_specs=[pl.BlockSpec((tq, H * DK), rmap(0)), pl.BlockSpec((tq, H * DK), rmap(1)),
                  pl.BlockSpec((tq, H * DV), rmap(1)), pl.BlockSpec((tq, H * DV), rmap(2)),
                  pl.BlockSpec((tq, LANES), rmap((2 * H * DK + 2 * H * DV) // LANES)),
                  pl.BlockSpec((1, LANES), lambda b, j: (0, 0)),
                  pl.BlockSpec((1, H * DV), lambda b, j: (0, 0)),
                  pl.BlockSpec((tq, D_MODEL), rmap(0)),
                  pl.BlockSpec((H * DV, D_MODEL), lambda b, j: (0, 0))],
        out_specs=[pl.BlockSpec((tq, D_MODEL), rmap(0)), pl.BlockSpec((1, H, DV, DK), bmap4),
                   pl.BlockSpec((1, H, 1, DK), bmap4), pl.BlockSpec((1, H, 1, 1), bmap4)],
        out_shape=[jax.ShapeDtypeStruct((B * T, D_MODEL), f32), jax.ShapeDtypeStruct((B, H, DV, DK), f32),
                   jax.ShapeDtypeStruct((B, H, 1, DK), f32), jax.ShapeDtypeStruct((B, H, 1, 1), f32)],
        scratch_shapes=[pltpu.VMEM((H, DV, DK), f32), pltpu.VMEM((H, 1, DK), f32), pltpu.VMEM((H, 1, 1), f32),
                        pltpu.VMEM((tq, H * DV), f32)],
        compiler_params=_cparams("parallel", "arbitrary"),
        name="mlstm_mix_fresh",
    )(pr, pr, pr, pr, pr, bias, gain.reshape(1, H * DV), x, w_out)
    return y, c, n.reshape(B, H, DK), m.reshape(B, H)


def _mlstm_seq_kernel(q_ref, k_ref, v_ref, o_ref, gt_ref, bias_ref, gain_ref, c0_ref, n0_ref, m0_ref,
                      y_ref, c_ref, n_ref, m_ref, *, T, bs):
    H, DK, DV = MLSTM_HEADS, MLSTM_DK, MLSTM_DV
    P = BF16_ROWS
    R = bs * T
    ri = lax.broadcasted_iota(jnp.int32, (R, R), 0)
    ci = lax.broadcasted_iota(jnp.int32, (R, R), 1)
    tril_b = (((ri // T) == (ci // T)) & (ci <= ri)).astype(bf16)
    rp = lax.broadcasted_iota(jnp.int32, (P, P), 0)
    cp = lax.broadcasted_iota(jnp.int32, (P, P), 1)
    causal = cp <= rp
    real_col = lax.broadcasted_iota(jnp.int32, (1, P), 1) < T
    scale = DK ** -0.5
    ones_b = jnp.ones((P, LANES), bf16)
    mean_b = jnp.full((DV, LANES), 1.0 / DV, bf16)
    wide = lambda x: jnp.concatenate([x] * (DV // LANES), axis=1)

    g = gt_ref[...] + bias_ref[...]
    fcum = _cumsum_rows(_log_sigmoid(g), tril_b)
    zrows = jnp.zeros((LANES - R, LANES), f32)
    g_t = jnp.concatenate([g, zrows], axis=0).T
    f_t = jnp.concatenate([fcum, zrows], axis=0).T

    def pad_rows(x, value=0.0):
        return jnp.concatenate([x, jnp.full((P - T, x.shape[1]), value, f32)], axis=0)

    pairs = [(c, h) for c in range(bs) for h in range(H)]
    ks_of = lambda h: slice(h * DK, (h + 1) * DK)
    vs_of = lambda h: slice(h * DV, (h + 1) * DV)
    st = {}
    for c, h in pairs:
        rows = slice(c * T, (c + 1) * T)
        last = slice((c + 1) * T - 1, (c + 1) * T)
        i_col = jnp.broadcast_to(pad_rows(g[rows, h:h + 1], -jnp.inf), (P, LANES))
        f_last = jnp.broadcast_to(fcum[last, H + h:H + h + 1], (1, LANES))
        fc_col = jnp.concatenate([jnp.broadcast_to(fcum[rows, H + h:H + h + 1], (T, LANES)),
                                  jnp.broadcast_to(f_last, (P - T, LANES))], axis=0)
        i_row = jnp.where(real_col, g_t[h:h + 1, c * T:c * T + P], -jnp.inf)
        fc_row = f_t[H + h:H + h + 1, c * T:c * T + P]
        dm = jnp.where(causal, fc_col[:, :P] - fc_row + i_row, -jnp.inf)
        rmax = jnp.broadcast_to(jnp.max(dm, axis=1, keepdims=True), (P, LANES))
        qb = pad_rows(q_ref[rows, ks_of(h)] * scale).astype(bf16)
        kc = pad_rows(k_ref[rows, ks_of(h)])
        vc = pad_rows(v_ref[rows, vs_of(h)])
        st[c, h] = dict(i_col=i_col, f_last=f_last, fc_col=fc_col, dm=dm, rmax=rmax, qb=qb, kc=kc, vc=vc,
                        qk=_mm_nt(qb, kc))
    for c, h in pairs:
        d = st[c, h]
        m = jnp.broadcast_to(m0_ref[c, h], (1, LANES))
        gg = d["fc_col"] + m
        mt = jnp.maximum(gg, d["rmax"])
        m_new = mt[T - 1:T, :]
        d["mt"] = mt
        d["eg"] = jnp.exp(gg - mt)
        s = (d["qk"] * jnp.exp(d["dm"] - mt[:, :P])).astype(bf16)
        d["s"] = s
        d["ssum"] = jnp.dot(s, ones_b, preferred_element_type=f32)
        w_c = jnp.exp(d["f_last"] + m - m_new)
        w_j = jnp.exp(d["f_last"] - d["fc_col"] + d["i_col"] - m_new)
        c_ref[c, h] = w_c * c0_ref[c, h] + _mm_tn(d["vc"] * wide(w_j), d["kc"])
        n_ref[c, h] = w_c * n0_ref[c, h] + jnp.sum(w_j * d["kc"], axis=0, keepdims=True)
        m_ref[c, h] = m_new[:, :1]
    for c, h in pairs:
        d = st[c, h]
        qn = _mm_nt(d["qb"], jnp.broadcast_to(n0_ref[c, h], (LANES, DK)))
        num = wide(d["eg"]) * _mm_nt(d["qb"], c0_ref[c, h]) + _mm(d["s"], d["vc"])
        den = d["eg"] * qn + d["ssum"]
        hh = num / wide(jnp.maximum(jnp.abs(den), jnp.exp(-d["mt"])))
        d["hh"] = hh
        d["msq"] = jnp.dot((hh * hh).astype(bf16), mean_b, preferred_element_type=f32)
    for c, h in pairs:
        d = st[c, h]
        rows, vs = slice(c * T, (c + 1) * T), vs_of(h)
        yn = (d["hh"] * wide(lax.rsqrt(d["msq"] + NORM_EPS)))[:T] * gain_ref[:, vs]
        y_ref[rows, vs] = _sigmoid(o_ref[rows, vs]) * yn


def _mlstm_mix_seq(pr, b_gates, gain, B, T, state, bs):
    H, DK, DV = MLSTM_HEADS, MLSTM_DK, MLSTM_DV
    R = bs * T
    rmap = lambda c: (lambda i: (i, c))
    bmap4 = lambda i: (i, 0, 0, 0)
    const2 = lambda i: (0, 0)
    bias = jnp.zeros((1, LANES), f32).at[0, :2 * H].set(b_gates)
    c0, n0, m0 = state
    y, c, n, m = pl.pallas_call(
        functools.partial(_mlstm_seq_kernel, T=T, bs=bs),
        grid=(B // bs,),
        in_specs=[pl.BlockSpec((R, H * DK), rmap(0)), pl.BlockSpec((R, H * DK), rmap(1)),
                  pl.BlockSpec((R, H * DV), rmap(1)), pl.BlockSpec((R, H * DV), rmap(2)),
                  pl.BlockSpec((R, LANES), rmap((2 * H * DK + 2 * H * DV) // LANES)),
                  pl.BlockSpec((1, LANES), const2), pl.BlockSpec((1, H * DV), const2),
                  pl.BlockSpec((bs, H, DV, DK), bmap4), pl.BlockSpec((bs, H, 1, DK), bmap4),
                  pl.BlockSpec((bs, H, 1, 1), bmap4)],
        out_specs=[pl.BlockSpec((R, H * DV), rmap(0)), pl.BlockSpec((bs, H, DV, DK), bmap4),
                   pl.BlockSpec((bs, H, 1, DK), bmap4), pl.BlockSpec((bs, H, 1, 1), bmap4)],
        out_shape=[jax.ShapeDtypeStruct((B * T, H * DV), f32), jax.ShapeDtypeStruct((B, H, DV, DK), f32),
                   jax.ShapeDtypeStruct((B, H, 1, DK), f32), jax.ShapeDtypeStruct((B, H, 1, 1), f32)],
        compiler_params=_cparams("parallel"),
        name="mlstm_mix_seq",
    )(pr, pr, pr, pr, pr, bias, gain.reshape(1, H * DV), c0, n0.reshape(B, H, 1, DK), m0.reshape(B, H, 1, 1))
    return y, c, n.reshape(B, H, DK), m.reshape(B, H)


def _gla_kernel(*refs, kind, layer, L, nchunks, H, DK, DV, has_init, mx):
    if kind == "hgrn":
        q_ref, f_ref, v_ref, g_ref, lbl_ref, gain_ref = refs[:6]
        rest = refs[6:]
    else:
        q_ref, k_ref, v_ref, g_ref, gr_ref, wup_ref, bgate_ref, gain_ref = refs[:8]
        rest = refs[8:]
    if has_init:
        s0_ref = rest[0]
        rest = rest[1:]
    y_ref, s_ref, st_s = rest
    j = pl.program_id(1)

    @pl.when(j == 0)
    def _():
        for h in range(H):
            if has_init:
                st_s[h] = s0_ref[0, h].T
            else:
                st_s[h] = jnp.zeros((DV, DK), f32)

    Lp = max(L, BF16_ROWS)
    rowi = lax.broadcasted_iota(jnp.int32, (Lp, Lp), 0)
    coli = lax.broadcasted_iota(jnp.int32, (Lp, Lp), 1)
    causal = coli <= rowi
    tril_b = causal.astype(bf16)

    def pad_rows(x):
        if Lp == L:
            return x
        return jnp.concatenate([x, jnp.zeros((Lp - L, x.shape[1]), f32)], axis=0)

    if kind == "hgrn":
        logits = lbl_ref[...]
        e = jnp.exp(logits - jnp.max(logits, axis=0, keepdims=True))
        sm = e / jnp.sum(e, axis=0, keepdims=True)
        lb_all = jnp.zeros((1, H * DK), f32)
        for li in range(layer):
            lb_all = lb_all + sm[li:li + 1, :]
    else:
        scale = DK ** -0.5

    def chunk(c, carry):
        rows = pl.ds(pl.multiple_of(c * L, L), L)
        if kind == "gla":
            gk = _mm(pad_rows(gr_ref[rows, :]), wup_ref[...], mx)[:L] + bgate_ref[...]
            ld_all = _log_sigmoid(gk) * (1.0 / GLA_GATE_NORMALIZER)
        for h in range(H):
            ks = slice(h * DK, (h + 1) * DK)
            vs = slice(h * DV, (h + 1) * DV)
            if kind == "hgrn":
                qraw = q_ref[rows, ks]
                fg = f_ref[rows, ks]
                lb = lb_all[:, ks]
                gc = jnp.log(lb + (1.0 - lb) * _sigmoid(fg))
                kc = (1.0 - lb) * _sigmoid(-fg)
                qc = qraw * _sigmoid(qraw)
            else:
                qc = q_ref[rows, ks] * scale
                kc = k_ref[rows, ks]
                gc = ld_all[:, ks]
            vc = pad_rows(v_ref[rows, vs])
            qc, kc, gc = pad_rows(qc), pad_rows(kc), pad_rows(gc)
            bc = _cumsum_rows(gc, tril_b)
            qe = qc * jnp.exp(bc)
            ke = kc * jnp.exp(-bc)
            a = jnp.where(causal, _mm_nt(qe, ke, mx), 0.0)
            st = st_s[h]
            o = (_mm_nt(qe, st, mx) + _mm(a, vc, mx))[:L]
            b_last = bc[L - 1:L, :]
            kdec = kc * jnp.exp(b_last - bc)
            st_s[h] = st * jnp.exp(b_last) + _mm_tn(vc, kdec, mx)
            yn = o * lax.rsqrt(jnp.mean(o * o, axis=1, keepdims=True) + NORM_EPS) * gain_ref[:, vs]
            gg = g_ref[rows, vs]
            y_ref[rows, vs] = yn * (gg * _sigmoid(gg))
        return carry

    lax.fori_loop(0, nchunks, chunk, 0)

    @pl.when(j == pl.num_programs(1) - 1)
    def _():
        for h in range(H):
            s_ref[0, h] = st_s[h].T


def _gla_mix(kind, pr, extra, gain, B, T, state, tq, layer=0):
    if kind == "hgrn":
        H, DK, DV = HGRN_HEADS, HGRN_DK, HGRN_DV
    else:
        H, DK, DV = GLA_HEADS, GLA_DK, GLA_DV
    L = _chunk_len(T, LIN_CHUNK)
    nj = T // tq
    has_init = state is not None
    mx = bf16
    rmap = lambda c: (lambda b, j: (b * nj + j, c))
    bmap4 = lambda b, j: (b, 0, 0, 0)
    const2 = lambda b, j: (0, 0)
    hk, hv = H * DK, H * DV
    if kind == "hgrn":
        (lb_logits,) = extra
        in_specs = [pl.BlockSpec((tq, hk), rmap(0)), pl.BlockSpec((tq, hk), rmap(1)),
                    pl.BlockSpec((tq, hv), rmap(2)), pl.BlockSpec((tq, hv), rmap(3)),
                    pl.BlockSpec((DEPTH, hk), const2), pl.BlockSpec((1, hv), const2)]
        args = [pr, pr, pr, pr, lb_logits, gain.reshape(1, hv)]
    else:
        w_up, b_gate = extra
        w_up_p = jnp.zeros((LANES, hk), f32).at[:GLA_GATE_RANK].set(w_up)
        in_specs = [pl.BlockSpec((tq, hk), rmap(0)), pl.BlockSpec((tq, hk), rmap(1)),
                    pl.BlockSpec((tq, hv), rmap(1)), pl.BlockSpec((tq, hv), rmap(2)),
                    pl.BlockSpec((tq, LANES), rmap((2 * hk + 2 * hv) // LANES)),
                    pl.BlockSpec((LANES, hk), const2), pl.BlockSpec((1, hk), const2),
                    pl.BlockSpec((1, hv), const2)]
        args = [pr, pr, pr, pr, pr, w_up_p, b_gate.reshape(1, hk), gain.reshape(1, hv)]
    if has_init:
        in_specs.append(pl.BlockSpec((1, H, DK, DV), bmap4))
        args.append(state)
    y, s = pl.pallas_call(
        functools.partial(_gla_kernel, kind=kind, layer=layer, L=L, nchunks=tq // L, H=H, DK=DK, DV=DV,
                          has_init=has_init, mx=mx),
        grid=(B, nj),
        in_specs=in_specs,
        out_specs=[pl.BlockSpec((tq, hv), rmap(0)), pl.BlockSpec((1, H, DK, DV), bmap4)],
        out_shape=[jax.ShapeDtypeStruct((B * T, hv), f32), jax.ShapeDtypeStruct((B, H, DK, DV), f32)],
        scratch_shapes=[pltpu.VMEM((H, DV, DK), f32)],
        compiler_params=_cparams("parallel", "arbitrary"),
        name=kind + "_mix",
    )(*args)
    return y, s


def _gla_seq_kernel(*refs, kind, layer, T, bs, H, DK, DV):
    if kind == "hgrn":
        q_ref, f_ref, v_ref, g_ref, lbl_ref, gain_ref = refs[:6]
        rest = refs[6:]
    else:
        q_ref, k_ref, v_ref, g_ref, gr_ref, wup_ref, bgate_ref, gain_ref = refs[:8]
        rest = refs[8:]
    s0_ref, y_ref, s_ref, qe_s, ke_s, kd_s, el_s, v_s = rest
    P = BF16_ROWS
    R = bs * T
    hk = H * DK

    ri = lax.broadcasted_iota(jnp.int32, (R, R), 0)
    ci = lax.broadcasted_iota(jnp.int32, (R, R), 1)
    same_seq = (ri // T) == (ci // T)
    tril_b = (same_seq & (ci <= ri)).astype(bf16)
    seq_b = same_seq.astype(bf16)
    rp = lax.broadcasted_iota(jnp.int32, (P, P), 0)
    cp = lax.broadcasted_iota(jnp.int32, (P, P), 1)
    causal = cp <= rp

    if kind == "hgrn":
        logits = lbl_ref[...]
        e = jnp.exp(logits - jnp.max(logits, axis=0, keepdims=True))
        sm = e / jnp.sum(e, axis=0, keepdims=True)
        lb = jnp.zeros((1, hk), f32)
        for li in range(layer):
            lb = lb + sm[li:li + 1, :]
        qraw = q_ref[...]
        fg = f_ref[...]
        gc = jnp.log(lb + (1.0 - lb) * _sigmoid(fg))
        kc = (1.0 - lb) * _sigmoid(-fg)
        qc = qraw * _sigmoid(qraw)
    else:
        gk = _mm(gr_ref[...], wup_ref[...]) + bgate_ref[...]
        gc = _log_sigmoid(gk) * (1.0 / GLA_GATE_NORMALIZER)
        qc = q_ref[...] * (DK ** -0.5)
        kc = k_ref[...]
    pieces = _split3(gc)
    bc = sum(jnp.dot(tril_b, x, preferred_element_type=f32) for x in pieces)
    bl = sum(jnp.dot(seq_b, x, preferred_element_type=f32) for x in pieces)
    qe = qc * jnp.exp(bc)
    ke = kc * jnp.exp(-bc)
    kd = kc * jnp.exp(bl - bc)
    e_hi, e_mid, e_lo = [x.astype(f32) for x in _split3(jnp.exp(bl))]
    vv = v_ref[...]
    zpad_k = jnp.zeros((P - T, hk), f32)
    zpad_v = jnp.zeros((P - T, vv.shape[1]), f32)
    prow = lax.broadcasted_iota(jnp.int32, (P, hk), 0)
    for c in range(bs):
        src = slice(c * T, (c + 1) * T)
        dst = slice(c * P, (c + 1) * P)
        qe_s[dst, :] = jnp.concatenate([qe[src], zpad_k], axis=0)
        ke_s[dst, :] = jnp.concatenate([ke[src], zpad_k], axis=0)
        kd_s[dst, :] = jnp.concatenate([kd[src], zpad_k], axis=0)
        v_s[dst, :] = jnp.concatenate([vv[src], zpad_v], axis=0)
        first = slice(c * T, c * T + 1)
        el_s[dst, :] = jnp.where(prow == 0, e_hi[first],
                                 jnp.where(prow == 1, e_mid[first], jnp.where(prow == 2, e_lo[first], 0.0)))

    ones_b = jnp.ones((P, LANES), bf16)
    pairs = [(c, h) for c in range(bs) for h in range(H)]
    rows_of = lambda c: slice(c * P, (c + 1) * P)
    ks_of = lambda h: slice(h * DK, (h + 1) * DK)
    vs_of = lambda h: slice(h * DV, (h + 1) * DV)
    qes, vbs, amats, news = {}, {}, {}, {}
    for c, h in pairs:
        rows, ks = rows_of(c), ks_of(h)
        qes[c, h] = qe_s[rows, ks].astype(bf16)
        vbs[c, h] = v_s[rows, vs_of(h)].astype(bf16)
        a = _mm_nt(qes[c, h], ke_s[rows, ks])
        amats[c, h] = jnp.where(causal, a, 0.0).astype(bf16)
        decay = _mm_tn(el_s[rows, ks], ones_b)
        decay = jnp.concatenate([decay] * (DV // LANES), axis=1)
        news[c, h] = s0_ref[c, h] * decay + _mm_tn(kd_s[rows, ks], vbs[c, h])
    for c, h in pairs:
        s_ref[c, h] = news[c, h]
    outs = {}
    for c, h in pairs:
        outs[c, h] = (_mm(qes[c, h], s0_ref[c, h]) + _mm(amats[c, h], vbs[c, h]))[:T]
    for c, h in pairs:
        vs = vs_of(h)
        rows = slice(c * T, (c + 1) * T)
        o = outs[c, h]
        yn = o * lax.rsqrt(jnp.mean(o * o, axis=1, keepdims=True) + NORM_EPS) * gain_ref[:, vs]
        gg = g_ref[rows, vs]
        y_ref[rows, vs] = yn * (gg * _sigmoid(gg))


def _gla_mix_seq(kind, pr, extra, gain, B, T, state, bs, layer=0):
    if kind == "hgrn":
        H, DK, DV = HGRN_HEADS, HGRN_DK, HGRN_DV
    else:
        H, DK, DV = GLA_HEADS, GLA_DK, GLA_DV
    R = bs * T
    rmap = lambda c: (lambda i: (i, c))
    bmap4 = lambda i: (i, 0, 0, 0)
    const2 = lambda i: (0, 0)
    hk, hv = H * DK, H * DV
    if kind == "hgrn":
        (lb_logits,) = extra
        in_specs = [pl.BlockSpec((R, hk), rmap(0)), pl.BlockSpec((R, hk), rmap(1)),
                    pl.BlockSpec((R, hv), rmap(2)), pl.BlockSpec((R, hv), rmap(3)),
                    pl.BlockSpec((DEPTH, hk), const2), pl.BlockSpec((1, hv), const2)]
        args = [pr, pr, pr, pr, lb_logits, gain.reshape(1, hv)]
    else:
        w_up, b_gate = extra
        w_up_p = jnp.zeros((LANES, hk), f32).at[:GLA_GATE_RANK].set(w_up)
        in_specs = [pl.BlockSpec((R, hk), rmap(0)), pl.BlockSpec((R, hk), rmap(1)),
                    pl.BlockSpec((R, hv), rmap(1)), pl.BlockSpec((R, hv), rmap(2)),
                    pl.BlockSpec((R, LANES), rmap((2 * hk + 2 * hv) // LANES)),
                    pl.BlockSpec((LANES, hk), const2), pl.BlockSpec((1, hk), const2),
                    pl.BlockSpec((1, hv), const2)]
        args = [pr, pr, pr, pr, pr, w_up_p, b_gate.reshape(1, hk), gain.reshape(1, hv)]
    in_specs.append(pl.BlockSpec((bs, H, DK, DV), bmap4))
    args.append(state)
    P = BF16_ROWS
    return pl.pallas_call(
        functools.partial(_gla_seq_kernel, kind=kind, layer=layer, T=T, bs=bs, H=H, DK=DK, DV=DV),
        grid=(B // bs,),
        in_specs=in_specs,
        out_specs=[pl.BlockSpec((R, hv), rmap(0)), pl.BlockSpec((bs, H, DK, DV), bmap4)],
        out_shape=[jax.ShapeDtypeStruct((B * T, hv), f32), jax.ShapeDtypeStruct((B, H, DK, DV), f32)],
        scratch_shapes=[pltpu.VMEM((bs * P, hk), f32)] * 4 + [pltpu.VMEM((bs * P, hv), f32)],
        compiler_params=_cparams("parallel"),
        name=kind + "_mix_seq",
    )(*args)


def _gla_block_kernel(*refs, kind, layer, L, nc, H, DK, DV):
    if kind == "hgrn":
        q_ref, f_ref, v_ref, g_ref, lbl_ref, gain_ref = refs[:6]
        rest = refs[6:]
    else:
        q_ref, k_ref, v_ref, g_ref, gr_ref, wup_ref, bgate_ref, gain_ref = refs[:8]
        rest = refs[8:]
    x_ref, wout_ref, xo_ref, s_ref, st_s, qe_s, ke_s, qs_s, kd_s, el_s, y_ref = rest
    j = pl.program_id(1)

    @pl.when(j == 0)
    def _():
        st_s[...] = jnp.zeros_like(st_s)

    rowi = lax.broadcasted_iota(jnp.int32, (L, L), 0)
    coli = lax.broadcasted_iota(jnp.int32, (L, L), 1)
    causal = coli <= rowi
    tril_b = causal.astype(bf16)

    if kind == "hgrn":
        logits = lbl_ref[...]
        e = jnp.exp(logits - jnp.max(logits, axis=0, keepdims=True))
        sm = e / jnp.sum(e, axis=0, keepdims=True)
        lb = jnp.zeros((1, H * DK), f32)
        for li in range(layer):
            lb = lb + sm[li:li + 1, :]
    else:
        scale = DK ** -0.5

    for c in range(nc):
        rows = slice(c * L, (c + 1) * L)
        if kind == "hgrn":
            qraw = q_ref[rows, :]
            fg = f_ref[rows, :]
            gc = jnp.log(lb + (1.0 - lb) * _sigmoid(fg))
            kc = (1.0 - lb) * _sigmoid(-fg)
            qc = qraw * _sigmoid(qraw)
        else:
            gk = jnp.dot(gr_ref[rows, :].astype(bf16), wup_ref[...].astype(bf16),
                         preferred_element_type=f32) + bgate_ref[...]
            gc = _log_sigmoid(gk) * (1.0 / GLA_GATE_NORMALIZER)
            qc = q_ref[rows, :] * scale
            kc = k_ref[rows, :]
        hi, mid, lo = _split3(gc)
        bc = (jnp.dot(tril_b, hi, preferred_element_type=f32) + jnp.dot(tril_b, mid, preferred_element_type=f32)
              + jnp.dot(tril_b, lo, preferred_element_type=f32))
        b_mid = bc[L // 2 - 1:L // 2, :]
        b_last = bc[L - 1:L, :]
        qe = qc * jnp.exp(bc - b_mid)
        ke = kc * jnp.exp(b_mid - bc)
        qe_s[rows, :] = qe.astype(bf16)
        ke_s[rows, :] = ke.astype(bf16)
        qs_s[rows, :] = (qe * jnp.exp(b_mid)).astype(bf16)
        kd_s[rows, :] = (ke * jnp.exp(b_last - b_mid)).astype(bf16)
        el_s[c] = jnp.exp(b_last)

    dn_nt = (((1,), (1,)), ((), ()))
    dn_tn = (((0,), (0,)), ((), ()))
    pairs = [(c, h) for c in range(nc) for h in range(H)]
    rows_of = lambda c: slice(c * L, (c + 1) * L)
    ks_of = lambda h: slice(h * DK, (h + 1) * DK)
    vs_of = lambda h: slice(h * DV, (h + 1) * DV)
    vbs, amats, dsts = {}, {}, {}
    for c, h in pairs:
        rows, ks = rows_of(c), ks_of(h)
        vb = v_ref[rows, vs_of(h)].astype(bf16)
        a = lax.dot_general(qe_s[rows, ks], ke_s[rows, ks], dn_nt, preferred_element_type=f32)
        vbs[c, h] = vb
        amats[c, h] = jnp.where(causal, a, 0.0).astype(bf16)
        dsts[c, h] = lax.dot_general(vb, kd_s[rows, ks], dn_tn, preferred_element_type=f32)
    sts = {}
    for h in range(H):
        st = st_s[h]
        for c in range(nc):
            sts[c, h] = st.astype(bf16)
            st = st * el_s[c][:, ks_of(h)] + dsts[c, h]
        st_s[h] = st
    outs = {}
    for c, h in pairs:
        rows, ks = rows_of(c), ks_of(h)
        outs[c, h] = (lax.dot_general(qs_s[rows, ks], sts[c, h], dn_nt, preferred_element_type=f32)
                      + jnp.dot(amats[c, h], vbs[c, h], preferred_element_type=f32))
    for c, h in pairs:
        rows, vs = rows_of(c), vs_of(h)
        o = outs[c, h]
        yn = o * lax.rsqrt(jnp.mean(o * o, axis=1, keepdims=True) + NORM_EPS) * gain_ref[:, vs]
        gg = g_ref[rows, vs]
        y_ref[rows, vs] = yn * (gg * _sigmoid(gg))
    xo_ref[...] = x_ref[...] + jnp.dot(y_ref[...].astype(bf16), wout_ref[...], preferred_element_type=f32)

    @pl.when(j == pl.num_programs(1) - 1)
    def _():
        for h in range(H):
            s_ref[0, h] = st_s[h].T


def _gla_mix_fresh(kind, pr, extra, gain, x, w_out, B, T, tq, layer=0):
    if kind == "hgrn":
        H, DK, DV = HGRN_HEADS, HGRN_DK, HGRN_DV
    else:
        H, DK, DV = GLA_HEADS, GLA_DK, GLA_DV
    L = 2 * LIN_CHUNK
    nc = tq // L
    nj = T // tq
    rmap = lambda c: (lambda b, j: (b * nj + j, c))
    bmap4 = lambda b, j: (b, 0, 0, 0)
    const2 = lambda b, j: (0, 0)
    hk, hv = H * DK, H * DV
    if kind == "hgrn":
        (lb_logits,) = extra
        in_specs = [pl.BlockSpec((tq, hk), rmap(0)), pl.BlockSpec((tq, hk), rmap(1)),
                    pl.BlockSpec((tq, hv), rmap(2)), pl.BlockSpec((tq, hv), rmap(3)),
                    pl.BlockSpec((DEPTH, hk), const2), pl.BlockSpec((1, hv), const2)]
        args = [pr, pr, pr, pr, lb_logits, gain.reshape(1, hv)]
    else:
        w_up, b_gate = extra
        w_up_p = jnp.zeros((LANES, hk), f32).at[:GLA_GATE_RANK].set(w_up)
        in_specs = [pl.BlockSpec((tq, hk), rmap(0)), pl.BlockSpec((tq, hk), rmap(1)),
                    pl.BlockSpec((tq, hv), rmap(1)), pl.BlockSpec((tq, hv), rmap(2)),
                    pl.BlockSpec((tq, LANES), rmap((2 * hk + 2 * hv) // LANES)),
                    pl.BlockSpec((LANES, hk), const2), pl.BlockSpec((1, hk), const2),
                    pl.BlockSpec((1, hv), const2)]
        args = [pr, pr, pr, pr, pr, w_up_p, b_gate.reshape(1, hk), gain.reshape(1, hv)]
    in_specs += [pl.BlockSpec((tq, D_MODEL), rmap(0)), pl.BlockSpec((hv, D_MODEL), const2)]
    args += [x, w_out]
    return pl.pallas_call(
        functools.partial(_gla_block_kernel, kind=kind, layer=layer, L=L, nc=nc, H=H, DK=DK, DV=DV),
        grid=(B, nj),
        in_specs=in_specs,
        out_specs=[pl.BlockSpec((tq, D_MODEL), rmap(0)), pl.BlockSpec((1, H, DK, DV), bmap4)],
        out_shape=[jax.ShapeDtypeStruct((B * T, D_MODEL), f32), jax.ShapeDtypeStruct((B, H, DK, DV), f32)],
        scratch_shapes=[pltpu.VMEM((H, DV, DK), f32)] + [pltpu.VMEM((tq, hk), bf16)] * 4
                       + [pltpu.VMEM((nc, 1, hk), f32), pltpu.VMEM((tq, hv), f32)],
        compiler_params=_cparams("parallel", "arbitrary"),
        name=kind + "_mix_fresh",
    )(*args)


def _rwkv_pre_kernel(x_ref, aux_ref, sh_ref, g_ref, mu_ref, wrkv_ref, la_ref, lbw_ref, lba_ref, lbg_ref,
                     w0_ref, a0_ref, r_ref, w_ref, k_ref, v_ref, alr_ref, gate_ref, *, long_seq, blocks_per_seq,
                     seq_len):
    g = g_ref[...]
    hn = _rms(x_ref[...], g)
    tm = hn.shape[0]
    rowi = lax.broadcasted_iota(jnp.int32, (tm, 1), 0)
    rolled = pltpu.roll(hn, 1, axis=0)
    if long_seq:
        p_last = _rms(aux_ref[SUBLANES - 1:SUBLANES, :], g)
        at_start = (pl.program_id(0) % blocks_per_seq) == 0
        first = jnp.where(at_start, sh_ref[0], p_last)
        prev = jnp.where(rowi == 0, first, rolled)
    else:
        prev = jnp.where(rowi % seq_len == 0, aux_ref[...], rolled)
    xx = prev - hn

    def lerp(c):
        return (hn + xx * mu_ref[c:c + 1, :]).astype(bf16)

    r_ref[...] = jnp.dot(lerp(0), wrkv_ref[0], preferred_element_type=f32)
    k_ref[...] = jnp.dot(lerp(1), wrkv_ref[1], preferred_element_type=f32)
    v_ref[...] = jnp.dot(lerp(2), wrkv_ref[2], preferred_element_type=f32)
    lw = jnp.tanh(jnp.dot(lerp(3), la_ref[:, 0:64], preferred_element_type=f32))
    wl = -_softplus(-(w0_ref[...] + jnp.dot(lw.astype(bf16), lbw_ref[...], preferred_element_type=f32))) - 0.5
    w_ref[...] = jnp.exp(-jnp.exp(wl))
    la = jnp.dot(lerp(4), la_ref[:, 64:128], preferred_element_type=f32)
    alr_ref[...] = _sigmoid(a0_ref[...] + jnp.dot(la.astype(bf16), lba_ref[...], preferred_element_type=f32))
    lg = _sigmoid(jnp.dot(lerp(5), la_ref[:, 128:256], preferred_element_type=f32))
    gate_ref[...] = jnp.dot(lg.astype(bf16), lbg_ref[...], preferred_element_type=f32)


def _rwkv_pre(x, shift0, g, mu, wrkv, la, lbw, lba, lbg, w0, a0, B, T, tm):
    n, d = x.shape
    long_seq = T % tm == 0
    row = lambda i: (i, 0)
    const2 = lambda i: (0, 0)
    if long_seq:
        bps = T // tm
        sub = tm // SUBLANES
        aux = x
        aux_spec = pl.BlockSpec((SUBLANES, d), lambda i: (jnp.maximum(i * sub - 1, 0), 0))
        sh = shift0.reshape(B, 1, d)
        sh_spec = pl.BlockSpec((1, 1, d), lambda i: (i // bps, 0, 0))
    else:
        assert tm % T == 0
        bps = 1
        aux = jnp.repeat(shift0, T, axis=0)
        aux_spec = pl.BlockSpec((tm, d), row)
        sh = shift0.reshape(B, 1, d)
        sh_spec = pl.BlockSpec((1, 1, d), lambda i: (0, 0, 0))
    out = jax.ShapeDtypeStruct((n, d), f32)
    return pl.pallas_call(
        functools.partial(_rwkv_pre_kernel, long_seq=long_seq, blocks_per_seq=bps, seq_len=T),
        grid=(n // tm,),
        in_specs=[pl.BlockSpec((tm, d), row), aux_spec, sh_spec, pl.BlockSpec((1, d), const2),
                  pl.BlockSpec((6, d), const2),
                  pl.BlockSpec((3, d, d), lambda i: (0, 0, 0), pipeline_mode=pl.Buffered(1)),
                  pl.BlockSpec((d, 256), const2), pl.BlockSpec((64, d), const2), pl.BlockSpec((64, d), const2),
                  pl.BlockSpec((128, d), const2), pl.BlockSpec((1, d), const2), pl.BlockSpec((1, d), const2)],
        out_specs=[pl.BlockSpec((tm, d), row)] * 6,
        out_shape=[out] * 6,
        compiler_params=_cparams("parallel"),
        name="rwkv_pre",
    )(x, aux, sh, g.reshape(1, d), mu, wrkv, la, lbw, lba, lbg, w0.reshape(1, d), a0.reshape(1, d))


def _rwkv_scan_kernel(*refs, tt, has_init, unroll):
    N = RWKV_N
    (r_ref, w_ref, k_ref, v_ref, alr_ref, gate_ref, x_ref, wout_ref,
     kk_ref, ka_ref, rk_ref, lnw_ref, lnb_ref) = refs[:13]
    if has_init:
        s0_ref = refs[13]
        rest = refs[14:]
    else:
        rest = refs[13:]
    xo_ref, sout_ref, s_s, y_s, vec_s, z_ref, zz_s = rest
    HP = RWKV_HEADS // 2
    j = pl.program_id(1)

    @pl.when(j == 0)
    def _():
        if has_init:
            s_s[...] = s0_ref[...]
        else:
            s_s[...] = jnp.zeros_like(s_s)

    low = lax.broadcasted_iota(jnp.int32, (N, LANES), 1) < N

    def load_pair(ref, t0):
        tiles = []
        for t in (t0, t0 + 1):
            x = ref[:, t, :]
            tiles += [x[:, hp * LANES:(hp + 1) * LANES] for hp in range(HP)]
        xt = jnp.concatenate(tiles, axis=0).T
        ev, od = xt[:N], xt[N:]
        return (jnp.where(low, ev, pltpu.roll(od, N, axis=1)),
                jnp.where(low, pltpu.roll(ev, N, axis=1), od))

    def store_pair(ref, t0, z0, z1):
        ev = jnp.where(low, z0, pltpu.roll(z1, N, axis=1))
        od = jnp.where(low, pltpu.roll(z0, N, axis=1), z1)
        x = jnp.concatenate([ev, od], axis=0).T
        for i, t in enumerate((t0, t0 + 1)):
            tiles = [x[(i * HP + hp) * SUBLANES:(i * HP + hp + 1) * SUBLANES, :] for hp in range(HP)]
            ref[:, t, :] = jnp.concatenate(tiles, axis=1)

    VA, VW, VB, VK, VR, VV = range(6)

    def stage(tk, r, w, k, v, alr):
        kkraw = k * kk_ref[...]
        nrm = jnp.sqrt(jnp.sum(kkraw * kkraw, axis=0, keepdims=True))
        kk = kkraw / jnp.maximum(nrm, 1e-12)
        vec_s[tk, VA] = -kk
        vec_s[tk, VW] = w
        vec_s[tk, VB] = kk * alr
        vec_s[tk, VK] = k * (1.0 + (alr - 1.0) * ka_ref[...])
        vec_s[tk, VR] = r
        vec_s[tk, VV] = v

    def key_row(tk, which, kc):
        return vec_s[tk, which, pl.ds(kc, 1), :]

    def project(tk):
        def body(i, sa):
            for u in range(unroll):
                kc = i * unroll + u
                sa = sa + s_s[kc] * key_row(tk, VA, kc)
            return sa

        return lax.fori_loop(0, N // unroll, body, jnp.zeros((N, LANES), f32))

    def update(tk, sa):
        vv = vec_s[tk, VV]
        y = jnp.zeros((N, LANES), f32)
        for kc in range(N):
            sk = s_s[kc] * key_row(tk, VW, kc) + sa * key_row(tk, VB, kc) + vv * key_row(tk, VK, kc)
            s_s[kc] = sk
            y = y + sk * key_row(tk, VR, kc)
        y_s[tk] = y

    def epilogue(tk):
        y = y_s[tk]
        mean = jnp.mean(y, axis=0, keepdims=True)
        yc = y - mean
        var = jnp.mean(yc * yc, axis=0, keepdims=True)
        yn = yc * lax.rsqrt(var + RWKV_GN_EPS)
        bonus = jnp.sum(vec_s[tk, VR] * vec_s[tk, VK] * rk_ref[...], axis=0, keepdims=True) * vec_s[tk, VV]
        return yn * lnw_ref[...] + lnb_ref[...] + bonus

    def stage_pair(slot, t0):
        streams = [load_pair(ref, t0) for ref in (r_ref, w_ref, k_ref, v_ref, alr_ref)]
        for i in range(2):
            stage(slot + i, *[s[i] for s in streams])

    @pl.when(j == 0)
    def _():
        vec_s[...] = jnp.zeros_like(vec_s)
        y_s[...] = jnp.zeros_like(y_s)

    npairs = tt // 2
    stage_pair(0, 0)

    def step(p, carry):
        slot = 2 * (p % 2)
        prev = 2 - slot
        sa = project(slot)
        update(slot, sa)
        zz_s[0] = epilogue(prev)
        zz_s[1] = epilogue(prev + 1)
        stage_pair(prev, 2 * jnp.minimum(p + 1, npairs - 1))
        sa = project(slot + 1)
        update(slot + 1, sa)
        store_pair(z_ref, 2 * jnp.maximum(p - 1, 0), zz_s[0], zz_s[1])
        return carry

    lax.fori_loop(0, npairs, step, 0)
    last = 2 * ((npairs - 1) % 2)
    store_pair(z_ref, tt - 2, epilogue(last), epilogue(last + 1))

    rows = RWKV_SEQ_PER_STEP * tt
    zg = (z_ref[...] * gate_ref[...]).reshape(rows, D_MODEL).astype(bf16)
    proj = jnp.dot(zg, wout_ref[...], preferred_element_type=f32)
    xo_ref[...] = x_ref[...] + proj.reshape(RWKV_SEQ_PER_STEP, tt, D_MODEL)

    @pl.when(j == pl.num_programs(1) - 1)
    def _():
        sout_ref[...] = s_s[...]


RWKV_SEQ_PER_STEP = LANES // RWKV_HEADS


def _rwkv_scan(r, w, k, v, alr, gate, x, w_out, slabs, s0, B, T, tt):
    N = RWKV_N
    G = B // RWKV_SEQ_PER_STEP
    has_init = s0 is not None
    view = lambda a: a.reshape(B, T, D_MODEL)
    tmap = lambda g, j: (g, j, 0)
    smap = lambda g, j: (0, 0, g)
    const2 = lambda g, j: (0, 0)
    stream = pl.BlockSpec((RWKV_SEQ_PER_STEP, tt, D_MODEL), tmap)
    in_specs = [stream] * 7 + [pl.BlockSpec((D_MODEL, D_MODEL), const2)] + [pl.BlockSpec((N, LANES), const2)] * 5
    args = [view(r), view(w), view(k), view(v), view(alr), view(gate), view(x), w_out] + list(slabs)
    if has_init:
        in_specs.append(pl.BlockSpec((N, N, LANES), smap))
        args.append(s0)
    z, s = pl.pallas_call(
        functools.partial(_rwkv_scan_kernel, tt=tt, has_init=has_init, unroll=16),
        grid=(G, T // tt),
        in_specs=in_specs,
        out_specs=[stream, pl.BlockSpec((N, N, LANES), smap)],
        out_shape=[jax.ShapeDtypeStruct((B, T, D_MODEL), f32),
                   jax.ShapeDtypeStruct((N, N, G * LANES), f32)],
        scratch_shapes=[pltpu.VMEM((N, N, LANES), f32), pltpu.VMEM((4, N, LANES), f32),
                        pltpu.VMEM((4, 6, N, LANES), f32), pltpu.VMEM((RWKV_SEQ_PER_STEP, tt, D_MODEL), f32),
                        pltpu.VMEM((2, N, LANES), f32)],
        compiler_params=_cparams("parallel", "arbitrary"),
        name="rwkv_scan",
    )(*args)
    return z.reshape(B * T, D_MODEL), s


def _lane_slab(p):
    q = p.reshape(RWKV_HEADS // 2, 2, RWKV_N).transpose(2, 1, 0)
    q = jnp.broadcast_to(q[:, :, :, None], (RWKV_N, 2, RWKV_HEADS // 2, RWKV_SEQ_PER_STEP))
    return q.reshape(RWKV_N, LANES)


def _rwkv_mix(x, p, B, T, state, tm):
    H, N = RWKV_HEADS, RWKV_N
    G = B // RWKV_SEQ_PER_STEP
    if state is None:
        shift0 = jnp.zeros((B, D_MODEL), f32)
        s0 = None
    else:
        s_in, shift0 = state
        s0 = (s_in.reshape(G, RWKV_SEQ_PER_STEP, H // 2, 2, N, N)
              .transpose(5, 4, 0, 3, 2, 1).reshape(N, N, G * LANES))
    r, w, k, v, alr, gate = _rwkv_pre(x, shift0, p["g"], p["mu"], p["wrkv"], p["la"], p["lbw"], p["lba"], p["lbg"],
                                      p["w0"], p["a0"], B, T, tm)
    slabs = [_lane_slab(p[n]) for n in ("k_k", "k_a", "r_k", "ln_w", "ln_b")]
    x_new, s = _rwkv_scan(r, w, k, v, alr, gate, x, p["w_out"], slabs, s0, B, T, min(T, 64))
    s = (s.reshape(N, N, G, 2, H // 2, RWKV_SEQ_PER_STEP)
         .transpose(2, 5, 4, 3, 1, 0).reshape(B, H, N, N))
    return x_new, s


def _pad_cols(w, to):
    return jnp.pad(w, ((0, 0), (0, to - w.shape[1])))


def _trunk(x, B, T, states, p):
    n = x.shape[0]
    tm = min(512, n)
    tmm = min(1024, n)
    tq = min(256, T)
    fresh = states is None
    block_ok = fresh and T % tq == 0 and tq % (2 * LIN_CHUNK) == 0
    seq_ok = (not fresh) and T == SUBLANES and T <= LIN_CHUNK and B % SEQS_PER_STEP == 0
    if not fresh:
        m_c, m_n, m_m, h_s, g_s, r_s, r_sh = states
    new = {}
    for li in range(DEPTH):
        g_mix = p["norm_mix"][li]
        if li == 0:
            pr = _norm_proj(x, g_mix, p["mlstm_w_in"], tmm, 1664)
            if fresh and T % tq == 0 and tq % (2 * MLSTM_CHUNK) == 0:
                x, c, nn, m = _mlstm_mix_fresh(pr, p["mlstm_b_gates"], p["mlstm_norm"], x, p["mlstm_w_out"],
                                               B, T, tq)
            else:
                if seq_ok:
                    y, c, nn, m = _mlstm_mix_seq(pr, p["mlstm_b_gates"], p["mlstm_norm"], B, T,
                                                 (m_c[0], m_n[0], m_m[0]), SEQS_PER_STEP)
                else:
                    y, c, nn, m = _mlstm_mix(pr, p["mlstm_b_gates"], p["mlstm_norm"], B, T,
                                             None if fresh else (m_c[0], m_n[0], m_m[0]), tq)
                x = _out_proj(y, p["mlstm_w_out"], x, tm)
            new["C"], new["n"], new["m"] = c[None], nn[None], m[None]
        elif li == 1:
            pr = _norm_proj(x, g_mix, p["hgrn_w_in"], tmm, 2048)
            if block_ok:
                x, s = _gla_mix_fresh("hgrn", pr, (p["hgrn_lb_logits"],), p["hgrn_norm"], x, p["hgrn_w_out"],
                                      B, T, tq, layer=li)
            else:
                if seq_ok:
                    y, s = _gla_mix_seq("hgrn", pr, (p["hgrn_lb_logits"],), p["hgrn_norm"], B, T, h_s[0],
                                        SEQS_PER_STEP, layer=li)
                else:
                    y, s = _gla_mix("hgrn", pr, (p["hgrn_lb_logits"],), p["hgrn_norm"], B, T,
                                    None if fresh else h_s[0], tq, layer=li)
                x = _out_proj(y, p["hgrn_w_out"], x, tm)
            new["hS"] = s[None]
        elif li == 2:
            pr = _norm_proj(x, g_mix, p["gla_w_in"], tmm, 1664)
            if block_ok:
                x, s = _gla_mix_fresh("gla", pr, (p["gla_w_gate_up"], p["gla_b_gate"]), p["gla_norm"], x,
                                      p["gla_w_out"], B, T, tq)
            else:
                if seq_ok:
                    y, s = _gla_mix_seq("gla", pr, (p["gla_w_gate_up"], p["gla_b_gate"]), p["gla_norm"], B, T,
                                        g_s[0], SEQS_PER_STEP)
                else:
                    y, s = _gla_mix("gla", pr, (p["gla_w_gate_up"], p["gla_b_gate"]), p["gla_norm"], B, T,
                                    None if fresh else g_s[0], tq)
                x = _out_proj(y, p["gla_w_out"], x, tm)
            new["gS"] = s[None]
        else:
            rp = dict(p["rwkv"], g=g_mix, w_out=p["rwkv_w_out"])
            x_last = x.reshape(B, T, D_MODEL)[:, T - 1, :]
            new["sh"] = _rmsnorm(x_last, g_mix, B)[None]
            x, s = _rwkv_mix(x, rp, B, T, None if fresh else (r_s[0], r_sh[0]), tm)
            new["rS"] = s[None]
        x = _ffn(x, p["norm_ffn"][li], p["ffn_w_gate_up"][li], p["ffn_w_down"][li], tm, 1408,
                 final_gain=p["norm_final"] if li == DEPTH - 1 else None)
    return x.reshape(B, T, D_MODEL), (new["C"], new["n"], new["m"], new["hS"], new["gS"], new["rS"], new["sh"])


def kernel(x_prompt, x_sample, state_mlstm_C, state_mlstm_n, state_mlstm_m, state_hgrn_S, state_gla_S, state_rwkv_S, state_rwkv_shift, norm_mix, norm_ffn, norm_final, mlstm_w_in, mlstm_b_gates, mlstm_norm, mlstm_w_out, hgrn_w_in, hgrn_lb_logits, hgrn_norm, hgrn_w_out, gla_w_in, gla_w_gate_up, gla_b_gate, gla_norm, gla_w_out, rwkv_mu, rwkv_w_rkv, rwkv_w_lora_a, rwkv_w_lora_b, rwkv_w0, rwkv_a_lora_a, rwkv_a_lora_b, rwkv_a0, rwkv_g_lora_a, rwkv_g_lora_b, rwkv_k_k, rwkv_k_a, rwkv_r_k, rwkv_ln_w, rwkv_ln_b, rwkv_w_out, ffn_w_gate_up, ffn_w_down):
    cast = lambda w: w.astype(bf16)
    p = dict(
        norm_mix=norm_mix, norm_ffn=norm_ffn, norm_final=norm_final,
        mlstm_w_in=cast(_pad_cols(mlstm_w_in[0], 3328)), mlstm_b_gates=mlstm_b_gates[0], mlstm_norm=mlstm_norm[0],
        mlstm_w_out=cast(mlstm_w_out[0]),
        hgrn_w_in=cast(hgrn_w_in[0]), hgrn_lb_logits=hgrn_lb_logits, hgrn_norm=hgrn_norm[0],
        hgrn_w_out=cast(hgrn_w_out[0]),
        gla_w_in=cast(_pad_cols(gla_w_in[0], 3328)), gla_w_gate_up=gla_w_gate_up[0], gla_b_gate=gla_b_gate[0],
        gla_norm=gla_norm[0], gla_w_out=cast(gla_w_out[0]),
        rwkv=dict(mu=rwkv_mu[0], wrkv=cast(rwkv_w_rkv[0]),
                  la=cast(jnp.concatenate([rwkv_w_lora_a[0], rwkv_a_lora_a[0], rwkv_g_lora_a[0]], axis=1)),
                  lbw=cast(rwkv_w_lora_b[0]), lba=cast(rwkv_a_lora_b[0]), lbg=cast(rwkv_g_lora_b[0]),
                  w0=rwkv_w0[0], a0=rwkv_a0[0], k_k=rwkv_k_k[0], k_a=rwkv_k_a[0], r_k=rwkv_r_k[0].reshape(-1),
                  ln_w=rwkv_ln_w[0], ln_b=rwkv_ln_b[0]),
        rwkv_w_out=cast(rwkv_w_out[0]),
        ffn_w_gate_up=cast(ffn_w_gate_up), ffn_w_down=cast(ffn_w_down),
    )
    bp, tp, _ = x_prompt.shape
    bs, ts, _ = x_sample.shape
    y_p, st_p = _trunk(x_prompt.reshape(bp * tp, D_MODEL), bp, tp, None, p)
    y_s, st_s = _trunk(x_sample.reshape(bs * ts, D_MODEL), bs, ts,
                       (state_mlstm_C, state_mlstm_n, state_mlstm_m, state_hgrn_S, state_gla_S, state_rwkv_S,
                        state_rwkv_shift), p)
    return (y_p, y_s) + st_p + st_s
```

```python
import functools

import jax
import jax.numpy as jnp
from jax import lax
from jax.experimental import pallas as pl
from jax.experimental.pallas import tpu as pltpu

f32 = jnp.float32
bf16 = jnp.bfloat16

D_MODEL = 1024
DEPTH = 4
NORM_EPS = 1e-6

MLSTM_HEADS, MLSTM_DK, MLSTM_DV, MLSTM_CHUNK = 4, 128, 256, 64
HGRN_HEADS, HGRN_DK, HGRN_DV = 8, 128, 128
GLA_HEADS, GLA_DK, GLA_DV = 4, 128, 256
GLA_GATE_RANK = 16
GLA_GATE_NORMALIZER = 16.0
LIN_CHUNK = 32
RWKV_HEADS, RWKV_N = 16, 64
RWKV_GN_EPS = 64e-5
FFN_HIDDEN = 2816

LANES = 128
SUBLANES = 8
BF16_ROWS = 16
SEQS_PER_STEP = 8
VMEM_LIMIT_BYTES = 52 * 1024 * 1024


def _cparams(*sem):
    return pltpu.CompilerParams(dimension_semantics=sem, vmem_limit_bytes=VMEM_LIMIT_BYTES)


def _chunk_len(t, cap):
    return max(d for d in range(1, min(cap, t) + 1) if t % d == 0)


def _rms(x, g):
    ms = jnp.mean(x * x, axis=-1, keepdims=True)
    return x * lax.rsqrt(ms + NORM_EPS) * g


def _sigmoid(x):
    return jax.nn.sigmoid(x)


def _softplus(x):
    return jnp.maximum(x, 0.0) + jnp.log1p(jnp.exp(-jnp.abs(x)))


def _log_sigmoid(x):
    return -_softplus(-x)


def _mm(a, b, mx=bf16):
    return jnp.dot(a.astype(mx), b.astype(mx), preferred_element_type=f32)


def _mm_nt(a, b, mx=bf16):
    return lax.dot_general(a.astype(mx), b.astype(mx), (((1,), (1,)), ((), ())), preferred_element_type=f32)


def _mm_tn(a, b, mx=bf16):
    return lax.dot_general(a.astype(mx), b.astype(mx), (((0,), (0,)), ((), ())), preferred_element_type=f32)


def _split3(x):
    hi = x.astype(bf16)
    r1 = x - hi.astype(f32)
    mid = r1.astype(bf16)
    lo = (r1 - mid.astype(f32)).astype(bf16)
    return hi, mid, lo


def _cumsum_rows(x, tril_b):
    hi, mid, lo = _split3(x)
    return (jnp.dot(tril_b, hi, preferred_element_type=f32)
            + jnp.dot(tril_b, mid, preferred_element_type=f32)
            + jnp.dot(tril_b, lo, preferred_element_type=f32))


def _norm_proj_kernel(x_ref, g_ref, w_ref, o_ref, hn_ref):
    @pl.when(pl.program_id(1) == 0)
    def _():
        hn_ref[...] = _rms(x_ref[...], g_ref[...]).astype(bf16)

    o_ref[...] = jnp.dot(hn_ref[...], w_ref[...], preferred_element_type=f32)


def _norm_proj(x, g, w, tm, tn):
    n, d = x.shape
    e = w.shape[1]
    return pl.pallas_call(
        _norm_proj_kernel,
        grid=(n // tm, e // tn),
        in_specs=[pl.BlockSpec((tm, d), lambda i, j: (i, 0)),
                  pl.BlockSpec((1, d), lambda i, j: (0, 0)),
                  pl.BlockSpec((d, tn), lambda i, j: (0, j))],
        out_specs=pl.BlockSpec((tm, tn), lambda i, j: (i, j)),
        out_shape=jax.ShapeDtypeStruct((n, e), f32),
        scratch_shapes=[pltpu.VMEM((tm, d), bf16)],
        compiler_params=_cparams("parallel", "arbitrary"),
        name="norm_proj",
    )(x, g.reshape(1, d), w)


def _out_proj_kernel(*refs, gated):
    if gated:
        y_ref, gate_ref, w_ref, res_ref, o_ref = refs
        y = y_ref[...] * gate_ref[...]
    else:
        y_ref, w_ref, res_ref, o_ref = refs
        y = y_ref[...]
    o_ref[...] = res_ref[...] + jnp.dot(y.astype(bf16), w_ref[...], preferred_element_type=f32)


def _out_proj(y, w, res, tm, gate=None):
    n, e = y.shape
    d = w.shape[1]
    row = lambda i: (i, 0)
    args = [y] + ([gate] if gate is not None else []) + [w, res]
    in_specs = ([pl.BlockSpec((tm, e), row)] + ([pl.BlockSpec((tm, e), row)] if gate is not None else [])
                + [pl.BlockSpec((e, d), lambda i: (0, 0)), pl.BlockSpec((tm, d), row)])
    return pl.pallas_call(
        functools.partial(_out_proj_kernel, gated=gate is not None),
        grid=(n // tm,),
        in_specs=in_specs,
        out_specs=pl.BlockSpec((tm, d), row),
        out_shape=jax.ShapeDtypeStruct((n, d), f32),
        compiler_params=_cparams("parallel"),
        name="out_proj",
    )(*args)


def _ffn_kernel(*refs, final_norm):
    if final_norm:
        x_ref, g_ref, wg_ref, wu_ref, wd_ref, gf_ref, o_ref, hn_ref, acc_ref = refs
    else:
        x_ref, g_ref, wg_ref, wu_ref, wd_ref, o_ref, hn_ref, acc_ref = refs
    j = pl.program_id(1)

    @pl.when(j == 0)
    def _():
        hn_ref[...] = _rms(x_ref[...], g_ref[...]).astype(bf16)
        acc_ref[...] = jnp.zeros_like(acc_ref)

    h = hn_ref[...]
    gt = jnp.dot(h, wg_ref[...], preferred_element_type=f32)
    up = jnp.dot(h, wu_ref[...], preferred_element_type=f32)
    act = (gt * _sigmoid(gt) * up).astype(bf16)
    acc_ref[...] += jnp.dot(act, wd_ref[...], preferred_element_type=f32)

    @pl.when(j == pl.num_programs(1) - 1)
    def _():
        out = x_ref[...] + acc_ref[...]
        o_ref[...] = _rms(out, gf_ref[...]) if final_norm else out


def _ffn(x, g, w_gu, w_down, tm, tf, final_gain=None):
    n, d = x.shape
    nf = FFN_HIDDEN // tf
    final_norm = final_gain is not None
    in_specs = [pl.BlockSpec((tm, d), lambda i, j: (i, 0)),
                pl.BlockSpec((1, d), lambda i, j: (0, 0)),
                pl.BlockSpec((d, tf), lambda i, j: (0, j)),
                pl.BlockSpec((d, tf), lambda i, j: (0, j + nf)),
                pl.BlockSpec((tf, d), lambda i, j: (j, 0))]
    args = [x, g.reshape(1, d), w_gu, w_gu, w_down]
    if final_norm:
        in_specs.append(pl.BlockSpec((1, d), lambda i, j: (0, 0)))
        args.append(final_gain.reshape(1, d))
    return pl.pallas_call(
        functools.partial(_ffn_kernel, final_norm=final_norm),
        grid=(n // tm, nf),
        in_specs=in_specs,
        out_specs=pl.BlockSpec((tm, d), lambda i, j: (i, 0)),
        out_shape=jax.ShapeDtypeStruct((n, d), f32),
        scratch_shapes=[pltpu.VMEM((tm, d), bf16), pltpu.VMEM((tm, d), f32)],
        compiler_params=_cparams("parallel", "arbitrary"),
        name="ffn",
    )(*args)


def _rmsnorm_kernel(x_ref, g_ref, o_ref):
    o_ref[...] = _rms(x_ref[...], g_ref[...])


def _rmsnorm(x, g, tm):
    n, d = x.shape
    return pl.pallas_call(
        _rmsnorm_kernel,
        grid=(n // tm,),
        in_specs=[pl.BlockSpec((tm, d), lambda i: (i, 0)), pl.BlockSpec((1, d), lambda i: (0, 0))],
        out_specs=pl.BlockSpec((tm, d), lambda i: (i, 0)),
        out_shape=jax.ShapeDtypeStruct((n, d), f32),
        compiler_params=_cparams("parallel"),
        name="rmsnorm",
    )(x, g.reshape(1, d))


def _mlstm_kernel(*refs, L, nchunks, has_init, mx):
    H, DK, DV = MLSTM_HEADS, MLSTM_DK, MLSTM_DV
    q_ref, k_ref, v_ref, o_ref, gt_ref, bias_ref, gain_ref = refs[:7]
    if has_init:
        c0_ref, n0_ref, m0_ref = refs[7:10]
        rest = refs[10:]
    else:
        rest = refs[7:]
    y_ref, c_ref, n_ref, m_ref, c_s, n_s, m_s = rest
    j = pl.program_id(1)

    @pl.when(j == 0)
    def _():
        if has_init:
            c_s[...] = c0_ref[0]
            n_s[...] = n0_ref[0]
            m_s[...] = m0_ref[0]
        else:
            c_s[...] = jnp.zeros_like(c_s)
            n_s[...] = jnp.zeros_like(n_s)
            m_s[...] = jnp.zeros_like(m_s)

    Lp = max(L, BF16_ROWS)
    rowi = lax.broadcasted_iota(jnp.int32, (Lp, Lp), 0)
    coli = lax.broadcasted_iota(jnp.int32, (Lp, Lp), 1)
    causal = coli <= rowi
    eye = coli == rowi
    scale = DK ** -0.5

    def to_row(col):
        return jnp.sum(jnp.where(eye, col, 0.0), axis=0, keepdims=True)

    def pad_rows(x, value=0.0):
        if Lp == L:
            return x
        return jnp.concatenate([x, jnp.full((Lp - L, x.shape[1]), value, f32)], axis=0)

    def chunk(c, carry):
        rows = pl.ds(pl.multiple_of(c * L, L), L)
        gts_raw = gt_ref[rows, :] + bias_ref[...]
        lsg = pad_rows(_log_sigmoid(gts_raw))
        gts = pad_rows(gts_raw, -jnp.inf)
        for h in range(H):
            i_col = gts[:, h:h + 1]
            f_col = lsg[:, H + h:H + h + 1]
            f_row = to_row(f_col)
            fc_col = jnp.sum(jnp.where(causal, f_row, 0.0), axis=1, keepdims=True)
            fc_row = to_row(fc_col)
            i_row = to_row(i_col)
            m = m_s[h]
            dm = jnp.where(causal, fc_col - fc_row + i_row, -jnp.inf)
            g = fc_col + m
            mt = jnp.maximum(g, jnp.max(dm, axis=1, keepdims=True))
            p = jnp.exp(dm - mt)
            qc = pad_rows(q_ref[rows, h * DK:(h + 1) * DK] * scale)
            kc = pad_rows(k_ref[rows, h * DK:(h + 1) * DK])
            vc = pad_rows(v_ref[rows, h * DV:(h + 1) * DV])
            s = _mm_nt(qc, kc, mx) * p
            eg = jnp.exp(g - mt)
            cst = c_s[h]
            nst = n_s[h]
            num = eg * _mm_nt(qc, cst, mx) + _mm(s, vc, mx)
            den = eg * jnp.sum(qc * nst, axis=1, keepdims=True) + jnp.sum(s, axis=1, keepdims=True)
            hh = num / jnp.maximum(jnp.abs(den), jnp.exp(-mt))
            m_new = mt[L - 1:L, :]
            f_last = fc_col[L - 1:L, :]
            w_c = jnp.exp(f_last + m - m_new)
            w_j = jnp.exp(f_last - fc_col + i_col - m_new)
            c_s[h] = w_c * cst + _mm_tn(vc * w_j, kc, mx)
            n_s[h] = w_c * nst + jnp.sum(w_j * kc, axis=0, keepdims=True)
            m_s[h] = m_new
            hh = hh[:L]
            yn = hh * lax.rsqrt(jnp.mean(hh * hh, axis=1, keepdims=True) + NORM_EPS)
            yn = yn * gain_ref[:, h * DV:(h + 1) * DV]
            y_ref[rows, h * DV:(h + 1) * DV] = _sigmoid(o_ref[rows, h * DV:(h + 1) * DV]) * yn
        return carry

    lax.fori_loop(0, nchunks, chunk, 0)

    @pl.when(j == pl.num_programs(1) - 1)
    def _():
        c_ref[0] = c_s[...]
        n_ref[0] = n_s[...]
        m_ref[0] = m_s[...]


def _mlstm_mix(pr, b_gates, gain, B, T, state, tq):
    H, DK, DV = MLSTM_HEADS, MLSTM_DK, MLSTM_DV
    L = _chunk_len(T, MLSTM_CHUNK)
    nj = T // tq
    has_init = state is not None
    mx = bf16
    rmap = lambda c: (lambda b, j: (b * nj + j, c))
    bmap4 = lambda b, j: (b, 0, 0, 0)
    bias = jnp.zeros((1, LANES), f32).at[0, :2 * H].set(b_gates)
    in_specs = [pl.BlockSpec((tq, H * DK), rmap(0)), pl.BlockSpec((tq, H * DK), rmap(1)),
                pl.BlockSpec((tq, H * DV), rmap(1)), pl.BlockSpec((tq, H * DV), rmap(2)),
                pl.BlockSpec((tq, LANES), rmap((2 * H * DK + 2 * H * DV) // LANES)),
                pl.BlockSpec((1, LANES), lambda b, j: (0, 0)),
                pl.BlockSpec((1, H * DV), lambda b, j: (0, 0))]
    args = [pr, pr, pr, pr, pr, bias, gain.reshape(1, H * DV)]
    if has_init:
        c0, n0, m0 = state
        in_specs += [pl.BlockSpec((1, H, DV, DK), bmap4), pl.BlockSpec((1, H, 1, DK), bmap4),
                     pl.BlockSpec((1, H, 1, 1), bmap4)]
        args += [c0, n0.reshape(B, H, 1, DK), m0.reshape(B, H, 1, 1)]
    y, c, n, m = pl.pallas_call(
        functools.partial(_mlstm_kernel, L=L, nchunks=tq // L, has_init=has_init, mx=mx),
        grid=(B, nj),
        in_specs=in_specs,
        out_specs=[pl.BlockSpec((tq, H * DV), rmap(0)), pl.BlockSpec((1, H, DV, DK), bmap4),
                   pl.BlockSpec((1, H, 1, DK), bmap4), pl.BlockSpec((1, H, 1, 1), bmap4)],
        out_shape=[jax.ShapeDtypeStruct((B * T, H * DV), f32), jax.ShapeDtypeStruct((B, H, DV, DK), f32),
                   jax.ShapeDtypeStruct((B, H, 1, DK), f32), jax.ShapeDtypeStruct((B, H, 1, 1), f32)],
        scratch_shapes=[pltpu.VMEM((H, DV, DK), f32), pltpu.VMEM((H, 1, DK), f32), pltpu.VMEM((H, 1, 1), f32)],
        compiler_params=_cparams("parallel", "arbitrary"),
        name="mlstm_mix",
    )(*args)
    return y, c, n.reshape(B, H, DK), m.reshape(B, H)


def _mlstm_block_kernel(q_ref, k_ref, v_ref, o_ref, gt_ref, bias_ref, gain_ref, x_ref, wout_ref,
                        xo_ref, c_ref, n_ref, m_ref, c_s, n_s, m_s, y_ref, *, L, nc):
    H, DK, DV = MLSTM_HEADS, MLSTM_DK, MLSTM_DV
    P = 2 * L
    j = pl.program_id(1)

    @pl.when(j == 0)
    def _():
        c_s[...] = jnp.zeros_like(c_s)
        n_s[...] = jnp.zeros_like(n_s)
        m_s[...] = jnp.zeros_like(m_s)

    rowi = lax.broadcasted_iota(jnp.int32, (L, L), 0)
    coli = lax.broadcasted_iota(jnp.int32, (L, L), 1)
    causal = coli <= rowi
    r2 = lax.broadcasted_iota(jnp.int32, (P, P), 0)
    c2 = lax.broadcasted_iota(jnp.int32, (P, P), 1)
    pair_tril = ((c2 <= r2) & ((c2 >= L) == (r2 >= L))).astype(bf16)
    scale = DK ** -0.5

    gates = []
    for pp in range(nc // 2):
        g = gt_ref[pp * P:(pp + 1) * P, :] + bias_ref[...]
        hi, mid, lo = _split3(_log_sigmoid(g))
        fcum = (jnp.dot(pair_tril, hi, preferred_element_type=f32)
                + jnp.dot(pair_tril, mid, preferred_element_type=f32)
                + jnp.dot(pair_tril, lo, preferred_element_type=f32))
        g_t = g.T
        f_t = fcum.T
        for half in range(2):
            rs = slice(half * L, (half + 1) * L)
            gates.append((g[rs, :], fcum[rs, :], g_t[:, rs], f_t[:, rs]))

    pairs = [(c, h) for c in range(nc) for h in range(H)]
    rows_of = lambda c: slice(c * L, (c + 1) * L)
    ks_of = lambda h: slice(h * DK, (h + 1) * DK)
    vs_of = lambda h: slice(h * DV, (h + 1) * DV)
    ones_b = jnp.ones((L, LANES), bf16)
    wide = lambda x: jnp.concatenate([x] * (DV // LANES), axis=1)
    dms, rmax, fcols, icols, qks, qbs = {}, {}, {}, {}, {}, {}
    for c, h in pairs:
        g_c, f_c, g_r, f_r = gates[c]
        icols[c, h] = jnp.broadcast_to(g_c[:, h:h + 1], (L, LANES))
        fcols[c, h] = jnp.broadcast_to(f_c[:, H + h:H + h + 1], (L, LANES))
        dm = jnp.where(causal, fcols[c, h][:, :L] - f_r[H + h:H + h + 1, :] + g_r[h:h + 1, :], -jnp.inf)
        dms[c, h] = dm
        rmax[c, h] = jnp.broadcast_to(jnp.max(dm, axis=1, keepdims=True), (L, LANES))
        qbs[c, h] = (q_ref[rows_of(c), ks_of(h)] * scale).astype(bf16)
        qks[c, h] = _mm_nt(qbs[c, h], k_ref[rows_of(c), ks_of(h)])
    m_in, m_out = {}, {}
    for h in range(H):
        m = jnp.broadcast_to(m_s[h], (1, LANES))
        for c in range(nc):
            m_in[c, h] = m
            m = jnp.maximum(fcols[c, h][L - 1:L, :] + m, rmax[c, h][L - 1:L, :])
            m_out[c, h] = m
        m_s[h] = m[:, :1]
    mts, egs, ss, ssums, wcs, dcs, dns = {}, {}, {}, {}, {}, {}, {}
    for c, h in pairs:
        fc_col = fcols[c, h]
        gg = fc_col + m_in[c, h]
        mt = jnp.maximum(gg, rmax[c, h])
        mts[c, h] = mt
        egs[c, h] = jnp.exp(gg - mt)
        s = (qks[c, h] * jnp.exp(dms[c, h] - mt[:, :L])).astype(bf16)
        ss[c, h] = s
        ssums[c, h] = jnp.dot(s, ones_b, preferred_element_type=f32)
        f_last = fc_col[L - 1:L, :]
        wcs[c, h] = jnp.exp(f_last + m_in[c, h] - m_out[c, h])
        w_j = jnp.exp(f_last - fc_col + icols[c, h] - m_out[c, h])
        kc = k_ref[rows_of(c), ks_of(h)]
        dcs[c, h] = _mm_tn(v_ref[rows_of(c), vs_of(h)] * wide(w_j), kc)
        dns[c, h] = jnp.sum(w_j * kc, axis=0, keepdims=True)
    c_in, n_in = {}, {}
    for h in range(H):
        cst = c_s[h]
        nst = n_s[h]
        for c in range(nc):
            c_in[c, h] = cst.astype(bf16)
            n_in[c, h] = nst
            cst = wcs[c, h] * cst + dcs[c, h]
            nst = wcs[c, h] * nst + dns[c, h]
        c_s[h] = cst
        n_s[h] = nst
    hhs, msq = {}, {}
    mean_b = jnp.full((DV, LANES), 1.0 / DV, bf16)
    for c, h in pairs:
        rows, vs = rows_of(c), vs_of(h)
        qn = _mm_nt(qbs[c, h], jnp.broadcast_to(n_in[c, h], (LANES, DK)))
        num = wide(egs[c, h]) * _mm_nt(qbs[c, h], c_in[c, h]) + _mm(ss[c, h], v_ref[rows, vs])
        den = egs[c, h] * qn + ssums[c, h]
        hh = num / wide(jnp.maximum(jnp.abs(den), jnp.exp(-mts[c, h])))
        hhs[c, h] = hh
        msq[c, h] = jnp.dot((hh * hh).astype(bf16), mean_b, preferred_element_type=f32)
    for c, h in pairs:
        rows, vs = rows_of(c), vs_of(h)
        yn = hhs[c, h] * wide(lax.rsqrt(msq[c, h] + NORM_EPS)) * gain_ref[:, vs]
        y_ref[rows, vs] = _sigmoid(o_ref[rows, vs]) * yn
    xo_ref[...] = x_ref[...] + jnp.dot(y_ref[...].astype(bf16), wout_ref[...], preferred_element_type=f32)

    @pl.when(j == pl.num_programs(1) - 1)
    def _():
        c_ref[0] = c_s[...]
        n_ref[0] = n_s[...]
        m_ref[0] = m_s[...]


def _mlstm_mix_fresh(pr, b_gates, gain, x, w_out, B, T, tq):
    H, DK, DV = MLSTM_HEADS, MLSTM_DK, MLSTM_DV
    L = MLSTM_CHUNK
    nj = T // tq
    rmap = lambda c: (lambda b, j: (b * nj + j, c))
    bmap4 = lambda b, j: (b, 0, 0, 0)
    bias = jnp.zeros((1, LANES), f32).at[0, :2 * H].set(b_gates)
    y, c, n, m = pl.pallas_call(
        functools.partial(_mlstm_block_kernel, L=L, nc=tq // L),
        grid=(B, nj),
        in_specs=[pl.BlockSpec((tq, H * DK), rmap(0)), pl.BlockSpec((tq, H * DK), rmap(1)),
                  pl.BlockSpec((tq, H * DV), rmap(1)), pl.BlockSpec((tq, H * DV), rmap(2)),
                  pl.BlockSpec((tq, LANES), rmap((2 * H * DK + 2 * H * DV) // LANES)),
                  pl.BlockSpec((1, LANES), lambda b, j: (0, 0)),
                  pl.BlockSpec((1, H * DV), lambda b, j: (0, 0)),
                  pl.BlockSpec((tq, D_MODEL), rmap(0)),
                  pl.BlockSpec((H * DV, D_MODEL), lambda b, j: (0, 0))],
        out_specs=[pl.BlockSpec((tq, D_MODEL), rmap(0)), pl.BlockSpec((1, H, DV, DK), bmap4),
                   pl.BlockSpec((1, H, 1, DK), bmap4), pl.BlockSpec((1, H, 1, 1), bmap4)],
        out_shape=[jax.ShapeDtypeStruct((B * T, D_MODEL), f32), jax.ShapeDtypeStruct((B, H, DV, DK), f32),
                   jax.ShapeDtypeStruct((B, H, 1, DK), f32), jax.ShapeDtypeStruct((B, H, 1, 1), f32)],
        scratch_shapes=[pltpu.VMEM((H, DV, DK), f32), pltpu.VMEM((H, 1, DK), f32), pltpu.VMEM((H, 1, 1), f32),
                        pltpu.VMEM((tq, H * DV), f32)],
        compiler_params=_cparams("parallel", "arbitrary"),
        name="mlstm_mix_fresh",
    )(pr, pr, pr, pr, pr, bias, gain.reshape(1, H * DV), x, w_out)
    return y, c, n.reshape(B, H, DK), m.reshape(B, H)


def _mlstm_seq_kernel(q_ref, k_ref, v_ref, o_ref, gt_ref, bias_ref, gain_ref, c0_ref, n0_ref, m0_ref,
                      y_ref, c_ref, n_ref, m_ref, *, T, bs):
    H, DK, DV = MLSTM_HEADS, MLSTM_DK, MLSTM_DV
    P = BF16_ROWS
    R = bs * T
    ri = lax.broadcasted_iota(jnp.int32, (R, R), 0)
    ci = lax.broadcasted_iota(jnp.int32, (R, R), 1)
    tril_b = (((ri // T) == (ci // T)) & (ci <= ri)).astype(bf16)
    rp = lax.broadcasted_iota(jnp.int32, (P, P), 0)
    cp = lax.broadcasted_iota(jnp.int32, (P, P), 1)
    causal = cp <= rp
    real_col = lax.broadcasted_iota(jnp.int32, (1, P), 1) < T
    scale = DK ** -0.5
    ones_b = jnp.ones((P, LANES), bf16)
    mean_b = jnp.full((DV, LANES), 1.0 / DV, bf16)
    wide = lambda x: jnp.concatenate([x] * (DV // LANES), axis=1)

    g = gt_ref[...] + bias_ref[...]
    fcum = _cumsum_rows(_log_sigmoid(g), tril_b)
    zrows = jnp.zeros((LANES - R, LANES), f32)
    g_t = jnp.concatenate([g, zrows], axis=0).T
    f_t = jnp.concatenate([fcum, zrows], axis=0).T

    def pad_rows(x, value=0.0):
        return jnp.concatenate([x, jnp.full((P - T, x.shape[1]), value, f32)], axis=0)

    pairs = [(c, h) for c in range(bs) for h in range(H)]
    ks_of = lambda h: slice(h * DK, (h + 1) * DK)
    vs_of = lambda h: slice(h * DV, (h + 1) * DV)
    st = {}
    for c, h in pairs:
        rows = slice(c * T, (c + 1) * T)
        last = slice((c + 1) * T - 1, (c + 1) * T)
        i_col = jnp.broadcast_to(pad_rows(g[rows, h:h + 1], -jnp.inf), (P, LANES))
        f_last = jnp.broadcast_to(fcum[last, H + h:H + h + 1], (1, LANES))
        fc_col = jnp.concatenate([jnp.broadcast_to(fcum[rows, H + h:H + h + 1], (T, LANES)),
                                  jnp.broadcast_to(f_last, (P - T, LANES))], axis=0)
        i_row = jnp.where(real_col, g_t[h:h + 1, c * T:c * T + P], -jnp.inf)
        fc_row = f_t[H + h:H + h + 1, c * T:c * T + P]
        dm = jnp.where(causal, fc_col[:, :P] - fc_row + i_row, -jnp.inf)
        rmax = jnp.broadcast_to(jnp.max(dm, axis=1, keepdims=True), (P, LANES))
        qb = pad_rows(q_ref[rows, ks_of(h)] * scale).astype(bf16)
        kc = pad_rows(k_ref[rows, ks_of(h)])
        vc = pad_rows(v_ref[rows, vs_of(h)])
        st[c, h] = dict(i_col=i_col, f_last=f_last, fc_col=fc_col, dm=dm, rmax=rmax, qb=qb, kc=kc, vc=vc,
                        qk=_mm_nt(qb, kc))
    for c, h in pairs:
        d = st[c, h]
        m = jnp.broadcast_to(m0_ref[c, h], (1, LANES))
        gg = d["fc_col"] + m
        mt = jnp.maximum(gg, d["rmax"])
        m_new = mt[T - 1:T, :]
        d["mt"] = mt
        d["eg"] = jnp.exp(gg - mt)
        s = (d["qk"] * jnp.exp(d["dm"] - mt[:, :P])).astype(bf16)
        d["s"] = s
        d["ssum"] = jnp.dot(s, ones_b, preferred_element_type=f32)
        w_c = jnp.exp(d["f_last"] + m - m_new)
        w_j = jnp.exp(d["f_last"] - d["fc_col"] + d["i_col"] - m_new)
        c_ref[c, h] = w_c * c0_ref[c, h] + _mm_tn(d["vc"] * wide(w_j), d["kc"])
        n_ref[c, h] = w_c * n0_ref[c, h] + jnp.sum(w_j * d["kc"], axis=0, keepdims=True)
        m_ref[c, h] = m_new[:, :1]
    for c, h in pairs:
        d = st[c, h]
        qn = _mm_nt(d["qb"], jnp.broadcast_to(n0_ref[c, h], (LANES, DK)))
        num = wide(d["eg"]) * _mm_nt(d["qb"], c0_ref[c, h]) + _mm(d["s"], d["vc"])
        den = d["eg"] * qn + d["ssum"]
        hh = num / wide(jnp.maximum(jnp.abs(den), jnp.exp(-d["mt"])))
        d["hh"] = hh
        d["msq"] = jnp.dot((hh * hh).astype(bf16), mean_b, preferred_element_type=f32)
    for c, h in pairs:
        d = st[c, h]
        rows, vs = slice(c * T, (c + 1) * T), vs_of(h)
        yn = (d["hh"] * wide(lax.rsqrt(d["msq"] + NORM_EPS)))[:T] * gain_ref[:, vs]
        y_ref[rows, vs] = _sigmoid(o_ref[rows, vs]) * yn


def _mlstm_mix_seq(pr, b_gates, gain, B, T, state, bs):
    H, DK, DV = MLSTM_HEADS, MLSTM_DK, MLSTM_DV
    R = bs * T
    rmap = lambda c: (lambda i: (i, c))
    bmap4 = lambda i: (i, 0, 0, 0)
    const2 = lambda i: (0, 0)
    bias = jnp.zeros((1, LANES), f32).at[0, :2 * H].set(b_gates)
    c0, n0, m0 = state
    y, c, n, m = pl.pallas_call(
        functools.partial(_mlstm_seq_kernel, T=T, bs=bs),
        grid=(B // bs,),
        in_specs=[pl.BlockSpec((R, H * DK), rmap(0)), pl.BlockSpec((R, H * DK), rmap(1)),
                  pl.BlockSpec((R, H * DV), rmap(1)), pl.BlockSpec((R, H * DV), rmap(2)),
                  pl.BlockSpec((R, LANES), rmap((2 * H * DK + 2 * H * DV) // LANES)),
                  pl.BlockSpec((1, LANES), const2), pl.BlockSpec((1, H * DV), const2),
                  pl.BlockSpec((bs, H, DV, DK), bmap4), pl.BlockSpec((bs, H, 1, DK), bmap4),
                  pl.BlockSpec((bs, H, 1, 1), bmap4)],
        out_specs=[pl.BlockSpec((R, H * DV), rmap(0)), pl.BlockSpec((bs, H, DV, DK), bmap4),
                   pl.BlockSpec((bs, H, 1, DK), bmap4), pl.BlockSpec((bs, H, 1, 1), bmap4)],
        out_shape=[jax.ShapeDtypeStruct((B * T, H * DV), f32), jax.ShapeDtypeStruct((B, H, DV, DK), f32),
                   jax.ShapeDtypeStruct((B, H, 1, DK), f32), jax.ShapeDtypeStruct((B, H, 1, 1), f32)],
        compiler_params=_cparams("parallel"),
        name="mlstm_mix_seq",
    )(pr, pr, pr, pr, pr, bias, gain.reshape(1, H * DV), c0, n0.reshape(B, H, 1, DK), m0.reshape(B, H, 1, 1))
    return y, c, n.reshape(B, H, DK), m.reshape(B, H)


def _gla_kernel(*refs, kind, layer, L, nchunks, H, DK, DV, has_init, mx):
    if kind == "hgrn":
        q_ref, f_ref, v_ref, g_ref, lbl_ref, gain_ref = refs[:6]
        rest = refs[6:]
    else:
        q_ref, k_ref, v_ref, g_ref, gr_ref, wup_ref, bgate_ref, gain_ref = refs[:8]
        rest = refs[8:]
    if has_init:
        s0_ref = rest[0]
        rest = rest[1:]
    y_ref, s_ref, st_s = rest
    j = pl.program_id(1)

    @pl.when(j == 0)
    def _():
        for h in range(H):
            if has_init:
                st_s[h] = s0_ref[0, h].T
            else:
                st_s[h] = jnp.zeros((DV, DK), f32)

    Lp = max(L, BF16_ROWS)
    rowi = lax.broadcasted_iota(jnp.int32, (Lp, Lp), 0)
    coli = lax.broadcasted_iota(jnp.int32, (Lp, Lp), 1)
    causal = coli <= rowi
    tril_b = causal.astype(bf16)

    def pad_rows(x):
        if Lp == L:
            return x
        return jnp.concatenate([x, jnp.zeros((Lp - L, x.shape[1]), f32)], axis=0)

    if kind == "hgrn":
        logits = lbl_ref[...]
        e = jnp.exp(logits - jnp.max(logits, axis=0, keepdims=True))
        sm = e / jnp.sum(e, axis=0, keepdims=True)
        lb_all = jnp.zeros((1, H * DK), f32)
        for li in range(layer):
            lb_all = lb_all + sm[li:li + 1, :]
    else:
        scale = DK ** -0.5

    def chunk(c, carry):
        rows = pl.ds(pl.multiple_of(c * L, L), L)
        if kind == "gla":
            gk = _mm(pad_rows(gr_ref[rows, :]), wup_ref[...], mx)[:L] + bgate_ref[...]
            ld_all = _log_sigmoid(gk) * (1.0 / GLA_GATE_NORMALIZER)
        for h in range(H):
            ks = slice(h * DK, (h + 1) * DK)
            vs = slice(h * DV, (h + 1) * DV)
            if kind == "hgrn":
                qraw = q_ref[rows, ks]
                fg = f_ref[rows, ks]
                lb = lb_all[:, ks]
                gc = jnp.log(lb + (1.0 - lb) * _sigmoid(fg))
                kc = (1.0 - lb) * _sigmoid(-fg)
                qc = qraw * _sigmoid(qraw)
            else:
                qc = q_ref[rows, ks] * scale
                kc = k_ref[rows, ks]
                gc = ld_all[:, ks]
            vc = pad_rows(v_ref[rows, vs])
            qc, kc, gc = pad_rows(qc), pad_rows(kc), pad_rows(gc)
            bc = _cumsum_rows(gc, tril_b)
            qe = qc * jnp.exp(bc)
            ke = kc * jnp.exp(-bc)
            a = jnp.where(causal, _mm_nt(qe, ke, mx), 0.0)
            st = st_s[h]
            o = (_mm_nt(qe, st, mx) + _mm(a, vc, mx))[:L]
            b_last = bc[L - 1:L, :]
            kdec = kc * jnp.exp(b_last - bc)
            st_s[h] = st * jnp.exp(b_last) + _mm_tn(vc, kdec, mx)
            yn = o * lax.rsqrt(jnp.mean(o * o, axis=1, keepdims=True) + NORM_EPS) * gain_ref[:, vs]
            gg = g_ref[rows, vs]
            y_ref[rows, vs] = yn * (gg * _sigmoid(gg))
        return carry

    lax.fori_loop(0, nchunks, chunk, 0)

    @pl.when(j == pl.num_programs(1) - 1)
    def _():
        for h in range(H):
            s_ref[0, h] = st_s[h].T


def _gla_mix(kind, pr, extra, gain, B, T, state, tq, layer=0):
    if kind == "hgrn":
        H, DK, DV = HGRN_HEADS, HGRN_DK, HGRN_DV
    else:
        H, DK, DV = GLA_HEADS, GLA_DK, GLA_DV
    L = _chunk_len(T, LIN_CHUNK)
    nj = T // tq
    has_init = state is not None
    mx = bf16
    rmap = lambda c: (lambda b, j: (b * nj + j, c))
    bmap4 = lambda b, j: (b, 0, 0, 0)
    const2 = lambda b, j: (0, 0)
    hk, hv = H * DK, H * DV
    if kind == "hgrn":
        (lb_logits,) = extra
        in_specs = [pl.BlockSpec((tq, hk), rmap(0)), pl.BlockSpec((tq, hk), rmap(1)),
                    pl.BlockSpec((tq, hv), rmap(2)), pl.BlockSpec((tq, hv), rmap(3)),
                    pl.BlockSpec((DEPTH, hk), const2), pl.BlockSpec((1, hv), const2)]
        args = [pr, pr, pr, pr, lb_logits, gain.reshape(1, hv)]
    else:
        w_up, b_gate = extra
        w_up_p = jnp.zeros((LANES, hk), f32).at[:GLA_GATE_RANK].set(w_up)
        in_specs = [pl.BlockSpec((tq, hk), rmap(0)), pl.BlockSpec((tq, hk), rmap(1)),
                    pl.BlockSpec((tq, hv), rmap(1)), pl.BlockSpec((tq, hv), rmap(2)),
                    pl.BlockSpec((tq, LANES), rmap((2 * hk + 2 * hv) // LANES)),
                    pl.BlockSpec((LANES, hk), const2), pl.BlockSpec((1, hk), const2),
                    pl.BlockSpec((1, hv), const2)]
        args = [pr, pr, pr, pr, pr, w_up_p, b_gate.reshape(1, hk), gain.reshape(1, hv)]
    if has_init:
        in_specs.append(pl.BlockSpec((1, H, DK, DV), bmap4))
        args.append(state)
    y, s = pl.pallas_call(
        functools.partial(_gla_kernel, kind=kind, layer=layer, L=L, nchunks=tq // L, H=H, DK=DK, DV=DV,
                          has_init=has_init, mx=mx),
        grid=(B, nj),
        in_specs=in_specs,
        out_specs=[pl.BlockSpec((tq, hv), rmap(0)), pl.BlockSpec((1, H, DK, DV), bmap4)],
        out_shape=[jax.ShapeDtypeStruct((B * T, hv), f32), jax.ShapeDtypeStruct((B, H, DK, DV), f32)],
        scratch_shapes=[pltpu.VMEM((H, DV, DK), f32)],
        compiler_params=_cparams("parallel", "arbitrary"),
        name=kind + "_mix",
    )(*args)
    return y, s


def _gla_seq_kernel(*refs, kind, layer, T, bs, H, DK, DV):
    if kind == "hgrn":
        q_ref, f_ref, v_ref, g_ref, lbl_ref, gain_ref = refs[:6]
        rest = refs[6:]
    else:
        q_ref, k_ref, v_ref, g_ref, gr_ref, wup_ref, bgate_ref, gain_ref = refs[:8]
        rest = refs[8:]
    s0_ref, y_ref, s_ref, qe_s, ke_s, kd_s, el_s, v_s = rest
    P = BF16_ROWS
    R = bs * T
    hk = H * DK

    ri = lax.broadcasted_iota(jnp.int32, (R, R), 0)
    ci = lax.broadcasted_iota(jnp.int32, (R, R), 1)
    same_seq = (ri // T) == (ci // T)
    tril_b = (same_seq & (ci <= ri)).astype(bf16)
    seq_b = same_seq.astype(bf16)
    rp = lax.broadcasted_iota(jnp.int32, (P, P), 0)
    cp = lax.broadcasted_iota(jnp.int32, (P, P), 1)
    causal = cp <= rp

    if kind == "hgrn":
        logits = lbl_ref[...]
        e = jnp.exp(logits - jnp.max(logits, axis=0, keepdims=True))
        sm = e / jnp.sum(e, axis=0, keepdims=True)
        lb = jnp.zeros((1, hk), f32)
        for li in range(layer):
            lb = lb + sm[li:li + 1, :]
        qraw = q_ref[...]
        fg = f_ref[...]
        gc = jnp.log(lb + (1.0 - lb) * _sigmoid(fg))
        kc = (1.0 - lb) * _sigmoid(-fg)
        qc = qraw * _sigmoid(qraw)
    else:
        gk = _mm(gr_ref[...], wup_ref[...]) + bgate_ref[...]
        gc = _log_sigmoid(gk) * (1.0 / GLA_GATE_NORMALIZER)
        qc = q_ref[...] * (DK ** -0.5)
        kc = k_ref[...]
    pieces = _split3(gc)
    bc = sum(jnp.dot(tril_b, x, preferred_element_type=f32) for x in pieces)
    bl = sum(jnp.dot(seq_b, x, preferred_element_type=f32) for x in pieces)
    qe = qc * jnp.exp(bc)
    ke = kc * jnp.exp(-bc)
    kd = kc * jnp.exp(bl - bc)
    e_hi, e_mid, e_lo = [x.astype(f32) for x in _split3(jnp.exp(bl))]
    vv = v_ref[...]
    zpad_k = jnp.zeros((P - T, hk), f32)
    zpad_v = jnp.zeros((P - T, vv.shape[1]), f32)
    prow = lax.broadcasted_iota(jnp.int32, (P, hk), 0)
    for c in range(bs):
        src = slice(c * T, (c + 1) * T)
        dst = slice(c * P, (c + 1) * P)
        qe_s[dst, :] = jnp.concatenate([qe[src], zpad_k], axis=0)
        ke_s[dst, :] = jnp.concatenate([ke[src], zpad_k], axis=0)
        kd_s[dst, :] = jnp.concatenate([kd[src], zpad_k], axis=0)
        v_s[dst, :] = jnp.concatenate([vv[src], zpad_v], axis=0)
        first = slice(c * T, c * T + 1)
        el_s[dst, :] = jnp.where(prow == 0, e_hi[first],
                                 jnp.where(prow == 1, e_mid[first], jnp.where(prow == 2, e_lo[first], 0.0)))

    ones_b = jnp.ones((P, LANES), bf16)
    pairs = [(c, h) for c in range(bs) for h in range(H)]
    rows_of = lambda c: slice(c * P, (c + 1) * P)
    ks_of = lambda h: slice(h * DK, (h + 1) * DK)
    vs_of = lambda h: slice(h * DV, (h + 1) * DV)
    qes, vbs, amats, news = {}, {}, {}, {}
    for c, h in pairs:
        rows, ks = rows_of(c), ks_of(h)
        qes[c, h] = qe_s[rows, ks].astype(bf16)
        vbs[c, h] = v_s[rows, vs_of(h)].astype(bf16)
        a = _mm_nt(qes[c, h], ke_s[rows, ks])
        amats[c, h] = jnp.where(causal, a, 0.0).astype(bf16)
        decay = _mm_tn(el_s[rows, ks], ones_b)
        decay = jnp.concatenate([decay] * (DV // LANES), axis=1)
        news[c, h] = s0_ref[c, h] * decay + _mm_tn(kd_s[rows, ks], vbs[c, h])
    for c, h in pairs:
        s_ref[c, h] = news[c, h]
    outs = {}
    for c, h in pairs:
        outs[c, h] = (_mm(qes[c, h], s0_ref[c, h]) + _mm(amats[c, h], vbs[c, h]))[:T]
    for c, h in pairs:
        vs = vs_of(h)
        rows = slice(c * T, (c + 1) * T)
        o = outs[c, h]
        yn = o * lax.rsqrt(jnp.mean(o * o, axis=1, keepdims=True) + NORM_EPS) * gain_ref[:, vs]
        gg = g_ref[rows, vs]
        y_ref[rows, vs] = yn * (gg * _sigmoid(gg))


def _gla_mix_seq(kind, pr, extra, gain, B, T, state, bs, layer=0):
    if kind == "hgrn":
        H, DK, DV = HGRN_HEADS, HGRN_DK, HGRN_DV
    else:
        H, DK, DV = GLA_HEADS, GLA_DK, GLA_DV
    R = bs * T
    rmap = lambda c: (lambda i: (i, c))
    bmap4 = lambda i: (i, 0, 0, 0)
    const2 = lambda i: (0, 0)
    hk, hv = H * DK, H * DV
    if kind == "hgrn":
        (lb_logits,) = extra
        in_specs = [pl.BlockSpec((R, hk), rmap(0)), pl.BlockSpec((R, hk), rmap(1)),
                    pl.BlockSpec((R, hv), rmap(2)), pl.BlockSpec((R, hv), rmap(3)),
                    pl.BlockSpec((DEPTH, hk), const2), pl.BlockSpec((1, hv), const2)]
        args = [pr, pr, pr, pr, lb_logits, gain.reshape(1, hv)]
    else:
        w_up, b_gate = extra
        w_up_p = jnp.zeros((LANES, hk), f32).at[:GLA_GATE_RANK].set(w_up)
        in_specs = [pl.BlockSpec((R, hk), rmap(0)), pl.BlockSpec((R, hk), rmap(1)),
                    pl.BlockSpec((R, hv), rmap(1)), pl.BlockSpec((R, hv), rmap(2)),
                    pl.BlockSpec((R, LANES), rmap((2 * hk + 2 * hv) // LANES)),
                    pl.BlockSpec((LANES, hk), const2), pl.BlockSpec((1, hk), const2),
                    pl.BlockSpec((1, hv), const2)]
        args = [pr, pr, pr, pr, pr, w_up_p, b_gate.reshape(1, hk), gain.reshape(1, hv)]
    in_specs.append(pl.BlockSpec((bs, H, DK, DV), bmap4))
    args.append(state)
    P = BF16_ROWS
    return pl.pallas_call(
        functools.partial(_gla_seq_kernel, kind=kind, layer=layer, T=T, bs=bs, H=H, DK=DK, DV=DV),
        grid=(B // bs,),
        in_specs=in_specs,
        out_specs=[pl.BlockSpec((R, hv), rmap(0)), pl.BlockSpec((bs, H, DK, DV), bmap4)],
        out_shape=[jax.ShapeDtypeStruct((B * T, hv), f32), jax.ShapeDtypeStruct((B, H, DK, DV), f32)],
        scratch_shapes=[pltpu.VMEM((bs * P, hk), f32)] * 4 + [pltpu.VMEM((bs * P, hv), f32)],
        compiler_params=_cparams("parallel"),
        name=kind + "_mix_seq",
    )(*args)


def _gla_block_kernel(*refs, kind, layer, L, nc, H, DK, DV):
    if kind == "hgrn":
        q_ref, f_ref, v_ref, g_ref, lbl_ref, gain_ref = refs[:6]
        rest = refs[6:]
    else:
        q_ref, k_ref, v_ref, g_ref, gr_ref, wup_ref, bgate_ref, gain_ref = refs[:8]
        rest = refs[8:]
    x_ref, wout_ref, xo_ref, s_ref, st_s, qe_s, ke_s, qs_s, kd_s, el_s, y_ref = rest
    j = pl.program_id(1)

    @pl.when(j == 0)
    def _():
        st_s[...] = jnp.zeros_like(st_s)

    rowi = lax.broadcasted_iota(jnp.int32, (L, L), 0)
    coli = lax.broadcasted_iota(jnp.int32, (L, L), 1)
    causal = coli <= rowi
    tril_b = causal.astype(bf16)

    if kind == "hgrn":
        logits = lbl_ref[...]
        e = jnp.exp(logits - jnp.max(logits, axis=0, keepdims=True))
        sm = e / jnp.sum(e, axis=0, keepdims=True)
        lb = jnp.zeros((1, H * DK), f32)
        for li in range(layer):
            lb = lb + sm[li:li + 1, :]
    else:
        scale = DK ** -0.5

    for c in range(nc):
        rows = slice(c * L, (c + 1) * L)
        if kind == "hgrn":
            qraw = q_ref[rows, :]
            fg = f_ref[rows, :]
            gc = jnp.log(lb + (1.0 - lb) * _sigmoid(fg))
            kc = (1.0 - lb) * _sigmoid(-fg)
            qc = qraw * _sigmoid(qraw)
        else:
            gk = jnp.dot(gr_ref[rows, :].astype(bf16), wup_ref[...].astype(bf16),
                         preferred_element_type=f32) + bgate_ref[...]
            gc = _log_sigmoid(gk) * (1.0 / GLA_GATE_NORMALIZER)
            qc = q_ref[rows, :] * scale
            kc = k_ref[rows, :]
        hi, mid, lo = _split3(gc)
        bc = (jnp.dot(tril_b, hi, preferred_element_type=f32) + jnp.dot(tril_b, mid, preferred_element_type=f32)
              + jnp.dot(tril_b, lo, preferred_element_type=f32))
        b_mid = bc[L // 2 - 1:L // 2, :]
        b_last = bc[L - 1:L, :]
        qe = qc * jnp.exp(bc - b_mid)
        ke = kc * jnp.exp(b_mid - bc)
        qe_s[rows, :] = qe.astype(bf16)
        ke_s[rows, :] = ke.astype(bf16)
        qs_s[rows, :] = (qe * jnp.exp(b_mid)).astype(bf16)
        kd_s[rows, :] = (ke * jnp.exp(b_last - b_mid)).astype(bf16)
        el_s[c] = jnp.exp(b_last)

    dn_nt = (((1,), (1,)), ((), ()))
    dn_tn = (((0,), (0,)), ((), ()))
    pairs = [(c, h) for c in range(nc) for h in range(H)]
    rows_of = lambda c: slice(c * L, (c + 1) * L)
    ks_of = lambda h: slice(h * DK, (h + 1) * DK)
    vs_of = lambda h: slice(h * DV, (h + 1) * DV)
    vbs, amats, dsts = {}, {}, {}
    for c, h in pairs:
        rows, ks = rows_of(c), ks_of(h)
        vb = v_ref[rows, vs_of(h)].astype(bf16)
        a = lax.dot_general(qe_s[rows, ks], ke_s[rows, ks], dn_nt, preferred_element_type=f32)
        vbs[c, h] = vb
        amats[c, h] = jnp.where(causal, a, 0.0).astype(bf16)
        dsts[c, h] = lax.dot_general(vb, kd_s[rows, ks], dn_tn, preferred_element_type=f32)
    sts = {}
    for h in range(H):
        st = st_s[h]
        for c in range(nc):
            sts[c, h] = st.astype(bf16)
            st = st * el_s[c][:, ks_of(h)] + dsts[c, h]
        st_s[h] = st
    outs = {}
    for c, h in pairs:
        rows, ks = rows_of(c), ks_of(h)
        outs[c, h] = (lax.dot_general(qs_s[rows, ks], sts[c, h], dn_nt, preferred_element_type=f32)
                      + jnp.dot(amats[c, h], vbs[c, h], preferred_element_type=f32))
    for c, h in pairs:
        rows, vs = rows_of(c), vs_of(h)
        o = outs[c, h]
        yn = o * lax.rsqrt(jnp.mean(o * o, axis=1, keepdims=True) + NORM_EPS) * gain_ref[:, vs]
        gg = g_ref[rows, vs]
        y_ref[rows, vs] = yn * (gg * _sigmoid(gg))
    xo_ref[...] = x_ref[...] + jnp.dot(y_ref[...].astype(bf16), wout_ref[...], preferred_element_type=f32)

    @pl.when(j == pl.num_programs(1) - 1)
    def _():
        for h in range(H):
            s_ref[0, h] = st_s[h].T


def _gla_mix_fresh(kind, pr, extra, gain, x, w_out, B, T, tq, layer=0):
    if kind == "hgrn":
        H, DK, DV = HGRN_HEADS, HGRN_DK, HGRN_DV
    else:
        H, DK, DV = GLA_HEADS, GLA_DK, GLA_DV
    L = 2 * LIN_CHUNK
    nc = tq // L
    nj = T // tq
    rmap = lambda c: (lambda b, j: (b * nj + j, c))
    bmap4 = lambda b, j: (b, 0, 0, 0)
    const2 = lambda b, j: (0, 0)
    hk, hv = H * DK, H * DV
    if kind == "hgrn":
        (lb_logits,) = extra
        in_specs = [pl.BlockSpec((tq, hk), rmap(0)), pl.BlockSpec((tq, hk), rmap(1)),
                    pl.BlockSpec((tq, hv), rmap(2)), pl.BlockSpec((tq, hv), rmap(3)),
                    pl.BlockSpec((DEPTH, hk), const2), pl.BlockSpec((1, hv), const2)]
        args = [pr, pr, pr, pr, lb_logits, gain.reshape(1, hv)]
    else:
        w_up, b_gate = extra
        w_up_p = jnp.zeros((LANES, hk), f32).at[:GLA_GATE_RANK].set(w_up)
        in_specs = [pl.BlockSpec((tq, hk), rmap(0)), pl.BlockSpec((tq, hk), rmap(1)),
                    pl.BlockSpec((tq, hv), rmap(1)), pl.BlockSpec((tq, hv), rmap(2)),
                    pl.BlockSpec((tq, LANES), rmap((2 * hk + 2 * hv) // LANES)),
                    pl.BlockSpec((LANES, hk), const2), pl.BlockSpec((1, hk), const2),
                    pl.BlockSpec((1, hv), const2)]
        args = [pr, pr, pr, pr, pr, w_up_p, b_gate.reshape(1, hk), gain.reshape(1, hv)]
    in_specs += [pl.BlockSpec((tq, D_MODEL), rmap(0)), pl.BlockSpec((hv, D_MODEL), const2)]
    args += [x, w_out]
    return pl.pallas_call(
        functools.partial(_gla_block_kernel, kind=kind, layer=layer, L=L, nc=nc, H=H, DK=DK, DV=DV),
        grid=(B, nj),
        in_specs=in_specs,
        out_specs=[pl.BlockSpec((tq, D_MODEL), rmap(0)), pl.BlockSpec((1, H, DK, DV), bmap4)],
        out_shape=[jax.ShapeDtypeStruct((B * T, D_MODEL), f32), jax.ShapeDtypeStruct((B, H, DK, DV), f32)],
        scratch_shapes=[pltpu.VMEM((H, DV, DK), f32)] + [pltpu.VMEM((tq, hk), bf16)] * 4
                       + [pltpu.VMEM((nc, 1, hk), f32), pltpu.VMEM((tq, hv), f32)],
        compiler_params=_cparams("parallel", "arbitrary"),
        name=kind + "_mix_fresh",
    )(*args)


def _rwkv_pre_kernel(x_ref, aux_ref, sh_ref, g_ref, mu_ref, wrkv_ref, la_ref, lbw_ref, lba_ref, lbg_ref,
                     w0_ref, a0_ref, r_ref, w_ref, k_ref, v_ref, alr_ref, gate_ref, *, long_seq, blocks_per_seq,
                     seq_len):
    g = g_ref[...]
    hn = _rms(x_ref[...], g)
    tm = hn.shape[0]
    rowi = lax.broadcasted_iota(jnp.int32, (tm, 1), 0)
    rolled = pltpu.roll(hn, 1, axis=0)
    if long_seq:
        p_last = _rms(aux_ref[SUBLANES - 1:SUBLANES, :], g)
        at_start = (pl.program_id(0) % blocks_per_seq) == 0
        first = jnp.where(at_start, sh_ref[0], p_last)
        prev = jnp.where(rowi == 0, first, rolled)
    else:
        prev = jnp.where(rowi % seq_len == 0, aux_ref[...], rolled)
    xx = prev - hn

    def lerp(c):
        return (hn + xx * mu_ref[c:c + 1, :]).astype(bf16)

    r_ref[...] = jnp.dot(lerp(0), wrkv_ref[0], preferred_element_type=f32)
    k_ref[...] = jnp.dot(lerp(1), wrkv_ref[1], preferred_element_type=f32)
    v_ref[...] = jnp.dot(lerp(2), wrkv_ref[2], preferred_element_type=f32)
    lw = jnp.tanh(jnp.dot(lerp(3), la_ref[:, 0:64], preferred_element_type=f32))
    wl = -_softplus(-(w0_ref[...] + jnp.dot(lw.astype(bf16), lbw_ref[...], preferred_element_type=f32))) - 0.5
    w_ref[...] = jnp.exp(-jnp.exp(wl))
    la = jnp.dot(lerp(4), la_ref[:, 64:128], preferred_element_type=f32)
    alr_ref[...] = _sigmoid(a0_ref[...] + jnp.dot(la.astype(bf16), lba_ref[...], preferred_element_type=f32))
    lg = _sigmoid(jnp.dot(lerp(5), la_ref[:, 128:256], preferred_element_type=f32))
    gate_ref[...] = jnp.dot(lg.astype(bf16), lbg_ref[...], preferred_element_type=f32)


def _rwkv_pre(x, shift0, g, mu, wrkv, la, lbw, lba, lbg, w0, a0, B, T, tm):
    n, d = x.shape
    long_seq = T % tm == 0
    row = lambda i: (i, 0)
    const2 = lambda i: (0, 0)
    if long_seq:
        bps = T // tm
        sub = tm // SUBLANES
        aux = x
        aux_spec = pl.BlockSpec((SUBLANES, d), lambda i: (jnp.maximum(i * sub - 1, 0), 0))
        sh = shift0.reshape(B, 1, d)
        sh_spec = pl.BlockSpec((1, 1, d), lambda i: (i // bps, 0, 0))
    else:
        assert tm % T == 0
        bps = 1
        aux = jnp.repeat(shift0, T, axis=0)
        aux_spec = pl.BlockSpec((tm, d), row)
        sh = shift0.reshape(B, 1, d)
        sh_spec = pl.BlockSpec((1, 1, d), lambda i: (0, 0, 0))
    out = jax.ShapeDtypeStruct((n, d), f32)
    return pl.pallas_call(
        functools.partial(_rwkv_pre_kernel, long_seq=long_seq, blocks_per_seq=bps, seq_len=T),
        grid=(n // tm,),
        in_specs=[pl.BlockSpec((tm, d), row), aux_spec, sh_spec, pl.BlockSpec((1, d), const2),
                  pl.BlockSpec((6, d), const2),
                  pl.BlockSpec((3, d, d), lambda i: (0, 0, 0), pipeline_mode=pl.Buffered(1)),
                  pl.BlockSpec((d, 256), const2), pl.BlockSpec((64, d), const2), pl.BlockSpec((64, d), const2),
                  pl.BlockSpec((128, d), const2), pl.BlockSpec((1, d), const2), pl.BlockSpec((1, d), const2)],
        out_specs=[pl.BlockSpec((tm, d), row)] * 6,
        out_shape=[out] * 6,
        compiler_params=_cparams("parallel"),
        name="rwkv_pre",
    )(x, aux, sh, g.reshape(1, d), mu, wrkv, la, lbw, lba, lbg, w0.reshape(1, d), a0.reshape(1, d))


def _rwkv_scan_kernel(*refs, tt, has_init, unroll):
    N = RWKV_N
    (r_ref, w_ref, k_ref, v_ref, alr_ref, gate_ref, x_ref, wout_ref,
     kk_ref, ka_ref, rk_ref, lnw_ref, lnb_ref) = refs[:13]
    if has_init:
        s0_ref = refs[13]
        rest = refs[14:]
    else:
        rest = refs[13:]
    xo_ref, sout_ref, s_s, y_s, vec_s, z_ref, zz_s = rest
    HP = RWKV_HEADS // 2
    j = pl.program_id(1)

    @pl.when(j == 0)
    def _():
        if has_init:
            s_s[...] = s0_ref[...]
        else:
            s_s[...] = jnp.zeros_like(s_s)

    low = lax.broadcasted_iota(jnp.int32, (N, LANES), 1) < N

    def load_pair(ref, t0):
        tiles = []
        for t in (t0, t0 + 1):
            x = ref[:, t, :]
            tiles += [x[:, hp * LANES:(hp + 1) * LANES] for hp in range(HP)]
        xt = jnp.concatenate(tiles, axis=0).T
        ev, od = xt[:N], xt[N:]
        return (jnp.where(low, ev, pltpu.roll(od, N, axis=1)),
                jnp.where(low, pltpu.roll(ev, N, axis=1), od))

    def store_pair(ref, t0, z0, z1):
        ev = jnp.where(low, z0, pltpu.roll(z1, N, axis=1))
        od = jnp.where(low, pltpu.roll(z0, N, axis=1), z1)
        x = jnp.concatenate([ev, od], axis=0).T
        for i, t in enumerate((t0, t0 + 1)):
            tiles = [x[(i * HP + hp) * SUBLANES:(i * HP + hp + 1) * SUBLANES, :] for hp in range(HP)]
            ref[:, t, :] = jnp.concatenate(tiles, axis=1)

    VA, VW, VB, VK, VR, VV = range(6)

    def stage(tk, r, w, k, v, alr):
        kkraw = k * kk_ref[...]
        nrm = jnp.sqrt(jnp.sum(kkraw * kkraw, axis=0, keepdims=True))
        kk = kkraw / jnp.maximum(nrm, 1e-12)
        vec_s[tk, VA] = -kk
        vec_s[tk, VW] = w
        vec_s[tk, VB] = kk * alr
        vec_s[tk, VK] = k * (1.0 + (alr - 1.0) * ka_ref[...])
        vec_s[tk, VR] = r
        vec_s[tk, VV] = v

    def key_row(tk, which, kc):
        return vec_s[tk, which, pl.ds(kc, 1), :]

    def project(tk):
        def body(i, sa):
            for u in range(unroll):
                kc = i * unroll + u
                sa = sa + s_s[kc] * key_row(tk, VA, kc)
            return sa

        return lax.fori_loop(0, N // unroll, body, jnp.zeros((N, LANES), f32))

    def update(tk, sa):
        vv = vec_s[tk, VV]
        y = jnp.zeros((N, LANES), f32)
        for kc in range(N):
            sk = s_s[kc] * key_row(tk, VW, kc) + sa * key_row(tk, VB, kc) + vv * key_row(tk, VK, kc)
            s_s[kc] = sk
            y = y + sk * key_row(tk, VR, kc)
        y_s[tk] = y

    def epilogue(tk):
        y = y_s[tk]
        mean = jnp.mean(y, axis=0, keepdims=True)
        yc = y - mean
        var = jnp.mean(yc * yc, axis=0, keepdims=True)
        yn = yc * lax.rsqrt(var + RWKV_GN_EPS)
        bonus = jnp.sum(vec_s[tk, VR] * vec_s[tk, VK] * rk_ref[...], axis=0, keepdims=True) * vec_s[tk, VV]
        return yn * lnw_ref[...] + lnb_ref[...] + bonus

    def stage_pair(slot, t0):
        streams = [load_pair(ref, t0) for ref in (r_ref, w_ref, k_ref, v_ref, alr_ref)]
        for i in range(2):
            stage(slot + i, *[s[i] for s in streams])

    @pl.when(j == 0)
    def _():
        vec_s[...] = jnp.zeros_like(vec_s)
        y_s[...] = jnp.zeros_like(y_s)

    npairs = tt // 2
    stage_pair(0, 0)

    def step(p, carry):
        slot = 2 * (p % 2)
        prev = 2 - slot
        sa = project(slot)
        update(slot, sa)
        zz_s[0] = epilogue(prev)
        zz_s[1] = epilogue(prev + 1)
        stage_pair(prev, 2 * jnp.minimum(p + 1, npairs - 1))
        sa = project(slot + 1)
        update(slot + 1, sa)
        store_pair(z_ref, 2 * jnp.maximum(p - 1, 0), zz_s[0], zz_s[1])
        return carry

    lax.fori_loop(0, npairs, step, 0)
    last = 2 * ((npairs - 1) % 2)
    store_pair(z_ref, tt - 2, epilogue(last), epilogue(last + 1))

    rows = RWKV_SEQ_PER_STEP * tt
    zg = (z_ref[...] * gate_ref[...]).reshape(rows, D_MODEL).astype(bf16)
    proj = jnp.dot(zg, wout_ref[...], preferred_element_type=f32)
    xo_ref[...] = x_ref[...] + proj.reshape(RWKV_SEQ_PER_STEP, tt, D_MODEL)

    @pl.when(j == pl.num_programs(1) - 1)
    def _():
        sout_ref[...] = s_s[...]


RWKV_SEQ_PER_STEP = LANES // RWKV_HEADS


def _rwkv_scan(r, w, k, v, alr, gate, x, w_out, slabs, s0, B, T, tt):
    N = RWKV_N
    G = B // RWKV_SEQ_PER_STEP
    has_init = s0 is not None
    view = lambda a: a.reshape(B, T, D_MODEL)
    tmap = lambda g, j: (g, j, 0)
    smap = lambda g, j: (0, 0, g)
    const2 = lambda g, j: (0, 0)
    stream = pl.BlockSpec((RWKV_SEQ_PER_STEP, tt, D_MODEL), tmap)
    in_specs = [stream] * 7 + [pl.BlockSpec((D_MODEL, D_MODEL), const2)] + [pl.BlockSpec((N, LANES), const2)] * 5
    args = [view(r), view(w), view(k), view(v), view(alr), view(gate), view(x), w_out] + list(slabs)
    if has_init:
        in_specs.append(pl.BlockSpec((N, N, LANES), smap))
        args.append(s0)
    z, s = pl.pallas_call(
        functools.partial(_rwkv_scan_kernel, tt=tt, has_init=has_init, unroll=16),
        grid=(G, T // tt),
        in_specs=in_specs,
        out_specs=[stream, pl.BlockSpec((N, N, LANES), smap)],
        out_shape=[jax.ShapeDtypeStruct((B, T, D_MODEL), f32),
                   jax.ShapeDtypeStruct((N, N, G * LANES), f32)],
        scratch_shapes=[pltpu.VMEM((N, N, LANES), f32), pltpu.VMEM((4, N, LANES), f32),
                        pltpu.VMEM((4, 6, N, LANES), f32), pltpu.VMEM((RWKV_SEQ_PER_STEP, tt, D_MODEL), f32),
                        pltpu.VMEM((2, N, LANES), f32)],
        compiler_params=_cparams("parallel", "arbitrary"),
        name="rwkv_scan",
    )(*args)
    return z.reshape(B * T, D_MODEL), s


def _lane_slab(p):
    q = p.reshape(RWKV_HEADS // 2, 2, RWKV_N).transpose(2, 1, 0)
    q = jnp.broadcast_to(q[:, :, :, None], (RWKV_N, 2, RWKV_HEADS // 2, RWKV_SEQ_PER_STEP))
    return q.reshape(RWKV_N, LANES)


def _rwkv_mix(x, p, B, T, state, tm):
    H, N = RWKV_HEADS, RWKV_N
    G = B // RWKV_SEQ_PER_STEP
    if state is None:
        shift0 = jnp.zeros((B, D_MODEL), f32)
        s0 = None
    else:
        s_in, shift0 = state
        s0 = (s_in.reshape(G, RWKV_SEQ_PER_STEP, H // 2, 2, N, N)
              .transpose(5, 4, 0, 3, 2, 1).reshape(N, N, G * LANES))
    r, w, k, v, alr, gate = _rwkv_pre(x, shift0, p["g"], p["mu"], p["wrkv"], p["la"], p["lbw"], p["lba"], p["lbg"],
                                      p["w0"], p["a0"], B, T, tm)
    slabs = [_lane_slab(p[n]) for n in ("k_k", "k_a", "r_k", "ln_w", "ln_b")]
    x_new, s = _rwkv_scan(r, w, k, v, alr, gate, x, p["w_out"], slabs, s0, B, T, min(T, 64))
    s = (s.reshape(N, N, G, 2, H // 2, RWKV_SEQ_PER_STEP)
         .transpose(2, 5, 4, 3, 1, 0).reshape(B, H, N, N))
    return x_new, s


def _pad_cols(w, to):
    return jnp.pad(w, ((0, 0), (0, to - w.shape[1])))


def _trunk(x, B, T, states, p):
    n = x.shape[0]
    tm = min(512, n)
    tmm = min(1024, n)
    tq = min(256, T)
    fresh = states is None
    block_ok = fresh and T % tq == 0 and tq % (2 * LIN_CHUNK) == 0
    seq_ok = (not fresh) and T == SUBLANES and T <= LIN_CHUNK and B % SEQS_PER_STEP == 0
    if not fresh:
        m_c, m_n, m_m, h_s, g_s, r_s, r_sh = states
    new = {}
    for li in range(DEPTH):
        g_mix = p["norm_mix"][li]
        if li == 0:
            pr = _norm_proj(x, g_mix, p["mlstm_w_in"], tmm, 1664)
            if fresh and T % tq == 0 and tq % (2 * MLSTM_CHUNK) == 0:
                x, c, nn, m = _mlstm_mix_fresh(pr, p["mlstm_b_gates"], p["mlstm_norm"], x, p["mlstm_w_out"],
                                               B, T, tq)
            else:
                if seq_ok:
                    y, c, nn, m = _mlstm_mix_seq(pr, p["mlstm_b_gates"], p["mlstm_norm"], B, T,
                                                 (m_c[0], m_n[0], m_m[0]), SEQS_PER_STEP)
                else:
                    y, c, nn, m = _mlstm_mix(pr, p["mlstm_b_gates"], p["mlstm_norm"], B, T,
                                             None if fresh else (m_c[0], m_n[0], m_m[0]), tq)
                x = _out_proj(y, p["mlstm_w_out"], x, tm)
            new["C"], new["n"], new["m"] = c[None], nn[None], m[None]
        elif li == 1:
            pr = _norm_proj(x, g_mix, p["hgrn_w_in"], tmm, 2048)
            if block_ok:
                x, s = _gla_mix_fresh("hgrn", pr, (p["hgrn_lb_logits"],), p["hgrn_norm"], x, p["hgrn_w_out"],
                                      B, T, tq, layer=li)
            else:
                if seq_ok:
                    y, s = _gla_mix_seq("hgrn", pr, (p["hgrn_lb_logits"],), p["hgrn_norm"], B, T, h_s[0],
                                        SEQS_PER_STEP, layer=li)
                else:
                    y, s = _gla_mix("hgrn", pr, (p["hgrn_lb_logits"],), p["hgrn_norm"], B, T,
                                    None if fresh else h_s[0], tq, layer=li)
                x = _out_proj(y, p["hgrn_w_out"], x, tm)
            new["hS"] = s[None]
        elif li == 2:
            pr = _norm_proj(x, g_mix, p["gla_w_in"], tmm, 1664)
            if block_ok:
                x, s = _gla_mix_fresh("gla", pr, (p["gla_w_gate_up"], p["gla_b_gate"]), p["gla_norm"], x,
                                      p["gla_w_out"], B, T, tq)
            else:
                if seq_ok:
                    y, s = _gla_mix_seq("gla", pr, (p["gla_w_gate_up"], p["gla_b_gate"]), p["gla_norm"], B, T,
                                        g_s[0], SEQS_PER_STEP)
                else:
                    y, s = _gla_mix("gla", pr, (p["gla_w_gate_up"], p["gla_b_gate"]), p["gla_norm"], B, T,
                                    None if fresh else g_s[0], tq)
                x = _out_proj(y, p["gla_w_out"], x, tm)
            new["gS"] = s[None]
        else:
            rp = dict(p["rwkv"], g=g_mix, w_out=p["rwkv_w_out"])
            x_last = x.reshape(B, T, D_MODEL)[:, T - 1, :]
            new["sh"] = _rmsnorm(x_last, g_mix, B)[None]
            x, s = _rwkv_mix(x, rp, B, T, None if fresh else (r_s[0], r_sh[0]), tm)
            new["rS"] = s[None]
        x = _ffn(x, p["norm_ffn"][li], p["ffn_w_gate_up"][li], p["ffn_w_down"][li], tm, 1408,
                 final_gain=p["norm_final"] if li == DEPTH - 1 else None)
    return x.reshape(B, T, D_MODEL), (new["C"], new["n"], new["m"], new["hS"], new["gS"], new["rS"], new["sh"])


def kernel(x_prompt, x_sample, state_mlstm_C, state_mlstm_n, state_mlstm_m, state_hgrn_S, state_gla_S, state_rwkv_S, state_rwkv_shift, norm_mix, norm_ffn, norm_final, mlstm_w_in, mlstm_b_gates, mlstm_norm, mlstm_w_out, hgrn_w_in, hgrn_lb_logits, hgrn_norm, hgrn_w_out, gla_w_in, gla_w_gate_up, gla_b_gate, gla_norm, gla_w_out, rwkv_mu, rwkv_w_rkv, rwkv_w_lora_a, rwkv_w_lora_b, rwkv_w0, rwkv_a_lora_a, rwkv_a_lora_b, rwkv_a0, rwkv_g_lora_a, rwkv_g_lora_b, rwkv_k_k, rwkv_k_a, rwkv_r_k, rwkv_ln_w, rwkv_ln_b, rwkv_w_out, ffn_w_gate_up, ffn_w_down):
    cast = lambda w: w.astype(bf16)
    p = dict(
        norm_mix=norm_mix, norm_ffn=norm_ffn, norm_final=norm_final,
        mlstm_w_in=cast(_pad_cols(mlstm_w_in[0], 3328)), mlstm_b_gates=mlstm_b_gates[0], mlstm_norm=mlstm_norm[0],
        mlstm_w_out=cast(mlstm_w_out[0]),
        hgrn_w_in=cast(hgrn_w_in[0]), hgrn_lb_logits=hgrn_lb_logits, hgrn_norm=hgrn_norm[0],
        hgrn_w_out=cast(hgrn_w_out[0]),
        gla_w_in=cast(_pad_cols(gla_w_in[0], 3328)), gla_w_gate_up=gla_w_gate_up[0], gla_b_gate=gla_b_gate[0],
        gla_norm=gla_norm[0], gla_w_out=cast(gla_w_out[0]),
        rwkv=dict(mu=rwkv_mu[0], wrkv=cast(rwkv_w_rkv[0]),
                  la=cast(jnp.concatenate([rwkv_w_lora_a[0], rwkv_a_lora_a[0], rwkv_g_lora_a[0]], axis=1)),
                  lbw=cast(rwkv_w_lora_b[0]), lba=cast(rwkv_a_lora_b[0]), lbg=cast(rwkv_g_lora_b[0]),
                  w0=rwkv_w0[0], a0=rwkv_a0[0], k_k=rwkv_k_k[0], k_a=rwkv_k_a[0], r_k=rwkv_r_k[0].reshape(-1),
                  ln_w=rwkv_ln_w[0], ln_b=rwkv_ln_b[0]),
        rwkv_w_out=cast(rwkv_w_out[0]),
        ffn_w_gate_up=cast(ffn_w_gate_up), ffn_w_down=cast(ffn_w_down),
    )
    bp, tp, _ = x_prompt.shape
    bs, ts, _ = x_sample.shape
    y_p, st_p = _trunk(x_prompt.reshape(bp * tp, D_MODEL), bp, tp, None, p)
    y_s, st_s = _trunk(x_sample.reshape(bs * ts, D_MODEL), bs, ts,
                       (state_mlstm_C, state_mlstm_n, state_mlstm_m, state_hgrn_S, state_gla_S, state_rwkv_S,
                        state_rwkv_shift), p)
    return (y_p, y_s) + st_p + st_s
```

```python
import functools

import jax
import jax.numpy as jnp
from jax import lax
from jax.experimental import pallas as pl
from jax.experimental.pallas import tpu as pltpu

f32 = jnp.float32
bf16 = jnp.bfloat16

D_MODEL = 1024
DEPTH = 4
NORM_EPS = 1e-6

MLSTM_HEADS, MLSTM_DK, MLSTM_DV, MLSTM_CHUNK = 4, 128, 256, 64
HGRN_HEADS, HGRN_DK, HGRN_DV = 8, 128, 128
GLA_HEADS, GLA_DK, GLA_DV = 4, 128, 256
GLA_GATE_RANK = 16
GLA_GATE_NORMALIZER = 16.0
LIN_CHUNK = 32
RWKV_HEADS, RWKV_N = 16, 64
RWKV_GN_EPS = 64e-5
FFN_HIDDEN = 2816

LANES = 128
SUBLANES = 8
BF16_ROWS = 16
SEQS_PER_STEP = 8
VMEM_LIMIT_BYTES = 52 * 1024 * 1024


def _cparams(*sem):
    return pltpu.CompilerParams(dimension_semantics=sem, vmem_limit_bytes=VMEM_LIMIT_BYTES)


def _chunk_len(t, cap):
    return max(d for d in range(1, min(cap, t) + 1) if t % d == 0)


def _rms(x, g):
    ms = jnp.mean(x * x, axis=-1, keepdims=True)
    return x * lax.rsqrt(ms + NORM_EPS) * g


def _sigmoid(x):
    return jax.nn.sigmoid(x)


def _softplus(x):
    return jnp.maximum(x, 0.0) + jnp.log1p(jnp.exp(-jnp.abs(x)))


def _log_sigmoid(x):
    return -_softplus(-x)


def _mm(a, b, mx=bf16):
    return jnp.dot(a.astype(mx), b.astype(mx), preferred_element_type=f32)


def _mm_nt(a, b, mx=bf16):
    return lax.dot_general(a.astype(mx), b.astype(mx), (((1,), (1,)), ((), ())), preferred_element_type=f32)


def _mm_tn(a, b, mx=bf16):
    return lax.dot_general(a.astype(mx), b.astype(mx), (((0,), (0,)), ((), ())), preferred_element_type=f32)


def _split3(x):
    hi = x.astype(bf16)
    r1 = x - hi.astype(f32)
    mid = r1.astype(bf16)
    lo = (r1 - mid.astype(f32)).astype(bf16)
    return hi, mid, lo


def _cumsum_rows(x, tril_b):
    hi, mid, lo = _split3(x)
    return (jnp.dot(tril_b, hi, preferred_element_type=f32)
            + jnp.dot(tril_b, mid, preferred_element_type=f32)
            + jnp.dot(tril_b, lo, preferred_element_type=f32))


def _norm_proj_kernel(x_ref, g_ref, w_ref, o_ref):
    hn = _rms(x_ref[...], g_ref[...]).astype(bf16)
    o_ref[...] = jnp.dot(hn, w_ref[...], preferred_element_type=f32)


def _norm_proj(x, g, w, tm, tn):
    n, d = x.shape
    e = w.shape[1]
    tm = min(tm, 512)
    return pl.pallas_call(
        _norm_proj_kernel,
        grid=(n // tm,),
        in_specs=[pl.BlockSpec((tm, d), lambda i: (i, 0)),
                  pl.BlockSpec((1, d), lambda i: (0, 0)),
                  pl.BlockSpec((d, e), lambda i: (0, 0), pipeline_mode=pl.Buffered(1))],
        out_specs=pl.BlockSpec((tm, e), lambda i: (i, 0)),
        out_shape=jax.ShapeDtypeStruct((n, e), f32),
        compiler_params=_cparams("parallel"),
        name="norm_proj",
    )(x, g.reshape(1, d), w)


def _out_proj_kernel(*refs, gated):
    if gated:
        y_ref, gate_ref, w_ref, res_ref, o_ref = refs
        y = y_ref[...] * gate_ref[...]
    else:
        y_ref, w_ref, res_ref, o_ref = refs
        y = y_ref[...]
    o_ref[...] = res_ref[...] + jnp.dot(y.astype(bf16), w_ref[...], preferred_element_type=f32)


def _out_proj(y, w, res, tm, gate=None):
    n, e = y.shape
    d = w.shape[1]
    row = lambda i: (i, 0)
    args = [y] + ([gate] if gate is not None else []) + [w, res]
    in_specs = ([pl.BlockSpec((tm, e), row)] + ([pl.BlockSpec((tm, e), row)] if gate is not None else [])
                + [pl.BlockSpec((e, d), lambda i: (0, 0)), pl.BlockSpec((tm, d), row)])
    return pl.pallas_call(
        functools.partial(_out_proj_kernel, gated=gate is not None),
        grid=(n // tm,),
        in_specs=in_specs,
        out_specs=pl.BlockSpec((tm, d), row),
        out_shape=jax.ShapeDtypeStruct((n, d), f32),
        compiler_params=_cparams("parallel"),
        name="out_proj",
    )(*args)


def _ffn_kernel(*refs, final_norm):
    if final_norm:
        x_ref, g_ref, wg_ref, wu_ref, wd_ref, gf_ref, o_ref, hn_ref, acc_ref = refs
    else:
        x_ref, g_ref, wg_ref, wu_ref, wd_ref, o_ref, hn_ref, acc_ref = refs
    j = pl.program_id(1)

    @pl.when(j == 0)
    def _():
        hn_ref[...] = _rms(x_ref[...], g_ref[...]).astype(bf16)
        acc_ref[...] = jnp.zeros_like(acc_ref)

    h = hn_ref[...]
    gt = jnp.dot(h, wg_ref[...], preferred_element_type=f32)
    up = jnp.dot(h, wu_ref[...], preferred_element_type=f32)
    act = (gt * _sigmoid(gt) * up).astype(bf16)
    acc_ref[...] += jnp.dot(act, wd_ref[...], preferred_element_type=f32)

    @pl.when(j == pl.num_programs(1) - 1)
    def _():
        out = x_ref[...] + acc_ref[...]
        o_ref[...] = _rms(out, gf_ref[...]) if final_norm else out


def _ffn(x, g, w_gu, w_down, tm, tf, final_gain=None):
    n, d = x.shape
    nf = FFN_HIDDEN // tf
    final_norm = final_gain is not None
    in_specs = [pl.BlockSpec((tm, d), lambda i, j: (i, 0)),
                pl.BlockSpec((1, d), lambda i, j: (0, 0)),
                pl.BlockSpec((d, tf), lambda i, j: (0, j)),
                pl.BlockSpec((d, tf), lambda i, j: (0, j + nf)),
                pl.BlockSpec((tf, d), lambda i, j: (j, 0))]
    args = [x, g.reshape(1, d), w_gu, w_gu, w_down]
    if final_norm:
        in_specs.append(pl.BlockSpec((1, d), lambda i, j: (0, 0)))
        args.append(final_gain.reshape(1, d))
    return pl.pallas_call(
        functools.partial(_ffn_kernel, final_norm=final_norm),
        grid=(n // tm, nf),
        in_specs=in_specs,
        out_specs=pl.BlockSpec((tm, d), lambda i, j: (i, 0)),
        out_shape=jax.ShapeDtypeStruct((n, d), f32),
        scratch_shapes=[pltpu.VMEM((tm, d), bf16), pltpu.VMEM((tm, d), f32)],
        compiler_params=_cparams("parallel", "arbitrary"),
        name="ffn",
    )(*args)


def _rmsnorm_kernel(x_ref, g_ref, o_ref):
    o_ref[...] = _rms(x_ref[...], g_ref[...])


def _rmsnorm(x, g, tm):
    n, d = x.shape
    return pl.pallas_call(
        _rmsnorm_kernel,
        grid=(n // tm,),
        in_specs=[pl.BlockSpec((tm, d), lambda i: (i, 0)), pl.BlockSpec((1, d), lambda i: (0, 0))],
        out_specs=pl.BlockSpec((tm, d), lambda i: (i, 0)),
        out_shape=jax.ShapeDtypeStruct((n, d), f32),
        compiler_params=_cparams("parallel"),
        name="rmsnorm",
    )(x, g.reshape(1, d))


def _mlstm_kernel(*refs, L, nchunks, has_init, mx):
    H, DK, DV = MLSTM_HEADS, MLSTM_DK, MLSTM_DV
    q_ref, k_ref, v_ref, o_ref, gt_ref, bias_ref, gain_ref = refs[:7]
    if has_init:
        c0_ref, n0_ref, m0_ref = refs[7:10]
        rest = refs[10:]
    else:
        rest = refs[7:]
    y_ref, c_ref, n_ref, m_ref, c_s, n_s, m_s = rest
    j = pl.program_id(1)

    @pl.when(j == 0)
    def _():
        if has_init:
            c_s[...] = c0_ref[0]
            n_s[...] = n0_ref[0]
            m_s[...] = m0_ref[0]
        else:
            c_s[...] = jnp.zeros_like(c_s)
            n_s[...] = jnp.zeros_like(n_s)
            m_s[...] = jnp.zeros_like(m_s)

    Lp = max(L, BF16_ROWS)
    rowi = lax.broadcasted_iota(jnp.int32, (Lp, Lp), 0)
    coli = lax.broadcasted_iota(jnp.int32, (Lp, Lp), 1)
    causal = coli <= rowi
    eye = coli == rowi
    scale = DK ** -0.5

    def to_row(col):
        return jnp.sum(jnp.where(eye, col, 0.0), axis=0, keepdims=True)

    def pad_rows(x, value=0.0):
        if Lp == L:
            return x
        return jnp.concatenate([x, jnp.full((Lp - L, x.shape[1]), value, f32)], axis=0)

    def chunk(c, carry):
        rows = pl.ds(pl.multiple_of(c * L, L), L)
        gts_raw = gt_ref[rows, :] + bias_ref[...]
        lsg = pad_rows(_log_sigmoid(gts_raw))
        gts = pad_rows(gts_raw, -jnp.inf)
        for h in range(H):
            i_col = gts[:, h:h + 1]
            f_col = lsg[:, H + h:H + h + 1]
            f_row = to_row(f_col)
            fc_col = jnp.sum(jnp.where(causal, f_row, 0.0), axis=1, keepdims=True)
            fc_row = to_row(fc_col)
            i_row = to_row(i_col)
            m = m_s[h]
            dm = jnp.where(causal, fc_col - fc_row + i_row, -jnp.inf)
            g = fc_col + m
            mt = jnp.maximum(g, jnp.max(dm, axis=1, keepdims=True))
            p = jnp.exp(dm - mt)
            qc = pad_rows(q_ref[rows, h * DK:(h + 1) * DK] * scale)
            kc = pad_rows(k_ref[rows, h * DK:(h + 1) * DK])
            vc = pad_rows(v_ref[rows, h * DV:(h + 1) * DV])
            s = _mm_nt(qc, kc, mx) * p
            eg = jnp.exp(g - mt)
            cst = c_s[h]
            nst = n_s[h]
            num = eg * _mm_nt(qc, cst, mx) + _mm(s, vc, mx)
            den = eg * jnp.sum(qc * nst, axis=1, keepdims=True) + jnp.sum(s, axis=1, keepdims=True)
            hh = num / jnp.maximum(jnp.abs(den), jnp.exp(-mt))
            m_new = mt[L - 1:L, :]
            f_last = fc_col[L - 1:L, :]
            w_c = jnp.exp(f_last + m - m_new)
            w_j = jnp.exp(f_last - fc_col + i_col - m_new)
            c_s[h] = w_c * cst + _mm_tn(vc * w_j, kc, mx)
            n_s[h] = w_c * nst + jnp.sum(w_j * kc, axis=0, keepdims=True)
            m_s[h] = m_new
            hh = hh[:L]
            yn = hh * lax.rsqrt(jnp.mean(hh * hh, axis=1, keepdims=True) + NORM_EPS)
            yn = yn * gain_ref[:, h * DV:(h + 1) * DV]
            y_ref[rows, h * DV:(h + 1) * DV] = _sigmoid(o_ref[rows, h * DV:(h + 1) * DV]) * yn
        return carry

    lax.fori_loop(0, nchunks, chunk, 0)

    @pl.when(j == pl.num_programs(1) - 1)
    def _():
        c_ref[0] = c_s[...]
        n_ref[0] = n_s[...]
        m_ref[0] = m_s[...]


def _mlstm_mix(pr, b_gates, gain, B, T, state, tq):
    H, DK, DV = MLSTM_HEADS, MLSTM_DK, MLSTM_DV
    L = _chunk_len(T, MLSTM_CHUNK)
    nj = T // tq
    has_init = state is not None
    mx = bf16
    rmap = lambda c: (lambda b, j: (b * nj + j, c))
    bmap4 = lambda b, j: (b, 0, 0, 0)
    bias = jnp.zeros((1, LANES), f32).at[0, :2 * H].set(b_gates)
    in_specs = [pl.BlockSpec((tq, H * DK), rmap(0)), pl.BlockSpec((tq, H * DK), rmap(1)),
                pl.BlockSpec((tq, H * DV), rmap(1)), pl.BlockSpec((tq, H * DV), rmap(2)),
                pl.BlockSpec((tq, LANES), rmap((2 * H * DK + 2 * H * DV) // LANES)),
                pl.BlockSpec((1, LANES), lambda b, j: (0, 0)),
                pl.BlockSpec((1, H * DV), lambda b, j: (0, 0))]
    args = [pr, pr, pr, pr, pr, bias, gain.reshape(1, H * DV)]
    if has_init:
        c0, n0, m0 = state
        in_specs += [pl.BlockSpec((1, H, DV, DK), bmap4), pl.BlockSpec((1, H, 1, DK), bmap4),
                     pl.BlockSpec((1, H, 1, 1), bmap4)]
        args += [c0, n0.reshape(B, H, 1, DK), m0.reshape(B, H, 1, 1)]
    y, c, n, m = pl.pallas_call(
        functools.partial(_mlstm_kernel, L=L, nchunks=tq // L, has_init=has_init, mx=mx),
        grid=(B, nj),
        in_specs=in_specs,
        out_specs=[pl.BlockSpec((tq, H * DV), rmap(0)), pl.BlockSpec((1, H, DV, DK), bmap4),
                   pl.BlockSpec((1, H, 1, DK), bmap4), pl.BlockSpec((1, H, 1, 1), bmap4)],
        out_shape=[jax.ShapeDtypeStruct((B * T, H * DV), f32), jax.ShapeDtypeStruct((B, H, DV, DK), f32),
                   jax.ShapeDtypeStruct((B, H, 1, DK), f32), jax.ShapeDtypeStruct((B, H, 1, 1), f32)],
        scratch_shapes=[pltpu.VMEM((H, DV, DK), f32), pltpu.VMEM((H, 1, DK), f32), pltpu.VMEM((H, 1, 1), f32)],
        compiler_params=_cparams("parallel", "arbitrary"),
        name="mlstm_mix",
    )(*args)
    return y, c, n.reshape(B, H, DK), m.reshape(B, H)


def _mlstm_block_kernel(q_ref, k_ref, v_ref, o_ref, gt_ref, bias_ref, gain_ref, x_ref, wout_ref,
                        xo_ref, c_ref, n_ref, m_ref, c_s, n_s, m_s, y_ref, *, L, nc):
    H, DK, DV = MLSTM_HEADS, MLSTM_DK, MLSTM_DV
    P = 2 * L
    j = pl.program_id(1)

    @pl.when(j == 0)
    def _():
        c_s[...] = jnp.zeros_like(c_s)
        n_s[...] = jnp.zeros_like(n_s)
        m_s[...] = jnp.zeros_like(m_s)

    rowi = lax.broadcasted_iota(jnp.int32, (L, L), 0)
    coli = lax.broadcasted_iota(jnp.int32, (L, L), 1)
    causal = coli <= rowi
    r2 = lax.broadcasted_iota(jnp.int32, (P, P), 0)
    c2 = lax.broadcasted_iota(jnp.int32, (P, P), 1)
    pair_tril = ((c2 <= r2) & ((c2 >= L) == (r2 >= L))).astype(bf16)
    scale = DK ** -0.5

    gates = []
    for pp in range(nc // 2):
        g = gt_ref[pp * P:(pp + 1) * P, :] + bias_ref[...]
        hi, mid, lo = _split3(_log_sigmoid(g))
        fcum = (jnp.dot(pair_tril, hi, preferred_element_type=f32)
                + jnp.dot(pair_tril, mid, preferred_element_type=f32)
                + jnp.dot(pair_tril, lo, preferred_element_type=f32))
        g_t = g.T
        f_t = fcum.T
        for half in range(2):
            rs = slice(half * L, (half + 1) * L)
            gates.append((g[rs, :], fcum[rs, :], g_t[:, rs], f_t[:, rs]))

    pairs = [(c, h) for c in range(nc) for h in range(H)]
    rows_of = lambda c: slice(c * L, (c + 1) * L)
    ks_of = lambda h: slice(h * DK, (h + 1) * DK)
    vs_of = lambda h: slice(h * DV, (h + 1) * DV)
    ones_b = jnp.ones((L, LANES), bf16)
    wide = lambda x: jnp.concatenate([x] * (DV // LANES), axis=1)
    dms, rmax, fcols, icols, qks, qbs = {}, {}, {}, {}, {}, {}
    for c, h in pairs:
        g_c, f_c, g_r, f_r = gates[c]
        icols[c, h] = jnp.broadcast_to(g_c[:, h:h + 1], (L, LANES))
        fcols[c, h] = jnp.broadcast_to(f_c[:, H + h:H + h + 1], (L, LANES))
        dm = jnp.where(causal, fcols[c, h][:, :L] - f_r[H + h:H + h + 1, :] + g_r[h:h + 1, :], -jnp.inf)
        dms[c, h] = dm
        rmax[c, h] = jnp.broadcast_to(jnp.max(dm, axis=1, keepdims=True), (L, LANES))
        qbs[c, h] = (q_ref[rows_of(c), ks_of(h)] * scale).astype(bf16)
        qks[c, h] = _mm_nt(qbs[c, h], k_ref[rows_of(c), ks_of(h)])
    m_in, m_out = {}, {}
    for h in range(H):
        m = jnp.broadcast_to(m_s[h], (1, LANES))
        for c in range(nc):
            m_in[c, h] = m
            m = jnp.maximum(fcols[c, h][L - 1:L, :] + m, rmax[c, h][L - 1:L, :])
            m_out[c, h] = m
        m_s[h] = m[:, :1]
    mts, egs, ss, ssums, wcs, dcs, dns = {}, {}, {}, {}, {}, {}, {}
    for c, h in pairs:
        fc_col = fcols[c, h]
        gg = fc_col + m_in[c, h]
        mt = jnp.maximum(gg, rmax[c, h])
        mts[c, h] = mt
        egs[c, h] = jnp.exp(gg - mt)
        s = (qks[c, h] * jnp.exp(dms[c, h] - mt[:, :L])).astype(bf16)
        ss[c, h] = s
        ssums[c, h] = jnp.dot(s, ones_b, preferred_element_type=f32)
        f_last = fc_col[L - 1:L, :]
        wcs[c, h] = jnp.exp(f_last + m_in[c, h] - m_out[c, h])
        w_j = jnp.exp(f_last - fc_col + icols[c, h] - m_out[c, h])
        kc = k_ref[rows_of(c), ks_of(h)]
        dcs[c, h] = _mm_tn(v_ref[rows_of(c), vs_of(h)] * wide(w_j), kc)
        dns[c, h] = jnp.sum(w_j * kc, axis=0, keepdims=True)
    c_in, n_in = {}, {}
    for h in range(H):
        cst = c_s[h]
        nst = n_s[h]
        for c in range(nc):
            c_in[c, h] = cst.astype(bf16)
            n_in[c, h] = nst
            cst = wcs[c, h] * cst + dcs[c, h]
            nst = wcs[c, h] * nst + dns[c, h]
        c_s[h] = cst
        n_s[h] = nst
    hhs, msq = {}, {}
    mean_b = jnp.full((DV, LANES), 1.0 / DV, bf16)
    for c, h in pairs:
        rows, vs = rows_of(c), vs_of(h)
        qn = _mm_nt(qbs[c, h], jnp.broadcast_to(n_in[c, h], (LANES, DK)))
        num = wide(egs[c, h]) * _mm_nt(qbs[c, h], c_in[c, h]) + _mm(ss[c, h], v_ref[rows, vs])
        den = egs[c, h] * qn + ssums[c, h]
        hh = num / wide(jnp.maximum(jnp.abs(den), jnp.exp(-mts[c, h])))
        hhs[c, h] = hh
        msq[c, h] = jnp.dot((hh * hh).astype(bf16), mean_b, preferred_element_type=f32)
    for c, h in pairs:
        rows, vs = rows_of(c), vs_of(h)
        yn = hhs[c, h] * wide(lax.rsqrt(msq[c, h] + NORM_EPS)) * gain_ref[:, vs]
        y_ref[rows, vs] = _sigmoid(o_ref[rows, vs]) * yn
    xo_ref[...] = x_ref[...] + jnp.dot(y_ref[...].astype(bf16), wout_ref[...], preferred_element_type=f32)

    @pl.when(j == pl.num_programs(1) - 1)
    def _():
        c_ref[0] = c_s[...]
        n_ref[0] = n_s[...]
        m_ref[0] = m_s[...]


def _mlstm_mix_fresh(pr, b_gates, gain, x, w_out, B, T, tq):
    H, DK, DV = MLSTM_HEADS, MLSTM_DK, MLSTM_DV
    L = MLSTM_CHUNK
    nj = T // tq
    rmap = lambda c: (lambda b, j: (b * nj + j, c))
    bmap4 = lambda b, j: (b, 0, 0, 0)
    bias = jnp.zeros((1, LANES), f32).at[0, :2 * H].set(b_gates)
    y, c, n, m = pl.pallas_call(
        functools.partial(_mlstm_block_kernel, L=L, nc=tq // L),
        grid=(B, nj),
        in_specs=[pl.BlockSpec((tq, H * DK), rmap(0)), pl.BlockSpec((tq, H * DK), rmap(1)),
                  pl.BlockSpec((tq, H * DV), rmap(1)), pl.BlockSpec((tq, H * DV), rmap(2)),
                  pl.BlockSpec((tq, LANES), rmap((2 * H * DK + 2 * H * DV) // LANES)),
                  pl.BlockSpec((1, LANES), lambda b, j: (0, 0)),
                  pl.BlockSpec((1, H * DV), lambda b, j: (0, 0)),
                  pl.BlockSpec((tq, D_MODEL), rmap(0)),
                  pl.BlockSpec((H * DV, D_MODEL), lambda b, j: (0, 0))],
        out_specs=[pl.BlockSpec((tq, D_MODEL), rmap(0)), pl.BlockSpec((1, H, DV, DK), bmap4),
                   pl.BlockSpec((1, H, 1, DK), bmap4), pl.BlockSpec((1, H, 1, 1), bmap4)],
        out_shape=[jax.ShapeDtypeStruct((B * T, D_MODEL), f32), jax.ShapeDtypeStruct((B, H, DV, DK), f32),
                   jax.ShapeDtypeStruct((B, H, 1, DK), f32), jax.ShapeDtypeStruct((B, H, 1, 1), f32)],
        scratch_shapes=[pltpu.VMEM((H, DV, DK), f32), pltpu.VMEM((H, 1, DK), f32), pltpu.VMEM((H, 1, 1), f32),
                        pltpu.VMEM((tq, H * DV), f32)],
        compiler_params=_cparams("parallel", "arbitrary"),
        name="mlstm_mix_fresh",
    )(pr, pr, pr, pr, pr, bias, gain.reshape(1, H * DV), x, w_out)
    return y, c, n.reshape(B, H, DK), m.reshape(B, H)


def _mlstm_seq_kernel(q_ref, k_ref, v_ref, o_ref, gt_ref, bias_ref, gain_ref, c0_ref, n0_ref, m0_ref,
                      y_ref, c_ref, n_ref, m_ref, *, T, bs):
    H, DK, DV = MLSTM_HEADS, MLSTM_DK, MLSTM_DV
    P = BF16_ROWS
    R = bs * T
    ri = lax.broadcasted_iota(jnp.int32, (R, R), 0)
    ci = lax.broadcasted_iota(jnp.int32, (R, R), 1)
    tril_b = (((ri // T) == (ci // T)) & (ci <= ri)).astype(bf16)
    rp = lax.broadcasted_iota(jnp.int32, (P, P), 0)
    cp = lax.broadcasted_iota(jnp.int32, (P, P), 1)
    causal = cp <= rp
    real_col = lax.broadcasted_iota(jnp.int32, (1, P), 1) < T
    scale = DK ** -0.5
    ones_b = jnp.ones((P, LANES), bf16)
    mean_b = jnp.full((DV, LANES), 1.0 / DV, bf16)
    wide = lambda x: jnp.concatenate([x] * (DV // LANES), axis=1)

    g = gt_ref[...] + bias_ref[...]
    fcum = _cumsum_rows(_log_sigmoid(g), tril_b)
    zrows = jnp.zeros((LANES - R, LANES), f32)
    g_t = jnp.concatenate([g, zrows], axis=0).T
    f_t = jnp.concatenate([fcum, zrows], axis=0).T

    def pad_rows(x, value=0.0):
        return jnp.concatenate([x, jnp.full((P - T, x.shape[1]), value, f32)], axis=0)

    pairs = [(c, h) for c in range(bs) for h in range(H)]
    ks_of = lambda h: slice(h * DK, (h + 1) * DK)
    vs_of = lambda h: slice(h * DV, (h + 1) * DV)
    st = {}
    for c, h in pairs:
        rows = slice(c * T, (c + 1) * T)
        last = slice((c + 1) * T - 1, (c + 1) * T)
        i_col = jnp.broadcast_to(pad_rows(g[rows, h:h + 1], -jnp.inf), (P, LANES))
        f_last = jnp.broadcast_to(fcum[last, H + h:H + h + 1], (1, LANES))
        fc_col = jnp.concatenate([jnp.broadcast_to(fcum[rows, H + h:H + h + 1], (T, LANES)),
                                  jnp.broadcast_to(f_last, (P - T, LANES))], axis=0)
        i_row = jnp.where(real_col, g_t[h:h + 1, c * T:c * T + P], -jnp.inf)
        fc_row = f_t[H + h:H + h + 1, c * T:c * T + P]
        dm = jnp.where(causal, fc_col[:, :P] - fc_row + i_row, -jnp.inf)
        rmax = jnp.broadcast_to(jnp.max(dm, axis=1, keepdims=True), (P, LANES))
        qb = pad_rows(q_ref[rows, ks_of(h)] * scale).astype(bf16)
        kc = pad_rows(k_ref[rows, ks_of(h)])
        vc = pad_rows(v_ref[rows, vs_of(h)])
        st[c, h] = dict(i_col=i_col, f_last=f_last, fc_col=fc_col, dm=dm, rmax=rmax, qb=qb, kc=kc, vc=vc,
                        qk=_mm_nt(qb, kc))
    for c, h in pairs:
        d = st[c, h]
        m = jnp.broadcast_to(m0_ref[c, h], (1, LANES))
        gg = d["fc_col"] + m
        mt = jnp.maximum(gg, d["rmax"])
        m_new = mt[T - 1:T, :]
        d["mt"] = mt
        d["eg"] = jnp.exp(gg - mt)
        s = (d["qk"] * jnp.exp(d["dm"] - mt[:, :P])).astype(bf16)
        d["s"] = s
        d["ssum"] = jnp.dot(s, ones_b, preferred_element_type=f32)
        w_c = jnp.exp(d["f_last"] + m - m_new)
        w_j = jnp.exp(d["f_last"] - d["fc_col"] + d["i_col"] - m_new)
        c_ref[c, h] = w_c * c0_ref[c, h] + _mm_tn(d["vc"] * wide(w_j), d["kc"])
        n_ref[c, h] = w_c * n0_ref[c, h] + jnp.sum(w_j * d["kc"], axis=0, keepdims=True)
        m_ref[c, h] = m_new[:, :1]
    for c, h in pairs:
        d = st[c, h]
        qn = _mm_nt(d["qb"], jnp.broadcast_to(n0_ref[c, h], (LANES, DK)))
        num = wide(d["eg"]) * _mm_nt(d["qb"], c0_ref[c, h]) + _mm(d["s"], d["vc"])
        den = d["eg"] * qn + d["ssum"]
        hh = num / wide(jnp.maximum(jnp.abs(den), jnp.exp(-d["mt"])))
        d["hh"] = hh
        d["msq"] = jnp.dot((hh * hh).astype(bf16), mean_b, preferred_element_type=f32)
    for c, h in pairs:
        d = st[c, h]
        rows, vs = slice(c * T, (c + 1) * T), vs_of(h)
        yn = (d["hh"] * wide(lax.rsqrt(d["msq"] + NORM_EPS)))[:T] * gain_ref[:, vs]
        y_ref[rows, vs] = _sigmoid(o_ref[rows, vs]) * yn


def _mlstm_mix_seq(pr, b_gates, gain, B, T, state, bs):
    H, DK, DV = MLSTM_HEADS, MLSTM_DK, MLSTM_DV
    R = bs * T
    rmap = lambda c: (lambda i: (i, c))
    bmap4 = lambda i: (i, 0, 0, 0)
    const2 = lambda i: (0, 0)
    bias = jnp.zeros((1, LANES), f32).at[0, :2 * H].set(b_gates)
    c0, n0, m0 = state
    y, c, n, m = pl.pallas_call(
        functools.partial(_mlstm_seq_kernel, T=T, bs=bs),
        grid=(B // bs,),
        in_specs=[pl.BlockSpec((R, H * DK), rmap(0)), pl.BlockSpec((R, H * DK), rmap(1)),
                  pl.BlockSpec((R, H * DV), rmap(1)), pl.BlockSpec((R, H * DV), rmap(2)),
                  pl.BlockSpec((R, LANES), rmap((2 * H * DK + 2 * H * DV) // LANES)),
                  pl.BlockSpec((1, LANES), const2), pl.BlockSpec((1, H * DV), const2),
                  pl.BlockSpec((bs, H, DV, DK), bmap4), pl.BlockSpec((bs, H, 1, DK), bmap4),
                  pl.BlockSpec((bs, H, 1, 1), bmap4)],
        out_specs=[pl.BlockSpec((R, H * DV), rmap(0)), pl.BlockSpec((bs, H, DV, DK), bmap4),
                   pl.BlockSpec((bs, H, 1, DK), bmap4), pl.BlockSpec((bs, H, 1, 1), bmap4)],
        out_shape=[jax.ShapeDtypeStruct((B * T, H * DV), f32), jax.ShapeDtypeStruct((B, H, DV, DK), f32),
                   jax.ShapeDtypeStruct((B, H, 1, DK), f32), jax.ShapeDtypeStruct((B, H, 1, 1), f32)],
        compiler_params=_cparams("parallel"),
        name="mlstm_mix_seq",
    )(pr, pr, pr, pr, pr, bias, gain.reshape(1, H * DV), c0, n0.reshape(B, H, 1, DK), m0.reshape(B, H, 1, 1))
    return y, c, n.reshape(B, H, DK), m.reshape(B, H)


def _gla_kernel(*refs, kind, layer, L, nchunks, H, DK, DV, has_init, mx):
    if kind == "hgrn":
        q_ref, f_ref, v_ref, g_ref, lbl_ref, gain_ref = refs[:6]
        rest = refs[6:]
    else:
        q_ref, k_ref, v_ref, g_ref, gr_ref, wup_ref, bgate_ref, gain_ref = refs[:8]
        rest = refs[8:]
    if has_init:
        s0_ref = rest[0]
        rest = rest[1:]
    y_ref, s_ref, st_s = rest
    j = pl.program_id(1)

    @pl.when(j == 0)
    def _():
        for h in range(H):
            if has_init:
                st_s[h] = s0_ref[0, h].T
            else:
                st_s[h] = jnp.zeros((DV, DK), f32)

    Lp = max(L, BF16_ROWS)
    rowi = lax.broadcasted_iota(jnp.int32, (Lp, Lp), 0)
    coli = lax.broadcasted_iota(jnp.int32, (Lp, Lp), 1)
    causal = coli <= rowi
    tril_b = causal.astype(bf16)

    def pad_rows(x):
        if Lp == L:
            return x
        return jnp.concatenate([x, jnp.zeros((Lp - L, x.shape[1]), f32)], axis=0)

    if kind == "hgrn":
        logits = lbl_ref[...]
        e = jnp.exp(logits - jnp.max(logits, axis=0, keepdims=True))
        sm = e / jnp.sum(e, axis=0, keepdims=True)
        lb_all = jnp.zeros((1, H * DK), f32)
        for li in range(layer):
            lb_all = lb_all + sm[li:li + 1, :]
    else:
        scale = DK ** -0.5

    def chunk(c, carry):
        rows = pl.ds(pl.multiple_of(c * L, L), L)
        if kind == "gla":
            gk = _mm(pad_rows(gr_ref[rows, :]), wup_ref[...], mx)[:L] + bgate_ref[...]
            ld_all = _log_sigmoid(gk) * (1.0 / GLA_GATE_NORMALIZER)
        for h in range(H):
            ks = slice(h * DK, (h + 1) * DK)
            vs = slice(h * DV, (h + 1) * DV)
            if kind == "hgrn":
                qraw = q_ref[rows, ks]
                fg = f_ref[rows, ks]
                lb = lb_all[:, ks]
                gc = jnp.log(lb + (1.0 - lb) * _sigmoid(fg))
                kc = (1.0 - lb) * _sigmoid(-fg)
                qc = qraw * _sigmoid(qraw)
            else:
                qc = q_ref[rows, ks] * scale
                kc = k_ref[rows, ks]
                gc = ld_all[:, ks]
            vc = pad_rows(v_ref[rows, vs])
            qc, kc, gc = pad_rows(qc), pad_rows(kc), pad_rows(gc)
            bc = _cumsum_rows(gc, tril_b)
            qe = qc * jnp.exp(bc)
            ke = kc * jnp.exp(-bc)
            a = jnp.where(causal, _mm_nt(qe, ke, mx), 0.0)
            st = st_s[h]
            o = (_mm_nt(qe, st, mx) + _mm(a, vc, mx))[:L]
            b_last = bc[L - 1:L, :]
            kdec = kc * jnp.exp(b_last - bc)
            st_s[h] = st * jnp.exp(b_last) + _mm_tn(vc, kdec, mx)
            yn = o * lax.rsqrt(jnp.mean(o * o, axis=1, keepdims=True) + NORM_EPS) * gain_ref[:, vs]
            gg = g_ref[rows, vs]
            y_ref[rows, vs] = yn * (gg * _sigmoid(gg))
        return carry

    lax.fori_loop(0, nchunks, chunk, 0)

    @pl.when(j == pl.num_programs(1) - 1)
    def _():
        for h in range(H):
            s_ref[0, h] = st_s[h].T


def _gla_mix(kind, pr, extra, gain, B, T, state, tq, layer=0):
    if kind == "hgrn":
        H, DK, DV = HGRN_HEADS, HGRN_DK, HGRN_DV
    else:
        H, DK, DV = GLA_HEADS, GLA_DK, GLA_DV
    L = _chunk_len(T, LIN_CHUNK)
    nj = T // tq
    has_init = state is not None
    mx = bf16
    rmap = lambda c: (lambda b, j: (b * nj + j, c))
    bmap4 = lambda b, j: (b, 0, 0, 0)
    const2 = lambda b, j: (0, 0)
    hk, hv = H * DK, H * DV
    if kind == "hgrn":
        (lb_logits,) = extra
        in_specs = [pl.BlockSpec((tq, hk), rmap(0)), pl.BlockSpec((tq, hk), rmap(1)),
                    pl.BlockSpec((tq, hv), rmap(2)), pl.BlockSpec((tq, hv), rmap(3)),
                    pl.BlockSpec((DEPTH, hk), const2), pl.BlockSpec((1, hv), const2)]
        args = [pr, pr, pr, pr, lb_logits, gain.reshape(1, hv)]
    else:
        w_up, b_gate = extra
        w_up_p = jnp.zeros((LANES, hk), f32).at[:GLA_GATE_RANK].set(w_up)
        in_specs = [pl.BlockSpec((tq, hk), rmap(0)), pl.BlockSpec((tq, hk), rmap(1)),
                    pl.BlockSpec((tq, hv), rmap(1)), pl.BlockSpec((tq, hv), rmap(2)),
                    pl.BlockSpec((tq, LANES), rmap((2 * hk + 2 * hv) // LANES)),
                    pl.BlockSpec((LANES, hk), const2), pl.BlockSpec((1, hk), const2),
                    pl.BlockSpec((1, hv), const2)]
        args = [pr, pr, pr, pr, pr, w_up_p, b_gate.reshape(1, hk), gain.reshape(1, hv)]
    if has_init:
        in_specs.append(pl.BlockSpec((1, H, DK, DV), bmap4))
        args.append(state)
    y, s = pl.pallas_call(
        functools.partial(_gla_kernel, kind=kind, layer=layer, L=L, nchunks=tq // L, H=H, DK=DK, DV=DV,
                          has_init=has_init, mx=mx),
        grid=(B, nj),
        in_specs=in_specs,
        out_specs=[pl.BlockSpec((tq, hv), rmap(0)), pl.BlockSpec((1, H, DK, DV), bmap4)],
        out_shape=[jax.ShapeDtypeStruct((B * T, hv), f32), jax.ShapeDtypeStruct((B, H, DK, DV), f32)],
        scratch_shapes=[pltpu.VMEM((H, DV, DK), f32)],
        compiler_params=_cparams("parallel", "arbitrary"),
        name=kind + "_mix",
    )(*args)
    return y, s


def _gla_seq_kernel(*refs, kind, layer, T, bs, H, DK, DV):
    if kind == "hgrn":
        q_ref, f_ref, v_ref, g_ref, lbl_ref, gain_ref = refs[:6]
        rest = refs[6:]
    else:
        q_ref, k_ref, v_ref, g_ref, gr_ref, wup_ref, bgate_ref, gain_ref = refs[:8]
        rest = refs[8:]
    s0_ref, y_ref, s_ref, qe_s, ke_s, kd_s, el_s, v_s = rest
    P = BF16_ROWS
    R = bs * T
    hk = H * DK

    ri = lax.broadcasted_iota(jnp.int32, (R, R), 0)
    ci = lax.broadcasted_iota(jnp.int32, (R, R), 1)
    same_seq = (ri // T) == (ci // T)
    tril_b = (same_seq & (ci <= ri)).astype(bf16)
    seq_b = same_seq.astype(bf16)
    rp = lax.broadcasted_iota(jnp.int32, (P, P), 0)
    cp = lax.broadcasted_iota(jnp.int32, (P, P), 1)
    causal = cp <= rp

    if kind == "hgrn":
        logits = lbl_ref[...]
        e = jnp.exp(logits - jnp.max(logits, axis=0, keepdims=True))
        sm = e / jnp.sum(e, axis=0, keepdims=True)
        lb = jnp.zeros((1, hk), f32)
        for li in range(layer):
            lb = lb + sm[li:li + 1, :]
        qraw = q_ref[...]
        fg = f_ref[...]
        gc = jnp.log(lb + (1.0 - lb) * _sigmoid(fg))
        kc = (1.0 - lb) * _sigmoid(-fg)
        qc = qraw * _sigmoid(qraw)
    else:
        gk = _mm(gr_ref[...], wup_ref[...]) + bgate_ref[...]
        gc = _log_sigmoid(gk) * (1.0 / GLA_GATE_NORMALIZER)
        qc = q_ref[...] * (DK ** -0.5)
        kc = k_ref[...]
    pieces = _split3(gc)
    bc = sum(jnp.dot(tril_b, x, preferred_element_type=f32) for x in pieces)
    bl = sum(jnp.dot(seq_b, x, preferred_element_type=f32) for x in pieces)
    qe = qc * jnp.exp(bc)
    ke = kc * jnp.exp(-bc)
    kd = kc * jnp.exp(bl - bc)
    e_hi, e_mid, e_lo = [x.astype(f32) for x in _split3(jnp.exp(bl))]
    vv = v_ref[...]
    zpad_k = jnp.zeros((P - T, hk), f32)
    zpad_v = jnp.zeros((P - T, vv.shape[1]), f32)
    prow = lax.broadcasted_iota(jnp.int32, (P, hk), 0)
    for c in range(bs):
        src = slice(c * T, (c + 1) * T)
        dst = slice(c * P, (c + 1) * P)
        qe_s[dst, :] = jnp.concatenate([qe[src], zpad_k], axis=0)
        ke_s[dst, :] = jnp.concatenate([ke[src], zpad_k], axis=0)
        kd_s[dst, :] = jnp.concatenate([kd[src], zpad_k], axis=0)
        v_s[dst, :] = jnp.concatenate([vv[src], zpad_v], axis=0)
        first = slice(c * T, c * T + 1)
        el_s[dst, :] = jnp.where(prow == 0, e_hi[first],
                                 jnp.where(prow == 1, e_mid[first], jnp.where(prow == 2, e_lo[first], 0.0)))

    ones_b = jnp.ones((P, LANES), bf16)
    pairs = [(c, h) for c in range(bs) for h in range(H)]
    rows_of = lambda c: slice(c * P, (c + 1) * P)
    ks_of = lambda h: slice(h * DK, (h + 1) * DK)
    vs_of = lambda h: slice(h * DV, (h + 1) * DV)
    qes, vbs, amats, news = {}, {}, {}, {}
    for c, h in pairs:
        rows, ks = rows_of(c), ks_of(h)
        qes[c, h] = qe_s[rows, ks].astype(bf16)
        vbs[c, h] = v_s[rows, vs_of(h)].astype(bf16)
        a = _mm_nt(qes[c, h], ke_s[rows, ks])
        amats[c, h] = jnp.where(causal, a, 0.0).astype(bf16)
        decay = _mm_tn(el_s[rows, ks], ones_b)
        decay = jnp.concatenate([decay] * (DV // LANES), axis=1)
        news[c, h] = s0_ref[c, h] * decay + _mm_tn(kd_s[rows, ks], vbs[c, h])
    for c, h in pairs:
        s_ref[c, h] = news[c, h]
    outs = {}
    for c, h in pairs:
        outs[c, h] = (_mm(qes[c, h], s0_ref[c, h]) + _mm(amats[c, h], vbs[c, h]))[:T]
    for c, h in pairs:
        vs = vs_of(h)
        rows = slice(c * T, (c + 1) * T)
        o = outs[c, h]
        yn = o * lax.rsqrt(jnp.mean(o * o, axis=1, keepdims=True) + NORM_EPS) * gain_ref[:, vs]
        gg = g_ref[rows, vs]
        y_ref[rows, vs] = yn * (gg * _sigmoid(gg))


def _gla_mix_seq(kind, pr, extra, gain, B, T, state, bs, layer=0):
    if kind == "hgrn":
        H, DK, DV = HGRN_HEADS, HGRN_DK, HGRN_DV
    else:
        H, DK, DV = GLA_HEADS, GLA_DK, GLA_DV
    R = bs * T
    rmap = lambda c: (lambda i: (i, c))
    bmap4 = lambda i: (i, 0, 0, 0)
    const2 = lambda i: (0, 0)
    hk, hv = H * DK, H * DV
    if kind == "hgrn":
        (lb_logits,) = extra
        in_specs = [pl.BlockSpec((R, hk), rmap(0)), pl.BlockSpec((R, hk), rmap(1)),
                    pl.BlockSpec((R, hv), rmap(2)), pl.BlockSpec((R, hv), rmap(3)),
                    pl.BlockSpec((DEPTH, hk), const2), pl.BlockSpec((1, hv), const2)]
        args = [pr, pr, pr, pr, lb_logits, gain.reshape(1, hv)]
    else:
        w_up, b_gate = extra
        w_up_p = jnp.zeros((LANES, hk), f32).at[:GLA_GATE_RANK].set(w_up)
        in_specs = [pl.BlockSpec((R, hk), rmap(0)), pl.BlockSpec((R, hk), rmap(1)),
                    pl.BlockSpec((R, hv), rmap(1)), pl.BlockSpec((R, hv), rmap(2)),
                    pl.BlockSpec((R, LANES), rmap((2 * hk + 2 * hv) // LANES)),
                    pl.BlockSpec((LANES, hk), const2), pl.BlockSpec((1, hk), const2),
                    pl.BlockSpec((1, hv), const2)]
        args = [pr, pr, pr, pr, pr, w_up_p, b_gate.reshape(1, hk), gain.reshape(1, hv)]
    in_specs.append(pl.BlockSpec((bs, H, DK, DV), bmap4))
    args.append(state)
    P = BF16_ROWS
    return pl.pallas_call(
        functools.partial(_gla_seq_kernel, kind=kind, layer=layer, T=T, bs=bs, H=H, DK=DK, DV=DV),
        grid=(B // bs,),
        in_specs=in_specs,
        out_specs=[pl.BlockSpec((R, hv), rmap(0)), pl.BlockSpec((bs, H, DK, DV), bmap4)],
        out_shape=[jax.ShapeDtypeStruct((B * T, hv), f32), jax.ShapeDtypeStruct((B, H, DK, DV), f32)],
        scratch_shapes=[pltpu.VMEM((bs * P, hk), f32)] * 4 + [pltpu.VMEM((bs * P, hv), f32)],
        compiler_params=_cparams("parallel"),
        name=kind + "_mix_seq",
    )(*args)


def _gla_block_kernel(*refs, kind, layer, L, nc, H, DK, DV):
    if kind == "hgrn":
        q_ref, f_ref, v_ref, g_ref, lbl_ref, gain_ref = refs[:6]
        rest = refs[6:]
    else:
        q_ref, k_ref, v_ref, g_ref, gr_ref, wup_ref, bgate_ref, gain_ref = refs[:8]
        rest = refs[8:]
    x_ref, wout_ref, xo_ref, s_ref, st_s, qe_s, ke_s, qs_s, kd_s, el_s, y_ref = rest
    j = pl.program_id(1)

    @pl.when(j == 0)
    def _():
        st_s[...] = jnp.zeros_like(st_s)

    rowi = lax.broadcasted_iota(jnp.int32, (L, L), 0)
    coli = lax.broadcasted_iota(jnp.int32, (L, L), 1)
    causal = coli <= rowi
    tril_b = causal.astype(bf16)

    if kind == "hgrn":
        logits = lbl_ref[...]
        e = jnp.exp(logits - jnp.max(logits, axis=0, keepdims=True))
        sm = e / jnp.sum(e, axis=0, keepdims=True)
        lb = jnp.zeros((1, H * DK), f32)
        for li in range(layer):
            lb = lb + sm[li:li + 1, :]
    else:
        scale = DK ** -0.5

    for c in range(nc):
        rows = slice(c * L, (c + 1) * L)
        if kind == "hgrn":
            qraw = q_ref[rows, :]
            fg = f_ref[rows, :]
            gc = jnp.log(lb + (1.0 - lb) * _sigmoid(fg))
            kc = (1.0 - lb) * _sigmoid(-fg)
            qc = qraw * _sigmoid(qraw)
        else:
            gk = jnp.dot(gr_ref[rows, :].astype(bf16), wup_ref[...].astype(bf16),
                         preferred_element_type=f32) + bgate_ref[...]
            gc = _log_sigmoid(gk) * (1.0 / GLA_GATE_NORMALIZER)
            qc = q_ref[rows, :] * scale
            kc = k_ref[rows, :]
        hi, mid, lo = _split3(gc)
        bc = (jnp.dot(tril_b, hi, preferred_element_type=f32) + jnp.dot(tril_b, mid, preferred_element_type=f32)
              + jnp.dot(tril_b, lo, preferred_element_type=f32))
        b_mid = bc[L // 2 - 1:L // 2, :]
        b_last = bc[L - 1:L, :]
        qe = qc * jnp.exp(bc - b_mid)
        ke = kc * jnp.exp(b_mid - bc)
        qe_s[rows, :] = qe.astype(bf16)
        ke_s[rows, :] = ke.astype(bf16)
        qs_s[rows, :] = (qe * jnp.exp(b_mid)).astype(bf16)
        kd_s[rows, :] = (ke * jnp.exp(b_last - b_mid)).astype(bf16)
        el_s[c] = jnp.exp(b_last)

    dn_nt = (((1,), (1,)), ((), ()))
    dn_tn = (((0,), (0,)), ((), ()))
    pairs = [(c, h) for c in range(nc) for h in range(H)]
    rows_of = lambda c: slice(c * L, (c + 1) * L)
    ks_of = lambda h: slice(h * DK, (h + 1) * DK)
    vs_of = lambda h: slice(h * DV, (h + 1) * DV)
    vbs, amats, dsts = {}, {}, {}
    for c, h in pairs:
        rows, ks = rows_of(c), ks_of(h)
        vb = v_ref[rows, vs_of(h)].astype(bf16)
        a = lax.dot_general(qe_s[rows, ks], ke_s[rows, ks], dn_nt, preferred_element_type=f32)
        vbs[c, h] = vb
        amats[c, h] = jnp.where(causal, a, 0.0).astype(bf16)
        dsts[c, h] = lax.dot_general(vb, kd_s[rows, ks], dn_tn, preferred_element_type=f32)
    sts = {}
    for h in range(H):
        st = st_s[h]
        for c in range(nc):
            sts[c, h] = st.astype(bf16)
            st = st * el_s[c][:, ks_of(h)] + dsts[c, h]
        st_s[h] = st
    outs = {}
    for c, h in pairs:
        rows, ks = rows_of(c), ks_of(h)
        outs[c, h] = (lax.dot_general(qs_s[rows, ks], sts[c, h], dn_nt, preferred_element_type=f32)
                      + jnp.dot(amats[c, h], vbs[c, h], preferred_element_type=f32))
    for c, h in pairs:
        rows, vs = rows_of(c), vs_of(h)
        o = outs[c, h]
        yn = o * lax.rsqrt(jnp.mean(o * o, axis=1, keepdims=True) + NORM_EPS) * gain_ref[:, vs]
        gg = g_ref[rows, vs]
        y_ref[rows, vs] = yn * (gg * _sigmoid(gg))
    xo_ref[...] = x_ref[...] + jnp.dot(y_ref[...].astype(bf16), wout_ref[...], preferred_element_type=f32)

    @pl.when(j == pl.num_programs(1) - 1)
    def _():
        for h in range(H):
            s_ref[0, h] = st_s[h].T


def _gla_mix_fresh(kind, pr, extra, gain, x, w_out, B, T, tq, layer=0):
    if kind == "hgrn":
        H, DK, DV = HGRN_HEADS, HGRN_DK, HGRN_DV
    else:
        H, DK, DV = GLA_HEADS, GLA_DK, GLA_DV
    L = 2 * LIN_CHUNK
    nc = tq // L
    nj = T // tq
    rmap = lambda c: (lambda b, j: (b * nj + j, c))
    bmap4 = lambda b, j: (b, 0, 0, 0)
    const2 = lambda b, j: (0, 0)
    hk, hv = H * DK, H * DV
    if kind == "hgrn":
        (lb_logits,) = extra
        in_specs = [pl.BlockSpec((tq, hk), rmap(0)), pl.BlockSpec((tq, hk), rmap(1)),
                    pl.BlockSpec((tq, hv), rmap(2)), pl.BlockSpec((tq, hv), rmap(3)),
                    pl.BlockSpec((DEPTH, hk), const2), pl.BlockSpec((1, hv), const2)]
        args = [pr, pr, pr, pr, lb_logits, gain.reshape(1, hv)]
    else:
        w_up, b_gate = extra
        w_up_p = jnp.zeros((LANES, hk), f32).at[:GLA_GATE_RANK].set(w_up)
        in_specs = [pl.BlockSpec((tq, hk), rmap(0)), pl.BlockSpec((tq, hk), rmap(1)),
                    pl.BlockSpec((tq, hv), rmap(1)), pl.BlockSpec((tq, hv), rmap(2)),
                    pl.BlockSpec((tq, LANES), rmap((2 * hk + 2 * hv) // LANES)),
                    pl.BlockSpec((LANES, hk), const2), pl.BlockSpec((1, hk), const2),
                    pl.BlockSpec((1, hv), const2)]
        args = [pr, pr, pr, pr, pr, w_up_p, b_gate.reshape(1, hk), gain.reshape(1, hv)]
    in_specs += [pl.BlockSpec((tq, D_MODEL), rmap(0)), pl.BlockSpec((hv, D_MODEL), const2)]
    args += [x, w_out]
    return pl.pallas_call(
        functools.partial(_gla_block_kernel, kind=kind, layer=layer, L=L, nc=nc, H=H, DK=DK, DV=DV),
        grid=(B, nj),
        in_specs=in_specs,
        out_specs=[pl.BlockSpec((tq, D_MODEL), rmap(0)), pl.BlockSpec((1, H, DK, DV), bmap4)],
        out_shape=[jax.ShapeDtypeStruct((B * T, D_MODEL), f32), jax.ShapeDtypeStruct((B, H, DK, DV), f32)],
        scratch_shapes=[pltpu.VMEM((H, DV, DK), f32)] + [pltpu.VMEM((tq, hk), bf16)] * 4
                       + [pltpu.VMEM((nc, 1, hk), f32), pltpu.VMEM((tq, hv), f32)],
        compiler_params=_cparams("parallel", "arbitrary"),
        name=kind + "_mix_fresh",
    )(*args)


def _rwkv_pre_kernel(x_ref, aux_ref, sh_ref, g_ref, mu_ref, wrkv_ref, la_ref, lbw_ref, lba_ref, lbg_ref,
                     w0_ref, a0_ref, r_ref, w_ref, k_ref, v_ref, alr_ref, gate_ref, *, long_seq, blocks_per_seq,
                     seq_len):
    g = g_ref[...]
    hn = _rms(x_ref[...], g)
    tm = hn.shape[0]
    rowi = lax.broadcasted_iota(jnp.int32, (tm, 1), 0)
    rolled = pltpu.roll(hn, 1, axis=0)
    if long_seq:
        p_last = _rms(aux_ref[SUBLANES - 1:SUBLANES, :], g)
        at_start = (pl.program_id(0) % blocks_per_seq) == 0
        first = jnp.where(at_start, sh_ref[0], p_last)
        prev = jnp.where(rowi == 0, first, rolled)
    else:
        prev = jnp.where(rowi % seq_len == 0, aux_ref[...], rolled)
    xx = prev - hn

    def lerp(c):
        return (hn + xx * mu_ref[c:c + 1, :]).astype(bf16)

    r_ref[...] = jnp.dot(lerp(0), wrkv_ref[0], preferred_element_type=f32)
    k_ref[...] = jnp.dot(lerp(1), wrkv_ref[1], preferred_element_type=f32)
    v_ref[...] = jnp.dot(lerp(2), wrkv_ref[2], preferred_element_type=f32)
    lw = jnp.tanh(jnp.dot(lerp(3), la_ref[:, 0:64], preferred_element_type=f32))
    wl = -_softplus(-(w0_ref[...] + jnp.dot(lw.astype(bf16), lbw_ref[...], preferred_element_type=f32))) - 0.5
    w_ref[...] = jnp.exp(-jnp.exp(wl))
    la = jnp.dot(lerp(4), la_ref[:, 64:128], preferred_element_type=f32)
    alr_ref[...] = _sigmoid(a0_ref[...] + jnp.dot(la.astype(bf16), lba_ref[...], preferred_element_type=f32))
    lg = _sigmoid(jnp.dot(lerp(5), la_ref[:, 128:256], preferred_element_type=f32))
    gate_ref[...] = jnp.dot(lg.astype(bf16), lbg_ref[...], preferred_element_type=f32)


def _rwkv_pre(x, shift0, g, mu, wrkv, la, lbw, lba, lbg, w0, a0, B, T, tm):
    n, d = x.shape
    long_seq = T % tm == 0
    row = lambda i: (i, 0)
    const2 = lambda i: (0, 0)
    if long_seq:
        bps = T // tm
        sub = tm // SUBLANES
        aux = x
        aux_spec = pl.BlockSpec((SUBLANES, d), lambda i: (jnp.maximum(i * sub - 1, 0), 0))
        sh = shift0.reshape(B, 1, d)
        sh_spec = pl.BlockSpec((1, 1, d), lambda i: (i // bps, 0, 0))
    else:
        assert tm % T == 0
        bps = 1
        aux = jnp.repeat(shift0, T, axis=0)
        aux_spec = pl.BlockSpec((tm, d), row)
        sh = shift0.reshape(B, 1, d)
        sh_spec = pl.BlockSpec((1, 1, d), lambda i: (0, 0, 0))
    out = jax.ShapeDtypeStruct((n, d), f32)
    return pl.pallas_call(
        functools.partial(_rwkv_pre_kernel, long_seq=long_seq, blocks_per_seq=bps, seq_len=T),
        grid=(n // tm,),
        in_specs=[pl.BlockSpec((tm, d), row), aux_spec, sh_spec, pl.BlockSpec((1, d), const2),
                  pl.BlockSpec((6, d), const2),
                  pl.BlockSpec((3, d, d), lambda i: (0, 0, 0), pipeline_mode=pl.Buffered(1)),
                  pl.BlockSpec((d, 256), const2), pl.BlockSpec((64, d), const2), pl.BlockSpec((64, d), const2),
                  pl.BlockSpec((128, d), const2), pl.BlockSpec((1, d), const2), pl.BlockSpec((1, d), const2)],
        out_specs=[pl.BlockSpec((tm, d), row)] * 6,
        out_shape=[out] * 6,
        compiler_params=_cparams("parallel"),
        name="rwkv_pre",
    )(x, aux, sh, g.reshape(1, d), mu, wrkv, la, lbw, lba, lbg, w0.reshape(1, d), a0.reshape(1, d))


def _rwkv_scan_kernel(*refs, tt, has_init, unroll):
    N = RWKV_N
    (r_ref, w_ref, k_ref, v_ref, alr_ref, gate_ref, x_ref, wout_ref,
     kk_ref, ka_ref, rk_ref, lnw_ref, lnb_ref) = refs[:13]
    if has_init:
        s0_ref = refs[13]
        rest = refs[14:]
    else:
        rest = refs[13:]
    xo_ref, sout_ref, s_s, y_s, vec_s, z_ref, zz_s = rest
    HP = RWKV_HEADS // 2
    j = pl.program_id(1)

    @pl.when(j == 0)
    def _():
        if has_init:
            s_s[...] = s0_ref[...]
        else:
            s_s[...] = jnp.zeros_like(s_s)

    low = lax.broadcasted_iota(jnp.int32, (N, LANES), 1) < N

    def load_pair(ref, t0):
        tiles = []
        for t in (t0, t0 + 1):
            x = ref[:, t, :]
            tiles += [x[:, hp * LANES:(hp + 1) * LANES] for hp in range(HP)]
        xt = jnp.concatenate(tiles, axis=0).T
        ev, od = xt[:N], xt[N:]
        return (jnp.where(low, ev, pltpu.roll(od, N, axis=1)),
                jnp.where(low, pltpu.roll(ev, N, axis=1), od))

    def store_pair(ref, t0, z0, z1):
        ev = jnp.where(low, z0, pltpu.roll(z1, N, axis=1))
        od = jnp.where(low, pltpu.roll(z0, N, axis=1), z1)
        x = jnp.concatenate([ev, od], axis=0).T
        for i, t in enumerate((t0, t0 + 1)):
            tiles = [x[(i * HP + hp) * SUBLANES:(i * HP + hp + 1) * SUBLANES, :] for hp in range(HP)]
            ref[:, t, :] = jnp.concatenate(tiles, axis=1)

    VA, VW, VB, VK, VR, VV = range(6)

    def stage(tk, r, w, k, v, alr):
        kkraw = k * kk_ref[...]
        nrm = jnp.sqrt(jnp.sum(kkraw * kkraw, axis=0, keepdims=True))
        kk = kkraw / jnp.maximum(nrm, 1e-12)
        vec_s[tk, VA] = -kk
        vec_s[tk, VW] = w
        vec_s[tk, VB] = kk * alr
        vec_s[tk, VK] = k * (1.0 + (alr - 1.0) * ka_ref[...])
        vec_s[tk, VR] = r
        vec_s[tk, VV] = v

    def key_row(tk, which, kc):
        return vec_s[tk, which, pl.ds(kc, 1), :]

    def project(tk):
        def body(i, sa):
            for u in range(unroll):
                kc = i * unroll + u
                sa = sa + s_s[kc] * key_row(tk, VA, kc)
            return sa

        return lax.fori_loop(0, N // unroll, body, jnp.zeros((N, LANES), f32))

    def update(tk, sa):
        vv = vec_s[tk, VV]
        y = jnp.zeros((N, LANES), f32)
        for kc in range(N):
            sk = s_s[kc] * key_row(tk, VW, kc) + sa * key_row(tk, VB, kc) + vv * key_row(tk, VK, kc)
            s_s[kc] = sk
            y = y + sk * key_row(tk, VR, kc)
        y_s[tk] = y

    def epilogue(tk):
        y = y_s[tk]
        mean = jnp.mean(y, axis=0, keepdims=True)
        yc = y - mean
        var = jnp.mean(yc * yc, axis=0, keepdims=True)
        yn = yc * lax.rsqrt(var + RWKV_GN_EPS)
        bonus = jnp.sum(vec_s[tk, VR] * vec_s[tk, VK] * rk_ref[...], axis=0, keepdims=True) * vec_s[tk, VV]
        return yn * lnw_ref[...] + lnb_ref[...] + bonus

    def stage_pair(slot, t0):
        streams = [load_pair(ref, t0) for ref in (r_ref, w_ref, k_ref, v_ref, alr_ref)]
        for i in range(2):
            stage(slot + i, *[s[i] for s in streams])

    @pl.when(j == 0)
    def _():
        vec_s[...] = jnp.zeros_like(vec_s)
        y_s[...] = jnp.zeros_like(y_s)

    npairs = tt // 2
    stage_pair(0, 0)

    def step(p, carry):
        slot = 2 * (p % 2)
        prev = 2 - slot
        sa = project(slot)
        update(slot, sa)
        zz_s[0] = epilogue(prev)
        zz_s[1] = epilogue(prev + 1)
        stage_pair(prev, 2 * jnp.minimum(p + 1, npairs - 1))
        sa = project(slot + 1)
        update(slot + 1, sa)
        store_pair(z_ref, 2 * jnp.maximum(p - 1, 0), zz_s[0], zz_s[1])
        return carry

    lax.fori_loop(0, npairs, step, 0)
    last = 2 * ((npairs - 1) % 2)
    store_pair(z_ref, tt - 2, epilogue(last), epilogue(last + 1))

    rows = RWKV_SEQ_PER_STEP * tt
    zg = (z_ref[...] * gate_ref[...]).reshape(rows, D_MODEL).astype(bf16)
    proj = jnp.dot(zg, wout_ref[...], preferred_element_type=f32)
    xo_ref[...] = x_ref[...] + proj.reshape(RWKV_SEQ_PER_STEP, tt, D_MODEL)

    @pl.when(j == pl.num_programs(1) - 1)
    def _():
        sout_ref[...] = s_s[...]


RWKV_SEQ_PER_STEP = LANES // RWKV_HEADS


def _rwkv_scan(r, w, k, v, alr, gate, x, w_out, slabs, s0, B, T, tt):
    N = RWKV_N
    G = B // RWKV_SEQ_PER_STEP
    has_init = s0 is not None
    view = lambda a: a.reshape(B, T, D_MODEL)
    tmap = lambda g, j: (g, j, 0)
    smap = lambda g, j: (0, 0, g)
    const2 = lambda g, j: (0, 0)
    stream = pl.BlockSpec((RWKV_SEQ_PER_STEP, tt, D_MODEL), tmap)
    in_specs = [stream] * 7 + [pl.BlockSpec((D_MODEL, D_MODEL), const2)] + [pl.BlockSpec((N, LANES), const2)] * 5
    args = [view(r), view(w), view(k), view(v), view(alr), view(gate), view(x), w_out] + list(slabs)
    if has_init:
        in_specs.append(pl.BlockSpec((N, N, LANES), smap))
        args.append(s0)
    z, s = pl.pallas_call(
        functools.partial(_rwkv_scan_kernel, tt=tt, has_init=has_init, unroll=16),
        grid=(G, T // tt),
        in_specs=in_specs,
        out_specs=[stream, pl.BlockSpec((N, N, LANES), smap)],
        out_shape=[jax.ShapeDtypeStruct((B, T, D_MODEL), f32),
                   jax.ShapeDtypeStruct((N, N, G * LANES), f32)],
        scratch_shapes=[pltpu.VMEM((N, N, LANES), f32), pltpu.VMEM((4, N, LANES), f32),
                        pltpu.VMEM((4, 6, N, LANES), f32), pltpu.VMEM((RWKV_SEQ_PER_STEP, tt, D_MODEL), f32),
                        pltpu.VMEM((2, N, LANES), f32)],
        compiler_params=_cparams("parallel", "arbitrary"),
        name="rwkv_scan",
    )(*args)
    return z.reshape(B * T, D_MODEL), s


def _lane_slab(p):
    q = p.reshape(RWKV_HEADS // 2, 2, RWKV_N).transpose(2, 1, 0)
    q = jnp.broadcast_to(q[:, :, :, None], (RWKV_N, 2, RWKV_HEADS // 2, RWKV_SEQ_PER_STEP))
    return q.reshape(RWKV_N, LANES)


def _rwkv_mix(x, p, B, T, state, tm):
    H, N = RWKV_HEADS, RWKV_N
    G = B // RWKV_SEQ_PER_STEP
    if state is None:
        shift0 = jnp.zeros((B, D_MODEL), f32)
        s0 = None
    else:
        s_in, shift0 = state
        s0 = (s_in.reshape(G, RWKV_SEQ_PER_STEP, H // 2, 2, N, N)
              .transpose(5, 4, 0, 3, 2, 1).reshape(N, N, G * LANES))
    r, w, k, v, alr, gate = _rwkv_pre(x, shift0, p["g"], p["mu"], p["wrkv"], p["la"], p["lbw"], p["lba"], p["lbg"],
                                      p["w0"], p["a0"], B, T, tm)
    slabs = [_lane_slab(p[n]) for n in ("k_k", "k_a", "r_k", "ln_w", "ln_b")]
    x_new, s = _rwkv_scan(r, w, k, v, alr, gate, x, p["w_out"], slabs, s0, B, T, min(T, 64))
    s = (s.reshape(N, N, G, 2, H // 2, RWKV_SEQ_PER_STEP)
         .transpose(2, 5, 4, 3, 1, 0).reshape(B, H, N, N))
    return x_new, s


def _pad_cols(w, to):
    return jnp.pad(w, ((0, 0), (0, to - w.shape[1])))


def _trunk(x, B, T, states, p):
    n = x.shape[0]
    tm = min(512, n)
    tmm = min(1024, n)
    tq = min(256, T)
    fresh = states is None
    block_ok = fresh and T % tq == 0 and tq % (2 * LIN_CHUNK) == 0
    seq_ok = (not fresh) and T == SUBLANES and T <= LIN_CHUNK and B % SEQS_PER_STEP == 0
    if not fresh:
        m_c, m_n, m_m, h_s, g_s, r_s, r_sh = states
    new = {}
    for li in range(DEPTH):
        g_mix = p["norm_mix"][li]
        if li == 0:
            pr = _norm_proj(x, g_mix, p["mlstm_w_in"], tmm, 1664)
            if fresh and T % tq == 0 and tq % (2 * MLSTM_CHUNK) == 0:
                x, c, nn, m = _mlstm_mix_fresh(pr, p["mlstm_b_gates"], p["mlstm_norm"], x, p["mlstm_w_out"],
                                               B, T, tq)
            else:
                if seq_ok:
                    y, c, nn, m = _mlstm_mix_seq(pr, p["mlstm_b_gates"], p["mlstm_norm"], B, T,
                                                 (m_c[0], m_n[0], m_m[0]), SEQS_PER_STEP)
                else:
                    y, c, nn, m = _mlstm_mix(pr, p["mlstm_b_gates"], p["mlstm_norm"], B, T,
                                             None if fresh else (m_c[0], m_n[0], m_m[0]), tq)
                x = _out_proj(y, p["mlstm_w_out"], x, tm)
            new["C"], new["n"], new["m"] = c[None], nn[None], m[None]
        elif li == 1:
            pr = _norm_proj(x, g_mix, p["hgrn_w_in"], tmm, 2048)
            if block_ok:
                x, s = _gla_mix_fresh("hgrn", pr, (p["hgrn_lb_logits"],), p["hgrn_norm"], x, p["hgrn_w_out"],
                                      B, T, tq, layer=li)
            else:
                if seq_ok:
                    y, s = _gla_mix_seq("hgrn", pr, (p["hgrn_lb_logits"],), p["hgrn_norm"], B, T, h_s[0],
                                        SEQS_PER_STEP, layer=li)
                else:
                    y, s = _gla_mix("hgrn", pr, (p["hgrn_lb_logits"],), p["hgrn_norm"], B, T,
                                    None if fresh else h_s[0], tq, layer=li)
                x = _out_proj(y, p["hgrn_w_out"], x, tm)
            new["hS"] = s[None]
        elif li == 2:
            pr = _norm_proj(x, g_mix, p["gla_w_in"], tmm, 1664)
            if block_ok:
                x, s = _gla_mix_fresh("gla", pr, (p["gla_w_gate_up"], p["gla_b_gate"]), p["gla_norm"], x,
                                      p["gla_w_out"], B, T, tq)
            else:
                if seq_ok:
                    y, s = _gla_mix_seq("gla", pr, (p["gla_w_gate_up"], p["gla_b_gate"]), p["gla_norm"], B, T,
                                        g_s[0], SEQS_PER_STEP)
                else:
                    y, s = _gla_mix("gla", pr, (p["gla_w_gate_up"], p["gla_b_gate"]), p["gla_norm"], B, T,
                                    None if fresh else g_s[0], tq)
                x = _out_proj(y, p["gla_w_out"], x, tm)
            new["gS"] = s[None]
        else:
            rp = dict(p["rwkv"], g=g_mix, w_out=p["rwkv_w_out"])
            x_last = x.reshape(B, T, D_MODEL)[:, T - 1, :]
            new["sh"] = _rmsnorm(x_last, g_mix, B)[None]
            x, s = _rwkv_mix(x, rp, B, T, None if fresh else (r_s[0], r_sh[0]), tm)
            new["rS"] = s[None]
        x = _ffn(x, p["norm_ffn"][li], p["ffn_w_gate_up"][li], p["ffn_w_down"][li], tm, 1408,
                 final_gain=p["norm_final"] if li == DEPTH - 1 else None)
    return x.reshape(B, T, D_MODEL), (new["C"], new["n"], new["m"], new["hS"], new["gS"], new["rS"], new["sh"])


def kernel(x_prompt, x_sample, state_mlstm_C, state_mlstm_n, state_mlstm_m, state_hgrn_S, state_gla_S, state_rwkv_S, state_rwkv_shift, norm_mix, norm_ffn, norm_final, mlstm_w_in, mlstm_b_gates, mlstm_norm, mlstm_w_out, hgrn_w_in, hgrn_lb_logits, hgrn_norm, hgrn_w_out, gla_w_in, gla_w_gate_up, gla_b_gate, gla_norm, gla_w_out, rwkv_mu, rwkv_w_rkv, rwkv_w_lora_a, rwkv_w_lora_b, rwkv_w0, rwkv_a_lora_a, rwkv_a_lora_b, rwkv_a0, rwkv_g_lora_a, rwkv_g_lora_b, rwkv_k_k, rwkv_k_a, rwkv_r_k, rwkv_ln_w, rwkv_ln_b, rwkv_w_out, ffn_w_gate_up, ffn_w_down):
    cast = lambda w: w.astype(bf16)
    p = dict(
        norm_mix=norm_mix, norm_ffn=norm_ffn, norm_final=norm_final,
        mlstm_w_in=cast(_pad_cols(mlstm_w_in[0], 3328)), mlstm_b_gates=mlstm_b_gates[0], mlstm_norm=mlstm_norm[0],
        mlstm_w_out=cast(mlstm_w_out[0]),
        hgrn_w_in=cast(hgrn_w_in[0]), hgrn_lb_logits=hgrn_lb_logits, hgrn_norm=hgrn_norm[0],
        hgrn_w_out=cast(hgrn_w_out[0]),
        gla_w_in=cast(_pad_cols(gla_w_in[0], 3328)), gla_w_gate_up=gla_w_gate_up[0], gla_b_gate=gla_b_gate[0],
        gla_norm=gla_norm[0], gla_w_out=cast(gla_w_out[0]),
        rwkv=dict(mu=rwkv_mu[0], wrkv=cast(rwkv_w_rkv[0]),
                  la=cast(jnp.concatenate([rwkv_w_lora_a[0], rwkv_a_lora_a[0], rwkv_g_lora_a[0]], axis=1)),
                  lbw=cast(rwkv_w_lora_b[0]), lba=cast(rwkv_a_lora_b[0]), lbg=cast(rwkv_g_lora_b[0]),
                  w0=rwkv_w0[0], a0=rwkv_a0[0], k_k=rwkv_k_k[0], k_a=rwkv_k_a[0], r_k=rwkv_r_k[0].reshape(-1),
                  ln_w=rwkv_ln_w[0], ln_b=rwkv_ln_b[0]),
        rwkv_w_out=cast(rwkv_w_out[0]),
        ffn_w_gate_up=cast(ffn_w_gate_up), ffn_w_down=cast(ffn_w_down),
    )
    bp, tp, _ = x_prompt.shape
    bs, ts, _ = x_sample.shape
    y_p, st_p = _trunk(x_prompt.reshape(bp * tp, D_MODEL), bp, tp, None, p)
    y_s, st_s = _trunk(x_sample.reshape(bs * ts, D_MODEL), bs, ts,
                       (state_mlstm_C, state_mlstm_n, state_mlstm_m, state_hgrn_S, state_gla_S, state_rwkv_S,
                        state_rwkv_shift), p)
    return (y_p, y_s) + st_p + st_s
```
